```python
import jax, jax.numpy as jnp
from jax import lax
import numpy as np

D_MODEL = 1024
BATCH = 8
SEQ = 8192
DEPTH = 2

POOL_WINDOWS = (2, 4, 8, 16)
POOL_GROUPS = 4
WIDTH_A = D_MODEL // 2
POOL_GROUP_DIM = WIDTH_A // POOL_GROUPS
CHUNK = 128
SGU_HEADS = 4
WIDTH_B = D_MODEL // 2
SGU_HEAD_DIM = WIDTH_B // SGU_HEADS
WIDTH_C = D_MODEL // 2
CONV_WIDTH = 3
N_BRANCHES = 3
W_IN_COLS = WIDTH_A + 2 * WIDTH_B + 3 * WIDTH_C + N_BRANCHES * D_MODEL
D_FF = 2816
EPS = 1e-6

kernel_name = "hybrid_pool_sgu_shortconv_gated_block"


def rmsnorm(x, g):
    xf = x.astype(jnp.float32)
    y = xf * lax.rsqrt(jnp.mean(xf * xf, axis=-1, keepdims=True) + EPS)
    return (y * g.astype(jnp.float32)).astype(x.dtype)


def causal_dwconv3(z, w):
    s = z.shape[1]
    zp = jnp.pad(z, ((0, 0), (CONV_WIDTH - 1, 0), (0, 0)))
    return zp[:, :s] * w[0] + zp[:, 1:s + 1] * w[1] + zp[:, 2:s + 2] * w[2]


def multiscale_pool(a):
    bsz, s = a.shape[0], a.shape[1]
    af = a.astype(jnp.float32)
    cs = jnp.cumsum(af, axis=1)
    cs_pad = jnp.pad(cs, ((0, 0), (1, 0), (0, 0), (0, 0)))
    t = jnp.arange(s, dtype=jnp.float32)
    outs = []
    for g, w in enumerate(POOL_WINDOWS):
        upper = cs[:, :, g]
        lower = jnp.concatenate(
            [jnp.zeros((bsz, w - 1, a.shape[3]), jnp.float32), cs_pad[:, :s - w + 1, g]], axis=1)
        cnt = jnp.minimum(t + 1.0, float(w))[None, :, None]
        outs.append((upper - lower) / cnt - af[:, :, g])
    return jnp.stack(outs, axis=2).astype(a.dtype)


def _fwd_setup_inputs(seed: int = 0) -> dict:
    key = jax.random.key(seed)
    k = jax.random.split(key, 20)
    n = jax.random.normal
    f32 = jnp.float32
    L = DEPTH
    tri = jnp.tril(jnp.ones((CHUNK, CHUNK), f32))
    row_scale = (jnp.arange(CHUNK, dtype=f32) + 1.0) ** -0.5
    w_spatial = n(k[5], (L, SGU_HEADS, CHUNK, CHUNK), f32) * tri * row_scale[:, None]
    return {
        "x": n(k[0], (BATCH, SEQ, D_MODEL), f32),
        "g_mix": 1.0 + 0.02 * n(k[1], (L, D_MODEL), f32),
        "w_in": n(k[2], (L, D_MODEL, W_IN_COLS), f32) * D_MODEL ** -0.5,
        "w_pool": n(k[3], (L, POOL_GROUPS, POOL_GROUP_DIM, POOL_GROUP_DIM), f32) * POOL_GROUP_DIM ** -0.5,
        "pool_scale": 1.0 + 0.1 * n(k[4], (L, WIDTH_A), f32),
        "g_sgu": 1.0 + 0.02 * n(k[6], (L, WIDTH_B), f32),
        "w_spatial": w_spatial,
        "b_spatial": 1.0 + 0.01 * n(k[7], (L, SGU_HEADS, CHUNK), f32),
        "conv_c": n(k[8], (L, CONV_WIDTH, WIDTH_C), f32) * CONV_WIDTH ** -0.5,
        "w_branch_a": n(k[9], (L, WIDTH_A, D_MODEL), f32) * WIDTH_A ** -0.5,
        "w_branch_b": n(k[10], (L, WIDTH_B, D_MODEL), f32) * WIDTH_B ** -0.5,
        "w_branch_c": n(k[11], (L, WIDTH_C, D_MODEL), f32) * WIDTH_C ** -0.5,
        "w_o": n(k[12], (L, D_MODEL, D_MODEL), f32) * D_MODEL ** -0.5,
        "g_ffn": 1.0 + 0.02 * n(k[13], (L, D_MODEL), f32),
        "w_up": n(k[14], (L, D_MODEL, 2 * D_FF), f32) * D_MODEL ** -0.5,
        "conv_ffn": n(k[15], (L, CONV_WIDTH, 2 * D_FF), f32) * CONV_WIDTH ** -0.5,
        "conv_ffn_b": 0.01 * n(k[16], (L, 2 * D_FF), f32),
        "w_down": n(k[17], (L, D_FF, D_MODEL), f32) * D_FF ** -0.5,
        "g_final": 1.0 + 0.02 * n(k[18], (D_MODEL,), f32),
    }


def _fwd_reference(x, g_mix, w_in, w_pool, pool_scale, g_sgu, w_spatial, b_spatial, conv_c,
              w_branch_a, w_branch_b, w_branch_c, w_o, g_ffn, w_up, conv_ffn, conv_ffn_b,
              w_down, g_final):
    bsz, s, _ = x.shape
    n_chunks = s // CHUNK
    splits = np.cumsum([WIDTH_A, 2 * WIDTH_B, WIDTH_C, WIDTH_C, WIDTH_C, D_MODEL, D_MODEL]).tolist()
    for l in range(DEPTH):
        h = rmsnorm(x, g_mix[l])
        p = h @ w_in[l]
        a, uv, c_b, c_c, c_x, ga, gb, gc = jnp.split(p, splits, axis=-1)

        a = a.reshape(bsz, s, POOL_GROUPS, POOL_GROUP_DIM)
        pa = multiscale_pool(a)
        ya = jnp.einsum("bsgd,gde->bsge", pa, w_pool[l]).reshape(bsz, s, WIDTH_A) * pool_scale[l]

        uv = jax.nn.gelu(uv)
        u, v = jnp.split(uv, 2, axis=-1)
        v = rmsnorm(v, g_sgu[l])
        v = v.reshape(bsz, n_chunks, CHUNK, SGU_HEADS, SGU_HEAD_DIM)
        ws = jnp.tril(w_spatial[l])
        sv = jnp.einsum("gts,bcsgd->bctgd", ws, v) + b_spatial[l].T[None, None, :, :, None]
        yb = u * sv.reshape(bsz, s, WIDTH_B)

        yc = c_b * causal_dwconv3(c_c * c_x, conv_c[l])

        merged = (jax.nn.sigmoid(ga) * (ya @ w_branch_a[l])
                  + jax.nn.sigmoid(gb) * (yb @ w_branch_b[l])
                  + jax.nn.sigmoid(gc) * (yc @ w_branch_c[l]))
        x = x + merged @ w_o[l]

        h = rmsnorm(x, g_ffn[l])
        up = causal_dwconv3(h @ w_up[l], conv_ffn[l]) + conv_ffn_b[l]
        gate, val = jnp.split(up, 2, axis=-1)
        x = x + (jax.nn.silu(gate) * val) @ w_down[l]
    return rmsnorm(x, g_final)


import jax as _jax
import jax.numpy as _jnp

TWIN_FORMAT = 'train_step'
FWD_PARAMS = ['x', 'g_mix', 'w_in', 'w_pool', 'pool_scale', 'g_sgu', 'w_spatial', 'b_spatial', 'conv_c', 'w_branch_a', 'w_branch_b', 'w_branch_c', 'w_o', 'g_ffn', 'w_up', 'conv_ffn', 'conv_ffn_b', 'w_down', 'g_final']
TWIN_WEIGHTS = ['g_mix', 'w_in', 'w_pool', 'pool_scale', 'g_sgu', 'w_spatial', 'b_spatial', 'conv_c', 'w_branch_a', 'w_branch_b', 'w_branch_c', 'w_o', 'g_ffn', 'w_up', 'conv_ffn', 'conv_ffn_b', 'w_down', 'g_final']
TWIN_DIFF_INPUT = 'x'
TWIN_INPUTS = ['x', 'g_mix', 'w_in', 'w_pool', 'pool_scale', 'g_sgu', 'w_spatial', 'b_spatial', 'conv_c', 'w_branch_a', 'w_branch_b', 'w_branch_c', 'w_o', 'g_ffn', 'w_up', 'conv_ffn', 'conv_ffn_b', 'w_down', 'g_final', 'loss_target', 'm_g_mix', 'm_w_in', 'm_w_pool', 'm_pool_scale', 'm_g_sgu', 'm_w_spatial', 'm_b_spatial', 'm_conv_c', 'm_w_branch_a', 'm_w_branch_b', 'm_w_branch_c', 'm_w_o', 'm_g_ffn', 'm_w_up', 'm_conv_ffn', 'm_conv_ffn_b', 'm_w_down', 'm_g_final', 'v_g_mix', 'v_w_in', 'v_w_pool', 'v_pool_scale', 'v_g_sgu', 'v_w_spatial', 'v_b_spatial', 'v_conv_c', 'v_w_branch_a', 'v_w_branch_b', 'v_w_branch_c', 'v_w_o', 'v_g_ffn', 'v_w_up', 'v_conv_ffn', 'v_conv_ffn_b', 'v_w_down', 'v_g_final']
TWIN_OUTPUTS = ['loss', 'grad_x', 'grad_g_mix', 'grad_w_in', 'grad_w_pool', 'grad_pool_scale', 'grad_g_sgu', 'grad_w_spatial', 'grad_b_spatial', 'grad_conv_c', 'grad_w_branch_a', 'grad_w_branch_b', 'grad_w_branch_c', 'grad_w_o', 'grad_g_ffn', 'grad_w_up', 'grad_conv_ffn', 'grad_conv_ffn_b', 'grad_w_down', 'grad_g_final', 'delta_g_mix', 'delta_w_in', 'delta_w_pool', 'delta_pool_scale', 'delta_g_sgu', 'delta_w_spatial', 'delta_b_spatial', 'delta_conv_c', 'delta_w_branch_a', 'delta_w_branch_b', 'delta_w_branch_c', 'delta_w_o', 'delta_g_ffn', 'delta_w_up', 'delta_conv_ffn', 'delta_conv_ffn_b', 'delta_w_down', 'delta_g_final', 'new_m_g_mix', 'new_m_w_in', 'new_m_w_pool', 'new_m_pool_scale', 'new_m_g_sgu', 'new_m_w_spatial', 'new_m_b_spatial', 'new_m_conv_c', 'new_m_w_branch_a', 'new_m_w_branch_b', 'new_m_w_branch_c', 'new_m_w_o', 'new_m_g_ffn', 'new_m_w_up', 'new_m_conv_ffn', 'new_m_conv_ffn_b', 'new_m_w_down', 'new_m_g_final', 'new_v_g_mix', 'new_v_w_in', 'new_v_w_pool', 'new_v_pool_scale', 'new_v_g_sgu', 'new_v_w_spatial', 'new_v_b_spatial', 'new_v_conv_c', 'new_v_w_branch_a', 'new_v_w_branch_b', 'new_v_w_branch_c', 'new_v_w_o', 'new_v_g_ffn', 'new_v_w_up', 'new_v_conv_ffn', 'new_v_conv_ffn_b', 'new_v_w_down', 'new_v_g_final']
TWIN_LEAF_KINDS = {'loss': 'loss', 'grad_x': 'grad_x', 'grad_g_mix': 'grad_w', 'grad_w_in': 'grad_w', 'grad_w_pool': 'grad_w', 'grad_pool_scale': 'grad_w', 'grad_g_sgu': 'grad_w', 'grad_w_spatial': 'grad_w', 'grad_b_spatial': 'grad_w', 'grad_conv_c': 'grad_w', 'grad_w_branch_a': 'grad_w', 'grad_w_branch_b': 'grad_w', 'grad_w_branch_c': 'grad_w', 'grad_w_o': 'grad_w', 'grad_g_ffn': 'grad_w', 'grad_w_up': 'grad_w', 'grad_conv_ffn': 'grad_w', 'grad_conv_ffn_b': 'grad_w', 'grad_w_down': 'grad_w', 'grad_g_final': 'grad_w', 'delta_g_mix': 'delta_w', 'delta_w_in': 'delta_w', 'delta_w_pool': 'delta_w', 'delta_pool_scale': 'delta_w', 'delta_g_sgu': 'delta_w', 'delta_w_spatial': 'delta_w', 'delta_b_spatial': 'delta_w', 'delta_conv_c': 'delta_w', 'delta_w_branch_a': 'delta_w', 'delta_w_branch_b': 'delta_w', 'delta_w_branch_c': 'delta_w', 'delta_w_o': 'delta_w', 'delta_g_ffn': 'delta_w', 'delta_w_up': 'delta_w', 'delta_conv_ffn': 'delta_w', 'delta_conv_ffn_b': 'delta_w', 'delta_w_down': 'delta_w', 'delta_g_final': 'delta_w', 'new_m_g_mix': 'new_m', 'new_m_w_in': 'new_m', 'new_m_w_pool': 'new_m', 'new_m_pool_scale': 'new_m', 'new_m_g_sgu': 'new_m', 'new_m_w_spatial': 'new_m', 'new_m_b_spatial': 'new_m', 'new_m_conv_c': 'new_m', 'new_m_w_branch_a': 'new_m', 'new_m_w_branch_b': 'new_m', 'new_m_w_branch_c': 'new_m', 'new_m_w_o': 'new_m', 'new_m_g_ffn': 'new_m', 'new_m_w_up': 'new_m', 'new_m_conv_ffn': 'new_m', 'new_m_conv_ffn_b': 'new_m', 'new_m_w_down': 'new_m', 'new_m_g_final': 'new_m', 'new_v_g_mix': 'new_v', 'new_v_w_in': 'new_v', 'new_v_w_pool': 'new_v', 'new_v_pool_scale': 'new_v', 'new_v_g_sgu': 'new_v', 'new_v_w_spatial': 'new_v', 'new_v_b_spatial': 'new_v', 'new_v_conv_c': 'new_v', 'new_v_w_branch_a': 'new_v', 'new_v_w_branch_b': 'new_v', 'new_v_w_branch_c': 'new_v', 'new_v_w_o': 'new_v', 'new_v_g_ffn': 'new_v', 'new_v_w_up': 'new_v', 'new_v_conv_ffn': 'new_v', 'new_v_conv_ffn_b': 'new_v', 'new_v_w_down': 'new_v', 'new_v_g_final': 'new_v'}


def _forward(args):
    return _fwd_reference(*[args[k] for k in FWD_PARAMS])


def _output_shape():
    out = _jax.eval_shape(lambda: _forward(_fwd_setup_inputs(0)))
    return out.shape, out.dtype

N_MICROBATCH = 1
ADAM_LR = 0.001
ADAM_B1 = 0.9
ADAM_B2 = 0.999
ADAM_EPS = 1e-08
ADAM_WD = 0.01
ADAM_STEP = 10
PER_EXAMPLE_BATCH_AXIS = {'x': 0, 'loss_target': 0}
SHARED_INPUTS = []
_WEIGHT_DTYPES = {'g_mix': _jnp.float32, 'w_in': _jnp.float32, 'w_pool': _jnp.float32, 'pool_scale': _jnp.float32, 'g_sgu': _jnp.float32, 'w_spatial': _jnp.float32, 'b_spatial': _jnp.float32, 'conv_c': _jnp.float32, 'w_branch_a': _jnp.float32, 'w_branch_b': _jnp.float32, 'w_branch_c': _jnp.float32, 'w_o': _jnp.float32, 'g_ffn': _jnp.float32, 'w_up': _jnp.float32, 'conv_ffn': _jnp.float32, 'conv_ffn_b': _jnp.float32, 'w_down': _jnp.float32, 'g_final': _jnp.float32}
MOMENT_SCALE = {'g_mix': 2.631638e-01, 'w_in': 1.071679e-01, 'w_pool': 1.397603e-01, 'pool_scale': 1.452568e-01, 'g_sgu': 1.074539e-01, 'w_spatial': 7.453781e-02, 'b_spatial': 1.080240e-01, 'conv_c': 1.594106e-01, 'w_branch_a': 9.978232e-02, 'w_branch_b': 1.022884e-01, 'w_branch_c': 1.108743e-01, 'w_o': 1.812873e-01, 'g_ffn': 1.588200e-01, 'w_up': 6.675312e-02, 'conv_ffn': 6.649065e-02, 'conv_ffn_b': 6.553704e-02, 'w_down': 1.088452e-01, 'g_final': 6.399041e+01}


def _to_microbatches(a, axis):
    t = _jnp.moveaxis(a, axis, 0)
    t = t.reshape((N_MICROBATCH, t.shape[0] // N_MICROBATCH) + t.shape[1:])
    return _jnp.moveaxis(t, 1, axis + 1)


def setup_inputs(seed: int = 0) -> dict:
    inp = _fwd_setup_inputs(seed)
    key = _jax.random.fold_in(_jax.random.key(seed), 7919)
    shape, _ = _output_shape()
    out = dict(inp)
    out["loss_target"] = _jax.random.normal(_jax.random.fold_in(key, 0), shape, _jnp.float32)
    for i, name in enumerate(TWIN_WEIGHTS):
        w = inp[name].astype(_jnp.float32)
        if MOMENT_SCALE is None:
            s = _jnp.sqrt(_jnp.mean(_jnp.square(w)) + 1e-30)
        else:
            s = MOMENT_SCALE[name]
        km, kv = _jax.random.split(_jax.random.fold_in(key, i + 1))
        out[name] = w
        out["m_" + name] = s * _jax.random.normal(km, w.shape, _jnp.float32)
        out["v_" + name] = (s * s) * _jax.random.uniform(kv, w.shape, _jnp.float32, 0.5, 1.5)
    if N_MICROBATCH > 1:
        for name, axis in PER_EXAMPLE_BATCH_AXIS.items():
            out[name] = _to_microbatches(out[name], axis)
    return {'x': out['x'], 'g_mix': out['g_mix'], 'w_in': out['w_in'], 'w_pool': out['w_pool'], 'pool_scale': out['pool_scale'], 'g_sgu': out['g_sgu'], 'w_spatial': out['w_spatial'], 'b_spatial': out['b_spatial'], 'conv_c': out['conv_c'], 'w_branch_a': out['w_branch_a'], 'w_branch_b': out['w_branch_b'], 'w_branch_c': out['w_branch_c'], 'w_o': out['w_o'], 'g_ffn': out['g_ffn'], 'w_up': out['w_up'], 'conv_ffn': out['conv_ffn'], 'conv_ffn_b': out['conv_ffn_b'], 'w_down': out['w_down'], 'g_final': out['g_final'], 'loss_target': out['loss_target'], 'm_g_mix': out['m_g_mix'], 'm_w_in': out['m_w_in'], 'm_w_pool': out['m_w_pool'], 'm_pool_scale': out['m_pool_scale'], 'm_g_sgu': out['m_g_sgu'], 'm_w_spatial': out['m_w_spatial'], 'm_b_spatial': out['m_b_spatial'], 'm_conv_c': out['m_conv_c'], 'm_w_branch_a': out['m_w_branch_a'], 'm_w_branch_b': out['m_w_branch_b'], 'm_w_branch_c': out['m_w_branch_c'], 'm_w_o': out['m_w_o'], 'm_g_ffn': out['m_g_ffn'], 'm_w_up': out['m_w_up'], 'm_conv_ffn': out['m_conv_ffn'], 'm_conv_ffn_b': out['m_conv_ffn_b'], 'm_w_down': out['m_w_down'], 'm_g_final': out['m_g_final'], 'v_g_mix': out['v_g_mix'], 'v_w_in': out['v_w_in'], 'v_w_pool': out['v_w_pool'], 'v_pool_scale': out['v_pool_scale'], 'v_g_sgu': out['v_g_sgu'], 'v_w_spatial': out['v_w_spatial'], 'v_b_spatial': out['v_b_spatial'], 'v_conv_c': out['v_conv_c'], 'v_w_branch_a': out['v_w_branch_a'], 'v_w_branch_b': out['v_w_branch_b'], 'v_w_branch_c': out['v_w_branch_c'], 'v_w_o': out['v_w_o'], 'v_g_ffn': out['v_g_ffn'], 'v_w_up': out['v_w_up'], 'v_conv_ffn': out['v_conv_ffn'], 'v_conv_ffn_b': out['v_conv_ffn_b'], 'v_w_down': out['v_w_down'], 'v_g_final': out['v_g_final']}


def _loss(weights, diff, rest, loss_target):
    with _jax.named_scope("forward"):
        args = {**rest, TWIN_DIFF_INPUT: diff, **{k: w.astype(_WEIGHT_DTYPES[k]) for k, w in weights.items()}}
        y = _forward(args)
    with _jax.named_scope("loss_head"):
        err = _jnp.square(y.astype(_jnp.float32) - loss_target)
        return 0.5 * _jnp.sum(_jnp.mean(err, axis=-1)) if err.ndim else 0.5 * err


def _adamw(w, g, m, v):
    m = ADAM_B1 * m + (1.0 - ADAM_B1) * g
    v = ADAM_B2 * v + (1.0 - ADAM_B2) * _jnp.square(g)
    m_hat = m / (1.0 - ADAM_B1 ** ADAM_STEP)
    v_hat = v / (1.0 - ADAM_B2 ** ADAM_STEP)
    delta = -ADAM_LR * (m_hat / (_jnp.sqrt(v_hat) + ADAM_EPS) + ADAM_WD * w)
    return delta, m, v


def reference(x, g_mix, w_in, w_pool, pool_scale, g_sgu, w_spatial, b_spatial, conv_c, w_branch_a, w_branch_b, w_branch_c, w_o, g_ffn, w_up, conv_ffn, conv_ffn_b, w_down, g_final, loss_target, m_g_mix, m_w_in, m_w_pool, m_pool_scale, m_g_sgu, m_w_spatial, m_b_spatial, m_conv_c, m_w_branch_a, m_w_branch_b, m_w_branch_c, m_w_o, m_g_ffn, m_w_up, m_conv_ffn, m_conv_ffn_b, m_w_down, m_g_final, v_g_mix, v_w_in, v_w_pool, v_pool_scale, v_g_sgu, v_w_spatial, v_b_spatial, v_conv_c, v_w_branch_a, v_w_branch_b, v_w_branch_c, v_w_o, v_g_ffn, v_w_up, v_conv_ffn, v_conv_ffn_b, v_w_down, v_g_final):
    given = dict(x=x, g_mix=g_mix, w_in=w_in, w_pool=w_pool, pool_scale=pool_scale, g_sgu=g_sgu, w_spatial=w_spatial, b_spatial=b_spatial, conv_c=conv_c, w_branch_a=w_branch_a, w_branch_b=w_branch_b, w_branch_c=w_branch_c, w_o=w_o, g_ffn=g_ffn, w_up=w_up, conv_ffn=conv_ffn, conv_ffn_b=conv_ffn_b, w_down=w_down, g_final=g_final, loss_target=loss_target, m_g_mix=m_g_mix, m_w_in=m_w_in, m_w_pool=m_w_pool, m_pool_scale=m_pool_scale, m_g_sgu=m_g_sgu, m_w_spatial=m_w_spatial, m_b_spatial=m_b_spatial, m_conv_c=m_conv_c, m_w_branch_a=m_w_branch_a, m_w_branch_b=m_w_branch_b, m_w_branch_c=m_w_branch_c, m_w_o=m_w_o, m_g_ffn=m_g_ffn, m_w_up=m_w_up, m_conv_ffn=m_conv_ffn, m_conv_ffn_b=m_conv_ffn_b, m_w_down=m_w_down, m_g_final=m_g_final, v_g_mix=v_g_mix, v_w_in=v_w_in, v_w_pool=v_w_pool, v_pool_scale=v_pool_scale, v_g_sgu=v_g_sgu, v_w_spatial=v_w_spatial, v_b_spatial=v_b_spatial, v_conv_c=v_conv_c, v_w_branch_a=v_w_branch_a, v_w_branch_b=v_w_branch_b, v_w_branch_c=v_w_branch_c, v_w_o=v_w_o, v_g_ffn=v_g_ffn, v_w_up=v_w_up, v_conv_ffn=v_conv_ffn, v_conv_ffn_b=v_conv_ffn_b, v_w_down=v_w_down, v_g_final=v_g_final)
    weights = {n: given[n] for n in TWIN_WEIGHTS}
    shared = {n: given[n] for n in SHARED_INPUTS}
    per_example = {n: given[n] for n in ['x']}
    grad_fn = _jax.value_and_grad(_loss, argnums=(0, 1))

    def one_microbatch(ex, loss_target):
        ex = dict(ex)
        diff = ex.pop(TWIN_DIFF_INPUT)
        return grad_fn(weights, diff, {**shared, **ex}, loss_target)

    if N_MICROBATCH == 1:
        loss, (grad_w, grad_x) = one_microbatch(per_example, given["loss_target"])
    else:
        def body(carry, xs):
            loss_sum, grad_sum = carry
            l_k, (gw_k, gx_k) = one_microbatch(xs[0], xs[1])
            with _jax.named_scope("update"):
                return (loss_sum + l_k, _jax.tree.map(_jnp.add, grad_sum, gw_k)), gx_k

        init = (_jnp.zeros((), _jnp.float32), _jax.tree.map(_jnp.zeros_like, weights))
        (loss, grad_w), grad_x = _jax.lax.scan(body, init, (per_example, given["loss_target"]))
    with _jax.named_scope("update"):
        delta_w, new_m, new_v = {}, {}, {}
        for n in TWIN_WEIGHTS:
            delta_w[n], new_m[n], new_v[n] = _adamw(weights[n], grad_w[n], given["m_" + n], given["v_" + n])
    return (loss, grad_x, *[grad_w[n] for n in TWIN_WEIGHTS], *[delta_w[n] for n in TWIN_WEIGHTS],
            *[new_m[n] for n in TWIN_WEIGHTS], *[new_v[n] for n in TWIN_WEIGHTS])
```

```python
import functools

import jax
import jax.numpy as jnp
from jax import lax
from jax.experimental import pallas as pl
from jax.experimental.pallas import tpu as pltpu

F32 = jnp.float32
BF16 = jnp.bfloat16
MXU = BF16
ACT = BF16
WIRE = BF16

N_DEV = 8
D = 1024
WA = 512
NCOL = 6144
DFF = 2816
NB_IN = NCOL // N_DEV
NB_UP = 2 * DFF // N_DEV
NB_BR = D // N_DEV
ROWS_O = D // N_DEV
ROWS_DN = DFF // N_DEV
CHUNK = 128
HEADS = 4
POOL_WINDOWS = (2, 4, 8, 16)
EPS = 1e-6
A0, UV0, CB0, CC0, CX0, GA0, GB0, GC0 = 0, 512, 1536, 2048, 2560, 3072, 4096, 5120

ADAM_LR = 0.001
ADAM_B1 = 0.9
ADAM_B2 = 0.999
ADAM_EPS = 1e-08
ADAM_WD = 0.01
ADAM_STEP = 10

TM = 256
TS_WGRAD = 1024
HALO_POOL = 16
HALO_CONV = 8
VMEM_LIMIT = 56 * 1024 * 1024
MESH = pl.DeviceIdType.MESH
AXES = ("x", "y", "c")


def _sds(shape, dtype):
    return jax.ShapeDtypeStruct(tuple(shape), dtype)


def _params(n_grid=1):
    return pltpu.CompilerParams(dimension_semantics=("arbitrary",) * n_grid, vmem_limit_bytes=VMEM_LIMIT)


def _const(block, index):
    return pl.BlockSpec(block, lambda *_: index, pipeline_mode=pl.Buffered(1))


def _dot(a, b):
    return jnp.dot(a, b, preferred_element_type=F32)


def _dot_nt(a, b):
    return lax.dot_general(a, b, (((1,), (1,)), ((), ())), preferred_element_type=F32)


def _dot_tn(a, b):
    return lax.dot_general(a, b, (((0,), (0,)), ((), ())), preferred_element_type=F32)


def _sigmoid(v):
    return 1.0 / (1.0 + jnp.exp(-v))


def _shift_down(v, k):
    return pltpu.roll(v, k, axis=0)


def _shift_up(v, k):
    return pltpu.roll(v, v.shape[0] - k, axis=0)


def _colsum(v):
    return jnp.sum(v, axis=0, keepdims=True)


def _lane_cat(ref):
    return jnp.concatenate([ref[d] for d in range(N_DEV)], axis=1)


def _rms_proj(x, g, w_all, layer, blocked_out, name):
    s = x.shape[0]
    nb = w_all.shape[-1]
    nt = s // TM

    def body(x_ref, g_ref, w_ref, p_ref, h_ref):
        xf = x_ref[...]
        r = lax.rsqrt(jnp.mean(xf * xf, axis=-1, keepdims=True) + EPS)
        h = (xf * r * g_ref[...]).astype(MXU)
        h_ref[...] = h
        for j in range(N_DEV):
            pj = _dot(h, w_ref[j]).astype(p_ref.dtype)
            if blocked_out:
                p_ref[j] = pj
            else:
                p_ref[:, j * nb:(j + 1) * nb] = pj

    if blocked_out:
        p_shape, p_spec = (N_DEV, s, nb), pl.BlockSpec((N_DEV, TM, nb), lambda i: (0, i, 0))
    else:
        p_shape, p_spec = (s, N_DEV * nb), pl.BlockSpec((TM, N_DEV * nb), lambda i: (i, 0))
    return pl.pallas_call(
        body, name=name, grid=(nt,),
        in_specs=[pl.BlockSpec((TM, D), lambda i: (i, 0)), _const((1, D), (0, 0)),
                  _const((N_DEV, None, D, nb), (0, layer, 0, 0))],
        out_specs=[p_spec, pl.BlockSpec((TM, D), lambda i: (i, 0))],
        out_shape=[_sds(p_shape, ACT), _sds((s, D), MXU)],
        compiler_params=_params(),
    )(x, g, w_all)


def _tril_mask():
    r = lax.broadcasted_iota(jnp.int32, (CHUNK, CHUNK), 0)
    c = lax.broadcasted_iota(jnp.int32, (CHUNK, CHUNK), 1)
    return r >= c


def _gelu_parts(v):
    c0 = 0.7978845608028654
    th = jnp.tanh(c0 * (v + 0.044715 * (v * v * v)))
    cdf = 0.5 * (1.0 + th)
    dgelu = cdf + v * (0.5 * c0) * (1.0 - th * th) * (1.0 + 3.0 * 0.044715 * (v * v))
    return v * cdf, dgelu


def _mixer_fwd(x, p, wpool, pscale, gsgu, wsp, bsp_t, convc, wa_all, wb_all, wc_all, wo_all, layer, name):
    s = x.shape[0]
    nt = s // TM

    def body(x_ref, p_ref, wpool_ref, ps_ref, gs_ref, wsp_ref, bsp_ref, cc_ref, wa_ref, wb_ref, wc_ref, wo_ref,
             xmid_ref, y_ref, pz_ref, b_ref, m_ref, carry_a, carry_z):
        i = pl.program_id(0)

        @pl.when(i == 0)
        def _():
            carry_a[...] = jnp.zeros_like(carry_a)
            carry_z[...] = jnp.zeros_like(carry_z)

        def pf(lo, n):
            return p_ref[:, lo:lo + n].astype(F32)

        a = pf(A0, WA)
        ext = jnp.concatenate([carry_a[...], a], axis=0)
        carry_a[...] = a[TM - HALO_POOL:, :]
        t_pos = (i * TM + lax.broadcasted_iota(jnp.int32, (TM, 1), 0)).astype(F32)
        for g, win in enumerate(POOL_WINDOWS):
            cols = slice(g * CHUNK, (g + 1) * CHUNK)
            acc = ext[:, cols]
            k = 1
            while k < win:
                acc = acc + _shift_down(acc, k)
                k *= 2
            cnt = jnp.minimum(t_pos + 1.0, float(win))
            pa_g = (acc[HALO_POOL:, :] / cnt - a[:, cols]).astype(MXU)
            pz_ref[:, cols] = pa_g
            y_ref[:, cols] = (_dot(pa_g, wpool_ref[g]) * ps_ref[:, cols]).astype(ACT)

        uvg, _ = _gelu_parts(pf(UV0, 2 * WA))
        u = uvg[:, :WA]
        v = uvg[:, WA:]
        rv = lax.rsqrt(jnp.mean(v * v, axis=-1, keepdims=True) + EPS)
        vn = (v * rv * gs_ref[...]).astype(MXU)
        mask = _tril_mask()
        for g in range(HEADS):
            cols = slice(g * CHUNK, (g + 1) * CHUNK)
            wt = jnp.where(mask, wsp_ref[g], 0.0).astype(MXU)
            bcol = bsp_ref[:, g:g + 1]
            for c in range(TM // CHUNK):
                rows = slice(c * CHUNK, (c + 1) * CHUNK)
                sv = _dot(wt, vn[rows, cols]) + bcol
                y_ref[rows, WA + g * CHUNK:WA + (g + 1) * CHUNK] = (u[rows, cols] * sv).astype(ACT)

        z = pf(CC0, WA) * pf(CX0, WA)
        extz = jnp.concatenate([carry_z[...], z], axis=0)
        carry_z[...] = z[TM - HALO_CONV:, :]
        cz = (cc_ref[0:1, :] * _shift_down(extz, 2)[HALO_CONV:, :]
              + cc_ref[1:2, :] * _shift_down(extz, 1)[HALO_CONV:, :] + cc_ref[2:3, :] * z)
        pz_ref[:, WA:2 * WA] = cz.astype(ACT)
        y_ref[:, 2 * WA:3 * WA] = (pf(CB0, WA) * cz).astype(ACT)

        merged = jnp.zeros((TM, D), F32)
        for k, (w_ref, glo) in enumerate(((wa_ref, GA0), (wb_ref, GB0), (wc_ref, GC0))):
            br = _dot(y_ref[:, k * WA:(k + 1) * WA], _lane_cat(w_ref))
            b_ref[:, k * D:(k + 1) * D] = br.astype(ACT)
            merged = merged + _sigmoid(pf(glo, D)) * br
        mb = merged.astype(MXU)
        m_ref[...] = mb
        xmid_ref[...] = x_ref[...] + _dot(mb, wo_ref[...].reshape(D, D))

    row = lambda n: pl.BlockSpec((TM, n), lambda i: (i, 0))
    br_spec = _const((N_DEV, None, WA, NB_BR), (0, layer, 0, 0))
    return pl.pallas_call(
        body, name=name, grid=(nt,),
        in_specs=[row(D), row(NCOL), _const((HEADS, CHUNK, CHUNK), (0, 0, 0)), _const((1, WA), (0, 0)),
                  _const((1, WA), (0, 0)), _const((HEADS, CHUNK, CHUNK), (0, 0, 0)), _const((CHUNK, HEADS), (0, 0)),
                  _const((3, WA), (0, 0)), br_spec, br_spec, br_spec,
                  _const((N_DEV, None, ROWS_O, D), (0, layer, 0, 0))],
        out_specs=[row(D), row(3 * WA), row(2 * WA), row(3 * D), row(D)],
        out_shape=[_sds((s, D), F32), _sds((s, 3 * WA), ACT), _sds((s, 2 * WA), ACT), _sds((s, 3 * D), ACT),
                   _sds((s, D), MXU)],
        scratch_shapes=[pltpu.VMEM((HALO_POOL, WA), F32), pltpu.VMEM((HALO_CONV, WA), F32)],
        compiler_params=_params(),
    )(x, p, wpool, pscale, gsgu, wsp, bsp_t, convc, wa_all, wb_all, wc_all, wo_all)


def _conv_up(ext, cur, w_ref, j, b_row):
    return (w_ref[j, 0:1, :] * _shift_down(ext, 2)[HALO_CONV:, :] + w_ref[j, 1:2, :] * _shift_down(ext, 1)[HALO_CONV:, :]
            + w_ref[j, 2:3, :] * cur + b_row)


def _ffn_fwd(xmid, upre, convf_all, convb, wd_all, layer, name):
    s = xmid.shape[0]
    nt = s // TM
    half = N_DEV // 2

    def body(x_ref, u_ref, cw_ref, cb_ref, wd_ref, xo_ref, act_ref, carry):
        i = pl.program_id(0)

        @pl.when(i == 0)
        def _():
            carry[...] = jnp.zeros_like(carry)

        def conv(j):
            cur = u_ref[j].astype(F32)
            ext = jnp.concatenate([carry[j], cur], axis=0)
            carry[j] = cur[TM - HALO_CONV:, :]
            return _conv_up(ext, cur, cw_ref, j, cb_ref[j:j + 1, :])

        acc = x_ref[...]
        for j in range(half):
            gate = conv(j)
            val = conv(j + half)
            act = (gate * _sigmoid(gate) * val).astype(MXU)
            act_ref[j] = act
            wd = jnp.concatenate([wd_ref[2 * j], wd_ref[2 * j + 1]], axis=0)
            acc = acc + _dot(act, wd)
        xo_ref[...] = acc

    return pl.pallas_call(
        body, name=name, grid=(nt,),
        in_specs=[pl.BlockSpec((TM, D), lambda i: (i, 0)), pl.BlockSpec((N_DEV, TM, NB_UP), lambda i: (0, i, 0)),
                  _const((N_DEV, None, 3, NB_UP), (0, layer, 0, 0)), _const((N_DEV, NB_UP), (0, 0)),
                  _const((N_DEV, None, ROWS_DN, D), (0, layer, 0, 0))],
        out_specs=[pl.BlockSpec((TM, D), lambda i: (i, 0)), pl.BlockSpec((half, TM, NB_UP), lambda i: (0, i, 0))],
        out_shape=[_sds((s, D), F32), _sds((half, s, NB_UP), MXU)],
        scratch_shapes=[pltpu.VMEM((N_DEV, HALO_CONV, NB_UP), F32)],
        compiler_params=_params(),
    )(xmid, upre, convf_all, convb, wd_all)


def _loss_head(x, g, target, name):
    s = x.shape[0]
    nt = s // TM

    def body(x_ref, g_ref, t_ref, dx_ref, dg_ref, loss_ref):
        i = pl.program_id(0)

        @pl.when(i == 0)
        def _():
            dg_ref[...] = jnp.zeros_like(dg_ref)
            loss_ref[...] = jnp.zeros_like(loss_ref)

        xf = x_ref[...]
        r = lax.rsqrt(jnp.mean(xf * xf, axis=-1, keepdims=True) + EPS)
        xn = xf * r
        err = xn * g_ref[...] - t_ref[...]
        loss_ref[...] += 0.5 * jnp.sum(jnp.mean(err * err, axis=-1, keepdims=True), axis=0, keepdims=True)
        dy = err * (1.0 / D)
        dg_ref[0:1, :] += _colsum(dy * xn)
        dyg = dy * g_ref[...]
        dx_ref[...] = r * (dyg - xn * jnp.mean(dyg * xn, axis=-1, keepdims=True))

    return pl.pallas_call(
        body, name=name, grid=(nt,),
        in_specs=[pl.BlockSpec((TM, D), lambda i: (i, 0)), _const((1, D), (0, 0)), pl.BlockSpec((TM, D), lambda i: (i, 0))],
        out_specs=[pl.BlockSpec((TM, D), lambda i: (i, 0)), pl.BlockSpec((8, D), lambda i: (0, 0)),
                   pl.BlockSpec((1, 1), lambda i: (0, 0))],
        out_shape=[_sds((s, D), F32), _sds((8, D), F32), _sds((1, 1), F32)],
        compiler_params=_params(),
    )(x, g, target)


def _ffn_bwd(dxo, upre, convf_all, convb, wd_all, layer, name):
    s = dxo.shape[0]
    nt = s // TM
    half = N_DEV // 2
    hb = TM // HALO_CONV

    def body(dx_ref, u_ref, halo_ref, cw_ref, cb_ref, wd_ref, du_ref, dc_ref, carry):
        step = pl.program_id(0)
        tile = nt - 1 - step

        @pl.when(step == 0)
        def _():
            carry[...] = jnp.zeros_like(carry)
            dc_ref[...] = jnp.zeros_like(dc_ref)

        dxb = dx_ref[...].astype(MXU)
        first = (tile > 0).astype(F32)

        def conv(j):
            cur = u_ref[j].astype(F32)
            ext = jnp.concatenate([halo_ref[j].astype(F32) * first, cur], axis=0)
            return cur, _conv_up(ext, cur, cw_ref, j, cb_ref[j:j + 1, :])

        def adjoint(j, cur, d_up):
            ext = jnp.concatenate([d_up, carry[j]], axis=0)
            carry[j] = d_up[:HALO_CONV, :]
            up1 = _shift_up(ext, 1)[:TM, :]
            up2 = _shift_up(ext, 2)[:TM, :]
            du_ref[j] = (cw_ref[j, 2:3, :] * d_up + cw_ref[j, 1:2, :] * up1 + cw_ref[j, 0:1, :] * up2).astype(du_ref.dtype)
            dc_ref[j, 0:1, :] += _colsum(cur * up2)
            dc_ref[j, 1:2, :] += _colsum(cur * up1)
            dc_ref[j, 2:3, :] += _colsum(cur * d_up)
            dc_ref[j, 3:4, :] += _colsum(d_up)

        for j in range(half):
            ug, gate = conv(j)
            uv, val = conv(j + half)
            sg = _sigmoid(gate)
            wd = jnp.concatenate([wd_ref[2 * j], wd_ref[2 * j + 1]], axis=0)
            dact = _dot_nt(dxb, wd)
            adjoint(j, ug, dact * val * sg * (1.0 + gate * (1.0 - sg)))
            adjoint(j + half, uv, dact * gate * sg)

    return pl.pallas_call(
        body, name=name, grid=(nt,),
        in_specs=[pl.BlockSpec((TM, D), lambda i: (nt - 1 - i, 0)),
                  pl.BlockSpec((N_DEV, TM, NB_UP), lambda i: (0, nt - 1 - i, 0)),
                  pl.BlockSpec((N_DEV, HALO_CONV, NB_UP), lambda i: (0, jnp.maximum((nt - 1 - i) * hb - 1, 0), 0)),
                  _const((N_DEV, None, 3, NB_UP), (0, layer, 0, 0)), _const((N_DEV, NB_UP), (0, 0)),
                  _const((N_DEV, None, ROWS_DN, D), (0, layer, 0, 0))],
        out_specs=[pl.BlockSpec((N_DEV, TM, NB_UP), lambda i: (0, nt - 1 - i, 0)),
                   pl.BlockSpec((N_DEV, 8, NB_UP), lambda i: (0, 0, 0))],
        out_shape=[_sds((N_DEV, s, NB_UP), MXU), _sds((N_DEV, 8, NB_UP), F32)],
        scratch_shapes=[pltpu.VMEM((N_DEV, HALO_CONV, NB_UP), F32)],
        compiler_params=_params(),
    )(dxo, upre, upre, convf_all, convb, wd_all)


def _proj_bwd(dy, blocked_dy, w_all, layer, x, g, dres, name):
    s = x.shape[0]
    nb = w_all.shape[-1]
    nt = s // TM

    def body(dy_ref, w_ref, x_ref, g_ref, dres_ref, dx_ref, dg_ref):
        i = pl.program_id(0)

        @pl.when(i == 0)
        def _():
            dg_ref[...] = jnp.zeros_like(dg_ref)

        dh = jnp.zeros((TM, D), F32)
        for j in range(N_DEV):
            dyj = dy_ref[j] if blocked_dy else dy_ref[:, j * nb:(j + 1) * nb]
            dh = dh + _dot_nt(dyj, w_ref[j])
        xf = x_ref[...]
        r = lax.rsqrt(jnp.mean(xf * xf, axis=-1, keepdims=True) + EPS)
        xn = xf * r
        dg_ref[0:1, :] += _colsum(dh * xn)
        dhg = dh * g_ref[...]
        dx_ref[...] = dres_ref[...] + r * (dhg - xn * jnp.mean(dhg * xn, axis=-1, keepdims=True))

    if blocked_dy:
        dy_spec = pl.BlockSpec((N_DEV, TM, nb), lambda i: (0, i, 0))
    else:
        dy_spec = pl.BlockSpec((TM, N_DEV * nb), lambda i: (i, 0))
    row = pl.BlockSpec((TM, D), lambda i: (i, 0))
    return pl.pallas_call(
        body, name=name, grid=(nt,),
        in_specs=[dy_spec, _const((N_DEV, None, D, nb), (0, layer, 0, 0)), row, _const((1, D), (0, 0)), row],
        out_specs=[row, pl.BlockSpec((8, D), lambda i: (0, 0))],
        out_shape=[_sds((s, D), F32), _sds((8, D), F32)],
        compiler_params=_params(),
    )(dy, w_all, x, g, dres)


def _mixer_bwd(dxmid, p, yabc, pacz, babc, wpool, pscale, gsgu, wsp, bsp_t, convc, wa_all, wb_all, wc_all, wo_all,
               layer, name):
    s = dxmid.shape[0]
    nt = s // TM

    def body(dx_ref, p_ref, y_ref, pz_ref, b_ref, wpool_ref, ps_ref, gs_ref, wsp_ref, bsp_ref, cc_ref,
             wa_ref, wb_ref, wc_ref, wo_ref,
             dp_ref, db_ref, dwp_ref, dws_ref, small_ref, dbs_ref,
             carry_pa, carry_cz, dbs_acc, du_s, dvn_s):
        step = pl.program_id(0)
        tile = nt - 1 - step

        @pl.when(step == 0)
        def _():
            for ref in (carry_pa, carry_cz, dbs_acc, dwp_ref, dws_ref, small_ref, dbs_ref):
                ref[...] = jnp.zeros_like(ref)

        def pf(lo, n):
            return p_ref[:, lo:lo + n].astype(F32)

        dm = _dot_nt(dx_ref[...].astype(MXU), wo_ref[...].reshape(D, D))

        def through_gate(k, glo, w_ref):
            sg = _sigmoid(pf(glo, D))
            br = b_ref[:, k * D:(k + 1) * D].astype(F32)
            dp_ref[:, glo:glo + D] = (dm * br * sg * (1.0 - sg)).astype(dp_ref.dtype)
            dbr = (dm * sg).astype(MXU)
            db_ref[:, k * D:(k + 1) * D] = dbr
            return _dot_nt(dbr, _lane_cat(w_ref))

        dya = through_gate(0, GA0, wa_ref)
        dyb = through_gate(1, GB0, wb_ref)
        dyc = through_gate(2, GC0, wc_ref)

        t_pos = (tile * TM + lax.broadcasted_iota(jnp.int32, (TM, 1), 0)).astype(F32)
        for g, win in enumerate(POOL_WINDOWS):
            cols = slice(g * CHUNK, (g + 1) * CHUNK)
            pa_g = pz_ref[:, cols]
            q = _dot(pa_g, wpool_ref[g])
            dya_g = dya[:, cols]
            small_ref[0:1, cols] += _colsum(dya_g * q)
            dq = (dya_g * ps_ref[:, cols]).astype(MXU)
            dpa_g = _dot_nt(dq, wpool_ref[g])
            dwp_ref[g] += _dot_tn(pa_g, dq)
            dpw = dpa_g / jnp.minimum(t_pos + 1.0, float(win))
            acc = jnp.concatenate([dpw, carry_pa[:, cols]], axis=0)
            carry_pa[:, cols] = dpw[:HALO_POOL, :]
            k = 1
            while k < win:
                acc = acc + _shift_up(acc, k)
                k *= 2
            dp_ref[:, cols] = (acc[:TM, :] - dpa_g).astype(dp_ref.dtype)

        uvp = pf(UV0, 2 * WA)
        uvg, dgelu = _gelu_parts(uvp)
        u = uvg[:, :WA]
        v = uvg[:, WA:]
        rv = lax.rsqrt(jnp.mean(v * v, axis=-1, keepdims=True) + EPS)
        vh = v * rv
        vn = (vh * gs_ref[...]).astype(MXU)
        mask = _tril_mask()
        for g in range(HEADS):
            cols = slice(g * CHUNK, (g + 1) * CHUNK)
            wt32 = jnp.where(mask, wsp_ref[g], 0.0)
            wt = wt32.astype(MXU)
            wt_t = wt32.T.astype(MXU)
            bcol = bsp_ref[:, g:g + 1]
            for c in range(TM // CHUNK):
                rows = slice(c * CHUNK, (c + 1) * CHUNK)
                vn_cg = vn[rows, cols]
                sv = _dot(wt, vn_cg) + bcol
                dyb_cg = dyb[rows, cols]
                du_s[rows, cols] = dyb_cg * sv
                dsv = dyb_cg * u[rows, cols]
                dbs_acc[g] += dsv
                dsv_b = dsv.astype(MXU)
                dws_ref[g] += _dot_nt(dsv_b, vn_cg)
                dvn_s[rows, cols] = _dot(wt_t, dsv_b)
        dvn = dvn_s[...]
        small_ref[1:2, :] += _colsum(dvn * vh)
        dvg = dvn * gs_ref[...]
        dv = rv * (dvg - vh * jnp.mean(dvg * vh, axis=-1, keepdims=True))
        dp_ref[:, UV0:UV0 + WA] = (du_s[...] * dgelu[:, :WA]).astype(dp_ref.dtype)
        dp_ref[:, UV0 + WA:UV0 + 2 * WA] = (dv * dgelu[:, WA:]).astype(dp_ref.dtype)

        cb = pf(CB0, WA)
        cc = pf(CC0, WA)
        cx = pf(CX0, WA)
        z = cc * cx
        dp_ref[:, CB0:CB0 + WA] = (dyc * pz_ref[:, WA:2 * WA].astype(F32)).astype(dp_ref.dtype)
        dcz = dyc * cb
        extz = jnp.concatenate([dcz, carry_cz[...]], axis=0)
        carry_cz[...] = dcz[:HALO_CONV, :]
        up1 = _shift_up(extz, 1)[:TM, :]
        up2 = _shift_up(extz, 2)[:TM, :]
        dz = cc_ref[2:3, :] * dcz + cc_ref[1:2, :] * up1 + cc_ref[0:1, :] * up2
        small_ref[2:3, :] += _colsum(z * up2)
        small_ref[3:4, :] += _colsum(z * up1)
        small_ref[4:5, :] += _colsum(z * dcz)
        dp_ref[:, CC0:CC0 + WA] = (dz * cx).astype(dp_ref.dtype)
        dp_ref[:, CX0:CX0 + WA] = (dz * cc).astype(dp_ref.dtype)

        @pl.when(step == nt - 1)
        def _():
            ones = jnp.ones((8, CHUNK), F32)
            for g in range(HEADS):
                dws_ref[g] = jnp.where(mask, dws_ref[g], 0.0)
                row = lax.dot_general(ones, dbs_acc[g], (((1,), (1,)), ((), ())), preferred_element_type=F32,
                                      precision=lax.Precision.HIGHEST)
                dbs_ref[g:g + 1, :] = row[0:1, :]

    row = lambda n: pl.BlockSpec((TM, n), lambda i: (nt - 1 - i, 0))
    br_spec = _const((N_DEV, None, WA, NB_BR), (0, layer, 0, 0))
    acc_spec = lambda shape: pl.BlockSpec(shape, lambda i: (0,) * len(shape))
    return pl.pallas_call(
        body, name=name, grid=(nt,),
        in_specs=[row(D), row(NCOL), row(3 * WA), row(2 * WA), row(3 * D),
                  _const((HEADS, CHUNK, CHUNK), (0, 0, 0)), _const((1, WA), (0, 0)), _const((1, WA), (0, 0)),
                  _const((HEADS, CHUNK, CHUNK), (0, 0, 0)), _const((CHUNK, HEADS), (0, 0)), _const((3, WA), (0, 0)),
                  br_spec, br_spec, br_spec, _const((N_DEV, None, ROWS_O, D), (0, layer, 0, 0))],
        out_specs=[row(NCOL), row(3 * D), acc_spec((HEADS, CHUNK, CHUNK)), acc_spec((HEADS, CHUNK, CHUNK)),
                   acc_spec((8, WA)), acc_spec((8, CHUNK))],
        out_shape=[_sds((s, NCOL), MXU), _sds((s, 3 * D), MXU), _sds((HEADS, CHUNK, CHUNK), F32),
                   _sds((HEADS, CHUNK, CHUNK), F32), _sds((8, WA), F32), _sds((8, CHUNK), F32)],
        scratch_shapes=[pltpu.VMEM((HALO_POOL, WA), F32), pltpu.VMEM((HALO_CONV, WA), F32),
                        pltpu.VMEM((HEADS, CHUNK, CHUNK), F32), pltpu.VMEM((TM, WA), F32), pltpu.VMEM((TM, WA), F32)],
        compiler_params=_params(),
    )(dxmid, p, yabc, pacz, babc, wpool, pscale, gsgu, wsp, bsp_t, convc, wa_all, wb_all, wc_all, wo_all)


def _wgrad(a, b, a_spec, b_spec, n_out, acc_shape, out_shape, out_spec, store, name):
    def body(a_ref, b_ref, o_ref, acc_ref):
        k = pl.program_id(1)

        @pl.when(k == 0)
        def _():
            acc_ref[...] = jnp.zeros_like(acc_ref)

        acc_ref[...] += _dot_tn(a_ref[...], b_ref[...].astype(MXU))

        @pl.when(k == pl.num_programs(1) - 1)
        def _():
            store(o_ref, acc_ref)

    s = a.shape[-2]
    ts = min(TS_WGRAD, s)
    return pl.pallas_call(
        body, name=name, grid=(n_out, s // ts),
        in_specs=[a_spec(ts), b_spec(ts)], out_specs=out_spec, out_shape=_sds(out_shape, WIRE),
        scratch_shapes=[pltpu.VMEM(acc_shape, F32)],
        compiler_params=_params(2),
    )(a, b)


def _store_plain(o_ref, acc_ref):
    o_ref[...] = acc_ref[...].astype(o_ref.dtype)


def _store_lane_blocks(o_ref, acc_ref):
    for d in range(N_DEV):
        o_ref[d] = acc_ref[:, d * NB_BR:(d + 1) * NB_BR].astype(o_ref.dtype)


def _wgrad_in(h, dp, name):
    return _wgrad(h, dp, lambda ts: pl.BlockSpec((ts, D), lambda j, k: (k, 0)),
                  lambda ts: pl.BlockSpec((ts, NB_IN), lambda j, k: (k, j)), N_DEV, (D, NB_IN),
                  (N_DEV, D, NB_IN), pl.BlockSpec((None, D, NB_IN), lambda j, k: (j, 0, 0)), _store_plain, name)


def _wgrad_up(h, du, name):
    return _wgrad(h, du, lambda ts: pl.BlockSpec((ts, D), lambda j, k: (k, 0)),
                  lambda ts: pl.BlockSpec((None, ts, NB_UP), lambda j, k: (j, k, 0)), N_DEV, (D, NB_UP),
                  (N_DEV, D, NB_UP), pl.BlockSpec((None, D, NB_UP), lambda j, k: (j, 0, 0)), _store_plain, name)


def _wgrad_down(act, dxo, name):
    return _wgrad(act, dxo, lambda ts: pl.BlockSpec((None, ts, NB_UP), lambda j, k: (j, k, 0)),
                  lambda ts: pl.BlockSpec((ts, D), lambda j, k: (k, 0)), N_DEV // 2, (NB_UP, D),
                  (DFF, D), pl.BlockSpec((NB_UP, D), lambda j, k: (j, 0)), _store_plain, name)


def _wgrad_o(merged, dxmid, name):
    return _wgrad(merged, dxmid, lambda ts: pl.BlockSpec((ts, D), lambda j, k: (k, 0)),
                  lambda ts: pl.BlockSpec((ts, D), lambda j, k: (k, 0)), 1, (D, D),
                  (D, D), pl.BlockSpec((D, D), lambda j, k: (0, 0)), _store_plain, name)


def _wgrad_branches(yabc, dbabc, name):
    return _wgrad(yabc, dbabc, lambda ts: pl.BlockSpec((ts, WA), lambda j, k: (k, j)),
                  lambda ts: pl.BlockSpec((ts, D), lambda j, k: (k, j)), 3, (WA, D),
                  (N_DEV, 3, WA, NB_BR), pl.BlockSpec((N_DEV, None, WA, NB_BR), lambda j, k: (0, j, 0, 0)),
                  _store_lane_blocks, name)


def _adamw_math(g, w, m, v):
    m = ADAM_B1 * m + (1.0 - ADAM_B1) * g
    v = ADAM_B2 * v + (1.0 - ADAM_B2) * (g * g)
    m_hat = m / (1.0 - ADAM_B1 ** ADAM_STEP)
    v_hat = v / (1.0 - ADAM_B2 ** ADAM_STEP)
    delta = -ADAM_LR * (m_hat / (jnp.sqrt(v_hat) + ADAM_EPS) + ADAM_WD * w)
    return delta, m, v


def _adamw_sum(parts, mid, w, m, v, layer, prev, tr, name):
    n_layers, r, c = w.shape

    def body(p_ref, w_ref, m_ref, v_ref, *rest):
        g_ref, d_ref, mo_ref, vo_ref = rest[-4:]
        g = p_ref[0].astype(F32)
        for k in range(1, N_DEV):
            g = g + p_ref[k].astype(F32)
        g_ref[...] = g
        d_ref[...], mo_ref[...], vo_ref[...] = _adamw_math(g, w_ref[...], m_ref[...], v_ref[...])

    blk = pl.BlockSpec((None, tr, c), lambda i: (layer, i, 0))
    extra = [] if prev is None else list(prev)
    return pl.pallas_call(
        body, name=name, grid=(r // tr,),
        in_specs=[pl.BlockSpec((N_DEV, None, tr, c), lambda i: (0, mid, i, 0)), blk, blk, blk]
        + [pl.BlockSpec(memory_space=pl.ANY)] * len(extra),
        out_specs=[blk] * 4, out_shape=[_sds((n_layers, r, c), F32)] * 4,
        input_output_aliases={4 + k: k for k in range(len(extra))},
        compiler_params=_params(),
    )(parts, w, m, v, *extra)


def _sum_parts(parts, name):
    _, r, c = parts.shape

    def body(p_ref, o_ref):
        g = p_ref[0]
        for k in range(1, N_DEV):
            g = g + p_ref[k]
        o_ref[...] = g

    return pl.pallas_call(body, name=name, out_shape=_sds((r, c), F32),
                          compiler_params=pltpu.CompilerParams(vmem_limit_bytes=VMEM_LIMIT))(parts)


def _adamw_small(g, w, m, v, name):
    def body(g_ref, w_ref, m_ref, v_ref, d_ref, mo_ref, vo_ref):
        d_ref[...], mo_ref[...], vo_ref[...] = _adamw_math(g_ref[...], w_ref[...], m_ref[...], v_ref[...])

    return pl.pallas_call(body, name=name, out_shape=[_sds(w.shape, F32)] * 3)(g, w, m, v)


HBM_SPEC = pl.BlockSpec(memory_space=pltpu.HBM)


def _position():
    return lax.axis_index("x"), lax.axis_index("y"), lax.axis_index("c")


def _all_gather(shards, name):
    n = len(shards)

    def body(*refs):
        ins, outs = refs[:n], refs[n:2 * n]
        send_sems, recv_sems, local_sems = refs[2 * n:]
        x, y, c = _position()
        me = 4 * x + 2 * y + c
        sibling = (x, y, 1 - c)
        chips = [(1 - x, y), (x, 1 - y), (1 - x, 1 - y)]

        def copy(t, k, block, to, src=None):
            dst = outs[t].at[block]
            return pltpu.make_async_remote_copy(
                src_ref=dst if src is None else src, dst_ref=dst, send_sem=send_sems.at[7 * t + k],
                recv_sem=recv_sems.at[7 * t + k], device_id=to, device_id_type=MESH)

        def index(chip, core):
            return 4 * chip[0] + 2 * chip[1] + core

        mine = [pltpu.make_async_copy(ins[t], outs[t].at[me], local_sems.at[t]) for t in range(n)]
        for cp in mine:
            cp.start()
        first = []
        for t in range(n):
            first.append(copy(t, 0, me, sibling, src=ins[t]))
            first += [copy(t, 1 + j, me, (*chip, c), src=ins[t]) for j, chip in enumerate(chips)]
        for cp in first:
            cp.start()
        passed = []
        for t in range(n):
            for j, chip in enumerate(chips):
                copy(t, 1 + j, index(chip, c), sibling).wait_recv()
                fwd = copy(t, 4 + j, index(chip, c), sibling)
                fwd.start()
                passed.append(fwd)
        for t in range(n):
            copy(t, 0, index((x, y), 1 - c), sibling).wait_recv()
            for j, chip in enumerate(chips):
                copy(t, 4 + j, index(chip, 1 - c), sibling).wait_recv()
        for cp in first + passed:
            cp.wait_send()
        for cp in mine:
            cp.wait()

    return pl.pallas_call(
        body, name=name,
        in_specs=[HBM_SPEC] * n, out_specs=[HBM_SPEC] * n,
        out_shape=[_sds((N_DEV,) + a.shape, a.dtype) for a in shards],
        scratch_shapes=[pltpu.SemaphoreType.DMA((7 * n,)), pltpu.SemaphoreType.DMA((7 * n,)),
                        pltpu.SemaphoreType.DMA((n,))],
    )(*shards)


def _exchange(parts, name):
    n = len(parts)
    flips = [(0, 0, 1), (1, 0, 0), (1, 0, 1), (0, 1, 0), (0, 1, 1), (1, 1, 0), (1, 1, 1)]

    def body(*refs):
        ins, outs = refs[:n], refs[n:2 * n]
        send_sems, recv_sems, local_sems = refs[2 * n:]
        x, y, c = _position()
        me = 4 * x + 2 * y + c
        mine = [pltpu.make_async_copy(ins[t].at[me], outs[t].at[me], local_sems.at[t]) for t in range(n)]
        for cp in mine:
            cp.start()
        copies = []
        for t in range(n):
            for k, (fx, fy, fc) in enumerate(flips):
                px, py, pc = (1 - x if fx else x), (1 - y if fy else y), (1 - c if fc else c)
                copies.append(pltpu.make_async_remote_copy(
                    src_ref=ins[t].at[4 * px + 2 * py + pc], dst_ref=outs[t].at[me], send_sem=send_sems.at[7 * t + k],
                    recv_sem=recv_sems.at[7 * t + k], device_id=(px, py, pc), device_id_type=MESH))
        for cp in copies:
            cp.start()
        for cp in copies:
            cp.wait()
        for cp in mine:
            cp.wait()

    return pl.pallas_call(
        body, name=name,
        in_specs=[HBM_SPEC] * n, out_specs=[HBM_SPEC] * n,
        out_shape=[_sds(a.shape, a.dtype) for a in parts],
        scratch_shapes=[pltpu.SemaphoreType.DMA((7 * n,)), pltpu.SemaphoreType.DMA((7 * n,)),
                        pltpu.SemaphoreType.DMA((n,))],
    )(*parts)


def _rows128(a):
    return a.reshape(-1, 128)


def kernel(x, g_mix, w_in, w_pool, pool_scale, g_sgu, w_spatial, b_spatial, conv_c, w_branch_a, w_branch_b, w_branch_c, w_o, g_ffn, w_up, conv_ffn, conv_ffn_b, w_down, g_final, loss_target, m_g_mix, m_w_in, m_w_pool, m_pool_scale, m_g_sgu, m_w_spatial, m_b_spatial, m_conv_c, m_w_branch_a, m_w_branch_b, m_w_branch_c, m_w_o, m_g_ffn, m_w_up, m_conv_ffn, m_conv_ffn_b, m_w_down, m_g_final, v_g_mix, v_w_in, v_w_pool, v_pool_scale, v_g_sgu, v_w_spatial, v_b_spatial, v_conv_c, v_w_branch_a, v_w_branch_b, v_w_branch_c, v_w_o, v_g_ffn, v_w_up, v_conv_ffn, v_conv_ffn_b, v_w_down, v_g_final):
    s = x.shape[1]
    n_layers = g_mix.shape[0]
    x0 = x.reshape(s, D)
    target = loss_target.reshape(s, D)
    me = 4 * lax.axis_index("x") + 2 * lax.axis_index("y") + lax.axis_index("c")

    (win_all, wa_all, wb_all, wc_all, wo_all, wup_all, wd_all, convc_all, convf_all) = _all_gather(
        [w_in.astype(MXU), w_branch_a.astype(MXU), w_branch_b.astype(MXU), w_branch_c.astype(MXU), w_o.astype(MXU),
         w_up.astype(MXU), w_down.astype(MXU), conv_c, conv_ffn], "gather_weights")
    convc_full = jnp.transpose(convc_all, (1, 2, 0, 3)).reshape(n_layers, 3, WA)
    wpool_b = w_pool.astype(MXU)
    bsp_t = jnp.swapaxes(b_spatial, 1, 2)
    convb_blk = conv_ffn_b.reshape(n_layers, N_DEV, NB_UP)

    saved = []
    xl = x0
    for l in range(n_layers):
        p, h = _rms_proj(xl, g_mix[l:l + 1], win_all, l, False, f"in_proj_{l}")
        xmid, yabc, pacz, babc, merged = _mixer_fwd(
            xl, p, wpool_b[l], pool_scale[l:l + 1], g_sgu[l:l + 1], w_spatial[l], bsp_t[l], convc_full[l],
            wa_all, wb_all, wc_all, wo_all, l, f"mixer_fwd_{l}")
        upre, h2 = _rms_proj(xmid, g_ffn[l:l + 1], wup_all, l, True, f"up_proj_{l}")
        xout, act = _ffn_fwd(xmid, upre, convf_all, convb_blk[l], wd_all, l, f"ffn_fwd_{l}")
        saved.append((xl, p, h, xmid, yabc, pacz, babc, merged, upre, h2, act))
        xl = xout

    dx, dg_final, loss_local = _loss_head(xl, g_final.reshape(1, D), target, "loss_head")

    big_parts = [None] * n_layers
    small_pieces = [None] * n_layers
    for l in reversed(range(n_layers)):
        xin, p, h, xmid, yabc, pacz, babc, merged, upre, h2, act = saved[l]
        dupre, dconvf = _ffn_bwd(dx, upre, convf_all, convb_blk[l], wd_all, l, f"ffn_bwd_{l}")
        g_wdown = _wgrad_down(act, dx, f"wgrad_down_{l}")
        g_wup = _wgrad_up(h2, dupre, f"wgrad_up_{l}")
        dxmid, dg_ffn = _proj_bwd(dupre, True, wup_all, l, xmid, g_ffn[l:l + 1], dx, f"up_proj_bwd_{l}")
        dp, dbabc, dwp, dws, small, dbs = _mixer_bwd(
            dxmid, p, yabc, pacz, babc, wpool_b[l], pool_scale[l:l + 1], g_sgu[l:l + 1], w_spatial[l], bsp_t[l],
            convc_full[l], wa_all, wb_all, wc_all, wo_all, l, f"mixer_bwd_{l}")
        g_wo = _wgrad_o(merged, dxmid, f"wgrad_o_{l}")
        g_br = _wgrad_branches(yabc, dbabc, f"wgrad_branches_{l}")
        g_win = _wgrad_in(h, dp, f"wgrad_in_{l}")
        dx, dg_mix = _proj_bwd(dp, False, win_all, l, xin, g_mix[l:l + 1], dxmid, f"in_proj_bwd_{l}")
        big_parts[l] = [g_win, g_br, g_wo.reshape(N_DEV, ROWS_O, D), g_wup, g_wdown.reshape(N_DEV, ROWS_DN, D)]
        small_pieces[l] = [dg_mix[0:1], dwp, small[0:1], small[1:2], dws, dbs[0:HEADS], small[2:5], dg_ffn[0:1],
                           dconvf[:, 0:3, :], dconvf[:, 3, :]]
    grad_x = dx.reshape(1, s, D)

    received = [_exchange(big_parts[l], f"exchange_grads_{l}") for l in range(n_layers)]

    def update_big(idx, mid, w, m, v, tr, tag):
        outs = None
        for l in range(n_layers):
            parts = received[l][idx]
            if parts.ndim == 3:
                parts = parts.reshape(N_DEV, 1, *parts.shape[1:])
            outs = _adamw_sum(parts, mid, w, m, v, l, outs, tr, f"adamw_{tag}_{l}")
        return outs

    up_in = update_big(0, 0, w_in, m_w_in, v_w_in, 256, "w_in")
    up_a = update_big(1, 0, w_branch_a, m_w_branch_a, v_w_branch_a, WA, "w_branch_a")
    up_b = update_big(1, 1, w_branch_b, m_w_branch_b, v_w_branch_b, WA, "w_branch_b")
    up_c = update_big(1, 2, w_branch_c, m_w_branch_c, v_w_branch_c, WA, "w_branch_c")
    up_o = update_big(2, 0, w_o, m_w_o, v_w_o, ROWS_O, "w_o")
    up_up = update_big(3, 0, w_up, m_w_up, v_w_up, 256, "w_up")
    up_down = update_big(4, 0, w_down, m_w_down, v_w_down, ROWS_DN, "w_down")

    pieces = [pc for l in range(n_layers) for pc in small_pieces[l]] + [dg_final[0:1]]
    packed = jnp.concatenate([_rows128(pc) for pc in pieces], axis=0)
    (gathered_small,) = _all_gather([packed], "gather_small_grads")
    summed = _sum_parts(gathered_small, "sum_small_grads")
    unpacked = []
    row = 0
    for pc in pieces:
        n_rows = pc.size // 128
        unpacked.append(summed[row:row + n_rows].reshape(pc.shape))
        row += n_rows
    per_layer = len(small_pieces[0])
    stack = lambda k: jnp.stack([unpacked[l * per_layer + k] for l in range(n_layers)], axis=0)
    grad_g_mix = stack(0).reshape(n_layers, D)
    grad_w_pool = stack(1)
    grad_pool_scale = stack(2).reshape(n_layers, WA)
    grad_g_sgu = stack(3).reshape(n_layers, WA)
    grad_w_spatial = stack(4)
    grad_b_spatial = stack(5)
    grad_conv_c = lax.dynamic_slice_in_dim(stack(6), me * (WA // N_DEV), WA // N_DEV, axis=2)
    grad_g_ffn = stack(7).reshape(n_layers, D)
    grad_conv_ffn = lax.dynamic_index_in_dim(stack(8), me, axis=1, keepdims=False)
    grad_conv_ffn_b = stack(9).reshape(n_layers, 2 * DFF)
    grad_g_final = unpacked[-1].reshape(D)

    def update_small(g, w, m, v, tag):
        shape2 = (-1, w.shape[-1])
        outs = _adamw_small(g.reshape(shape2), w.reshape(shape2), m.reshape(shape2), v.reshape(shape2), f"adamw_{tag}")
        return [g] + [o.reshape(w.shape) for o in outs]

    up = {
        "g_mix": update_small(grad_g_mix, g_mix, m_g_mix, v_g_mix, "g_mix"),
        "w_in": up_in,
        "w_pool": update_small(grad_w_pool, w_pool, m_w_pool, v_w_pool, "w_pool"),
        "pool_scale": update_small(grad_pool_scale, pool_scale, m_pool_scale, v_pool_scale, "pool_scale"),
        "g_sgu": update_small(grad_g_sgu, g_sgu, m_g_sgu, v_g_sgu, "g_sgu"),
        "w_spatial": update_small(grad_w_spatial, w_spatial, m_w_spatial, v_w_spatial, "w_spatial"),
        "b_spatial": update_small(grad_b_spatial, b_spatial, m_b_spatial, v_b_spatial, "b_spatial"),
        "conv_c": update_small(grad_conv_c, conv_c, m_conv_c, v_conv_c, "conv_c"),
        "w_branch_a": up_a,
        "w_branch_b": up_b,
        "w_branch_c": up_c,
        "w_o": up_o,
        "g_ffn": update_small(grad_g_ffn, g_ffn, m_g_ffn, v_g_ffn, "g_ffn"),
        "w_up": up_up,
        "conv_ffn": update_small(grad_conv_ffn, conv_ffn, m_conv_ffn, v_conv_ffn, "conv_ffn"),
        "conv_ffn_b": update_small(grad_conv_ffn_b, conv_ffn_b, m_conv_ffn_b, v_conv_ffn_b, "conv_ffn_b"),
        "w_down": up_down,
        "g_final": update_small(grad_g_final, g_final, m_g_final, v_g_final, "g_final"),
    }
    loss = lax.psum(loss_local[0, 0], AXES)
    order = list(up)
    return (loss, grad_x, *[up[k][0] for k in order], *[up[k][1] for k in order], *[up[k][2] for k in order],
            *[up[k][3] for k in order])
```

```python
import functools

import jax
import jax.numpy as jnp
from jax import lax
from jax.experimental import pallas as pl
from jax.experimental.pallas import tpu as pltpu

F32 = jnp.float32
BF16 = jnp.bfloat16
MXU = BF16
ACT = BF16
WIRE = BF16

N_DEV = 8
D = 1024
WA = 512
NCOL = 6144
DFF = 2816
NB_IN = NCOL // N_DEV
NB_UP = 2 * DFF // N_DEV
NB_BR = D // N_DEV
ROWS_O = D // N_DEV
ROWS_DN = DFF // N_DEV
CHUNK = 128
HEADS = 4
POOL_WINDOWS = (2, 4, 8, 16)
EPS = 1e-6
A0, UV0, CB0, CC0, CX0, GA0, GB0, GC0 = 0, 512, 1536, 2048, 2560, 3072, 4096, 5120

ADAM_LR = 0.001
ADAM_B1 = 0.9
ADAM_B2 = 0.999
ADAM_EPS = 1e-08
ADAM_WD = 0.01
ADAM_STEP = 10

TM = 256
TS_WGRAD = 1024
HALO_POOL = 16
HALO_CONV = 8
VMEM_LIMIT = 56 * 1024 * 1024
MESH = pl.DeviceIdType.MESH
AXES = ("x", "y", "c")


def _sds(shape, dtype):
    return jax.ShapeDtypeStruct(tuple(shape), dtype)


def _params(n_grid=1):
    return pltpu.CompilerParams(dimension_semantics=("arbitrary",) * n_grid, vmem_limit_bytes=VMEM_LIMIT)


def _const(block, index):
    return pl.BlockSpec(block, lambda *_: index, pipeline_mode=pl.Buffered(1))


def _dot(a, b):
    return jnp.dot(a, b, preferred_element_type=F32)


def _dot_nt(a, b):
    return lax.dot_general(a, b, (((1,), (1,)), ((), ())), preferred_element_type=F32)


def _dot_tn(a, b):
    return lax.dot_general(a, b, (((0,), (0,)), ((), ())), preferred_element_type=F32)


def _sigmoid(v):
    return 1.0 / (1.0 + jnp.exp(-v))


def _shift_down(v, k):
    return pltpu.roll(v, k, axis=0)


def _shift_up(v, k):
    return pltpu.roll(v, v.shape[0] - k, axis=0)


def _colsum(v):
    return jnp.sum(v, axis=0, keepdims=True)


def _lane_cat(ref):
    return jnp.concatenate([ref[d] for d in range(N_DEV)], axis=1)


class _Comm:
    def __init__(self, operands, out_shapes, n_sems, start, finish, aliases=None):
        self.operands = list(operands)
        self.out_shapes = list(out_shapes)
        self.n_sems = n_sems
        self.start = start
        self.finish = finish
        self.aliases = dict(aliases or {})


def _call(body, comm, is_first, is_last, operands, *, in_specs, out_specs, out_shape, scratch_shapes=(), **kw):
    n_in, n_out, n_scr = len(in_specs), len(out_specs), len(scratch_shapes)
    if comm is None:
        res = pl.pallas_call(body, in_specs=in_specs, out_specs=out_specs, out_shape=out_shape,
                             scratch_shapes=list(scratch_shapes), **kw)(*operands)
        return res, []
    nci, nco = len(comm.operands), len(comm.out_shapes)

    def carrier(*refs):
        ins, refs = refs[:n_in], refs[n_in:]
        cins, refs = refs[:nci], refs[nci:]
        outs, refs = refs[:n_out], refs[n_out:]
        couts, refs = refs[:nco], refs[nco:]
        scr, (send, recv) = refs[:n_scr], refs[n_scr:]

        @pl.when(is_first())
        def _():
            comm.start(cins, couts, send, recv)

        body(*ins, *outs, *scr)

        @pl.when(is_last())
        def _():
            comm.finish(cins, couts, send, recv)

    res = pl.pallas_call(
        carrier, in_specs=list(in_specs) + [HBM_SPEC] * nci, out_specs=list(out_specs) + [HBM_SPEC] * nco,
        out_shape=list(out_shape) + comm.out_shapes,
        scratch_shapes=list(scratch_shapes) + [pltpu.SemaphoreType.DMA((comm.n_sems,))] * 2,
        input_output_aliases={n_in + i: n_out + o for i, o in comm.aliases.items()}, **kw,
    )(*operands, *comm.operands)
    return res[:n_out], res[n_out:]


def _rms_proj(x, g, w_all, blocked_out, name, comm=None):
    s = x.shape[0]
    nb = w_all.shape[-1]
    nt = s // TM

    def body(x_ref, g_ref, w_ref, p_ref, h_ref):
        xf = x_ref[...]
        r = lax.rsqrt(jnp.mean(xf * xf, axis=-1, keepdims=True) + EPS)
        h = (xf * r * g_ref[...]).astype(MXU)
        h_ref[...] = h
        for j in range(N_DEV):
            pj = _dot(h, w_ref[j]).astype(p_ref.dtype)
            if blocked_out:
                p_ref[j] = pj
            else:
                p_ref[:, j * nb:(j + 1) * nb] = pj

    if blocked_out:
        p_shape, p_spec = (N_DEV, s, nb), pl.BlockSpec((N_DEV, TM, nb), lambda i: (0, i, 0))
    else:
        p_shape, p_spec = (s, N_DEV * nb), pl.BlockSpec((TM, N_DEV * nb), lambda i: (i, 0))
    return _call(
        body, comm, lambda: pl.program_id(0) == 0, lambda: pl.program_id(0) == nt - 1, (x, g, w_all),
        name=name, grid=(nt,),
        in_specs=[pl.BlockSpec((TM, D), lambda i: (i, 0)), _const((1, D), (0, 0)),
                  _const((N_DEV, D, nb), (0, 0, 0))],
        out_specs=[p_spec, pl.BlockSpec((TM, D), lambda i: (i, 0))],
        out_shape=[_sds(p_shape, ACT), _sds((s, D), MXU)],
        compiler_params=_params(),
    )


def _tril_mask():
    r = lax.broadcasted_iota(jnp.int32, (CHUNK, CHUNK), 0)
    c = lax.broadcasted_iota(jnp.int32, (CHUNK, CHUNK), 1)
    return r >= c


def _gelu_parts(v):
    c0 = 0.7978845608028654
    th = jnp.tanh(c0 * (v + 0.044715 * (v * v * v)))
    cdf = 0.5 * (1.0 + th)
    dgelu = cdf + v * (0.5 * c0) * (1.0 - th * th) * (1.0 + 3.0 * 0.044715 * (v * v))
    return v * cdf, dgelu


def _mixer_fwd(x, p, wpool, pscale, gsgu, wsp, bsp_t, convc, wa_all, wb_all, wc_all, wo_all, name):
    s = x.shape[0]
    nt = s // TM

    def body(x_ref, p_ref, wpool_ref, ps_ref, gs_ref, wsp_ref, bsp_ref, cc_ref, wa_ref, wb_ref, wc_ref, wo_ref,
             xmid_ref, y_ref, pz_ref, b_ref, m_ref, carry_a, carry_z):
        i = pl.program_id(0)

        @pl.when(i == 0)
        def _():
            carry_a[...] = jnp.zeros_like(carry_a)
            carry_z[...] = jnp.zeros_like(carry_z)

        def pf(lo, n):
            return p_ref[:, lo:lo + n].astype(F32)

        a = pf(A0, WA)
        ext = jnp.concatenate([carry_a[...], a], axis=0)
        carry_a[...] = a[TM - HALO_POOL:, :]
        t_pos = (i * TM + lax.broadcasted_iota(jnp.int32, (TM, 1), 0)).astype(F32)
        for g, win in enumerate(POOL_WINDOWS):
            cols = slice(g * CHUNK, (g + 1) * CHUNK)
            acc = ext[:, cols]
            k = 1
            while k < win:
                acc = acc + _shift_down(acc, k)
                k *= 2
            cnt = jnp.minimum(t_pos + 1.0, float(win))
            pa_g = (acc[HALO_POOL:, :] / cnt - a[:, cols]).astype(MXU)
            pz_ref[:, cols] = pa_g
            y_ref[:, cols] = (_dot(pa_g, wpool_ref[g]) * ps_ref[:, cols]).astype(ACT)

        uvg, _ = _gelu_parts(pf(UV0, 2 * WA))
        u = uvg[:, :WA]
        v = uvg[:, WA:]
        rv = lax.rsqrt(jnp.mean(v * v, axis=-1, keepdims=True) + EPS)
        vn = (v * rv * gs_ref[...]).astype(MXU)
        mask = _tril_mask()
        for g in range(HEADS):
            cols = slice(g * CHUNK, (g + 1) * CHUNK)
            wt = jnp.where(mask, wsp_ref[g], 0.0).astype(MXU)
            bcol = bsp_ref[:, g:g + 1]
            for c in range(TM // CHUNK):
                rows = slice(c * CHUNK, (c + 1) * CHUNK)
                sv = _dot(wt, vn[rows, cols]) + bcol
                y_ref[rows, WA + g * CHUNK:WA + (g + 1) * CHUNK] = (u[rows, cols] * sv).astype(ACT)

        z = pf(CC0, WA) * pf(CX0, WA)
        extz = jnp.concatenate([carry_z[...], z], axis=0)
        carry_z[...] = z[TM - HALO_CONV:, :]
        cz = (cc_ref[0:1, :] * _shift_down(extz, 2)[HALO_CONV:, :]
              + cc_ref[1:2, :] * _shift_down(extz, 1)[HALO_CONV:, :] + cc_ref[2:3, :] * z)
        pz_ref[:, WA:2 * WA] = cz.astype(ACT)
        y_ref[:, 2 * WA:3 * WA] = (pf(CB0, WA) * cz).astype(ACT)

        merged = jnp.zeros((TM, D), F32)
        for k, (w_ref, glo) in enumerate(((wa_ref, GA0), (wb_ref, GB0), (wc_ref, GC0))):
            br = _dot(y_ref[:, k * WA:(k + 1) * WA], _lane_cat(w_ref))
            b_ref[:, k * D:(k + 1) * D] = br.astype(ACT)
            merged = merged + _sigmoid(pf(glo, D)) * br
        mb = merged.astype(MXU)
        m_ref[...] = mb
        xmid_ref[...] = x_ref[...] + _dot(mb, wo_ref[...].reshape(D, D))

    row = lambda n: pl.BlockSpec((TM, n), lambda i: (i, 0))
    br_spec = _const((N_DEV, WA, NB_BR), (0, 0, 0))
    return pl.pallas_call(
        body, name=name, grid=(nt,),
        in_specs=[row(D), row(NCOL), _const((HEADS, CHUNK, CHUNK), (0, 0, 0)), _const((1, WA), (0, 0)),
                  _const((1, WA), (0, 0)), _const((HEADS, CHUNK, CHUNK), (0, 0, 0)), _const((CHUNK, HEADS), (0, 0)),
                  _const((3, WA), (0, 0)), br_spec, br_spec, br_spec,
                  _const((N_DEV, ROWS_O, D), (0, 0, 0))],
        out_specs=[row(D), row(3 * WA), row(2 * WA), row(3 * D), row(D)],
        out_shape=[_sds((s, D), F32), _sds((s, 3 * WA), ACT), _sds((s, 2 * WA), ACT), _sds((s, 3 * D), ACT),
                   _sds((s, D), MXU)],
        scratch_shapes=[pltpu.VMEM((HALO_POOL, WA), F32), pltpu.VMEM((HALO_CONV, WA), F32)],
        compiler_params=_params(),
    )(x, p, wpool, pscale, gsgu, wsp, bsp_t, convc, wa_all, wb_all, wc_all, wo_all)


def _conv_up(ext, cur, w_ref, j, b_row):
    return (w_ref[j, 0:1, :] * _shift_down(ext, 2)[HALO_CONV:, :] + w_ref[j, 1:2, :] * _shift_down(ext, 1)[HALO_CONV:, :]
            + w_ref[j, 2:3, :] * cur + b_row)


def _ffn_fwd(xmid, upre, convf_all, convb, wd_all, name):
    s = xmid.shape[0]
    nt = s // TM
    half = N_DEV // 2

    def body(x_ref, u_ref, cw_ref, cb_ref, wd_ref, xo_ref, act_ref, carry):
        i = pl.program_id(0)

        @pl.when(i == 0)
        def _():
            carry[...] = jnp.zeros_like(carry)

        def conv(j):
            cur = u_ref[j].astype(F32)
            ext = jnp.concatenate([carry[j], cur], axis=0)
            carry[j] = cur[TM - HALO_CONV:, :]
            return _conv_up(ext, cur, cw_ref, j, cb_ref[j:j + 1, :])

        acc = x_ref[...]
        for j in range(half):
            gate = conv(j)
            val = conv(j + half)
            act = (gate * _sigmoid(gate) * val).astype(MXU)
            act_ref[j] = act
            wd = jnp.concatenate([wd_ref[2 * j], wd_ref[2 * j + 1]], axis=0)
            acc = acc + _dot(act, wd)
        xo_ref[...] = acc

    return pl.pallas_call(
        body, name=name, grid=(nt,),
        in_specs=[pl.BlockSpec((TM, D), lambda i: (i, 0)), pl.BlockSpec((N_DEV, TM, NB_UP), lambda i: (0, i, 0)),
                  _const((N_DEV, 3, NB_UP), (0, 0, 0)), _const((N_DEV, NB_UP), (0, 0)),
                  _const((N_DEV, ROWS_DN, D), (0, 0, 0))],
        out_specs=[pl.BlockSpec((TM, D), lambda i: (i, 0)), pl.BlockSpec((half, TM, NB_UP), lambda i: (0, i, 0))],
        out_shape=[_sds((s, D), F32), _sds((half, s, NB_UP), MXU)],
        scratch_shapes=[pltpu.VMEM((N_DEV, HALO_CONV, NB_UP), F32)],
        compiler_params=_params(),
    )(xmid, upre, convf_all, convb, wd_all)


def _loss_head(x, g, target, name):
    s = x.shape[0]
    nt = s // TM

    def body(x_ref, g_ref, t_ref, dx_ref, dg_ref, loss_ref):
        i = pl.program_id(0)

        @pl.when(i == 0)
        def _():
            dg_ref[...] = jnp.zeros_like(dg_ref)
            loss_ref[...] = jnp.zeros_like(loss_ref)

        xf = x_ref[...]
        r = lax.rsqrt(jnp.mean(xf * xf, axis=-1, keepdims=True) + EPS)
        xn = xf * r
        err = xn * g_ref[...] - t_ref[...]
        loss_ref[...] += 0.5 * jnp.sum(jnp.mean(err * err, axis=-1, keepdims=True), axis=0, keepdims=True)
        dy = err * (1.0 / D)
        dg_ref[0:1, :] += _colsum(dy * xn)
        dyg = dy * g_ref[...]
        dx_ref[...] = r * (dyg - xn * jnp.mean(dyg * xn, axis=-1, keepdims=True))

    return pl.pallas_call(
        body, name=name, grid=(nt,),
        in_specs=[pl.BlockSpec((TM, D), lambda i: (i, 0)), _const((1, D), (0, 0)), pl.BlockSpec((TM, D), lambda i: (i, 0))],
        out_specs=[pl.BlockSpec((TM, D), lambda i: (i, 0)), pl.BlockSpec((8, D), lambda i: (0, 0)),
                   pl.BlockSpec((1, 1), lambda i: (0, 0))],
        out_shape=[_sds((s, D), F32), _sds((8, D), F32), _sds((1, 1), F32)],
        compiler_params=_params(),
    )(x, g, target)


def _ffn_bwd(dxo, upre, convf_all, convb, wd_all, name):
    s = dxo.shape[0]
    nt = s // TM
    half = N_DEV // 2
    hb = TM // HALO_CONV

    def body(dx_ref, u_ref, halo_ref, cw_ref, cb_ref, wd_ref, du_ref, dc_ref, carry):
        step = pl.program_id(0)
        tile = nt - 1 - step

        @pl.when(step == 0)
        def _():
            carry[...] = jnp.zeros_like(carry)
            dc_ref[...] = jnp.zeros_like(dc_ref)

        dxb = dx_ref[...].astype(MXU)
        first = (tile > 0).astype(F32)

        def conv(j):
            cur = u_ref[j].astype(F32)
            ext = jnp.concatenate([halo_ref[j].astype(F32) * first, cur], axis=0)
            return cur, _conv_up(ext, cur, cw_ref, j, cb_ref[j:j + 1, :])

        def adjoint(j, cur, d_up):
            ext = jnp.concatenate([d_up, carry[j]], axis=0)
            carry[j] = d_up[:HALO_CONV, :]
            up1 = _shift_up(ext, 1)[:TM, :]
            up2 = _shift_up(ext, 2)[:TM, :]
            du_ref[j] = (cw_ref[j, 2:3, :] * d_up + cw_ref[j, 1:2, :] * up1 + cw_ref[j, 0:1, :] * up2).astype(du_ref.dtype)
            dc_ref[j, 0:1, :] += _colsum(cur * up2)
            dc_ref[j, 1:2, :] += _colsum(cur * up1)
            dc_ref[j, 2:3, :] += _colsum(cur * d_up)
            dc_ref[j, 3:4, :] += _colsum(d_up)

        for j in range(half):
            ug, gate = conv(j)
            uv, val = conv(j + half)
            sg = _sigmoid(gate)
            wd = jnp.concatenate([wd_ref[2 * j], wd_ref[2 * j + 1]], axis=0)
            dact = _dot_nt(dxb, wd)
            adjoint(j, ug, dact * val * sg * (1.0 + gate * (1.0 - sg)))
            adjoint(j + half, uv, dact * gate * sg)

    return pl.pallas_call(
        body, name=name, grid=(nt,),
        in_specs=[pl.BlockSpec((TM, D), lambda i: (nt - 1 - i, 0)),
                  pl.BlockSpec((N_DEV, TM, NB_UP), lambda i: (0, nt - 1 - i, 0)),
                  pl.BlockSpec((N_DEV, HALO_CONV, NB_UP), lambda i: (0, jnp.maximum((nt - 1 - i) * hb - 1, 0), 0)),
                  _const((N_DEV, 3, NB_UP), (0, 0, 0)), _const((N_DEV, NB_UP), (0, 0)),
                  _const((N_DEV, ROWS_DN, D), (0, 0, 0))],
        out_specs=[pl.BlockSpec((N_DEV, TM, NB_UP), lambda i: (0, nt - 1 - i, 0)),
                   pl.BlockSpec((N_DEV, 8, NB_UP), lambda i: (0, 0, 0))],
        out_shape=[_sds((N_DEV, s, NB_UP), MXU), _sds((N_DEV, 8, NB_UP), F32)],
        scratch_shapes=[pltpu.VMEM((N_DEV, HALO_CONV, NB_UP), F32)],
        compiler_params=_params(),
    )(dxo, upre, upre, convf_all, convb, wd_all)


def _proj_bwd(dy, blocked_dy, w_all, x, g, dres, name, comm=None):
    s = x.shape[0]
    nb = w_all.shape[-1]
    nt = s // TM

    def body(dy_ref, w_ref, x_ref, g_ref, dres_ref, dx_ref, dg_ref):
        i = pl.program_id(0)

        @pl.when(i == 0)
        def _():
            dg_ref[...] = jnp.zeros_like(dg_ref)

        dh = jnp.zeros((TM, D), F32)
        for j in range(N_DEV):
            dyj = dy_ref[j] if blocked_dy else dy_ref[:, j * nb:(j + 1) * nb]
            dh = dh + _dot_nt(dyj, w_ref[j])
        xf = x_ref[...]
        r = lax.rsqrt(jnp.mean(xf * xf, axis=-1, keepdims=True) + EPS)
        xn = xf * r
        dg_ref[0:1, :] += _colsum(dh * xn)
        dhg = dh * g_ref[...]
        dx_ref[...] = dres_ref[...] + r * (dhg - xn * jnp.mean(dhg * xn, axis=-1, keepdims=True))

    if blocked_dy:
        dy_spec = pl.BlockSpec((N_DEV, TM, nb), lambda i: (0, i, 0))
    else:
        dy_spec = pl.BlockSpec((TM, N_DEV * nb), lambda i: (i, 0))
    row = pl.BlockSpec((TM, D), lambda i: (i, 0))
    return _call(
        body, comm, lambda: pl.program_id(0) == 0, lambda: pl.program_id(0) == nt - 1, (dy, w_all, x, g, dres),
        name=name, grid=(nt,),
        in_specs=[dy_spec, _const((N_DEV, D, nb), (0, 0, 0)), row, _const((1, D), (0, 0)), row],
        out_specs=[row, pl.BlockSpec((8, D), lambda i: (0, 0))],
        out_shape=[_sds((s, D), F32), _sds((8, D), F32)],
        compiler_params=_params(),
    )


def _mixer_bwd(dxmid, p, yabc, pacz, babc, wpool, pscale, gsgu, wsp, bsp_t, convc, wa_all, wb_all, wc_all, wo_all,
               name):
    s = dxmid.shape[0]
    nt = s // TM

    def body(dx_ref, p_ref, y_ref, pz_ref, b_ref, wpool_ref, ps_ref, gs_ref, wsp_ref, bsp_ref, cc_ref,
             wa_ref, wb_ref, wc_ref, wo_ref,
             dp_ref, db_ref, dwp_ref, dws_ref, small_ref, dbs_ref,
             carry_pa, carry_cz, dbs_acc, du_s, dvn_s):
        step = pl.program_id(0)
        tile = nt - 1 - step

        @pl.when(step == 0)
        def _():
            for ref in (carry_pa, carry_cz, dbs_acc, dwp_ref, dws_ref, small_ref, dbs_ref):
                ref[...] = jnp.zeros_like(ref)

        def pf(lo, n):
            return p_ref[:, lo:lo + n].astype(F32)

        dm = _dot_nt(dx_ref[...].astype(MXU), wo_ref[...].reshape(D, D))

        def through_gate(k, glo, w_ref):
            sg = _sigmoid(pf(glo, D))
            br = b_ref[:, k * D:(k + 1) * D].astype(F32)
            dp_ref[:, glo:glo + D] = (dm * br * sg * (1.0 - sg)).astype(dp_ref.dtype)
            dbr = (dm * sg).astype(MXU)
            db_ref[:, k * D:(k + 1) * D] = dbr
            return _dot_nt(dbr, _lane_cat(w_ref))

        dya = through_gate(0, GA0, wa_ref)
        dyb = through_gate(1, GB0, wb_ref)
        dyc = through_gate(2, GC0, wc_ref)

        t_pos = (tile * TM + lax.broadcasted_iota(jnp.int32, (TM, 1), 0)).astype(F32)
        for g, win in enumerate(POOL_WINDOWS):
            cols = slice(g * CHUNK, (g + 1) * CHUNK)
            pa_g = pz_ref[:, cols]
            q = _dot(pa_g, wpool_ref[g])
            dya_g = dya[:, cols]
            small_ref[0:1, cols] += _colsum(dya_g * q)
            dq = (dya_g * ps_ref[:, cols]).astype(MXU)
            dpa_g = _dot_nt(dq, wpool_ref[g])
            dwp_ref[g] += _dot_tn(pa_g, dq)
            dpw = dpa_g / jnp.minimum(t_pos + 1.0, float(win))
            acc = jnp.concatenate([dpw, carry_pa[:, cols]], axis=0)
            carry_pa[:, cols] = dpw[:HALO_POOL, :]
            k = 1
            while k < win:
                acc = acc + _shift_up(acc, k)
                k *= 2
            dp_ref[:, cols] = (acc[:TM, :] - dpa_g).astype(dp_ref.dtype)

        uvp = pf(UV0, 2 * WA)
        uvg, dgelu = _gelu_parts(uvp)
        u = uvg[:, :WA]
        v = uvg[:, WA:]
        rv = lax.rsqrt(jnp.mean(v * v, axis=-1, keepdims=True) + EPS)
        vh = v * rv
        vn = (vh * gs_ref[...]).astype(MXU)
        mask = _tril_mask()
        for g in range(HEADS):
            cols = slice(g * CHUNK, (g + 1) * CHUNK)
            wt32 = jnp.where(mask, wsp_ref[g], 0.0)
            wt = wt32.astype(MXU)
            wt_t = wt32.T.astype(MXU)
            bcol = bsp_ref[:, g:g + 1]
            for c in range(TM // CHUNK):
                rows = slice(c * CHUNK, (c + 1) * CHUNK)
                vn_cg = vn[rows, cols]
                sv = _dot(wt, vn_cg) + bcol
                dyb_cg = dyb[rows, cols]
                du_s[rows, cols] = dyb_cg * sv
                dsv = dyb_cg * u[rows, cols]
                dbs_acc[g] += dsv
                dsv_b = dsv.astype(MXU)
                dws_ref[g] += _dot_nt(dsv_b, vn_cg)
                dvn_s[rows, cols] = _dot(wt_t, dsv_b)
        dvn = dvn_s[...]
        small_ref[1:2, :] += _colsum(dvn * vh)
        dvg = dvn * gs_ref[...]
        dv = rv * (dvg - vh * jnp.mean(dvg * vh, axis=-1, keepdims=True))
        dp_ref[:, UV0:UV0 + WA] = (du_s[...] * dgelu[:, :WA]).astype(dp_ref.dtype)
        dp_ref[:, UV0 + WA:UV0 + 2 * WA] = (dv * dgelu[:, WA:]).astype(dp_ref.dtype)

        cb = pf(CB0, WA)
        cc = pf(CC0, WA)
        cx = pf(CX0, WA)
        z = cc * cx
        dp_ref[:, CB0:CB0 + WA] = (dyc * pz_ref[:, WA:2 * WA].astype(F32)).astype(dp_ref.dtype)
        dcz = dyc * cb
        extz = jnp.concatenate([dcz, carry_cz[...]], axis=0)
        carry_cz[...] = dcz[:HALO_CONV, :]
        up1 = _shift_up(extz, 1)[:TM, :]
        up2 = _shift_up(extz, 2)[:TM, :]
        dz = cc_ref[2:3, :] * dcz + cc_ref[1:2, :] * up1 + cc_ref[0:1, :] * up2
        small_ref[2:3, :] += _colsum(z * up2)
        small_ref[3:4, :] += _colsum(z * up1)
        small_ref[4:5, :] += _colsum(z * dcz)
        dp_ref[:, CC0:CC0 + WA] = (dz * cx).astype(dp_ref.dtype)
        dp_ref[:, CX0:CX0 + WA] = (dz * cc).astype(dp_ref.dtype)

        @pl.when(step == nt - 1)
        def _():
            ones = jnp.ones((8, CHUNK), F32)
            for g in range(HEADS):
                dws_ref[g] = jnp.where(mask, dws_ref[g], 0.0)
                row = lax.dot_general(ones, dbs_acc[g], (((1,), (1,)), ((), ())), preferred_element_type=F32,
                                      precision=lax.Precision.HIGHEST)
                dbs_ref[g:g + 1, :] = row[0:1, :]

    row = lambda n: pl.BlockSpec((TM, n), lambda i: (nt - 1 - i, 0))
    br_spec = _const((N_DEV, WA, NB_BR), (0, 0, 0))
    acc_spec = lambda shape: pl.BlockSpec(shape, lambda i: (0,) * len(shape))
    return pl.pallas_call(
        body, name=name, grid=(nt,),
        in_specs=[row(D), row(NCOL), row(3 * WA), row(2 * WA), row(3 * D),
                  _const((HEADS, CHUNK, CHUNK), (0, 0, 0)), _const((1, WA), (0, 0)), _const((1, WA), (0, 0)),
                  _const((HEADS, CHUNK, CHUNK), (0, 0, 0)), _const((CHUNK, HEADS), (0, 0)), _const((3, WA), (0, 0)),
                  br_spec, br_spec, br_spec, _const((N_DEV, ROWS_O, D), (0, 0, 0))],
        out_specs=[row(NCOL), row(3 * D), acc_spec((HEADS, CHUNK, CHUNK)), acc_spec((HEADS, CHUNK, CHUNK)),
                   acc_spec((8, WA)), acc_spec((8, CHUNK))],
        out_shape=[_sds((s, NCOL), MXU), _sds((s, 3 * D), MXU), _sds((HEADS, CHUNK, CHUNK), F32),
                   _sds((HEADS, CHUNK, CHUNK), F32), _sds((8, WA), F32), _sds((8, CHUNK), F32)],
        scratch_shapes=[pltpu.VMEM((HALO_POOL, WA), F32), pltpu.VMEM((HALO_CONV, WA), F32),
                        pltpu.VMEM((HEADS, CHUNK, CHUNK), F32), pltpu.VMEM((TM, WA), F32), pltpu.VMEM((TM, WA), F32)],
        compiler_params=_params(),
    )(dxmid, p, yabc, pacz, babc, wpool, pscale, gsgu, wsp, bsp_t, convc, wa_all, wb_all, wc_all, wo_all)


def _wgrad(a, b, a_spec, b_spec, n_out, acc_shape, out_shape, out_spec, store, name, comm=None):
    s = a.shape[-2]
    ts = min(TS_WGRAD, s)
    n_steps = s // ts

    def body(a_ref, b_ref, o_ref, acc_ref):
        k = pl.program_id(1)

        @pl.when(k == 0)
        def _():
            acc_ref[...] = jnp.zeros_like(acc_ref)

        acc_ref[...] += _dot_tn(a_ref[...], b_ref[...].astype(MXU))

        @pl.when(k == n_steps - 1)
        def _():
            store(o_ref, acc_ref)

    (res,), extra = _call(
        body, comm, lambda: (pl.program_id(0) == 0) & (pl.program_id(1) == 0),
        lambda: (pl.program_id(0) == n_out - 1) & (pl.program_id(1) == n_steps - 1), (a, b),
        name=name, grid=(n_out, n_steps),
        in_specs=[a_spec(ts), b_spec(ts)], out_specs=[out_spec], out_shape=[_sds(out_shape, WIRE)],
        scratch_shapes=[pltpu.VMEM(acc_shape, F32)],
        compiler_params=_params(2),
    )
    return res, extra


def _store_plain(o_ref, acc_ref):
    o_ref[...] = acc_ref[...].astype(o_ref.dtype)


def _store_lane_blocks(o_ref, acc_ref):
    for d in range(N_DEV):
        o_ref[d] = acc_ref[:, d * NB_BR:(d + 1) * NB_BR].astype(o_ref.dtype)


def _wgrad_in(h, dp, name, comm=None):
    return _wgrad(h, dp, lambda ts: pl.BlockSpec((ts, D), lambda j, k: (k, 0)),
                  lambda ts: pl.BlockSpec((ts, NB_IN), lambda j, k: (k, j)), N_DEV, (D, NB_IN),
                  (N_DEV, D, NB_IN), pl.BlockSpec((None, D, NB_IN), lambda j, k: (j, 0, 0)), _store_plain, name, comm)


def _wgrad_up(h, du, name, comm=None):
    return _wgrad(h, du, lambda ts: pl.BlockSpec((ts, D), lambda j, k: (k, 0)),
                  lambda ts: pl.BlockSpec((None, ts, NB_UP), lambda j, k: (j, k, 0)), N_DEV, (D, NB_UP),
                  (N_DEV, D, NB_UP), pl.BlockSpec((None, D, NB_UP), lambda j, k: (j, 0, 0)), _store_plain, name, comm)


def _wgrad_down(act, dxo, name, comm=None):
    return _wgrad(act, dxo, lambda ts: pl.BlockSpec((None, ts, NB_UP), lambda j, k: (j, k, 0)),
                  lambda ts: pl.BlockSpec((ts, D), lambda j, k: (k, 0)), N_DEV // 2, (NB_UP, D),
                  (DFF, D), pl.BlockSpec((NB_UP, D), lambda j, k: (j, 0)), _store_plain, name, comm)


def _wgrad_o(merged, dxmid, name, comm=None):
    return _wgrad(merged, dxmid, lambda ts: pl.BlockSpec((ts, D), lambda j, k: (k, 0)),
                  lambda ts: pl.BlockSpec((ts, D), lambda j, k: (k, 0)), 1, (D, D),
                  (D, D), pl.BlockSpec((D, D), lambda j, k: (0, 0)), _store_plain, name, comm)


def _wgrad_branches(yabc, dbabc, name, comm=None):
    return _wgrad(yabc, dbabc, lambda ts: pl.BlockSpec((ts, WA), lambda j, k: (k, j)),
                  lambda ts: pl.BlockSpec((ts, D), lambda j, k: (k, j)), 3, (WA, D),
                  (N_DEV, 3, WA, NB_BR), pl.BlockSpec((N_DEV, None, WA, NB_BR), lambda j, k: (0, j, 0, 0)),
                  _store_lane_blocks, name, comm)


def _adamw_math(g, w, m, v):
    m = ADAM_B1 * m + (1.0 - ADAM_B1) * g
    v = ADAM_B2 * v + (1.0 - ADAM_B2) * (g * g)
    m_hat = m / (1.0 - ADAM_B1 ** ADAM_STEP)
    v_hat = v / (1.0 - ADAM_B2 ** ADAM_STEP)
    delta = -ADAM_LR * (m_hat / (jnp.sqrt(v_hat) + ADAM_EPS) + ADAM_WD * w)
    return delta, m, v


def _adamw_sum(parts, mid, w, m, v, layer, prev, tr, name):
    n_layers, r, c = w.shape

    def body(p_ref, w_ref, m_ref, v_ref, *rest):
        g_ref, d_ref, mo_ref, vo_ref = rest[-4:]
        g = p_ref[0].astype(F32)
        for k in range(1, N_DEV):
            g = g + p_ref[k].astype(F32)
        g_ref[...] = g
        d_ref[...], mo_ref[...], vo_ref[...] = _adamw_math(g, w_ref[...], m_ref[...], v_ref[...])

    blk = pl.BlockSpec((None, tr, c), lambda i: (layer, i, 0))
    extra = [] if prev is None else list(prev)
    return pl.pallas_call(
        body, name=name, grid=(r // tr,),
        in_specs=[pl.BlockSpec((N_DEV, None, tr, c), lambda i: (0, mid, i, 0)), blk, blk, blk]
        + [pl.BlockSpec(memory_space=pl.ANY)] * len(extra),
        out_specs=[blk] * 4, out_shape=[_sds((n_layers, r, c), F32)] * 4,
        input_output_aliases={4 + k: k for k in range(len(extra))},
        compiler_params=_params(),
    )(parts, w, m, v, *extra)


def _sum_parts(parts, name):
    _, r, c = parts.shape

    def body(p_ref, o_ref):
        g = p_ref[0]
        for k in range(1, N_DEV):
            g = g + p_ref[k]
        o_ref[...] = g

    return pl.pallas_call(body, name=name, out_shape=_sds((r, c), F32),
                          compiler_params=pltpu.CompilerParams(vmem_limit_bytes=VMEM_LIMIT))(parts)


def _adamw_small(g, w, m, v, name):
    def body(g_ref, w_ref, m_ref, v_ref, d_ref, mo_ref, vo_ref):
        d_ref[...], mo_ref[...], vo_ref[...] = _adamw_math(g_ref[...], w_ref[...], m_ref[...], v_ref[...])

    return pl.pallas_call(body, name=name, out_shape=[_sds(w.shape, F32)] * 3)(g, w, m, v)


HBM_SPEC = pl.BlockSpec(memory_space=pltpu.HBM)


def _position():
    return lax.axis_index("x"), lax.axis_index("y"), lax.axis_index("c")


def _device_index(chip, core):
    return 4 * chip[0] + 2 * chip[1] + core


def _gather_across_chips(shards, layer):
    n = len(shards)
    per = 5

    def copies(ins, outs, send, recv):
        x, y, c = _position()
        me = 4 * x + 2 * y + c
        targets = [(x, y, 1 - c), (1 - x, y, c), (x, 1 - y, c), (1 - x, 1 - y, c)]
        remote = [pltpu.make_async_remote_copy(
            src_ref=ins[t].at[layer], dst_ref=outs[t].at[me], send_sem=send.at[per * t + k],
            recv_sem=recv.at[per * t + k], device_id=to, device_id_type=MESH)
            for t in range(n) for k, to in enumerate(targets)]
        local = [pltpu.make_async_copy(ins[t].at[layer], outs[t].at[me], send.at[per * t + 4]) for t in range(n)]
        return remote, local

    def start(ins, outs, send, recv):
        remote, local = copies(ins, outs, send, recv)
        for cp in local + remote:
            cp.start()

    def finish(ins, outs, send, recv):
        remote, local = copies(ins, outs, send, recv)
        for cp in remote:
            cp.wait()
        for cp in local:
            cp.wait()

    return _Comm(shards, [_sds((N_DEV,) + a.shape[1:], a.dtype) for a in shards], per * n, start, finish)


def _gather_within_chip(buffers):
    n = len(buffers)

    def copies(outs, send, recv):
        x, y, c = _position()
        chips = [(1 - x, y), (x, 1 - y), (1 - x, 1 - y)]
        return [pltpu.make_async_remote_copy(
            src_ref=outs[t].at[_device_index(chip, c)], dst_ref=outs[t].at[_device_index(chip, c)],
            send_sem=send.at[3 * t + j], recv_sem=recv.at[3 * t + j], device_id=(x, y, 1 - c), device_id_type=MESH)
            for t in range(n) for j, chip in enumerate(chips)]

    def start(ins, outs, send, recv):
        for cp in copies(outs, send, recv):
            cp.start()

    def finish(ins, outs, send, recv):
        for cp in copies(outs, send, recv):
            cp.wait()

    return _Comm(buffers, [_sds(a.shape, a.dtype) for a in buffers], 3 * n, start, finish,
                 aliases={t: t for t in range(n)})


def _run_comms(comms, name):
    first = comms[0]
    n = len(first.operands)
    n_out = len(first.out_shapes)

    def body(*refs):
        ins, outs, sems = refs[:n], refs[n:n + n_out], refs[n + n_out:]
        for k, comm in enumerate(comms):
            comm.start(ins, outs, sems[2 * k], sems[2 * k + 1])
            comm.finish(ins, outs, sems[2 * k], sems[2 * k + 1])

    return pl.pallas_call(
        body, name=name, in_specs=[HBM_SPEC] * n, out_specs=[HBM_SPEC] * n_out, out_shape=first.out_shapes,
        scratch_shapes=[pltpu.SemaphoreType.DMA((comm.n_sems,)) for comm in comms for _ in range(2)],
    )(*first.operands)


def _exchange(parts):
    n = len(parts)
    per = 8
    flips = [(0, 0, 1), (1, 0, 0), (1, 0, 1), (0, 1, 0), (0, 1, 1), (1, 1, 0), (1, 1, 1)]

    def copies(ins, outs, send, recv):
        x, y, c = _position()
        me = 4 * x + 2 * y + c
        remote = []
        for t in range(n):
            for k, (fx, fy, fc) in enumerate(flips):
                peer = ((1 - x if fx else x), (1 - y if fy else y), (1 - c if fc else c))
                remote.append(pltpu.make_async_remote_copy(
                    src_ref=ins[t].at[_device_index(peer[:2], peer[2])], dst_ref=outs[t].at[me],
                    send_sem=send.at[per * t + k], recv_sem=recv.at[per * t + k], device_id=peer, device_id_type=MESH))
        local = [pltpu.make_async_copy(ins[t].at[me], outs[t].at[me], send.at[per * t + 7]) for t in range(n)]
        return remote, local

    def start(ins, outs, send, recv):
        remote, local = copies(ins, outs, send, recv)
        for cp in local + remote:
            cp.start()

    def finish(ins, outs, send, recv):
        remote, local = copies(ins, outs, send, recv)
        for cp in remote:
            cp.wait()
        for cp in local:
            cp.wait()

    return _Comm(parts, [_sds(a.shape, a.dtype) for a in parts], per * n, start, finish)


def _rows128(a):
    return a.reshape(-1, 128)


def kernel(x, g_mix, w_in, w_pool, pool_scale, g_sgu, w_spatial, b_spatial, conv_c, w_branch_a, w_branch_b, w_branch_c, w_o, g_ffn, w_up, conv_ffn, conv_ffn_b, w_down, g_final, loss_target, m_g_mix, m_w_in, m_w_pool, m_pool_scale, m_g_sgu, m_w_spatial, m_b_spatial, m_conv_c, m_w_branch_a, m_w_branch_b, m_w_branch_c, m_w_o, m_g_ffn, m_w_up, m_conv_ffn, m_conv_ffn_b, m_w_down, m_g_final, v_g_mix, v_w_in, v_w_pool, v_pool_scale, v_g_sgu, v_w_spatial, v_b_spatial, v_conv_c, v_w_branch_a, v_w_branch_b, v_w_branch_c, v_w_o, v_g_ffn, v_w_up, v_conv_ffn, v_conv_ffn_b, v_w_down, v_g_final):
    s = x.shape[1]
    n_layers = g_mix.shape[0]
    x0 = x.reshape(s, D)
    target = loss_target.reshape(s, D)
    me = 4 * lax.axis_index("x") + 2 * lax.axis_index("y") + lax.axis_index("c")

    shards = [w_in.astype(MXU), w_branch_a.astype(MXU), w_branch_b.astype(MXU), w_branch_c.astype(MXU), w_o.astype(MXU),
              w_up.astype(MXU), w_down.astype(MXU), conv_c, conv_ffn]
    across = _gather_across_chips(shards, 0)
    gathered = _run_comms([across, _gather_within_chip(across.out_shapes)], "gather_weights_0")
    wpool_b = w_pool.astype(MXU)
    bsp_t = jnp.swapaxes(b_spatial, 1, 2)
    convb_blk = conv_ffn_b.reshape(n_layers, N_DEV, NB_UP)

    saved = []
    weights = []
    xl = x0
    for l in range(n_layers):
        win8, wa8, wb8, wc8, wo8, wup8, wd8, convc8, convf8 = gathered
        convc_full = jnp.transpose(convc8, (1, 0, 2)).reshape(3, WA)
        weights.append((win8, wa8, wb8, wc8, wo8, wup8, wd8, convc_full, convf8))
        more = l + 1 < n_layers
        (p, h), half = _rms_proj(xl, g_mix[l:l + 1], win8, False, f"in_proj_{l}",
                                 _gather_across_chips(shards, l + 1) if more else None)
        xmid, yabc, pacz, babc, merged = _mixer_fwd(
            xl, p, wpool_b[l], pool_scale[l:l + 1], g_sgu[l:l + 1], w_spatial[l], bsp_t[l], convc_full,
            wa8, wb8, wc8, wo8, f"mixer_fwd_{l}")
        (upre, h2), gathered = _rms_proj(xmid, g_ffn[l:l + 1], wup8, True, f"up_proj_{l}",
                                         _gather_within_chip(half) if more else None)
        xout, act = _ffn_fwd(xmid, upre, convf8, convb_blk[l], wd8, f"ffn_fwd_{l}")
        saved.append((xl, p, h, xmid, yabc, pacz, babc, merged, upre, h2, act))
        xl = xout

    dx, dg_final, loss_local = _loss_head(xl, g_final.reshape(1, D), target, "loss_head")

    received = [dict() for _ in range(n_layers)]
    small_pieces = [None] * n_layers
    pending = None

    def riding(group):
        return None if pending is None else _exchange([a for _, a in pending[1 + group]])

    def landed(group, arrays):
        if pending is not None:
            received[pending[0]].update({k: a for (k, _), a in zip(pending[1 + group], arrays)})

    for l in reversed(range(n_layers)):
        xin, p, h, xmid, yabc, pacz, babc, merged, upre, h2, act = saved[l]
        win8, wa8, wb8, wc8, wo8, wup8, wd8, convc_full, convf8 = weights[l]
        dupre, dconvf = _ffn_bwd(dx, upre, convf8, convb_blk[l], wd8, f"ffn_bwd_{l}")
        g_wdown, _ = _wgrad_down(act, dx, f"wgrad_down_{l}")
        g_wup, got = _wgrad_up(h2, dupre, f"wgrad_up_{l}", riding(0))
        landed(0, got)
        (dxmid, dg_ffn), got = _proj_bwd(dupre, True, wup8, xmid, g_ffn[l:l + 1], dx, f"up_proj_bwd_{l}", riding(1))
        landed(1, got)
        dp, dbabc, dwp, dws, small, dbs = _mixer_bwd(
            dxmid, p, yabc, pacz, babc, wpool_b[l], pool_scale[l:l + 1], g_sgu[l:l + 1], w_spatial[l], bsp_t[l],
            convc_full, wa8, wb8, wc8, wo8, f"mixer_bwd_{l}")
        g_wo, _ = _wgrad_o(merged, dxmid, f"wgrad_o_{l}")
        g_br, _ = _wgrad_branches(yabc, dbabc, f"wgrad_branches_{l}")
        g_win, got = _wgrad_in(h, dp, f"wgrad_in_{l}", riding(2))
        landed(2, got)
        pending = (l, [("w_in", g_win)], [("w_up", g_wup)],
                   [("branches", g_br), ("w_o", g_wo.reshape(N_DEV, ROWS_O, D)),
                    ("w_down", g_wdown.reshape(N_DEV, ROWS_DN, D))])
        early = None
        if l == 0:
            early = pending[2] + pending[3][2:]
            pending = (l, pending[1], [], pending[3][:2])
        (dx, dg_mix), got = _proj_bwd(dp, False, win8, xin, g_mix[l:l + 1], dxmid, f"in_proj_bwd_{l}",
                                      None if early is None else _exchange([a for _, a in early]))
        if early is not None:
            received[l].update({k: a for (k, _), a in zip(early, got)})
        small_pieces[l] = [dg_mix, dwp, small, dws, dbs, dg_ffn, dconvf]
    grad_x = dx.reshape(1, s, D)
    last = pending[1] + pending[3]
    got = _run_comms([_exchange([a for _, a in last])], "exchange_grads_last")
    received[0].update({k: a for (k, _), a in zip(last, got)})

    def update_big(key, mid, w, m, v, tr, tag):
        outs = None
        for l in range(n_layers):
            parts = received[l][key]
            if parts.ndim == 3:
                parts = parts.reshape(N_DEV, 1, *parts.shape[1:])
            outs = _adamw_sum(parts, mid, w, m, v, l, outs, tr, f"adamw_{tag}_{l}")
        return outs

    up_in = update_big("w_in", 0, w_in, m_w_in, v_w_in, 256, "w_in")
    up_a = update_big("branches", 0, w_branch_a, m_w_branch_a, v_w_branch_a, WA, "w_branch_a")
    up_b = update_big("branches", 1, w_branch_b, m_w_branch_b, v_w_branch_b, WA, "w_branch_b")
    up_c = update_big("branches", 2, w_branch_c, m_w_branch_c, v_w_branch_c, WA, "w_branch_c")
    up_o = update_big("w_o", 0, w_o, m_w_o, v_w_o, ROWS_O, "w_o")
    up_up = update_big("w_up", 0, w_up, m_w_up, v_w_up, 256, "w_up")
    up_down = update_big("w_down", 0, w_down, m_w_down, v_w_down, ROWS_DN, "w_down")

    pieces = [pc for l in range(n_layers) for pc in small_pieces[l]] + [dg_final]
    packed = jnp.concatenate([_rows128(pc) for pc in pieces], axis=0)[None]
    across = _gather_across_chips([packed], 0)
    (gathered_small,) = _run_comms([across, _gather_within_chip(across.out_shapes)], "gather_small_grads")
    summed = _sum_parts(gathered_small, "sum_small_grads")
    unpacked = []
    row = 0
    for pc in pieces:
        n_rows = pc.size // 128
        unpacked.append(summed[row:row + n_rows].reshape(pc.shape))
        row += n_rows
    per_layer = len(small_pieces[0])
    stack = lambda k: jnp.stack([unpacked[l * per_layer + k] for l in range(n_layers)], axis=0)
    grad_g_mix = stack(0)[:, 0, :]
    grad_w_pool = stack(1)
    mixer_small = stack(2)
    grad_pool_scale = mixer_small[:, 0, :]
    grad_g_sgu = mixer_small[:, 1, :]
    grad_conv_c = lax.dynamic_slice_in_dim(mixer_small[:, 2:5, :], me * (WA // N_DEV), WA // N_DEV, axis=2)
    grad_w_spatial = stack(3)
    grad_b_spatial = stack(4)[:, 0:HEADS, :]
    grad_g_ffn = stack(5)[:, 0, :]
    conv_grads = stack(6)
    grad_conv_ffn = lax.dynamic_index_in_dim(conv_grads, me, axis=1, keepdims=False)[:, 0:3, :]
    grad_conv_ffn_b = conv_grads[:, :, 3, :].reshape(n_layers, 2 * DFF)
    grad_g_final = unpacked[-1][0]

    def update_small(g, w, m, v, tag):
        shape2 = (-1, w.shape[-1])
        outs = _adamw_small(g.reshape(shape2), w.reshape(shape2), m.reshape(shape2), v.reshape(shape2), f"adamw_{tag}")
        return [g] + [o.reshape(w.shape) for o in outs]

    up = {
        "g_mix": update_small(grad_g_mix, g_mix, m_g_mix, v_g_mix, "g_mix"),
        "w_in": up_in,
        "w_pool": update_small(grad_w_pool, w_pool, m_w_pool, v_w_pool, "w_pool"),
        "pool_scale": update_small(grad_pool_scale, pool_scale, m_pool_scale, v_pool_scale, "pool_scale"),
        "g_sgu": update_small(grad_g_sgu, g_sgu, m_g_sgu, v_g_sgu, "g_sgu"),
        "w_spatial": update_small(grad_w_spatial, w_spatial, m_w_spatial, v_w_spatial, "w_spatial"),
        "b_spatial": update_small(grad_b_spatial, b_spatial, m_b_spatial, v_b_spatial, "b_spatial"),
        "conv_c": update_small(grad_conv_c, conv_c, m_conv_c, v_conv_c, "conv_c"),
        "w_branch_a": up_a,
        "w_branch_b": up_b,
        "w_branch_c": up_c,
        "w_o": up_o,
        "g_ffn": update_small(grad_g_ffn, g_ffn, m_g_ffn, v_g_ffn, "g_ffn"),
        "w_up": up_up,
        "conv_ffn": update_small(grad_conv_ffn, conv_ffn, m_conv_ffn, v_conv_ffn, "conv_ffn"),
        "conv_ffn_b": update_small(grad_conv_ffn_b, conv_ffn_b, m_conv_ffn_b, v_conv_ffn_b, "conv_ffn_b"),
        "w_down": up_down,
        "g_final": update_small(grad_g_final, g_final, m_g_final, v_g_final, "g_final"),
    }
    loss = lax.psum(loss_local[0, 0], AXES)
    order = list(up)
    return (loss, grad_x, *[up[k][0] for k in order], *[up[k][1] for k in order], *[up[k][2] for k in order],
            *[up[k][3] for k in order])
```

```python
import functools

import jax
import jax.numpy as jnp
from jax import lax
from jax.experimental import pallas as pl
from jax.experimental.pallas import tpu as pltpu

F32 = jnp.float32
BF16 = jnp.bfloat16
MXU = BF16
ACT = BF16
WIRE = BF16

N_DEV = 8
D = 1024
WA = 512
NCOL = 6144
DFF = 2816
NB_IN = NCOL // N_DEV
NB_UP = 2 * DFF // N_DEV
NB_BR = D // N_DEV
ROWS_O = D // N_DEV
ROWS_DN = DFF // N_DEV
CHUNK = 128
HEADS = 4
POOL_WINDOWS = (2, 4, 8, 16)
EPS = 1e-6
A0, UV0, CB0, CC0, CX0, GA0, GB0, GC0 = 0, 512, 1536, 2048, 2560, 3072, 4096, 5120

ADAM_LR = 0.001
ADAM_B1 = 0.9
ADAM_B2 = 0.999
ADAM_EPS = 1e-08
ADAM_WD = 0.01
ADAM_STEP = 10

TM = 256
TS_WGRAD = 1024
HALO_POOL = 16
HALO_CONV = 8
VMEM_LIMIT = 56 * 1024 * 1024
MESH = pl.DeviceIdType.MESH
AXES = ("x", "y", "c")


def _sds(shape, dtype):
    return jax.ShapeDtypeStruct(tuple(shape), dtype)


def _params(n_grid=1):
    return pltpu.CompilerParams(dimension_semantics=("arbitrary",) * n_grid, vmem_limit_bytes=VMEM_LIMIT)


def _const(block, index):
    return pl.BlockSpec(block, lambda *_: index, pipeline_mode=pl.Buffered(1))


def _dot(a, b):
    return jnp.dot(a, b, preferred_element_type=F32)


def _dot_nt(a, b):
    return lax.dot_general(a, b, (((1,), (1,)), ((), ())), preferred_element_type=F32)


def _dot_tn(a, b):
    return lax.dot_general(a, b, (((0,), (0,)), ((), ())), preferred_element_type=F32)


def _sigmoid(v):
    return 0.5 * jnp.tanh(0.5 * v) + 0.5


def _shift_down(v, k):
    return pltpu.roll(v, k, axis=0)


def _shift_up(v, k):
    return pltpu.roll(v, v.shape[0] - k, axis=0)


def _colsum(v):
    return jnp.sum(v, axis=0, keepdims=True)


def _lane_cat(ref):
    return jnp.concatenate([ref[d] for d in range(N_DEV)], axis=1)


class _Comm:
    def __init__(self, operands, out_shapes, n_sems, start, finish, mid=None):
        self.operands = list(operands)
        self.out_shapes = list(out_shapes)
        self.n_sems = n_sems
        self.start = start
        self.mid = mid
        self.finish = finish


def _call(body, comms, is_first, is_mid, is_last, operands, *, in_specs, out_specs, out_shape, scratch_shapes=(), **kw):
    n_in, n_out, n_scr = len(in_specs), len(out_specs), len(scratch_shapes)
    comms = [c for c in (comms or []) if c is not None]
    if not comms:
        res = pl.pallas_call(body, in_specs=in_specs, out_specs=out_specs, out_shape=out_shape,
                             scratch_shapes=list(scratch_shapes), **kw)(*operands)
        return res, []
    nci = [len(c.operands) for c in comms]
    nco = [len(c.out_shapes) for c in comms]

    def split(refs, sizes):
        parts = []
        for n in sizes:
            parts.append(refs[:n])
            refs = refs[n:]
        return parts, refs

    def carrier(*refs):
        ins, refs = refs[:n_in], refs[n_in:]
        cins, refs = split(refs, nci)
        outs, refs = refs[:n_out], refs[n_out:]
        couts, refs = split(refs, nco)
        scr, sems = refs[:n_scr], refs[n_scr:]

        def run(step):
            for k, c in enumerate(comms):
                if getattr(c, step) is not None:
                    getattr(c, step)(cins[k], couts[k], sems[2 * k], sems[2 * k + 1])

        def at(mark, step):
            if mark is None:
                run(step)
            else:
                pl.when(mark())(lambda: run(step))

        at(is_first, "start")
        body(*ins, *outs, *scr)
        at(is_mid, "mid")
        at(is_last, "finish")

    res = pl.pallas_call(
        carrier, in_specs=list(in_specs) + [HBM_SPEC] * sum(nci), out_specs=list(out_specs) + [HBM_SPEC] * sum(nco),
        out_shape=list(out_shape) + [s for c in comms for s in c.out_shapes],
        scratch_shapes=list(scratch_shapes) + [pltpu.SemaphoreType.DMA((c.n_sems,)) for c in comms for _ in range(2)],
        **kw,
    )(*operands, *[a for c in comms for a in c.operands])
    extra, _ = split(res[n_out:], nco)
    return res[:n_out], extra


def _grid_marks(nt):
    return (lambda: pl.program_id(0) == 0), (lambda: pl.program_id(0) == nt // 2), (lambda: pl.program_id(0) == nt - 1)


def _rms_proj(x, g, w_all, blocked_out, name, comms=None):
    s = x.shape[0]
    nb = w_all.shape[-1]
    nt = s // TM

    def body(x_ref, g_ref, w_ref, p_ref, h_ref):
        xf = x_ref[...]
        r = lax.rsqrt(jnp.mean(xf * xf, axis=-1, keepdims=True) + EPS)
        h = (xf * r * g_ref[...]).astype(MXU)
        h_ref[...] = h
        for j in range(N_DEV):
            pj = _dot(h, w_ref[j]).astype(p_ref.dtype)
            if blocked_out:
                p_ref[j] = pj
            else:
                p_ref[:, j * nb:(j + 1) * nb] = pj

    if blocked_out:
        p_shape, p_spec = (N_DEV, s, nb), pl.BlockSpec((N_DEV, TM, nb), lambda i: (0, i, 0))
    else:
        p_shape, p_spec = (s, N_DEV * nb), pl.BlockSpec((TM, N_DEV * nb), lambda i: (i, 0))
    return _call(
        body, comms, *_grid_marks(nt), (x, g, w_all),
        name=name, grid=(nt,),
        in_specs=[pl.BlockSpec((TM, D), lambda i: (i, 0)), _const((1, D), (0, 0)),
                  _const((N_DEV, D, nb), (0, 0, 0))],
        out_specs=[p_spec, pl.BlockSpec((TM, D), lambda i: (i, 0))],
        out_shape=[_sds(p_shape, ACT), _sds((s, D), MXU)],
        compiler_params=_params(),
    )


def _tril_mask():
    r = lax.broadcasted_iota(jnp.int32, (CHUNK, CHUNK), 0)
    c = lax.broadcasted_iota(jnp.int32, (CHUNK, CHUNK), 1)
    return r >= c


def _gelu_parts(v):
    c0 = 0.7978845608028654
    th = jnp.tanh(c0 * (v + 0.044715 * (v * v * v)))
    cdf = 0.5 * (1.0 + th)
    dgelu = cdf + v * (0.5 * c0) * (1.0 - th * th) * (1.0 + 3.0 * 0.044715 * (v * v))
    return v * cdf, dgelu


def _mixer_fwd(x, p, wpool, pscale, gsgu, wsp, bsp_t, convc, wa_all, wb_all, wc_all, wo_all, name, comms=None):
    s = x.shape[0]
    nt = s // TM

    def body(x_ref, p_ref, wpool_ref, ps_ref, gs_ref, wsp_ref, bsp_ref, cc_ref, wa_ref, wb_ref, wc_ref, wo_ref,
             xmid_ref, y_ref, pz_ref, b_ref, m_ref, carry_a, carry_z):
        i = pl.program_id(0)

        @pl.when(i == 0)
        def _():
            carry_a[...] = jnp.zeros_like(carry_a)
            carry_z[...] = jnp.zeros_like(carry_z)

        def pf(lo, n):
            return p_ref[:, lo:lo + n].astype(F32)

        a = pf(A0, WA)
        ext = jnp.concatenate([carry_a[...], a], axis=0)
        carry_a[...] = a[TM - HALO_POOL:, :]
        t_pos = (i * TM + lax.broadcasted_iota(jnp.int32, (TM, 1), 0)).astype(F32)
        for g, win in enumerate(POOL_WINDOWS):
            cols = slice(g * CHUNK, (g + 1) * CHUNK)
            acc = ext[:, cols]
            k = 1
            while k < win:
                acc = acc + _shift_down(acc, k)
                k *= 2
            cnt = jnp.minimum(t_pos + 1.0, float(win))
            pa_g = (acc[HALO_POOL:, :] / cnt - a[:, cols]).astype(MXU)
            pz_ref[:, cols] = pa_g
            y_ref[:, cols] = (_dot(pa_g, wpool_ref[g]) * ps_ref[:, cols]).astype(ACT)

        uvg, _ = _gelu_parts(pf(UV0, 2 * WA))
        u = uvg[:, :WA]
        v = uvg[:, WA:]
        rv = lax.rsqrt(jnp.mean(v * v, axis=-1, keepdims=True) + EPS)
        vn = (v * rv * gs_ref[...]).astype(MXU)
        mask = _tril_mask()
        for g in range(HEADS):
            cols = slice(g * CHUNK, (g + 1) * CHUNK)
            wt = jnp.where(mask, wsp_ref[g], 0.0).astype(MXU)
            bcol = bsp_ref[:, g:g + 1]
            for c in range(TM // CHUNK):
                rows = slice(c * CHUNK, (c + 1) * CHUNK)
                sv = _dot(wt, vn[rows, cols]) + bcol
                y_ref[rows, WA + g * CHUNK:WA + (g + 1) * CHUNK] = (u[rows, cols] * sv).astype(ACT)

        z = pf(CC0, WA) * pf(CX0, WA)
        extz = jnp.concatenate([carry_z[...], z], axis=0)
        carry_z[...] = z[TM - HALO_CONV:, :]
        cz = (cc_ref[0:1, :] * _shift_down(extz, 2)[HALO_CONV:, :]
              + cc_ref[1:2, :] * _shift_down(extz, 1)[HALO_CONV:, :] + cc_ref[2:3, :] * z)
        pz_ref[:, WA:2 * WA] = cz.astype(ACT)
        y_ref[:, 2 * WA:3 * WA] = (pf(CB0, WA) * cz).astype(ACT)

        merged = jnp.zeros((TM, D), F32)
        for k, (w_ref, glo) in enumerate(((wa_ref, GA0), (wb_ref, GB0), (wc_ref, GC0))):
            br = _dot(y_ref[:, k * WA:(k + 1) * WA], _lane_cat(w_ref))
            b_ref[:, k * D:(k + 1) * D] = br.astype(ACT)
            merged = merged + _sigmoid(pf(glo, D)) * br
        mb = merged.astype(MXU)
        m_ref[...] = mb
        xmid_ref[...] = x_ref[...] + _dot(mb, wo_ref[...].reshape(D, D))

    row = lambda n: pl.BlockSpec((TM, n), lambda i: (i, 0))
    br_spec = _const((N_DEV, WA, NB_BR), (0, 0, 0))
    return _call(
        body, comms, *_grid_marks(nt), (x, p, wpool, pscale, gsgu, wsp, bsp_t, convc, wa_all, wb_all, wc_all, wo_all),
        name=name, grid=(nt,),
        in_specs=[row(D), row(NCOL), _const((HEADS, CHUNK, CHUNK), (0, 0, 0)), _const((1, WA), (0, 0)),
                  _const((1, WA), (0, 0)), _const((HEADS, CHUNK, CHUNK), (0, 0, 0)), _const((CHUNK, HEADS), (0, 0)),
                  _const((3, WA), (0, 0)), br_spec, br_spec, br_spec,
                  _const((N_DEV, ROWS_O, D), (0, 0, 0))],
        out_specs=[row(D), row(3 * WA), row(2 * WA), row(3 * D), row(D)],
        out_shape=[_sds((s, D), F32), _sds((s, 3 * WA), ACT), _sds((s, 2 * WA), ACT), _sds((s, 3 * D), ACT),
                   _sds((s, D), MXU)],
        scratch_shapes=[pltpu.VMEM((HALO_POOL, WA), F32), pltpu.VMEM((HALO_CONV, WA), F32)],
        compiler_params=_params(),
    )


def _conv_up(ext, cur, w_ref, j, b_row):
    return (w_ref[j, 0:1, :] * _shift_down(ext, 2)[HALO_CONV:, :] + w_ref[j, 1:2, :] * _shift_down(ext, 1)[HALO_CONV:, :]
            + w_ref[j, 2:3, :] * cur + b_row)


def _ffn_fwd(xmid, upre, convf_all, convb, wd_all, name):
    s = xmid.shape[0]
    nt = s // TM
    half = N_DEV // 2

    def body(x_ref, u_ref, cw_ref, cb_ref, wd_ref, xo_ref, act_ref, up_ref, carry):
        i = pl.program_id(0)

        @pl.when(i == 0)
        def _():
            carry[...] = jnp.zeros_like(carry)

        def conv(j):
            cur = u_ref[j].astype(F32)
            ext = jnp.concatenate([carry[j], cur], axis=0)
            carry[j] = cur[TM - HALO_CONV:, :]
            up = _conv_up(ext, cur, cw_ref, j, cb_ref[j:j + 1, :])
            up_ref[j] = up.astype(ACT)
            return up

        acc = x_ref[...]
        for j in range(half):
            gate = conv(j)
            val = conv(j + half)
            act = (gate * _sigmoid(gate) * val).astype(MXU)
            act_ref[j] = act
            wd = jnp.concatenate([wd_ref[2 * j], wd_ref[2 * j + 1]], axis=0)
            acc = acc + _dot(act, wd)
        xo_ref[...] = acc

    blocks = pl.BlockSpec((N_DEV, TM, NB_UP), lambda i: (0, i, 0))
    return pl.pallas_call(
        body, name=name, grid=(nt,),
        in_specs=[pl.BlockSpec((TM, D), lambda i: (i, 0)), blocks,
                  _const((N_DEV, 3, NB_UP), (0, 0, 0)), _const((N_DEV, NB_UP), (0, 0)),
                  _const((N_DEV, ROWS_DN, D), (0, 0, 0))],
        out_specs=[pl.BlockSpec((TM, D), lambda i: (i, 0)), pl.BlockSpec((half, TM, NB_UP), lambda i: (0, i, 0)), blocks],
        out_shape=[_sds((s, D), F32), _sds((half, s, NB_UP), MXU), _sds((N_DEV, s, NB_UP), ACT)],
        scratch_shapes=[pltpu.VMEM((N_DEV, HALO_CONV, NB_UP), F32)],
        compiler_params=_params(),
    )(xmid, upre, convf_all, convb, wd_all)


def _loss_head(x, g, target, name):
    s = x.shape[0]
    nt = s // TM

    def body(x_ref, g_ref, t_ref, dx_ref, dg_ref, loss_ref):
        i = pl.program_id(0)

        @pl.when(i == 0)
        def _():
            dg_ref[...] = jnp.zeros_like(dg_ref)
            loss_ref[...] = jnp.zeros_like(loss_ref)

        xf = x_ref[...]
        r = lax.rsqrt(jnp.mean(xf * xf, axis=-1, keepdims=True) + EPS)
        xn = xf * r
        err = xn * g_ref[...] - t_ref[...]
        loss_ref[...] += 0.5 * jnp.sum(jnp.mean(err * err, axis=-1, keepdims=True), axis=0, keepdims=True)
        dy = err * (1.0 / D)
        dg_ref[0:1, :] += _colsum(dy * xn)
        dyg = dy * g_ref[...]
        dx_ref[...] = r * (dyg - xn * jnp.mean(dyg * xn, axis=-1, keepdims=True))

    return pl.pallas_call(
        body, name=name, grid=(nt,),
        in_specs=[pl.BlockSpec((TM, D), lambda i: (i, 0)), _const((1, D), (0, 0)), pl.BlockSpec((TM, D), lambda i: (i, 0))],
        out_specs=[pl.BlockSpec((TM, D), lambda i: (i, 0)), pl.BlockSpec((8, D), lambda i: (0, 0)),
                   pl.BlockSpec((1, 1), lambda i: (0, 0))],
        out_shape=[_sds((s, D), F32), _sds((8, D), F32), _sds((1, 1), F32)],
        compiler_params=_params(),
    )(x, g, target)


def _ffn_bwd(dxo, upre, up, convf_all, wd_all, name, comms=None):
    s = dxo.shape[0]
    nt = s // TM
    half = N_DEV // 2

    def body(dx_ref, u_ref, up_ref, cw_ref, wd_ref, du_ref, dc_ref, carry):
        step = pl.program_id(0)

        @pl.when(step == 0)
        def _():
            carry[...] = jnp.zeros_like(carry)
            dc_ref[...] = jnp.zeros_like(dc_ref)

        dxb = dx_ref[...].astype(MXU)

        def adjoint(j, d_up):
            cur = u_ref[j].astype(F32)
            ext = jnp.concatenate([d_up, carry[j]], axis=0)
            carry[j] = d_up[:HALO_CONV, :]
            up1 = _shift_up(ext, 1)[:TM, :]
            up2 = _shift_up(ext, 2)[:TM, :]
            du_ref[j] = (cw_ref[j, 2:3, :] * d_up + cw_ref[j, 1:2, :] * up1 + cw_ref[j, 0:1, :] * up2).astype(du_ref.dtype)
            dc_ref[j, 0:1, :] += _colsum(cur * up2)
            dc_ref[j, 1:2, :] += _colsum(cur * up1)
            dc_ref[j, 2:3, :] += _colsum(cur * d_up)
            dc_ref[j, 3:4, :] += _colsum(d_up)

        for j in range(half):
            gate = up_ref[j].astype(F32)
            val = up_ref[j + half].astype(F32)
            sg = _sigmoid(gate)
            wd = jnp.concatenate([wd_ref[2 * j], wd_ref[2 * j + 1]], axis=0)
            dact = _dot_nt(dxb, wd)
            adjoint(j, dact * val * sg * (1.0 + gate * (1.0 - sg)))
            adjoint(j + half, dact * gate * sg)

    blocks = pl.BlockSpec((N_DEV, TM, NB_UP), lambda i: (0, nt - 1 - i, 0))
    return _call(
        body, comms, *_grid_marks(nt), (dxo, upre, up, convf_all, wd_all),
        name=name, grid=(nt,),
        in_specs=[pl.BlockSpec((TM, D), lambda i: (nt - 1 - i, 0)), blocks, blocks,
                  _const((N_DEV, 3, NB_UP), (0, 0, 0)), _const((N_DEV, ROWS_DN, D), (0, 0, 0))],
        out_specs=[blocks, pl.BlockSpec((N_DEV, 8, NB_UP), lambda i: (0, 0, 0))],
        out_shape=[_sds((N_DEV, s, NB_UP), MXU), _sds((N_DEV, 8, NB_UP), F32)],
        scratch_shapes=[pltpu.VMEM((N_DEV, HALO_CONV, NB_UP), F32)],
        compiler_params=_params(),
    )


def _proj_bwd(dy, blocked_dy, w_all, x, g, dres, name, comms=None):
    s = x.shape[0]
    nb = w_all.shape[-1]
    nt = s // TM

    def body(dy_ref, w_ref, x_ref, g_ref, dres_ref, dx_ref, dg_ref):
        i = pl.program_id(0)

        @pl.when(i == 0)
        def _():
            dg_ref[...] = jnp.zeros_like(dg_ref)

        dh = jnp.zeros((TM, D), F32)
        for j in range(N_DEV):
            dyj = dy_ref[j] if blocked_dy else dy_ref[:, j * nb:(j + 1) * nb]
            dh = dh + _dot_nt(dyj, w_ref[j])
        xf = x_ref[...]
        r = lax.rsqrt(jnp.mean(xf * xf, axis=-1, keepdims=True) + EPS)
        xn = xf * r
        dg_ref[0:1, :] += _colsum(dh * xn)
        dhg = dh * g_ref[...]
        dx_ref[...] = dres_ref[...] + r * (dhg - xn * jnp.mean(dhg * xn, axis=-1, keepdims=True))

    if blocked_dy:
        dy_spec = pl.BlockSpec((N_DEV, TM, nb), lambda i: (0, i, 0))
    else:
        dy_spec = pl.BlockSpec((TM, N_DEV * nb), lambda i: (i, 0))
    row = pl.BlockSpec((TM, D), lambda i: (i, 0))
    return _call(
        body, comms, *_grid_marks(nt), (dy, w_all, x, g, dres),
        name=name, grid=(nt,),
        in_specs=[dy_spec, _const((N_DEV, D, nb), (0, 0, 0)), row, _const((1, D), (0, 0)), row],
        out_specs=[row, pl.BlockSpec((8, D), lambda i: (0, 0))],
        out_shape=[_sds((s, D), F32), _sds((8, D), F32)],
        compiler_params=_params(),
    )


def _mixer_bwd(dxmid, p, yabc, pacz, babc, wpool, pscale, gsgu, wsp, bsp_t, convc, wa_all, wb_all, wc_all, wo_all,
               name):
    s = dxmid.shape[0]
    nt = s // TM

    def body(dx_ref, p_ref, y_ref, pz_ref, b_ref, wpool_ref, ps_ref, gs_ref, wsp_ref, bsp_ref, cc_ref,
             wa_ref, wb_ref, wc_ref, wo_ref,
             dp_ref, db_ref, dwp_ref, dws_ref, small_ref, dbs_ref,
             carry_pa, carry_cz, dbs_acc, du_s, dvn_s):
        step = pl.program_id(0)
        tile = nt - 1 - step

        @pl.when(step == 0)
        def _():
            for ref in (carry_pa, carry_cz, dbs_acc, dwp_ref, dws_ref, small_ref, dbs_ref):
                ref[...] = jnp.zeros_like(ref)

        def pf(lo, n):
            return p_ref[:, lo:lo + n].astype(F32)

        dm = _dot_nt(dx_ref[...].astype(MXU), wo_ref[...].reshape(D, D))

        def through_gate(k, glo, w_ref):
            sg = _sigmoid(pf(glo, D))
            br = b_ref[:, k * D:(k + 1) * D].astype(F32)
            dp_ref[:, glo:glo + D] = (dm * br * sg * (1.0 - sg)).astype(dp_ref.dtype)
            dbr = (dm * sg).astype(MXU)
            db_ref[:, k * D:(k + 1) * D] = dbr
            return _dot_nt(dbr, _lane_cat(w_ref))

        dya = through_gate(0, GA0, wa_ref)
        dyb = through_gate(1, GB0, wb_ref)
        dyc = through_gate(2, GC0, wc_ref)

        t_pos = (tile * TM + lax.broadcasted_iota(jnp.int32, (TM, 1), 0)).astype(F32)
        for g, win in enumerate(POOL_WINDOWS):
            cols = slice(g * CHUNK, (g + 1) * CHUNK)
            pa_g = pz_ref[:, cols]
            q = _dot(pa_g, wpool_ref[g])
            dya_g = dya[:, cols]
            small_ref[0:1, cols] += _colsum(dya_g * q)
            dq = (dya_g * ps_ref[:, cols]).astype(MXU)
            dpa_g = _dot_nt(dq, wpool_ref[g])
            dwp_ref[g] += _dot_tn(pa_g, dq)
            dpw = dpa_g / jnp.minimum(t_pos + 1.0, float(win))
            acc = jnp.concatenate([dpw, carry_pa[:, cols]], axis=0)
            carry_pa[:, cols] = dpw[:HALO_POOL, :]
            k = 1
            while k < win:
                acc = acc + _shift_up(acc, k)
                k *= 2
            dp_ref[:, cols] = (acc[:TM, :] - dpa_g).astype(dp_ref.dtype)

        uvp = pf(UV0, 2 * WA)
        uvg, dgelu = _gelu_parts(uvp)
        u = uvg[:, :WA]
        v = uvg[:, WA:]
        rv = lax.rsqrt(jnp.mean(v * v, axis=-1, keepdims=True) + EPS)
        vh = v * rv
        vn = (vh * gs_ref[...]).astype(MXU)
        mask = _tril_mask()
        for g in range(HEADS):
            cols = slice(g * CHUNK, (g + 1) * CHUNK)
            wt32 = jnp.where(mask, wsp_ref[g], 0.0)
            wt = wt32.astype(MXU)
            wt_t = wt32.T.astype(MXU)
            bcol = bsp_ref[:, g:g + 1]
            for c in range(TM // CHUNK):
                rows = slice(c * CHUNK, (c + 1) * CHUNK)
                vn_cg = vn[rows, cols]
                sv = _dot(wt, vn_cg) + bcol
                dyb_cg = dyb[rows, cols]
                du_s[rows, cols] = dyb_cg * sv
                dsv = dyb_cg * u[rows, cols]
                dbs_acc[g] += dsv
                dsv_b = dsv.astype(MXU)
                dws_ref[g] += _dot_nt(dsv_b, vn_cg)
                dvn_s[rows, cols] = _dot(wt_t, dsv_b)
        dvn = dvn_s[...]
        small_ref[1:2, :] += _colsum(dvn * vh)
        dvg = dvn * gs_ref[...]
        dv = rv * (dvg - vh * jnp.mean(dvg * vh, axis=-1, keepdims=True))
        dp_ref[:, UV0:UV0 + WA] = (du_s[...] * dgelu[:, :WA]).astype(dp_ref.dtype)
        dp_ref[:, UV0 + WA:UV0 + 2 * WA] = (dv * dgelu[:, WA:]).astype(dp_ref.dtype)

        cb = pf(CB0, WA)
        cc = pf(CC0, WA)
        cx = pf(CX0, WA)
        z = cc * cx
        dp_ref[:, CB0:CB0 + WA] = (dyc * pz_ref[:, WA:2 * WA].astype(F32)).astype(dp_ref.dtype)
        dcz = dyc * cb
        extz = jnp.concatenate([dcz, carry_cz[...]], axis=0)
        carry_cz[...] = dcz[:HALO_CONV, :]
        up1 = _shift_up(extz, 1)[:TM, :]
        up2 = _shift_up(extz, 2)[:TM, :]
        dz = cc_ref[2:3, :] * dcz + cc_ref[1:2, :] * up1 + cc_ref[0:1, :] * up2
        small_ref[2:3, :] += _colsum(z * up2)
        small_ref[3:4, :] += _colsum(z * up1)
        small_ref[4:5, :] += _colsum(z * dcz)
        dp_ref[:, CC0:CC0 + WA] = (dz * cx).astype(dp_ref.dtype)
        dp_ref[:, CX0:CX0 + WA] = (dz * cc).astype(dp_ref.dtype)

        @pl.when(step == nt - 1)
        def _():
            ones = jnp.ones((8, CHUNK), F32)
            for g in range(HEADS):
                dws_ref[g] = jnp.where(mask, dws_ref[g], 0.0)
                row = lax.dot_general(ones, dbs_acc[g], (((1,), (1,)), ((), ())), preferred_element_type=F32,
                                      precision=lax.Precision.HIGHEST)
                dbs_ref[g:g + 1, :] = row[0:1, :]

    row = lambda n: pl.BlockSpec((TM, n), lambda i: (nt - 1 - i, 0))
    br_spec = _const((N_DEV, WA, NB_BR), (0, 0, 0))
    acc_spec = lambda shape: pl.BlockSpec(shape, lambda i: (0,) * len(shape))
    return pl.pallas_call(
        body, name=name, grid=(nt,),
        in_specs=[row(D), row(NCOL), row(3 * WA), row(2 * WA), row(3 * D),
                  _const((HEADS, CHUNK, CHUNK), (0, 0, 0)), _const((1, WA), (0, 0)), _const((1, WA), (0, 0)),
                  _const((HEADS, CHUNK, CHUNK), (0, 0, 0)), _const((CHUNK, HEADS), (0, 0)), _const((3, WA), (0, 0)),
                  br_spec, br_spec, br_spec, _const((N_DEV, ROWS_O, D), (0, 0, 0))],
        out_specs=[row(NCOL), row(3 * D), acc_spec((HEADS, CHUNK, CHUNK)), acc_spec((HEADS, CHUNK, CHUNK)),
                   acc_spec((8, WA)), acc_spec((8, CHUNK))],
        out_shape=[_sds((s, NCOL), MXU), _sds((s, 3 * D), MXU), _sds((HEADS, CHUNK, CHUNK), F32),
                   _sds((HEADS, CHUNK, CHUNK), F32), _sds((8, WA), F32), _sds((8, CHUNK), F32)],
        scratch_shapes=[pltpu.VMEM((HALO_POOL, WA), F32), pltpu.VMEM((HALO_CONV, WA), F32),
                        pltpu.VMEM((HEADS, CHUNK, CHUNK), F32), pltpu.VMEM((TM, WA), F32), pltpu.VMEM((TM, WA), F32)],
        compiler_params=_params(),
    )(dxmid, p, yabc, pacz, babc, wpool, pscale, gsgu, wsp, bsp_t, convc, wa_all, wb_all, wc_all, wo_all)


def _wgrad(a, b, a_spec, b_spec, n_out, acc_shape, out_shape, out_spec, store, name, comms=None):
    s = a.shape[-2]
    ts = min(TS_WGRAD, s)
    n_steps = s // ts

    def body(a_ref, b_ref, o_ref, acc_ref):
        k = pl.program_id(1)

        @pl.when(k == 0)
        def _():
            acc_ref[...] = jnp.zeros_like(acc_ref)

        acc_ref[...] += _dot_tn(a_ref[...], b_ref[...].astype(MXU))

        @pl.when(k == n_steps - 1)
        def _():
            store(o_ref, acc_ref)

    (res,), extra = _call(
        body, comms, lambda: (pl.program_id(0) == 0) & (pl.program_id(1) == 0),
        lambda: (pl.program_id(0) == n_out // 2) & (pl.program_id(1) == 0),
        lambda: (pl.program_id(0) == n_out - 1) & (pl.program_id(1) == n_steps - 1), (a, b),
        name=name, grid=(n_out, n_steps),
        in_specs=[a_spec(ts), b_spec(ts)], out_specs=[out_spec], out_shape=[_sds(out_shape, WIRE)],
        scratch_shapes=[pltpu.VMEM(acc_shape, F32)],
        compiler_params=_params(2),
    )
    return res, extra


def _store_plain(o_ref, acc_ref):
    o_ref[...] = acc_ref[...].astype(o_ref.dtype)


def _store_lane_blocks(o_ref, acc_ref):
    for d in range(N_DEV):
        o_ref[d] = acc_ref[:, d * NB_BR:(d + 1) * NB_BR].astype(o_ref.dtype)


def _wgrad_in(h, dp, name, comm=None):
    return _wgrad(h, dp, lambda ts: pl.BlockSpec((ts, D), lambda j, k: (k, 0)),
                  lambda ts: pl.BlockSpec((ts, NB_IN), lambda j, k: (k, j)), N_DEV, (D, NB_IN),
                  (N_DEV, D, NB_IN), pl.BlockSpec((None, D, NB_IN), lambda j, k: (j, 0, 0)), _store_plain, name, comm)


def _wgrad_up(h, du, name, comm=None):
    return _wgrad(h, du, lambda ts: pl.BlockSpec((ts, D), lambda j, k: (k, 0)),
                  lambda ts: pl.BlockSpec((None, ts, NB_UP), lambda j, k: (j, k, 0)), N_DEV, (D, NB_UP),
                  (N_DEV, D, NB_UP), pl.BlockSpec((None, D, NB_UP), lambda j, k: (j, 0, 0)), _store_plain, name, comm)


def _wgrad_down(act, dxo, name, comm=None):
    return _wgrad(act, dxo, lambda ts: pl.BlockSpec((None, ts, NB_UP), lambda j, k: (j, k, 0)),
                  lambda ts: pl.BlockSpec((ts, D), lambda j, k: (k, 0)), N_DEV // 2, (NB_UP, D),
                  (DFF, D), pl.BlockSpec((NB_UP, D), lambda j, k: (j, 0)), _store_plain, name, comm)


def _wgrad_o(merged, dxmid, name, comm=None):
    return _wgrad(merged, dxmid, lambda ts: pl.BlockSpec((ts, D), lambda j, k: (k, 0)),
                  lambda ts: pl.BlockSpec((ts, D), lambda j, k: (k, 0)), 1, (D, D),
                  (D, D), pl.BlockSpec((D, D), lambda j, k: (0, 0)), _store_plain, name, comm)


def _wgrad_branches(yabc, dbabc, name, comm=None):
    return _wgrad(yabc, dbabc, lambda ts: pl.BlockSpec((ts, WA), lambda j, k: (k, j)),
                  lambda ts: pl.BlockSpec((ts, D), lambda j, k: (k, j)), 3, (WA, D),
                  (N_DEV, 3, WA, NB_BR), pl.BlockSpec((N_DEV, None, WA, NB_BR), lambda j, k: (0, j, 0, 0)),
                  _store_lane_blocks, name, comm)


def _adamw_math(g, w, m, v):
    m = ADAM_B1 * m + (1.0 - ADAM_B1) * g
    v = ADAM_B2 * v + (1.0 - ADAM_B2) * (g * g)
    m_hat = m / (1.0 - ADAM_B1 ** ADAM_STEP)
    v_hat = v / (1.0 - ADAM_B2 ** ADAM_STEP)
    delta = -ADAM_LR * (m_hat / (jnp.sqrt(v_hat) + ADAM_EPS) + ADAM_WD * w)
    return delta, m, v


def _adamw_sum(parts, mid, w, m, v, layer, prev, tr, name):
    n_layers, r, c = w.shape

    def body(p_ref, w_ref, m_ref, v_ref, *rest):
        g_ref, d_ref, mo_ref, vo_ref = rest[-4:]
        g = p_ref[0].astype(F32)
        for k in range(1, N_DEV):
            g = g + p_ref[k].astype(F32)
        g_ref[...] = g
        d_ref[...], mo_ref[...], vo_ref[...] = _adamw_math(g, w_ref[...], m_ref[...], v_ref[...])

    blk = pl.BlockSpec((None, tr, c), lambda i: (layer, i, 0))
    extra = [] if prev is None else list(prev)
    return pl.pallas_call(
        body, name=name, grid=(r // tr,),
        in_specs=[pl.BlockSpec((N_DEV, None, tr, c), lambda i: (0, mid, i, 0)), blk, blk, blk]
        + [pl.BlockSpec(memory_space=pl.ANY)] * len(extra),
        out_specs=[blk] * 4, out_shape=[_sds((n_layers, r, c), F32)] * 4,
        input_output_aliases={4 + k: k for k in range(len(extra))},
        compiler_params=_params(),
    )(parts, w, m, v, *extra)


def _sum_parts(parts, name):
    _, r, c = parts.shape

    def body(p_ref, o_ref):
        g = p_ref[0]
        for k in range(1, N_DEV):
            g = g + p_ref[k]
        o_ref[...] = g

    return pl.pallas_call(body, name=name, out_shape=_sds((r, c), F32),
                          compiler_params=pltpu.CompilerParams(vmem_limit_bytes=VMEM_LIMIT))(parts)


def _adamw_small(g, w, m, v, name):
    def body(g_ref, w_ref, m_ref, v_ref, d_ref, mo_ref, vo_ref):
        d_ref[...], mo_ref[...], vo_ref[...] = _adamw_math(g_ref[...], w_ref[...], m_ref[...], v_ref[...])

    return pl.pallas_call(body, name=name, out_shape=[_sds(w.shape, F32)] * 3)(g, w, m, v)


HBM_SPEC = pl.BlockSpec(memory_space=pltpu.HBM)


def _position():
    return lax.axis_index("x"), lax.axis_index("y"), lax.axis_index("c")


def _device_index(chip, core):
    return 4 * chip[0] + 2 * chip[1] + core


def _gather(shards, layer):
    n = len(shards)
    per = 8

    def first_copies(ins, outs, send, recv):
        x, y, c = _position()
        me = 4 * x + 2 * y + c
        targets = [(x, y, 1 - c), (1 - x, y, c), (x, 1 - y, c), (1 - x, 1 - y, c)]
        remote = [pltpu.make_async_remote_copy(
            src_ref=ins[t].at[layer], dst_ref=outs[t].at[me], send_sem=send.at[per * t + k],
            recv_sem=recv.at[per * t + k], device_id=to, device_id_type=MESH)
            for t in range(n) for k, to in enumerate(targets)]
        local = [pltpu.make_async_copy(ins[t].at[layer], outs[t].at[me], send.at[per * t + 4]) for t in range(n)]
        return remote, local

    def passed_on(outs, send, recv):
        x, y, c = _position()
        chips = [(1 - x, y), (x, 1 - y), (1 - x, 1 - y)]
        return [pltpu.make_async_remote_copy(
            src_ref=outs[t].at[_device_index(chip, c)], dst_ref=outs[t].at[_device_index(chip, c)],
            send_sem=send.at[per * t + 5 + j], recv_sem=recv.at[per * t + 5 + j], device_id=(x, y, 1 - c),
            device_id_type=MESH)
            for t in range(n) for j, chip in enumerate(chips)]

    def start(ins, outs, send, recv):
        remote, local = first_copies(ins, outs, send, recv)
        for cp in local + remote:
            cp.start()

    def mid(ins, outs, send, recv):
        remote, local = first_copies(ins, outs, send, recv)
        for cp in remote:
            cp.wait()
        for cp in local:
            cp.wait()
        for cp in passed_on(outs, send, recv):
            cp.start()

    def finish(ins, outs, send, recv):
        for cp in passed_on(outs, send, recv):
            cp.wait()

    return _Comm(shards, [_sds((N_DEV,) + a.shape[1:], a.dtype) for a in shards], per * n, start, finish, mid)


def _run_comms(comms, name):
    def body():
        pass

    _, extra = _call(body, comms, None, None, None, (), name=name, in_specs=[], out_specs=[], out_shape=[])
    return extra


def _exchange(parts):
    n = len(parts)
    per = 8
    flips = [(0, 0, 1), (1, 0, 0), (1, 0, 1), (0, 1, 0), (0, 1, 1), (1, 1, 0), (1, 1, 1)]

    def copies(ins, outs, send, recv):
        x, y, c = _position()
        me = 4 * x + 2 * y + c
        remote = []
        for t in range(n):
            for k, (fx, fy, fc) in enumerate(flips):
                peer = ((1 - x if fx else x), (1 - y if fy else y), (1 - c if fc else c))
                remote.append(pltpu.make_async_remote_copy(
                    src_ref=ins[t].at[_device_index(peer[:2], peer[2])], dst_ref=outs[t].at[me],
                    send_sem=send.at[per * t + k], recv_sem=recv.at[per * t + k], device_id=peer, device_id_type=MESH))
        local = [pltpu.make_async_copy(ins[t].at[me], outs[t].at[me], send.at[per * t + 7]) for t in range(n)]
        return remote, local

    def start(ins, outs, send, recv):
        remote, local = copies(ins, outs, send, recv)
        for cp in local + remote:
            cp.start()

    def finish(ins, outs, send, recv):
        remote, local = copies(ins, outs, send, recv)
        for cp in remote:
            cp.wait()
        for cp in local:
            cp.wait()

    return _Comm(parts, [_sds(a.shape, a.dtype) for a in parts], per * n, start, finish)


def _rows128(a):
    return a.reshape(-1, 128)


def kernel(x, g_mix, w_in, w_pool, pool_scale, g_sgu, w_spatial, b_spatial, conv_c, w_branch_a, w_branch_b, w_branch_c, w_o, g_ffn, w_up, conv_ffn, conv_ffn_b, w_down, g_final, loss_target, m_g_mix, m_w_in, m_w_pool, m_pool_scale, m_g_sgu, m_w_spatial, m_b_spatial, m_conv_c, m_w_branch_a, m_w_branch_b, m_w_branch_c, m_w_o, m_g_ffn, m_w_up, m_conv_ffn, m_conv_ffn_b, m_w_down, m_g_final, v_g_mix, v_w_in, v_w_pool, v_pool_scale, v_g_sgu, v_w_spatial, v_b_spatial, v_conv_c, v_w_branch_a, v_w_branch_b, v_w_branch_c, v_w_o, v_g_ffn, v_w_up, v_conv_ffn, v_conv_ffn_b, v_w_down, v_g_final):
    s = x.shape[1]
    n_layers = g_mix.shape[0]
    x0 = x.reshape(s, D)
    target = loss_target.reshape(s, D)
    me = 4 * lax.axis_index("x") + 2 * lax.axis_index("y") + lax.axis_index("c")

    first_shards = [w_in.astype(MXU), conv_c]
    rest_shards = [w_branch_a.astype(MXU), w_branch_b.astype(MXU), w_branch_c.astype(MXU), w_o.astype(MXU),
                   w_up.astype(MXU), w_down.astype(MXU), conv_ffn]
    (first_now,) = _run_comms([_gather(first_shards, 0)], "gather_first_0")
    rest_now = None
    wpool_b = w_pool.astype(MXU)
    bsp_t = jnp.swapaxes(b_spatial, 1, 2)
    convb_blk = conv_ffn_b.reshape(n_layers, N_DEV, NB_UP)

    saved = []
    weights = []
    xl = x0
    for l in range(n_layers):
        win8, convc8 = first_now
        convc_full = jnp.transpose(convc8, (1, 0, 2)).reshape(3, WA)
        more = l + 1 < n_layers
        (p, h), got = _rms_proj(xl, g_mix[l:l + 1], win8, False, f"in_proj_{l}",
                                [_gather(rest_shards, 0)] if rest_now is None else None)
        if rest_now is None:
            (rest_now,) = got
        wa8, wb8, wc8, wo8, wup8, wd8, convf8 = rest_now
        weights.append((win8, wa8, wb8, wc8, wo8, wup8, wd8, convc_full, convf8))
        (xmid, yabc, pacz, babc, merged), got = _mixer_fwd(
            xl, p, wpool_b[l], pool_scale[l:l + 1], g_sgu[l:l + 1], w_spatial[l], bsp_t[l], convc_full,
            wa8, wb8, wc8, wo8, f"mixer_fwd_{l}", [_gather(first_shards, l + 1)] if more else None)
        if more:
            (first_now,) = got
        (upre, h2), got = _rms_proj(xmid, g_ffn[l:l + 1], wup8, True, f"up_proj_{l}",
                                    [_gather(rest_shards, l + 1)] if more else None)
        if more:
            (rest_now,) = got
        xout, act, up = _ffn_fwd(xmid, upre, convf8, convb_blk[l], wd8, f"ffn_fwd_{l}")
        saved.append((xl, p, h, xmid, yabc, pacz, babc, merged, upre, h2, act, up))
        xl = xout

    dx, dg_final, loss_local = _loss_head(xl, g_final.reshape(1, D), target, "loss_head")

    received = [dict() for _ in range(n_layers)]
    small = {("final", "g_final"): dg_final}
    small_sums = {}
    waiting = None

    def exchange_of(named):
        return [_exchange([a for _, a in named])]

    def land(layer, named, got):
        received[layer].update({k: a for (k, _), a in zip(named, got[0])})

    def gather_small(keys):
        packed = jnp.concatenate([_rows128(small[k]) for k in keys], axis=0)[None]
        return _gather([packed], 0)

    def sum_small(keys, gathered, name):
        summed = _sum_parts(gathered, name)
        row = 0
        for k in keys:
            n_rows = small[k].size // 128
            small_sums[k] = summed[row:row + n_rows].reshape(small[k].shape)
            row += n_rows

    for l in reversed(range(n_layers)):
        xin, p, h, xmid, yabc, pacz, babc, merged, upre, h2, act, up = saved[l]
        win8, wa8, wb8, wc8, wo8, wup8, wd8, convc_full, convf8 = weights[l]
        rides = l == 0
        (dupre, dconvf), got = _ffn_bwd(dx, upre, up, convf8, wd8, f"ffn_bwd_{l}",
                                        None if waiting is None else exchange_of(waiting[1]))
        if waiting is not None:
            land(waiting[0], waiting[1], got)
        g_wdown, _ = _wgrad_down(act, dx, f"wgrad_down_{l}")
        down = [("w_down", g_wdown.reshape(N_DEV, ROWS_DN, D))]
        g_wup, got = _wgrad_up(h2, dupre, f"wgrad_up_{l}", exchange_of(down) if rides else None)
        if rides:
            land(l, down, got)
        upw = [("w_up", g_wup)]
        (dxmid, dg_ffn), got = _proj_bwd(dupre, True, wup8, xmid, g_ffn[l:l + 1], dx, f"up_proj_bwd_{l}",
                                         exchange_of(upw) if rides else None)
        if rides:
            land(l, upw, got)
        dp, dbabc, dwp, dws, mixer_small, dbs = _mixer_bwd(
            dxmid, p, yabc, pacz, babc, wpool_b[l], pool_scale[l:l + 1], g_sgu[l:l + 1], w_spatial[l], bsp_t[l],
            convc_full, wa8, wb8, wc8, wo8, f"mixer_bwd_{l}")
        small.update({(l, "g_ffn"): dg_ffn, (l, "conv_ffn"): dconvf, (l, "w_pool"): dwp,
                      (l, "mixer_small"): mixer_small, (l, "w_spatial"): dws, (l, "b_spatial"): dbs})
        g_wo, _ = _wgrad_o(merged, dxmid, f"wgrad_o_{l}")
        g_br, _ = _wgrad_branches(yabc, dbabc, f"wgrad_branches_{l}")
        mixer_w = [("branches", g_br), ("w_o", g_wo.reshape(N_DEV, ROWS_O, D))]
        early_keys = list(small)
        g_win, got = _wgrad_in(h, dp, f"wgrad_in_{l}", exchange_of(mixer_w) + [gather_small(early_keys)] if rides else None)
        if rides:
            land(l, mixer_w, got)
            sum_small(early_keys, got[1][0], "sum_small_grads")
        inw = [("w_in", g_win)]
        (dx, dg_mix), got = _proj_bwd(dp, False, win8, xin, g_mix[l:l + 1], dxmid, f"in_proj_bwd_{l}",
                                      exchange_of(inw) if rides else None)
        if rides:
            land(l, inw, got)
        else:
            waiting = (l, down + upw + mixer_w + inw)
        small[(l, "g_mix")] = dg_mix
    grad_x = dx.reshape(1, s, D)
    late_keys = [k for k in small if k not in small_sums]
    (gathered_late,) = _run_comms([gather_small(late_keys)], "gather_last_small_grads")[0]
    sum_small(late_keys, gathered_late, "sum_last_small_grads")

    def update_big(key, mid, w, m, v, tr, tag):
        outs = None
        for l in range(n_layers):
            parts = received[l][key]
            if parts.ndim == 3:
                parts = parts.reshape(N_DEV, 1, *parts.shape[1:])
            outs = _adamw_sum(parts, mid, w, m, v, l, outs, tr, f"adamw_{tag}_{l}")
        return outs

    up_in = update_big("w_in", 0, w_in, m_w_in, v_w_in, 256, "w_in")
    up_a = update_big("branches", 0, w_branch_a, m_w_branch_a, v_w_branch_a, WA, "w_branch_a")
    up_b = update_big("branches", 1, w_branch_b, m_w_branch_b, v_w_branch_b, WA, "w_branch_b")
    up_c = update_big("branches", 2, w_branch_c, m_w_branch_c, v_w_branch_c, WA, "w_branch_c")
    up_o = update_big("w_o", 0, w_o, m_w_o, v_w_o, ROWS_O, "w_o")
    up_up = update_big("w_up", 0, w_up, m_w_up, v_w_up, 256, "w_up")
    up_down = update_big("w_down", 0, w_down, m_w_down, v_w_down, ROWS_DN, "w_down")

    stack = lambda kind: jnp.stack([small_sums[(l, kind)] for l in range(n_layers)], axis=0)
    grad_g_mix = stack("g_mix")[:, 0, :]
    grad_w_pool = stack("w_pool")
    mixer_sums = stack("mixer_small")
    grad_pool_scale = mixer_sums[:, 0, :]
    grad_g_sgu = mixer_sums[:, 1, :]
    grad_conv_c = lax.dynamic_slice_in_dim(mixer_sums[:, 2:5, :], me * (WA // N_DEV), WA // N_DEV, axis=2)
    grad_w_spatial = stack("w_spatial")
    grad_b_spatial = stack("b_spatial")[:, 0:HEADS, :]
    grad_g_ffn = stack("g_ffn")[:, 0, :]
    conv_grads = stack("conv_ffn")
    grad_conv_ffn = lax.dynamic_index_in_dim(conv_grads, me, axis=1, keepdims=False)[:, 0:3, :]
    grad_conv_ffn_b = conv_grads[:, :, 3, :].reshape(n_layers, 2 * DFF)
    grad_g_final = small_sums[("final", "g_final")][0]

    def update_small(g, w, m, v, tag):
        shape2 = (-1, w.shape[-1])
        outs = _adamw_small(g.reshape(shape2), w.reshape(shape2), m.reshape(shape2), v.reshape(shape2), f"adamw_{tag}")
        return [g] + [o.reshape(w.shape) for o in outs]

    up = {
        "g_mix": update_small(grad_g_mix, g_mix, m_g_mix, v_g_mix, "g_mix"),
        "w_in": up_in,
        "w_pool": update_small(grad_w_pool, w_pool, m_w_pool, v_w_pool, "w_pool"),
        "pool_scale": update_small(grad_pool_scale, pool_scale, m_pool_scale, v_pool_scale, "pool_scale"),
        "g_sgu": update_small(grad_g_sgu, g_sgu, m_g_sgu, v_g_sgu, "g_sgu"),
        "w_spatial": update_small(grad_w_spatial, w_spatial, m_w_spatial, v_w_spatial, "w_spatial"),
        "b_spatial": update_small(grad_b_spatial, b_spatial, m_b_spatial, v_b_spatial, "b_spatial"),
        "conv_c": update_small(grad_conv_c, conv_c, m_conv_c, v_conv_c, "conv_c"),
        "w_branch_a": up_a,
        "w_branch_b": up_b,
        "w_branch_c": up_c,
        "w_o": up_o,
        "g_ffn": update_small(grad_g_ffn, g_ffn, m_g_ffn, v_g_ffn, "g_ffn"),
        "w_up": up_up,
        "conv_ffn": update_small(grad_conv_ffn, conv_ffn, m_conv_ffn, v_conv_ffn, "conv_ffn"),
        "conv_ffn_b": update_small(grad_conv_ffn_b, conv_ffn_b, m_conv_ffn_b, v_conv_ffn_b, "conv_ffn_b"),
        "w_down": up_down,
        "g_final": update_small(grad_g_final, g_final, m_g_final, v_g_final, "g_final"),
    }
    loss = lax.psum(loss_local[0, 0], AXES)
    order = list(up)
    return (loss, grad_x, *[up[k][0] for k in order], *[up[k][1] for k in order], *[up[k][2] for k in order],
            *[up[k][3] for k in order])
```

```python
import functools

import jax
import jax.numpy as jnp
from jax import lax
from jax.experimental import pallas as pl
from jax.experimental.pallas import tpu as pltpu

F32 = jnp.float32
BF16 = jnp.bfloat16
MXU = BF16
ACT = BF16
WIRE = BF16

N_DEV = 8
D = 1024
WA = 512
NCOL = 6144
DFF = 2816
NB_IN = NCOL // N_DEV
NB_UP = 2 * DFF // N_DEV
NB_BR = D // N_DEV
ROWS_O = D // N_DEV
ROWS_DN = DFF // N_DEV
CHUNK = 128
HEADS = 4
POOL_WINDOWS = (2, 4, 8, 16)
EPS = 1e-6
A0, UV0, CB0, CC0, CX0, GA0, GB0, GC0 = 0, 512, 1536, 2048, 2560, 3072, 4096, 5120

ADAM_LR = 0.001
ADAM_B1 = 0.9
ADAM_B2 = 0.999
ADAM_EPS = 1e-08
ADAM_WD = 0.01
ADAM_STEP = 10

TM = 256
TS_WGRAD = 4096
HALO_POOL = 16
HALO_CONV = 8
VMEM_LIMIT = 56 * 1024 * 1024
MESH = pl.DeviceIdType.MESH
AXES = ("x", "y", "c")


def _sds(shape, dtype):
    return jax.ShapeDtypeStruct(tuple(shape), dtype)


def _params(n_grid=1):
    return pltpu.CompilerParams(dimension_semantics=("arbitrary",) * n_grid, vmem_limit_bytes=VMEM_LIMIT)


def _const(block, index):
    return pl.BlockSpec(block, lambda *_: index, pipeline_mode=pl.Buffered(1))


def _dot(a, b):
    return jnp.dot(a, b, preferred_element_type=F32)


def _dot_nt(a, b):
    return lax.dot_general(a, b, (((1,), (1,)), ((), ())), preferred_element_type=F32)


def _dot_tn(a, b):
    return lax.dot_general(a, b, (((0,), (0,)), ((), ())), preferred_element_type=F32)


def _sigmoid(v):
    return 0.5 * jnp.tanh(0.5 * v) + 0.5


def _shift_down(v, k):
    return pltpu.roll(v, k, axis=0)


def _shift_up(v, k):
    return pltpu.roll(v, v.shape[0] - k, axis=0)


def _colsum(v):
    return jnp.sum(v, axis=0, keepdims=True)


def _lane_cat(ref):
    return jnp.concatenate([ref[d] for d in range(N_DEV)], axis=1)


class _Comm:
    def __init__(self, operands, out_shapes, n_sems, start, finish, mid=None):
        self.operands = list(operands)
        self.out_shapes = list(out_shapes)
        self.n_sems = n_sems
        self.start = start
        self.mid = mid
        self.finish = finish


def _call(body, comms, is_first, is_mid, is_last, operands, *, in_specs, out_specs, out_shape, scratch_shapes=(), **kw):
    n_in, n_out, n_scr = len(in_specs), len(out_specs), len(scratch_shapes)
    comms = [c for c in (comms or []) if c is not None]
    if not comms:
        res = pl.pallas_call(body, in_specs=in_specs, out_specs=out_specs, out_shape=out_shape,
                             scratch_shapes=list(scratch_shapes), **kw)(*operands)
        return res, []
    nci = [len(c.operands) for c in comms]
    nco = [len(c.out_shapes) for c in comms]

    def split(refs, sizes):
        parts = []
        for n in sizes:
            parts.append(refs[:n])
            refs = refs[n:]
        return parts, refs

    def carrier(*refs):
        ins, refs = refs[:n_in], refs[n_in:]
        cins, refs = split(refs, nci)
        outs, refs = refs[:n_out], refs[n_out:]
        couts, refs = split(refs, nco)
        scr, sems = refs[:n_scr], refs[n_scr:]

        def run(step):
            for k, c in enumerate(comms):
                if getattr(c, step) is not None:
                    getattr(c, step)(cins[k], couts[k], sems[2 * k], sems[2 * k + 1])

        def at(mark, step):
            if mark is None:
                run(step)
            else:
                pl.when(mark())(lambda: run(step))

        at(is_first, "start")
        body(*ins, *outs, *scr)
        at(is_mid, "mid")
        at(is_last, "finish")

    res = pl.pallas_call(
        carrier, in_specs=list(in_specs) + [HBM_SPEC] * sum(nci), out_specs=list(out_specs) + [HBM_SPEC] * sum(nco),
        out_shape=list(out_shape) + [s for c in comms for s in c.out_shapes],
        scratch_shapes=list(scratch_shapes) + [pltpu.SemaphoreType.DMA((c.n_sems,)) for c in comms for _ in range(2)],
        **kw,
    )(*operands, *[a for c in comms for a in c.operands])
    extra, _ = split(res[n_out:], nco)
    return res[:n_out], extra


def _grid_marks(nt):
    return (lambda: pl.program_id(0) == 0), (lambda: pl.program_id(0) == (3 * nt) // 4), (lambda: pl.program_id(0) == nt - 1)


def _rms_proj(x, g, w_all, blocked_out, name, comms=None):
    s = x.shape[0]
    nb = w_all.shape[-1]
    nt = s // TM

    def body(x_ref, g_ref, w_ref, p_ref, h_ref):
        xf = x_ref[...]
        r = lax.rsqrt(jnp.mean(xf * xf, axis=-1, keepdims=True) + EPS)
        h = (xf * r * g_ref[...]).astype(MXU)
        h_ref[...] = h
        for j in range(N_DEV):
            pj = _dot(h, w_ref[j]).astype(p_ref.dtype)
            if blocked_out:
                p_ref[j] = pj
            else:
                p_ref[:, j * nb:(j + 1) * nb] = pj

    if blocked_out:
        p_shape, p_spec = (N_DEV, s, nb), pl.BlockSpec((N_DEV, TM, nb), lambda i: (0, i, 0))
    else:
        p_shape, p_spec = (s, N_DEV * nb), pl.BlockSpec((TM, N_DEV * nb), lambda i: (i, 0))
    return _call(
        body, comms, *_grid_marks(nt), (x, g, w_all),
        name=name, grid=(nt,),
        in_specs=[pl.BlockSpec((TM, D), lambda i: (i, 0)), _const((1, D), (0, 0)),
                  _const((N_DEV, D, nb), (0, 0, 0))],
        out_specs=[p_spec, pl.BlockSpec((TM, D), lambda i: (i, 0))],
        out_shape=[_sds(p_shape, ACT), _sds((s, D), MXU)],
        compiler_params=_params(),
    )


def _tril_mask():
    r = lax.broadcasted_iota(jnp.int32, (CHUNK, CHUNK), 0)
    c = lax.broadcasted_iota(jnp.int32, (CHUNK, CHUNK), 1)
    return r >= c


def _gelu_parts(v):
    c0 = 0.7978845608028654
    th = jnp.tanh(c0 * (v + 0.044715 * (v * v * v)))
    cdf = 0.5 * (1.0 + th)
    dgelu = cdf + v * (0.5 * c0) * (1.0 - th * th) * (1.0 + 3.0 * 0.044715 * (v * v))
    return v * cdf, dgelu


def _mixer_fwd(x, p, wpool, pscale, gsgu, wsp, bsp_t, convc, wa_all, wb_all, wc_all, wo_all, name, comms=None):
    s = x.shape[0]
    nt = s // TM

    def body(x_ref, p_ref, wpool_ref, ps_ref, gs_ref, wsp_ref, bsp_ref, cc_ref, wa_ref, wb_ref, wc_ref, wo_ref,
             xmid_ref, y_ref, pz_ref, b_ref, m_ref, carry_a, carry_z):
        i = pl.program_id(0)

        @pl.when(i == 0)
        def _():
            carry_a[...] = jnp.zeros_like(carry_a)
            carry_z[...] = jnp.zeros_like(carry_z)

        def pf(lo, n):
            return p_ref[:, lo:lo + n].astype(F32)

        a = pf(A0, WA)
        ext = jnp.concatenate([carry_a[...], a], axis=0)
        carry_a[...] = a[TM - HALO_POOL:, :]
        t_pos = (i * TM + lax.broadcasted_iota(jnp.int32, (TM, 1), 0)).astype(F32)
        for g, win in enumerate(POOL_WINDOWS):
            cols = slice(g * CHUNK, (g + 1) * CHUNK)
            acc = ext[:, cols]
            k = 1
            while k < win:
                acc = acc + _shift_down(acc, k)
                k *= 2
            cnt = jnp.minimum(t_pos + 1.0, float(win))
            pa_g = (acc[HALO_POOL:, :] / cnt - a[:, cols]).astype(MXU)
            pz_ref[:, cols] = pa_g
            y_ref[:, cols] = (_dot(pa_g, wpool_ref[g]) * ps_ref[:, cols]).astype(ACT)

        uvg, _ = _gelu_parts(pf(UV0, 2 * WA))
        u = uvg[:, :WA]
        v = uvg[:, WA:]
        rv = lax.rsqrt(jnp.mean(v * v, axis=-1, keepdims=True) + EPS)
        vn = (v * rv * gs_ref[...]).astype(MXU)
        mask = _tril_mask()
        for g in range(HEADS):
            cols = slice(g * CHUNK, (g + 1) * CHUNK)
            wt = jnp.where(mask, wsp_ref[g], 0.0).astype(MXU)
            bcol = bsp_ref[:, g:g + 1]
            for c in range(TM // CHUNK):
                rows = slice(c * CHUNK, (c + 1) * CHUNK)
                sv = _dot(wt, vn[rows, cols]) + bcol
                y_ref[rows, WA + g * CHUNK:WA + (g + 1) * CHUNK] = (u[rows, cols] * sv).astype(ACT)

        z = pf(CC0, WA) * pf(CX0, WA)
        extz = jnp.concatenate([carry_z[...], z], axis=0)
        carry_z[...] = z[TM - HALO_CONV:, :]
        cz = (cc_ref[0:1, :] * _shift_down(extz, 2)[HALO_CONV:, :]
              + cc_ref[1:2, :] * _shift_down(extz, 1)[HALO_CONV:, :] + cc_ref[2:3, :] * z)
        pz_ref[:, WA:2 * WA] = cz.astype(ACT)
        y_ref[:, 2 * WA:3 * WA] = (pf(CB0, WA) * cz).astype(ACT)

        merged = jnp.zeros((TM, D), F32)
        for k, (w_ref, glo) in enumerate(((wa_ref, GA0), (wb_ref, GB0), (wc_ref, GC0))):
            br = _dot(y_ref[:, k * WA:(k + 1) * WA], _lane_cat(w_ref))
            b_ref[:, k * D:(k + 1) * D] = br.astype(ACT)
            merged = merged + _sigmoid(pf(glo, D)) * br
        mb = merged.astype(MXU)
        m_ref[...] = mb
        xmid_ref[...] = x_ref[...] + _dot(mb, wo_ref[...].reshape(D, D))

    row = lambda n: pl.BlockSpec((TM, n), lambda i: (i, 0))
    br_spec = _const((N_DEV, WA, NB_BR), (0, 0, 0))
    return _call(
        body, comms, *_grid_marks(nt), (x, p, wpool, pscale, gsgu, wsp, bsp_t, convc, wa_all, wb_all, wc_all, wo_all),
        name=name, grid=(nt,),
        in_specs=[row(D), row(NCOL), _const((HEADS, CHUNK, CHUNK), (0, 0, 0)), _const((1, WA), (0, 0)),
                  _const((1, WA), (0, 0)), _const((HEADS, CHUNK, CHUNK), (0, 0, 0)), _const((CHUNK, HEADS), (0, 0)),
                  _const((3, WA), (0, 0)), br_spec, br_spec, br_spec,
                  _const((N_DEV, ROWS_O, D), (0, 0, 0))],
        out_specs=[row(D), row(3 * WA), row(2 * WA), row(3 * D), row(D)],
        out_shape=[_sds((s, D), F32), _sds((s, 3 * WA), ACT), _sds((s, 2 * WA), ACT), _sds((s, 3 * D), ACT),
                   _sds((s, D), MXU)],
        scratch_shapes=[pltpu.VMEM((HALO_POOL, WA), F32), pltpu.VMEM((HALO_CONV, WA), F32)],
        compiler_params=_params(),
    )


def _conv_up(ext, cur, w_ref, j, b_row):
    return (w_ref[j, 0:1, :] * _shift_down(ext, 2)[HALO_CONV:, :] + w_ref[j, 1:2, :] * _shift_down(ext, 1)[HALO_CONV:, :]
            + w_ref[j, 2:3, :] * cur + b_row)


def _ffn_fwd(xmid, upre, convf_all, convb, wd_all, name, comms=None):
    s = xmid.shape[0]
    nt = s // TM
    half = N_DEV // 2

    def body(x_ref, u_ref, cw_ref, cb_ref, wd_ref, xo_ref, act_ref, up_ref, carry):
        i = pl.program_id(0)

        @pl.when(i == 0)
        def _():
            carry[...] = jnp.zeros_like(carry)

        def conv(j):
            cur = u_ref[j].astype(F32)
            ext = jnp.concatenate([carry[j], cur], axis=0)
            carry[j] = cur[TM - HALO_CONV:, :]
            up = _conv_up(ext, cur, cw_ref, j, cb_ref[j:j + 1, :])
            up_ref[j] = up.astype(ACT)
            return up

        acc = x_ref[...]
        for j in range(half):
            gate = conv(j)
            val = conv(j + half)
            act = (gate * _sigmoid(gate) * val).astype(MXU)
            act_ref[j] = act
            wd = jnp.concatenate([wd_ref[2 * j], wd_ref[2 * j + 1]], axis=0)
            acc = acc + _dot(act, wd)
        xo_ref[...] = acc

    blocks = pl.BlockSpec((N_DEV, TM, NB_UP), lambda i: (0, i, 0))
    return _call(
        body, comms, *_grid_marks(nt), (xmid, upre, convf_all, convb, wd_all),
        name=name, grid=(nt,),
        in_specs=[pl.BlockSpec((TM, D), lambda i: (i, 0)), blocks,
                  _const((N_DEV, 3, NB_UP), (0, 0, 0)), _const((N_DEV, NB_UP), (0, 0)),
                  _const((N_DEV, ROWS_DN, D), (0, 0, 0))],
        out_specs=[pl.BlockSpec((TM, D), lambda i: (i, 0)), pl.BlockSpec((half, TM, NB_UP), lambda i: (0, i, 0)), blocks],
        out_shape=[_sds((s, D), F32), _sds((half, s, NB_UP), MXU), _sds((N_DEV, s, NB_UP), ACT)],
        scratch_shapes=[pltpu.VMEM((N_DEV, HALO_CONV, NB_UP), F32)],
        compiler_params=_params(),
    )


def _loss_head(x, g, target, name):
    s = x.shape[0]
    nt = s // TM

    def body(x_ref, g_ref, t_ref, dx_ref, dg_ref, loss_ref):
        i = pl.program_id(0)

        @pl.when(i == 0)
        def _():
            dg_ref[...] = jnp.zeros_like(dg_ref)
            loss_ref[...] = jnp.zeros_like(loss_ref)

        xf = x_ref[...]
        r = lax.rsqrt(jnp.mean(xf * xf, axis=-1, keepdims=True) + EPS)
        xn = xf * r
        err = xn * g_ref[...] - t_ref[...]
        loss_ref[...] += 0.5 * jnp.sum(jnp.mean(err * err, axis=-1, keepdims=True), axis=0, keepdims=True)
        dy = err * (1.0 / D)
        dg_ref[0:1, :] += _colsum(dy * xn)
        dyg = dy * g_ref[...]
        dx_ref[...] = r * (dyg - xn * jnp.mean(dyg * xn, axis=-1, keepdims=True))

    return pl.pallas_call(
        body, name=name, grid=(nt,),
        in_specs=[pl.BlockSpec((TM, D), lambda i: (i, 0)), _const((1, D), (0, 0)), pl.BlockSpec((TM, D), lambda i: (i, 0))],
        out_specs=[pl.BlockSpec((TM, D), lambda i: (i, 0)), pl.BlockSpec((8, D), lambda i: (0, 0)),
                   pl.BlockSpec((1, 1), lambda i: (0, 0))],
        out_shape=[_sds((s, D), F32), _sds((8, D), F32), _sds((1, 1), F32)],
        compiler_params=_params(),
    )(x, g, target)


def _ffn_bwd(dxo, upre, up, convf_all, wd_all, name, comms=None):
    s = dxo.shape[0]
    nt = s // TM
    half = N_DEV // 2

    def body(dx_ref, u_ref, up_ref, cw_ref, wd_ref, du_ref, dc_ref, carry):
        step = pl.program_id(0)

        @pl.when(step == 0)
        def _():
            carry[...] = jnp.zeros_like(carry)
            dc_ref[...] = jnp.zeros_like(dc_ref)

        dxb = dx_ref[...].astype(MXU)

        def adjoint(j, d_up):
            cur = u_ref[j].astype(F32)
            ext = jnp.concatenate([d_up, carry[j]], axis=0)
            carry[j] = d_up[:HALO_CONV, :]
            up1 = _shift_up(ext, 1)[:TM, :]
            up2 = _shift_up(ext, 2)[:TM, :]
            du_ref[j] = (cw_ref[j, 2:3, :] * d_up + cw_ref[j, 1:2, :] * up1 + cw_ref[j, 0:1, :] * up2).astype(du_ref.dtype)
            dc_ref[j, 0:1, :] += _colsum(cur * up2)
            dc_ref[j, 1:2, :] += _colsum(cur * up1)
            dc_ref[j, 2:3, :] += _colsum(cur * d_up)
            dc_ref[j, 3:4, :] += _colsum(d_up)

        for j in range(half):
            gate = up_ref[j].astype(F32)
            val = up_ref[j + half].astype(F32)
            sg = _sigmoid(gate)
            wd = jnp.concatenate([wd_ref[2 * j], wd_ref[2 * j + 1]], axis=0)
            dact = _dot_nt(dxb, wd)
            adjoint(j, dact * val * sg * (1.0 + gate * (1.0 - sg)))
            adjoint(j + half, dact * gate * sg)

    blocks = pl.BlockSpec((N_DEV, TM, NB_UP), lambda i: (0, nt - 1 - i, 0))
    return _call(
        body, comms, *_grid_marks(nt), (dxo, upre, up, convf_all, wd_all),
        name=name, grid=(nt,),
        in_specs=[pl.BlockSpec((TM, D), lambda i: (nt - 1 - i, 0)), blocks, blocks,
                  _const((N_DEV, 3, NB_UP), (0, 0, 0)), _const((N_DEV, ROWS_DN, D), (0, 0, 0))],
        out_specs=[blocks, pl.BlockSpec((N_DEV, 8, NB_UP), lambda i: (0, 0, 0))],
        out_shape=[_sds((N_DEV, s, NB_UP), MXU), _sds((N_DEV, 8, NB_UP), F32)],
        scratch_shapes=[pltpu.VMEM((N_DEV, HALO_CONV, NB_UP), F32)],
        compiler_params=_params(),
    )


def _proj_bwd(dy, blocked_dy, w_all, x, g, dres, name, comms=None):
    s = x.shape[0]
    nb = w_all.shape[-1]
    nt = s // TM

    def body(dy_ref, w_ref, x_ref, g_ref, dres_ref, dx_ref, dg_ref):
        i = pl.program_id(0)

        @pl.when(i == 0)
        def _():
            dg_ref[...] = jnp.zeros_like(dg_ref)

        dh = jnp.zeros((TM, D), F32)
        for j in range(N_DEV):
            dyj = dy_ref[j] if blocked_dy else dy_ref[:, j * nb:(j + 1) * nb]
            dh = dh + _dot_nt(dyj, w_ref[j])
        xf = x_ref[...]
        r = lax.rsqrt(jnp.mean(xf * xf, axis=-1, keepdims=True) + EPS)
        xn = xf * r
        dg_ref[0:1, :] += _colsum(dh * xn)
        dhg = dh * g_ref[...]
        dx_ref[...] = dres_ref[...] + r * (dhg - xn * jnp.mean(dhg * xn, axis=-1, keepdims=True))

    if blocked_dy:
        dy_spec = pl.BlockSpec((N_DEV, TM, nb), lambda i: (0, i, 0))
    else:
        dy_spec = pl.BlockSpec((TM, N_DEV * nb), lambda i: (i, 0))
    row = pl.BlockSpec((TM, D), lambda i: (i, 0))
    return _call(
        body, comms, *_grid_marks(nt), (dy, w_all, x, g, dres),
        name=name, grid=(nt,),
        in_specs=[dy_spec, _const((N_DEV, D, nb), (0, 0, 0)), row, _const((1, D), (0, 0)), row],
        out_specs=[row, pl.BlockSpec((8, D), lambda i: (0, 0))],
        out_shape=[_sds((s, D), F32), _sds((8, D), F32)],
        compiler_params=_params(),
    )


def _mixer_bwd(dxmid, p, yabc, pacz, babc, wpool, pscale, gsgu, wsp, bsp_t, convc, wa_all, wb_all, wc_all, wo_all,
               name):
    s = dxmid.shape[0]
    nt = s // TM

    def body(dx_ref, p_ref, y_ref, pz_ref, b_ref, wpool_ref, ps_ref, gs_ref, wsp_ref, bsp_ref, cc_ref,
             wa_ref, wb_ref, wc_ref, wo_ref,
             dp_ref, db_ref, dwp_ref, dws_ref, small_ref, dbs_ref,
             carry_pa, carry_cz, dbs_acc, du_s, dvn_s):
        step = pl.program_id(0)
        tile = nt - 1 - step

        @pl.when(step == 0)
        def _():
            for ref in (carry_pa, carry_cz, dbs_acc, dwp_ref, dws_ref, small_ref, dbs_ref):
                ref[...] = jnp.zeros_like(ref)

        def pf(lo, n):
            return p_ref[:, lo:lo + n].astype(F32)

        dm = _dot_nt(dx_ref[...].astype(MXU), wo_ref[...].reshape(D, D))

        def through_gate(k, glo, w_ref):
            sg = _sigmoid(pf(glo, D))
            br = b_ref[:, k * D:(k + 1) * D].astype(F32)
            dp_ref[:, glo:glo + D] = (dm * br * sg * (1.0 - sg)).astype(dp_ref.dtype)
            dbr = (dm * sg).astype(MXU)
            db_ref[:, k * D:(k + 1) * D] = dbr
            return _dot_nt(dbr, _lane_cat(w_ref))

        dya = through_gate(0, GA0, wa_ref)
        dyb = through_gate(1, GB0, wb_ref)
        dyc = through_gate(2, GC0, wc_ref)

        t_pos = (tile * TM + lax.broadcasted_iota(jnp.int32, (TM, 1), 0)).astype(F32)
        for g, win in enumerate(POOL_WINDOWS):
            cols = slice(g * CHUNK, (g + 1) * CHUNK)
            pa_g = pz_ref[:, cols]
            q = _dot(pa_g, wpool_ref[g])
            dya_g = dya[:, cols]
            small_ref[0:1, cols] += _colsum(dya_g * q)
            dq = (dya_g * ps_ref[:, cols]).astype(MXU)
            dpa_g = _dot_nt(dq, wpool_ref[g])
            dwp_ref[g] += _dot_tn(pa_g, dq)
            dpw = dpa_g / jnp.minimum(t_pos + 1.0, float(win))
            acc = jnp.concatenate([dpw, carry_pa[:, cols]], axis=0)
            carry_pa[:, cols] = dpw[:HALO_POOL, :]
            k = 1
            while k < win:
                acc = acc + _shift_up(acc, k)
                k *= 2
            dp_ref[:, cols] = (acc[:TM, :] - dpa_g).astype(dp_ref.dtype)

        uvp = pf(UV0, 2 * WA)
        uvg, dgelu = _gelu_parts(uvp)
        u = uvg[:, :WA]
        v = uvg[:, WA:]
        rv = lax.rsqrt(jnp.mean(v * v, axis=-1, keepdims=True) + EPS)
        vh = v * rv
        vn = (vh * gs_ref[...]).astype(MXU)
        mask = _tril_mask()
        for g in range(HEADS):
            cols = slice(g * CHUNK, (g + 1) * CHUNK)
            wt32 = jnp.where(mask, wsp_ref[g], 0.0)
            wt = wt32.astype(MXU)
            wt_t = wt32.T.astype(MXU)
            bcol = bsp_ref[:, g:g + 1]
            for c in range(TM // CHUNK):
                rows = slice(c * CHUNK, (c + 1) * CHUNK)
                vn_cg = vn[rows, cols]
                sv = _dot(wt, vn_cg) + bcol
                dyb_cg = dyb[rows, cols]
                du_s[rows, cols] = dyb_cg * sv
                dsv = dyb_cg * u[rows, cols]
                dbs_acc[g] += dsv
                dsv_b = dsv.astype(MXU)
                dws_ref[g] += _dot_nt(dsv_b, vn_cg)
                dvn_s[rows, cols] = _dot(wt_t, dsv_b)
        dvn = dvn_s[...]
        small_ref[1:2, :] += _colsum(dvn * vh)
        dvg = dvn * gs_ref[...]
        dv = rv * (dvg - vh * jnp.mean(dvg * vh, axis=-1, keepdims=True))
        dp_ref[:, UV0:UV0 + WA] = (du_s[...] * dgelu[:, :WA]).astype(dp_ref.dtype)
        dp_ref[:, UV0 + WA:UV0 + 2 * WA] = (dv * dgelu[:, WA:]).astype(dp_ref.dtype)

        cb = pf(CB0, WA)
        cc = pf(CC0, WA)
        cx = pf(CX0, WA)
        z = cc * cx
        dp_ref[:, CB0:CB0 + WA] = (dyc * pz_ref[:, WA:2 * WA].astype(F32)).astype(dp_ref.dtype)
        dcz = dyc * cb
        extz = jnp.concatenate([dcz, carry_cz[...]], axis=0)
        carry_cz[...] = dcz[:HALO_CONV, :]
        up1 = _shift_up(extz, 1)[:TM, :]
        up2 = _shift_up(extz, 2)[:TM, :]
        dz = cc_ref[2:3, :] * dcz + cc_ref[1:2, :] * up1 + cc_ref[0:1, :] * up2
        small_ref[2:3, :] += _colsum(z * up2)
        small_ref[3:4, :] += _colsum(z * up1)
        small_ref[4:5, :] += _colsum(z * dcz)
        dp_ref[:, CC0:CC0 + WA] = (dz * cx).astype(dp_ref.dtype)
        dp_ref[:, CX0:CX0 + WA] = (dz * cc).astype(dp_ref.dtype)

        @pl.when(step == nt - 1)
        def _():
            ones = jnp.ones((8, CHUNK), F32)
            for g in range(HEADS):
                dws_ref[g] = jnp.where(mask, dws_ref[g], 0.0)
                row = lax.dot_general(ones, dbs_acc[g], (((1,), (1,)), ((), ())), preferred_element_type=F32,
                                      precision=lax.Precision.HIGHEST)
                dbs_ref[g:g + 1, :] = row[0:1, :]

    row = lambda n: pl.BlockSpec((TM, n), lambda i: (nt - 1 - i, 0))
    br_spec = _const((N_DEV, WA, NB_BR), (0, 0, 0))
    acc_spec = lambda shape: pl.BlockSpec(shape, lambda i: (0,) * len(shape))
    return pl.pallas_call(
        body, name=name, grid=(nt,),
        in_specs=[row(D), row(NCOL), row(3 * WA), row(2 * WA), row(3 * D),
                  _const((HEADS, CHUNK, CHUNK), (0, 0, 0)), _const((1, WA), (0, 0)), _const((1, WA), (0, 0)),
                  _const((HEADS, CHUNK, CHUNK), (0, 0, 0)), _const((CHUNK, HEADS), (0, 0)), _const((3, WA), (0, 0)),
                  br_spec, br_spec, br_spec, _const((N_DEV, ROWS_O, D), (0, 0, 0))],
        out_specs=[row(NCOL), row(3 * D), acc_spec((HEADS, CHUNK, CHUNK)), acc_spec((HEADS, CHUNK, CHUNK)),
                   acc_spec((8, WA)), acc_spec((8, CHUNK))],
        out_shape=[_sds((s, NCOL), MXU), _sds((s, 3 * D), MXU), _sds((HEADS, CHUNK, CHUNK), F32),
                   _sds((HEADS, CHUNK, CHUNK), F32), _sds((8, WA), F32), _sds((8, CHUNK), F32)],
        scratch_shapes=[pltpu.VMEM((HALO_POOL, WA), F32), pltpu.VMEM((HALO_CONV, WA), F32),
                        pltpu.VMEM((HEADS, CHUNK, CHUNK), F32), pltpu.VMEM((TM, WA), F32), pltpu.VMEM((TM, WA), F32)],
        compiler_params=_params(),
    )(dxmid, p, yabc, pacz, babc, wpool, pscale, gsgu, wsp, bsp_t, convc, wa_all, wb_all, wc_all, wo_all)


def _wgrad(a, b, a_spec, b_spec, n_out, acc_shape, out_shape, out_spec, store, name, comms=None):
    s = a.shape[-2]
    ts = min(TS_WGRAD if b.dtype == MXU else TS_WGRAD // 2, s)
    n_steps = s // ts

    def body(a_ref, b_ref, o_ref, acc_ref):
        k = pl.program_id(1)

        @pl.when(k == 0)
        def _():
            acc_ref[...] = jnp.zeros_like(acc_ref)

        acc_ref[...] += _dot_tn(a_ref[...], b_ref[...].astype(MXU))

        @pl.when(k == n_steps - 1)
        def _():
            store(o_ref, acc_ref)

    (res,), extra = _call(
        body, comms, lambda: (pl.program_id(0) == 0) & (pl.program_id(1) == 0),
        lambda: (pl.program_id(0) == (3 * n_out) // 4) & (pl.program_id(1) == 0),
        lambda: (pl.program_id(0) == n_out - 1) & (pl.program_id(1) == n_steps - 1), (a, b),
        name=name, grid=(n_out, n_steps),
        in_specs=[a_spec(ts), b_spec(ts)], out_specs=[out_spec], out_shape=[_sds(out_shape, WIRE)],
        scratch_shapes=[pltpu.VMEM(acc_shape, F32)],
        compiler_params=_params(2),
    )
    return res, extra


def _store_plain(o_ref, acc_ref):
    o_ref[...] = acc_ref[...].astype(o_ref.dtype)


def _store_lane_blocks(o_ref, acc_ref):
    for d in range(N_DEV):
        o_ref[d] = acc_ref[:, d * NB_BR:(d + 1) * NB_BR].astype(o_ref.dtype)


def _wgrad_in(h, dp, name, comm=None):
    return _wgrad(h, dp, lambda ts: pl.BlockSpec((ts, D), lambda j, k: (k, 0)),
                  lambda ts: pl.BlockSpec((ts, NB_IN), lambda j, k: (k, j)), N_DEV, (D, NB_IN),
                  (N_DEV, D, NB_IN), pl.BlockSpec((None, D, NB_IN), lambda j, k: (j, 0, 0)), _store_plain, name, comm)


def _wgrad_up(h, du, name, comm=None):
    return _wgrad(h, du, lambda ts: pl.BlockSpec((ts, D), lambda j, k: (k, 0)),
                  lambda ts: pl.BlockSpec((None, ts, NB_UP), lambda j, k: (j, k, 0)), N_DEV, (D, NB_UP),
                  (N_DEV, D, NB_UP), pl.BlockSpec((None, D, NB_UP), lambda j, k: (j, 0, 0)), _store_plain, name, comm)


def _wgrad_down(act, dxo, name, comm=None):
    return _wgrad(act, dxo, lambda ts: pl.BlockSpec((None, ts, NB_UP), lambda j, k: (j, k, 0)),
                  lambda ts: pl.BlockSpec((ts, D), lambda j, k: (k, 0)), N_DEV // 2, (NB_UP, D),
                  (DFF, D), pl.BlockSpec((NB_UP, D), lambda j, k: (j, 0)), _store_plain, name, comm)


def _wgrad_o(merged, dxmid, name, comm=None):
    return _wgrad(merged, dxmid, lambda ts: pl.BlockSpec((ts, D), lambda j, k: (k, 0)),
                  lambda ts: pl.BlockSpec((ts, D), lambda j, k: (k, 0)), 1, (D, D),
                  (D, D), pl.BlockSpec((D, D), lambda j, k: (0, 0)), _store_plain, name, comm)


def _wgrad_branches(yabc, dbabc, name, comm=None):
    return _wgrad(yabc, dbabc, lambda ts: pl.BlockSpec((ts, WA), lambda j, k: (k, j)),
                  lambda ts: pl.BlockSpec((ts, D), lambda j, k: (k, j)), 3, (WA, D),
                  (N_DEV, 3, WA, NB_BR), pl.BlockSpec((N_DEV, None, WA, NB_BR), lambda j, k: (0, j, 0, 0)),
                  _store_lane_blocks, name, comm)


def _adamw_math(g, w, m, v):
    m = ADAM_B1 * m + (1.0 - ADAM_B1) * g
    v = ADAM_B2 * v + (1.0 - ADAM_B2) * (g * g)
    m_hat = m / (1.0 - ADAM_B1 ** ADAM_STEP)
    v_hat = v / (1.0 - ADAM_B2 ** ADAM_STEP)
    delta = -ADAM_LR * (m_hat / (jnp.sqrt(v_hat) + ADAM_EPS) + ADAM_WD * w)
    return delta, m, v


def _adamw_sum(parts, mid, w, m, v, layer, prev, tr, name):
    n_layers, r, c = w.shape

    def body(p_ref, w_ref, m_ref, v_ref, *rest):
        g_ref, d_ref, mo_ref, vo_ref = rest[-4:]
        g = p_ref[0].astype(F32)
        for k in range(1, N_DEV):
            g = g + p_ref[k].astype(F32)
        g_ref[...] = g
        d_ref[...], mo_ref[...], vo_ref[...] = _adamw_math(g, w_ref[...], m_ref[...], v_ref[...])

    blk = pl.BlockSpec((None, tr, c), lambda i: (layer, i, 0))
    extra = [] if prev is None else list(prev)
    return pl.pallas_call(
        body, name=name, grid=(r // tr,),
        in_specs=[pl.BlockSpec((N_DEV, None, tr, c), lambda i: (0, mid, i, 0)), blk, blk, blk]
        + [pl.BlockSpec(memory_space=pl.ANY)] * len(extra),
        out_specs=[blk] * 4, out_shape=[_sds((n_layers, r, c), F32)] * 4,
        input_output_aliases={4 + k: k for k in range(len(extra))},
        compiler_params=_params(),
    )(parts, w, m, v, *extra)


def _sum_parts(parts, name):
    _, r, c = parts.shape

    def body(p_ref, o_ref):
        g = p_ref[0]
        for k in range(1, N_DEV):
            g = g + p_ref[k]
        o_ref[...] = g

    return pl.pallas_call(body, name=name, out_shape=_sds((r, c), F32),
                          compiler_params=pltpu.CompilerParams(vmem_limit_bytes=VMEM_LIMIT))(parts)


def _adamw_small(g, w, m, v, name):
    def body(g_ref, w_ref, m_ref, v_ref, d_ref, mo_ref, vo_ref):
        d_ref[...], mo_ref[...], vo_ref[...] = _adamw_math(g_ref[...], w_ref[...], m_ref[...], v_ref[...])

    return pl.pallas_call(body, name=name, out_shape=[_sds(w.shape, F32)] * 3)(g, w, m, v)


HBM_SPEC = pl.BlockSpec(memory_space=pltpu.HBM)


def _position():
    return lax.axis_index("x"), lax.axis_index("y"), lax.axis_index("c")


def _device_index(chip, core):
    return 4 * chip[0] + 2 * chip[1] + core


def _gather(shards, layer):
    n = len(shards)
    per = 8

    def first_copies(ins, outs, send, recv):
        x, y, c = _position()
        me = 4 * x + 2 * y + c
        targets = [(x, y, 1 - c), (1 - x, y, c), (x, 1 - y, c), (1 - x, 1 - y, c)]
        remote = [pltpu.make_async_remote_copy(
            src_ref=ins[t].at[layer], dst_ref=outs[t].at[me], send_sem=send.at[per * t + k],
            recv_sem=recv.at[per * t + k], device_id=to, device_id_type=MESH)
            for t in range(n) for k, to in enumerate(targets)]
        local = [pltpu.make_async_copy(ins[t].at[layer], outs[t].at[me], send.at[per * t + 4]) for t in range(n)]
        return remote, local

    def passed_on(outs, send, recv):
        x, y, c = _position()
        chips = [(1 - x, y), (x, 1 - y), (1 - x, 1 - y)]
        return [pltpu.make_async_remote_copy(
            src_ref=outs[t].at[_device_index(chip, c)], dst_ref=outs[t].at[_device_index(chip, c)],
            send_sem=send.at[per * t + 5 + j], recv_sem=recv.at[per * t + 5 + j], device_id=(x, y, 1 - c),
            device_id_type=MESH)
            for t in range(n) for j, chip in enumerate(chips)]

    def start(ins, outs, send, recv):
        remote, local = first_copies(ins, outs, send, recv)
        for cp in local + remote:
            cp.start()

    def mid(ins, outs, send, recv):
        remote, local = first_copies(ins, outs, send, recv)
        for cp in remote:
            cp.wait()
        for cp in local:
            cp.wait()
        for cp in passed_on(outs, send, recv):
            cp.start()

    def finish(ins, outs, send, recv):
        for cp in passed_on(outs, send, recv):
            cp.wait()

    return _Comm(shards, [_sds((N_DEV,) + a.shape[1:], a.dtype) for a in shards], per * n, start, finish, mid)


def _run_comms(comms, name):
    def body():
        pass

    _, extra = _call(body, comms, None, None, None, (), name=name, in_specs=[], out_specs=[], out_shape=[])
    return extra


def _exchange(parts):
    n = len(parts)
    per = 8
    flips = [(0, 0, 1), (1, 0, 0), (1, 0, 1), (0, 1, 0), (0, 1, 1), (1, 1, 0), (1, 1, 1)]

    def copies(ins, outs, send, recv):
        x, y, c = _position()
        me = 4 * x + 2 * y + c
        remote = []
        for t in range(n):
            for k, (fx, fy, fc) in enumerate(flips):
                peer = ((1 - x if fx else x), (1 - y if fy else y), (1 - c if fc else c))
                remote.append(pltpu.make_async_remote_copy(
                    src_ref=ins[t].at[_device_index(peer[:2], peer[2])], dst_ref=outs[t].at[me],
                    send_sem=send.at[per * t + k], recv_sem=recv.at[per * t + k], device_id=peer, device_id_type=MESH))
        local = [pltpu.make_async_copy(ins[t].at[me], outs[t].at[me], send.at[per * t + 7]) for t in range(n)]
        return remote, local

    def start(ins, outs, send, recv):
        remote, local = copies(ins, outs, send, recv)
        for cp in local + remote:
            cp.start()

    def finish(ins, outs, send, recv):
        remote, local = copies(ins, outs, send, recv)
        for cp in remote:
            cp.wait()
        for cp in local:
            cp.wait()

    return _Comm(parts, [_sds(a.shape, a.dtype) for a in parts], per * n, start, finish)


def _rows128(a):
    return a.reshape(-1, 128)


def kernel(x, g_mix, w_in, w_pool, pool_scale, g_sgu, w_spatial, b_spatial, conv_c, w_branch_a, w_branch_b, w_branch_c, w_o, g_ffn, w_up, conv_ffn, conv_ffn_b, w_down, g_final, loss_target, m_g_mix, m_w_in, m_w_pool, m_pool_scale, m_g_sgu, m_w_spatial, m_b_spatial, m_conv_c, m_w_branch_a, m_w_branch_b, m_w_branch_c, m_w_o, m_g_ffn, m_w_up, m_conv_ffn, m_conv_ffn_b, m_w_down, m_g_final, v_g_mix, v_w_in, v_w_pool, v_pool_scale, v_g_sgu, v_w_spatial, v_b_spatial, v_conv_c, v_w_branch_a, v_w_branch_b, v_w_branch_c, v_w_o, v_g_ffn, v_w_up, v_conv_ffn, v_conv_ffn_b, v_w_down, v_g_final):
    s = x.shape[1]
    n_layers = g_mix.shape[0]
    x0 = x.reshape(s, D)
    target = loss_target.reshape(s, D)
    me = 4 * lax.axis_index("x") + 2 * lax.axis_index("y") + lax.axis_index("c")

    first_shards = [w_in.astype(MXU), conv_c]
    mix_shards = [w_branch_a.astype(MXU), w_branch_b.astype(MXU), w_branch_c.astype(MXU), w_o.astype(MXU),
                  w_down.astype(MXU)]
    up_shards = [w_up.astype(MXU), conv_ffn]
    (first_now,) = _run_comms([_gather(first_shards, 0)], "gather_first_0")
    mix_now = up_now = None
    wpool_b = w_pool.astype(MXU)
    bsp_t = jnp.swapaxes(b_spatial, 1, 2)
    convb_blk = conv_ffn_b.reshape(n_layers, N_DEV, NB_UP)

    saved = []
    weights = []
    xl = x0
    for l in range(n_layers):
        win8, convc8 = first_now
        convc_full = jnp.transpose(convc8, (1, 0, 2)).reshape(3, WA)
        more = l + 1 < n_layers
        (p, h), got = _rms_proj(xl, g_mix[l:l + 1], win8, False, f"in_proj_{l}",
                                [_gather(mix_shards if l == 0 else up_shards, l)])
        if l == 0:
            (mix_now,) = got
        else:
            (up_now,) = got
        wa8, wb8, wc8, wo8, wd8 = mix_now
        (xmid, yabc, pacz, babc, merged), got = _mixer_fwd(
            xl, p, wpool_b[l], pool_scale[l:l + 1], g_sgu[l:l + 1], w_spatial[l], bsp_t[l], convc_full,
            wa8, wb8, wc8, wo8, f"mixer_fwd_{l}", [_gather(up_shards, l)] if l == 0 else None)
        if l == 0:
            (up_now,) = got
        wup8, convf8 = up_now
        weights.append((win8, wa8, wb8, wc8, wo8, wup8, wd8, convc_full, convf8))
        (upre, h2), got = _rms_proj(xmid, g_ffn[l:l + 1], wup8, True, f"up_proj_{l}",
                                    [_gather(first_shards, l + 1)] if more else None)
        if more:
            (first_now,) = got
        (xout, act, up), got = _ffn_fwd(xmid, upre, convf8, convb_blk[l], wd8, f"ffn_fwd_{l}",
                                        [_gather(mix_shards, l + 1)] if more else None)
        if more:
            (mix_now,) = got
        saved.append((xl, p, h, xmid, yabc, pacz, babc, merged, upre, h2, act, up))
        xl = xout

    dx, dg_final, loss_local = _loss_head(xl, g_final.reshape(1, D), target, "loss_head")

    received = [dict() for _ in range(n_layers)]
    small = {("final", "g_final"): dg_final}
    small_sums = {}
    waiting = None

    def exchange_of(named):
        return [_exchange([a for _, a in named])]

    def land(layer, named, got):
        received[layer].update({k: a for (k, _), a in zip(named, got[0])})

    def gather_small(keys):
        packed = jnp.concatenate([_rows128(small[k]) for k in keys], axis=0)[None]
        return _gather([packed], 0)

    def sum_small(keys, gathered, name):
        summed = _sum_parts(gathered, name)
        row = 0
        for k in keys:
            n_rows = small[k].size // 128
            small_sums[k] = summed[row:row + n_rows].reshape(small[k].shape)
            row += n_rows

    for l in reversed(range(n_layers)):
        xin, p, h, xmid, yabc, pacz, babc, merged, upre, h2, act, up = saved[l]
        win8, wa8, wb8, wc8, wo8, wup8, wd8, convc_full, convf8 = weights[l]
        last = l == 0
        (dupre, dconvf), got = _ffn_bwd(dx, upre, up, convf8, wd8, f"ffn_bwd_{l}",
                                        None if waiting is None else exchange_of(waiting[1]))
        if waiting is not None:
            land(waiting[0], waiting[1], got)
        small[(l, "conv_ffn")] = dconvf
        keys_a = [k for k in small if k not in small_sums]
        g_wdown, got = _wgrad_down(act, dx, f"wgrad_down_{l}", [gather_small(keys_a)] if last else None)
        if last:
            sum_small(keys_a, got[0][0], "sum_small_grads_a")
        down = [("w_down", g_wdown.reshape(N_DEV, ROWS_DN, D))]
        g_wup, got = _wgrad_up(h2, dupre, f"wgrad_up_{l}", exchange_of(down) if last else None)
        if last:
            land(l, down, got)
        upw = [("w_up", g_wup)]
        (dxmid, dg_ffn), got = _proj_bwd(dupre, True, wup8, xmid, g_ffn[l:l + 1], dx, f"up_proj_bwd_{l}",
                                         exchange_of(upw if last else down))
        land(l, upw if last else down, got)
        dp, dbabc, dwp, dws, mixer_small, dbs = _mixer_bwd(
            dxmid, p, yabc, pacz, babc, wpool_b[l], pool_scale[l:l + 1], g_sgu[l:l + 1], w_spatial[l], bsp_t[l],
            convc_full, wa8, wb8, wc8, wo8, f"mixer_bwd_{l}")
        small.update({(l, "g_ffn"): dg_ffn, (l, "w_pool"): dwp, (l, "mixer_small"): mixer_small,
                      (l, "w_spatial"): dws, (l, "b_spatial"): dbs})
        g_wo, _ = _wgrad_o(merged, dxmid, f"wgrad_o_{l}")
        g_br, _ = _wgrad_branches(yabc, dbabc, f"wgrad_branches_{l}")
        mixer_w = [("branches", g_br), ("w_o", g_wo.reshape(N_DEV, ROWS_O, D))]
        keys_b = [k for k in small if k not in small_sums]
        g_win, got = _wgrad_in(h, dp, f"wgrad_in_{l}",
                               exchange_of(mixer_w) + [gather_small(keys_b)] if last else exchange_of(upw))
        if last:
            land(l, mixer_w, got)
            sum_small(keys_b, got[1][0], "sum_small_grads_b")
        else:
            land(l, upw, got)
        inw = [("w_in", g_win)]
        (dx, dg_mix), got = _proj_bwd(dp, False, win8, xin, g_mix[l:l + 1], dxmid, f"in_proj_bwd_{l}",
                                      exchange_of(inw if last else mixer_w))
        land(l, inw if last else mixer_w, got)
        waiting = None if last else (l, inw)
        small[(l, "g_mix")] = dg_mix
    grad_x = dx.reshape(1, s, D)
    late_keys = [k for k in small if k not in small_sums]
    (gathered_late,) = _run_comms([gather_small(late_keys)], "gather_last_small_grads")[0]
    sum_small(late_keys, gathered_late, "sum_last_small_grads")

    def update_big(key, mid, w, m, v, tr, tag):
        outs = None
        for l in range(n_layers):
            parts = received[l][key]
            if parts.ndim == 3:
                parts = parts.reshape(N_DEV, 1, *parts.shape[1:])
            outs = _adamw_sum(parts, mid, w, m, v, l, outs, tr, f"adamw_{tag}_{l}")
        return outs

    up_in = update_big("w_in", 0, w_in, m_w_in, v_w_in, 256, "w_in")
    up_a = update_big("branches", 0, w_branch_a, m_w_branch_a, v_w_branch_a, WA, "w_branch_a")
    up_b = update_big("branches", 1, w_branch_b, m_w_branch_b, v_w_branch_b, WA, "w_branch_b")
    up_c = update_big("branches", 2, w_branch_c, m_w_branch_c, v_w_branch_c, WA, "w_branch_c")
    up_o = update_big("w_o", 0, w_o, m_w_o, v_w_o, ROWS_O, "w_o")
    up_up = update_big("w_up", 0, w_up, m_w_up, v_w_up, 256, "w_up")
    up_down = update_big("w_down", 0, w_down, m_w_down, v_w_down, ROWS_DN, "w_down")

    stack = lambda kind: jnp.stack([small_sums[(l, kind)] for l in range(n_layers)], axis=0)
    grad_g_mix = stack("g_mix")[:, 0, :]
    grad_w_pool = stack("w_pool")
    mixer_sums = stack("mixer_small")
    grad_pool_scale = mixer_sums[:, 0, :]
    grad_g_sgu = mixer_sums[:, 1, :]
    grad_conv_c = lax.dynamic_slice_in_dim(mixer_sums[:, 2:5, :], me * (WA // N_DEV), WA // N_DEV, axis=2)
    grad_w_spatial = stack("w_spatial")
    grad_b_spatial = stack("b_spatial")[:, 0:HEADS, :]
    grad_g_ffn = stack("g_ffn")[:, 0, :]
    conv_grads = stack("conv_ffn")
    grad_conv_ffn = lax.dynamic_index_in_dim(conv_grads, me, axis=1, keepdims=False)[:, 0:3, :]
    grad_conv_ffn_b = conv_grads[:, :, 3, :].reshape(n_layers, 2 * DFF)
    grad_g_final = small_sums[("final", "g_final")][0]

    def update_small(g, w, m, v, tag):
        shape2 = (-1, w.shape[-1])
        outs = _adamw_small(g.reshape(shape2), w.reshape(shape2), m.reshape(shape2), v.reshape(shape2), f"adamw_{tag}")
        return [g] + [o.reshape(w.shape) for o in outs]

    up = {
        "g_mix": update_small(grad_g_mix, g_mix, m_g_mix, v_g_mix, "g_mix"),
        "w_in": up_in,
        "w_pool": update_small(grad_w_pool, w_pool, m_w_pool, v_w_pool, "w_pool"),
        "pool_scale": update_small(grad_pool_scale, pool_scale, m_pool_scale, v_pool_scale, "pool_scale"),
        "g_sgu": update_small(grad_g_sgu, g_sgu, m_g_sgu, v_g_sgu, "g_sgu"),
        "w_spatial": update_small(grad_w_spatial, w_spatial, m_w_spatial, v_w_spatial, "w_spatial"),
        "b_spatial": update_small(grad_b_spatial, b_spatial, m_b_spatial, v_b_spatial, "b_spatial"),
        "conv_c": update_small(grad_conv_c, conv_c, m_conv_c, v_conv_c, "conv_c"),
        "w_branch_a": up_a,
        "w_branch_b": up_b,
        "w_branch_c": up_c,
        "w_o": up_o,
        "g_ffn": update_small(grad_g_ffn, g_ffn, m_g_ffn, v_g_ffn, "g_ffn"),
        "w_up": up_up,
        "conv_ffn": update_small(grad_conv_ffn, conv_ffn, m_conv_ffn, v_conv_ffn, "conv_ffn"),
        "conv_ffn_b": update_small(grad_conv_ffn_b, conv_ffn_b, m_conv_ffn_b, v_conv_ffn_b, "conv_ffn_b"),
        "w_down": up_down,
        "g_final": update_small(grad_g_final, g_final, m_g_final, v_g_final, "g_final"),
    }
    loss = lax.psum(loss_local[0, 0], AXES)
    order = list(up)
    return (loss, grad_x, *[up[k][0] for k in order], *[up[k][1] for k in order], *[up[k][2] for k in order],
            *[up[k][3] for k in order])
```

```python
import functools

import jax
import jax.numpy as jnp
from jax import lax
from jax.experimental import pallas as pl
from jax.experimental.pallas import tpu as pltpu

F32 = jnp.float32
BF16 = jnp.bfloat16
MXU = BF16
ACT = BF16
WIRE = BF16

N_DEV = 8
D = 1024
WA = 512
NCOL = 6144
DFF = 2816
NB_IN = NCOL // N_DEV
NB_UP = 2 * DFF // N_DEV
NB_BR = D // N_DEV
ROWS_O = D // N_DEV
ROWS_DN = DFF // N_DEV
CHUNK = 128
HEADS = 4
POOL_WINDOWS = (2, 4, 8, 16)
EPS = 1e-6
A0, UV0, CB0, CC0, CX0, GA0, GB0, GC0 = 0, 512, 1536, 2048, 2560, 3072, 4096, 5120

ADAM_LR = 0.001
ADAM_B1 = 0.9
ADAM_B2 = 0.999
ADAM_EPS = 1e-08
ADAM_WD = 0.01
ADAM_STEP = 10

TM = 256
TS_WGRAD = 4096
HALO_POOL = 16
HALO_CONV = 8
VMEM_LIMIT = 56 * 1024 * 1024
MESH = pl.DeviceIdType.MESH
AXES = ("x", "y", "c")


def _sds(shape, dtype):
    return jax.ShapeDtypeStruct(tuple(shape), dtype)


def _params(n_grid=1):
    return pltpu.CompilerParams(dimension_semantics=("arbitrary",) * n_grid, vmem_limit_bytes=VMEM_LIMIT)


def _const(block, index):
    return pl.BlockSpec(block, lambda *_: index, pipeline_mode=pl.Buffered(1))


def _dot(a, b):
    return jnp.dot(a, b, preferred_element_type=F32)


def _dot_nt(a, b):
    return lax.dot_general(a, b, (((1,), (1,)), ((), ())), preferred_element_type=F32)


def _dot_tn(a, b):
    return lax.dot_general(a, b, (((0,), (0,)), ((), ())), preferred_element_type=F32)


def _sigmoid(v):
    return 0.5 * jnp.tanh(0.5 * v) + 0.5


def _shift_down(v, k):
    return pltpu.roll(v, k, axis=0)


def _shift_up(v, k):
    return pltpu.roll(v, v.shape[0] - k, axis=0)


def _colsum(v):
    return jnp.sum(v, axis=0, keepdims=True)


def _lane_cat(ref):
    return jnp.concatenate([ref[d] for d in range(N_DEV)], axis=1)


class _Comm:
    def __init__(self, operands, out_shapes, n_sems, start, finish, mid=None):
        self.operands = list(operands)
        self.out_shapes = list(out_shapes)
        self.n_sems = n_sems
        self.start = start
        self.mid = mid
        self.finish = finish


def _call(body, comms, is_first, is_mid, is_last, operands, *, in_specs, out_specs, out_shape, scratch_shapes=(), **kw):
    n_in, n_out, n_scr = len(in_specs), len(out_specs), len(scratch_shapes)
    comms = [c for c in (comms or []) if c is not None]
    if not comms:
        res = pl.pallas_call(body, in_specs=in_specs, out_specs=out_specs, out_shape=out_shape,
                             scratch_shapes=list(scratch_shapes), **kw)(*operands)
        return res, []
    nci = [len(c.operands) for c in comms]
    nco = [len(c.out_shapes) for c in comms]

    def split(refs, sizes):
        parts = []
        for n in sizes:
            parts.append(refs[:n])
            refs = refs[n:]
        return parts, refs

    def carrier(*refs):
        ins, refs = refs[:n_in], refs[n_in:]
        cins, refs = split(refs, nci)
        outs, refs = refs[:n_out], refs[n_out:]
        couts, refs = split(refs, nco)
        scr, sems = refs[:n_scr], refs[n_scr:]

        def run(step):
            for k, c in enumerate(comms):
                if getattr(c, step) is not None:
                    getattr(c, step)(cins[k], couts[k], sems[2 * k], sems[2 * k + 1])

        def at(mark, step):
            if mark is None:
                run(step)
            else:
                pl.when(mark())(lambda: run(step))

        at(is_first, "start")
        body(*ins, *outs, *scr)
        at(is_mid, "mid")
        at(is_last, "finish")

    res = pl.pallas_call(
        carrier, in_specs=list(in_specs) + [HBM_SPEC] * sum(nci), out_specs=list(out_specs) + [HBM_SPEC] * sum(nco),
        out_shape=list(out_shape) + [s for c in comms for s in c.out_shapes],
        scratch_shapes=list(scratch_shapes) + [pltpu.SemaphoreType.DMA((c.n_sems,)) for c in comms for _ in range(2)],
        **kw,
    )(*operands, *[a for c in comms for a in c.operands])
    extra, _ = split(res[n_out:], nco)
    return res[:n_out], extra


def _grid_marks(nt):
    return (lambda: pl.program_id(0) == 0), (lambda: pl.program_id(0) == (3 * nt) // 4), (lambda: pl.program_id(0) == nt - 1)


def _rms_proj(x, g, w_all, name, comms=None):
    s = x.shape[0]
    nb = w_all.shape[-1]
    nt = s // TM

    def body(x_ref, g_ref, w_ref, p_ref, h_ref):
        xf = x_ref[...]
        r = lax.rsqrt(jnp.mean(xf * xf, axis=-1, keepdims=True) + EPS)
        h = (xf * r * g_ref[...]).astype(MXU)
        h_ref[...] = h
        for j in range(N_DEV):
            p_ref[:, j * nb:(j + 1) * nb] = _dot(h, w_ref[j]).astype(p_ref.dtype)

    row = lambda n: pl.BlockSpec((TM, n), lambda i: (i, 0))
    return _call(
        body, comms, *_grid_marks(nt), (x, g, w_all),
        name=name, grid=(nt,),
        in_specs=[row(D), _const((1, D), (0, 0)), _const((N_DEV, D, nb), (0, 0, 0))],
        out_specs=[row(N_DEV * nb), row(D)],
        out_shape=[_sds((s, N_DEV * nb), ACT), _sds((s, D), MXU)],
        compiler_params=_params(),
    )


def _tril_mask():
    r = lax.broadcasted_iota(jnp.int32, (CHUNK, CHUNK), 0)
    c = lax.broadcasted_iota(jnp.int32, (CHUNK, CHUNK), 1)
    return r >= c


def _gelu_parts(v):
    c0 = 0.7978845608028654
    th = jnp.tanh(c0 * (v + 0.044715 * (v * v * v)))
    cdf = 0.5 * (1.0 + th)
    dgelu = cdf + v * (0.5 * c0) * (1.0 - th * th) * (1.0 + 3.0 * 0.044715 * (v * v))
    return v * cdf, dgelu


def _mixer_fwd(x, p, wpool, pscale, gsgu, wsp, bsp_t, convc, wa_all, wb_all, wc_all, wo_all, name, comms=None):
    s = x.shape[0]
    nt = s // TM

    def body(x_ref, p_ref, wpool_ref, ps_ref, gs_ref, wsp_ref, bsp_ref, cc_ref, wa_ref, wb_ref, wc_ref, wo_ref,
             xmid_ref, y_ref, pz_ref, b_ref, m_ref, carry_a, carry_z):
        i = pl.program_id(0)

        @pl.when(i == 0)
        def _():
            carry_a[...] = jnp.zeros_like(carry_a)
            carry_z[...] = jnp.zeros_like(carry_z)

        def pf(lo, n):
            return p_ref[:, lo:lo + n].astype(F32)

        a = pf(A0, WA)
        ext = jnp.concatenate([carry_a[...], a], axis=0)
        carry_a[...] = a[TM - HALO_POOL:, :]
        t_pos = (i * TM + lax.broadcasted_iota(jnp.int32, (TM, 1), 0)).astype(F32)
        for g, win in enumerate(POOL_WINDOWS):
            cols = slice(g * CHUNK, (g + 1) * CHUNK)
            acc = ext[:, cols]
            k = 1
            while k < win:
                acc = acc + _shift_down(acc, k)
                k *= 2
            cnt = jnp.minimum(t_pos + 1.0, float(win))
            pa_g = (acc[HALO_POOL:, :] / cnt - a[:, cols]).astype(MXU)
            pz_ref[:, cols] = pa_g
            y_ref[:, cols] = (_dot(pa_g, wpool_ref[g]) * ps_ref[:, cols]).astype(ACT)

        uvg, _ = _gelu_parts(pf(UV0, 2 * WA))
        u = uvg[:, :WA]
        v = uvg[:, WA:]
        rv = lax.rsqrt(jnp.mean(v * v, axis=-1, keepdims=True) + EPS)
        vn = (v * rv * gs_ref[...]).astype(MXU)
        mask = _tril_mask()
        for g in range(HEADS):
            cols = slice(g * CHUNK, (g + 1) * CHUNK)
            wt = jnp.where(mask, wsp_ref[g], 0.0).astype(MXU)
            bcol = bsp_ref[:, g:g + 1]
            for c in range(TM // CHUNK):
                rows = slice(c * CHUNK, (c + 1) * CHUNK)
                sv = _dot(wt, vn[rows, cols]) + bcol
                y_ref[rows, WA + g * CHUNK:WA + (g + 1) * CHUNK] = (u[rows, cols] * sv).astype(ACT)

        z = pf(CC0, WA) * pf(CX0, WA)
        extz = jnp.concatenate([carry_z[...], z], axis=0)
        carry_z[...] = z[TM - HALO_CONV:, :]
        cz = (cc_ref[0:1, :] * _shift_down(extz, 2)[HALO_CONV:, :]
              + cc_ref[1:2, :] * _shift_down(extz, 1)[HALO_CONV:, :] + cc_ref[2:3, :] * z)
        pz_ref[:, WA:2 * WA] = cz.astype(ACT)
        y_ref[:, 2 * WA:3 * WA] = (pf(CB0, WA) * cz).astype(ACT)

        merged = jnp.zeros((TM, D), F32)
        for k, (w_ref, glo) in enumerate(((wa_ref, GA0), (wb_ref, GB0), (wc_ref, GC0))):
            br = _dot(y_ref[:, k * WA:(k + 1) * WA], _lane_cat(w_ref))
            b_ref[:, k * D:(k + 1) * D] = br.astype(ACT)
            merged = merged + _sigmoid(pf(glo, D)) * br
        mb = merged.astype(MXU)
        m_ref[...] = mb
        xmid_ref[...] = x_ref[...] + _dot(mb, wo_ref[...].reshape(D, D))

    row = lambda n: pl.BlockSpec((TM, n), lambda i: (i, 0))
    br_spec = _const((N_DEV, WA, NB_BR), (0, 0, 0))
    return _call(
        body, comms, *_grid_marks(nt), (x, p, wpool, pscale, gsgu, wsp, bsp_t, convc, wa_all, wb_all, wc_all, wo_all),
        name=name, grid=(nt,),
        in_specs=[row(D), row(NCOL), _const((HEADS, CHUNK, CHUNK), (0, 0, 0)), _const((1, WA), (0, 0)),
                  _const((1, WA), (0, 0)), _const((HEADS, CHUNK, CHUNK), (0, 0, 0)), _const((CHUNK, HEADS), (0, 0)),
                  _const((3, WA), (0, 0)), br_spec, br_spec, br_spec,
                  _const((N_DEV, ROWS_O, D), (0, 0, 0))],
        out_specs=[row(D), row(3 * WA), row(2 * WA), row(3 * D), row(D)],
        out_shape=[_sds((s, D), F32), _sds((s, 3 * WA), ACT), _sds((s, 2 * WA), ACT), _sds((s, 3 * D), ACT),
                   _sds((s, D), MXU)],
        scratch_shapes=[pltpu.VMEM((HALO_POOL, WA), F32), pltpu.VMEM((HALO_CONV, WA), F32)],
        compiler_params=_params(),
    )


def _conv_up(ext, cur, w_ref, j, b_row):
    return (w_ref[j, 0:1, :] * _shift_down(ext, 2)[HALO_CONV:, :] + w_ref[j, 1:2, :] * _shift_down(ext, 1)[HALO_CONV:, :]
            + w_ref[j, 2:3, :] * cur + b_row)


def _ffn_block_fwd(xmid, g, wup_all, convf_all, convb, wd_all, name, comms=None):
    s = xmid.shape[0]
    nt = s // TM
    half = N_DEV // 2

    def body(x_ref, g_ref, wup_ref, cw_ref, cb_ref, wd_ref, xo_ref, h_ref, u_ref, up_ref, act_ref, carry):
        i = pl.program_id(0)

        @pl.when(i == 0)
        def _():
            carry[...] = jnp.zeros_like(carry)

        xf = x_ref[...]
        r = lax.rsqrt(jnp.mean(xf * xf, axis=-1, keepdims=True) + EPS)
        h = (xf * r * g_ref[...]).astype(MXU)
        h_ref[...] = h

        def project(j):
            return _dot(h, wup_ref[j]).astype(ACT)

        def conv(j, pre):
            u_ref[j] = pre
            cur = pre.astype(F32)
            ext = jnp.concatenate([carry[j], cur], axis=0)
            carry[j] = cur[TM - HALO_CONV:, :]
            up = _conv_up(ext, cur, cw_ref, j, cb_ref[j:j + 1, :])
            up_ref[j] = up.astype(ACT)
            return up

        order = [j + k * half for j in range(half) for k in range(2)]
        acc = xf
        ahead = 2
        pres = {n: project(order[n]) for n in range(ahead)}
        ups = {}
        for n, j in enumerate(order):
            if n + ahead < N_DEV:
                pres[n + ahead] = project(order[n + ahead])
            ups[j] = conv(j, pres.pop(n))
            if j >= half:
                gate, val = ups.pop(j - half), ups.pop(j)
                act = (gate * _sigmoid(gate) * val).astype(MXU)
                act_ref[j - half] = act
                wd = jnp.concatenate([wd_ref[2 * (j - half)], wd_ref[2 * (j - half) + 1]], axis=0)
                acc = acc + _dot(act, wd)
        xo_ref[...] = acc

    row = pl.BlockSpec((TM, D), lambda i: (i, 0))
    blocks = pl.BlockSpec((N_DEV, TM, NB_UP), lambda i: (0, i, 0))
    return _call(
        body, comms, *_grid_marks(nt), (xmid, g, wup_all, convf_all, convb, wd_all),
        name=name, grid=(nt,),
        in_specs=[row, _const((1, D), (0, 0)), _const((N_DEV, D, NB_UP), (0, 0, 0)),
                  _const((N_DEV, 3, NB_UP), (0, 0, 0)), _const((N_DEV, NB_UP), (0, 0)),
                  _const((N_DEV, ROWS_DN, D), (0, 0, 0))],
        out_specs=[row, row, blocks, blocks, pl.BlockSpec((half, TM, NB_UP), lambda i: (0, i, 0))],
        out_shape=[_sds((s, D), F32), _sds((s, D), MXU), _sds((N_DEV, s, NB_UP), ACT), _sds((N_DEV, s, NB_UP), ACT),
                   _sds((half, s, NB_UP), MXU)],
        scratch_shapes=[pltpu.VMEM((N_DEV, HALO_CONV, NB_UP), F32)],
        compiler_params=_params(),
    )


def _loss_head(x, g, target, name):
    s = x.shape[0]
    nt = s // TM

    def body(x_ref, g_ref, t_ref, dx_ref, dg_ref, loss_ref):
        i = pl.program_id(0)

        @pl.when(i == 0)
        def _():
            dg_ref[...] = jnp.zeros_like(dg_ref)
            loss_ref[...] = jnp.zeros_like(loss_ref)

        xf = x_ref[...]
        r = lax.rsqrt(jnp.mean(xf * xf, axis=-1, keepdims=True) + EPS)
        xn = xf * r
        err = xn * g_ref[...] - t_ref[...]
        loss_ref[...] += 0.5 * jnp.sum(jnp.mean(err * err, axis=-1, keepdims=True), axis=0, keepdims=True)
        dy = err * (1.0 / D)
        dg_ref[0:1, :] += _colsum(dy * xn)
        dyg = dy * g_ref[...]
        dx_ref[...] = r * (dyg - xn * jnp.mean(dyg * xn, axis=-1, keepdims=True))

    return pl.pallas_call(
        body, name=name, grid=(nt,),
        in_specs=[pl.BlockSpec((TM, D), lambda i: (i, 0)), _const((1, D), (0, 0)), pl.BlockSpec((TM, D), lambda i: (i, 0))],
        out_specs=[pl.BlockSpec((TM, D), lambda i: (i, 0)), pl.BlockSpec((8, D), lambda i: (0, 0)),
                   pl.BlockSpec((1, 1), lambda i: (0, 0))],
        out_shape=[_sds((s, D), F32), _sds((8, D), F32), _sds((1, 1), F32)],
        compiler_params=_params(),
    )(x, g, target)


def _ffn_block_bwd(dxo, upre, up, convf_all, wd_all, wup_all, xmid, g, name, comms=None):
    s = dxo.shape[0]
    nt = s // TM
    half = N_DEV // 2

    def body(dx_ref, u_ref, up_ref, cw_ref, wd_ref, wup_ref, x_ref, g_ref, du_ref, dc_ref, dxm_ref, dg_ref, carry):
        step = pl.program_id(0)

        @pl.when(step == 0)
        def _():
            carry[...] = jnp.zeros_like(carry)
            dc_ref[...] = jnp.zeros_like(dc_ref)
            dg_ref[...] = jnp.zeros_like(dg_ref)

        dxo_t = dx_ref[...]
        dxb = dxo_t.astype(MXU)

        def adjoint(j, d_up):
            cur = u_ref[j].astype(F32)
            ext = jnp.concatenate([d_up, carry[j]], axis=0)
            carry[j] = d_up[:HALO_CONV, :]
            up1 = _shift_up(ext, 1)[:TM, :]
            up2 = _shift_up(ext, 2)[:TM, :]
            du = (cw_ref[j, 2:3, :] * d_up + cw_ref[j, 1:2, :] * up1 + cw_ref[j, 0:1, :] * up2).astype(du_ref.dtype)
            du_ref[j] = du
            dc_ref[j, 0:1, :] += _colsum(cur * up2)
            dc_ref[j, 1:2, :] += _colsum(cur * up1)
            dc_ref[j, 2:3, :] += _colsum(cur * d_up)
            dc_ref[j, 3:4, :] += _colsum(d_up)
            return _dot_nt(du, wup_ref[j])

        def d_act(j):
            return _dot_nt(dxb, jnp.concatenate([wd_ref[2 * j], wd_ref[2 * j + 1]], axis=0))

        dh = jnp.zeros((TM, D), F32)
        dact = d_act(0)
        for j in range(half):
            nxt = d_act(j + 1) if j + 1 < half else None
            gate = up_ref[j].astype(F32)
            val = up_ref[j + half].astype(F32)
            sg = _sigmoid(gate)
            dh = dh + adjoint(j, dact * val * sg * (1.0 + gate * (1.0 - sg)))
            dh = dh + adjoint(j + half, dact * gate * sg)
            dact = nxt

        xf = x_ref[...]
        r = lax.rsqrt(jnp.mean(xf * xf, axis=-1, keepdims=True) + EPS)
        xn = xf * r
        dg_ref[0:1, :] += _colsum(dh * xn)
        dhg = dh * g_ref[...]
        dxm_ref[...] = dxo_t + r * (dhg - xn * jnp.mean(dhg * xn, axis=-1, keepdims=True))

    blocks = pl.BlockSpec((N_DEV, TM, NB_UP), lambda i: (0, nt - 1 - i, 0))
    row = pl.BlockSpec((TM, D), lambda i: (nt - 1 - i, 0))
    return _call(
        body, comms, *_grid_marks(nt), (dxo, upre, up, convf_all, wd_all, wup_all, xmid, g),
        name=name, grid=(nt,),
        in_specs=[row, blocks, blocks, _const((N_DEV, 3, NB_UP), (0, 0, 0)), _const((N_DEV, ROWS_DN, D), (0, 0, 0)),
                  _const((N_DEV, D, NB_UP), (0, 0, 0)), row, _const((1, D), (0, 0))],
        out_specs=[blocks, pl.BlockSpec((N_DEV, 8, NB_UP), lambda i: (0, 0, 0)), row,
                   pl.BlockSpec((8, D), lambda i: (0, 0))],
        out_shape=[_sds((N_DEV, s, NB_UP), MXU), _sds((N_DEV, 8, NB_UP), F32), _sds((s, D), F32), _sds((8, D), F32)],
        scratch_shapes=[pltpu.VMEM((N_DEV, HALO_CONV, NB_UP), F32)],
        compiler_params=_params(),
    )


def _proj_bwd(dy, w_all, x, g, dres, name, comms=None):
    s = x.shape[0]
    nb = w_all.shape[-1]
    nt = s // TM

    def body(dy_ref, w_ref, x_ref, g_ref, dres_ref, dx_ref, dg_ref):
        i = pl.program_id(0)

        @pl.when(i == 0)
        def _():
            dg_ref[...] = jnp.zeros_like(dg_ref)

        dh = jnp.zeros((TM, D), F32)
        for j in range(N_DEV):
            dh = dh + _dot_nt(dy_ref[:, j * nb:(j + 1) * nb], w_ref[j])
        xf = x_ref[...]
        r = lax.rsqrt(jnp.mean(xf * xf, axis=-1, keepdims=True) + EPS)
        xn = xf * r
        dg_ref[0:1, :] += _colsum(dh * xn)
        dhg = dh * g_ref[...]
        dx_ref[...] = dres_ref[...] + r * (dhg - xn * jnp.mean(dhg * xn, axis=-1, keepdims=True))

    dy_spec = pl.BlockSpec((TM, N_DEV * nb), lambda i: (i, 0))
    row = pl.BlockSpec((TM, D), lambda i: (i, 0))
    return _call(
        body, comms, *_grid_marks(nt), (dy, w_all, x, g, dres),
        name=name, grid=(nt,),
        in_specs=[dy_spec, _const((N_DEV, D, nb), (0, 0, 0)), row, _const((1, D), (0, 0)), row],
        out_specs=[row, pl.BlockSpec((8, D), lambda i: (0, 0))],
        out_shape=[_sds((s, D), F32), _sds((8, D), F32)],
        compiler_params=_params(),
    )


def _mixer_bwd(dxmid, p, yabc, pacz, babc, wpool, pscale, gsgu, wsp, bsp_t, convc, wa_all, wb_all, wc_all, wo_all,
               name, comms=None):
    s = dxmid.shape[0]
    nt = s // TM

    def body(dx_ref, p_ref, y_ref, pz_ref, b_ref, wpool_ref, ps_ref, gs_ref, wsp_ref, bsp_ref, cc_ref,
             wa_ref, wb_ref, wc_ref, wo_ref,
             dp_ref, db_ref, dwp_ref, dws_ref, small_ref, dbs_ref,
             carry_pa, carry_cz, dbs_acc, du_s, dvn_s):
        step = pl.program_id(0)
        tile = nt - 1 - step

        @pl.when(step == 0)
        def _():
            for ref in (carry_pa, carry_cz, dbs_acc, dwp_ref, dws_ref, small_ref, dbs_ref):
                ref[...] = jnp.zeros_like(ref)

        def pf(lo, n):
            return p_ref[:, lo:lo + n].astype(F32)

        dm = _dot_nt(dx_ref[...].astype(MXU), wo_ref[...].reshape(D, D))

        def through_gate(k, glo, w_ref):
            sg = _sigmoid(pf(glo, D))
            br = b_ref[:, k * D:(k + 1) * D].astype(F32)
            dp_ref[:, glo:glo + D] = (dm * br * sg * (1.0 - sg)).astype(dp_ref.dtype)
            dbr = (dm * sg).astype(MXU)
            db_ref[:, k * D:(k + 1) * D] = dbr
            return _dot_nt(dbr, _lane_cat(w_ref))

        dya = through_gate(0, GA0, wa_ref)
        dyb = through_gate(1, GB0, wb_ref)
        dyc = through_gate(2, GC0, wc_ref)

        t_pos = (tile * TM + lax.broadcasted_iota(jnp.int32, (TM, 1), 0)).astype(F32)
        for g, win in enumerate(POOL_WINDOWS):
            cols = slice(g * CHUNK, (g + 1) * CHUNK)
            pa_g = pz_ref[:, cols]
            q = _dot(pa_g, wpool_ref[g])
            dya_g = dya[:, cols]
            small_ref[0:1, cols] += _colsum(dya_g * q)
            dq = (dya_g * ps_ref[:, cols]).astype(MXU)
            dpa_g = _dot_nt(dq, wpool_ref[g])
            dwp_ref[g] += _dot_tn(pa_g, dq)
            dpw = dpa_g / jnp.minimum(t_pos + 1.0, float(win))
            acc = jnp.concatenate([dpw, carry_pa[:, cols]], axis=0)
            carry_pa[:, cols] = dpw[:HALO_POOL, :]
            k = 1
            while k < win:
                acc = acc + _shift_up(acc, k)
                k *= 2
            dp_ref[:, cols] = (acc[:TM, :] - dpa_g).astype(dp_ref.dtype)

        uvp = pf(UV0, 2 * WA)
        uvg, dgelu = _gelu_parts(uvp)
        u = uvg[:, :WA]
        v = uvg[:, WA:]
        rv = lax.rsqrt(jnp.mean(v * v, axis=-1, keepdims=True) + EPS)
        vh = v * rv
        vn = (vh * gs_ref[...]).astype(MXU)
        mask = _tril_mask()
        for g in range(HEADS):
            cols = slice(g * CHUNK, (g + 1) * CHUNK)
            wt32 = jnp.where(mask, wsp_ref[g], 0.0)
            wt = wt32.astype(MXU)
            wt_t = wt32.T.astype(MXU)
            bcol = bsp_ref[:, g:g + 1]
            for c in range(TM // CHUNK):
                rows = slice(c * CHUNK, (c + 1) * CHUNK)
                vn_cg = vn[rows, cols]
                sv = _dot(wt, vn_cg) + bcol
                dyb_cg = dyb[rows, cols]
                du_s[rows, cols] = dyb_cg * sv
                dsv = dyb_cg * u[rows, cols]
                dbs_acc[g] += dsv
                dsv_b = dsv.astype(MXU)
                dws_ref[g] += _dot_nt(dsv_b, vn_cg)
                dvn_s[rows, cols] = _dot(wt_t, dsv_b)
        dvn = dvn_s[...]
        small_ref[1:2, :] += _colsum(dvn * vh)
        dvg = dvn * gs_ref[...]
        dv = rv * (dvg - vh * jnp.mean(dvg * vh, axis=-1, keepdims=True))
        dp_ref[:, UV0:UV0 + WA] = (du_s[...] * dgelu[:, :WA]).astype(dp_ref.dtype)
        dp_ref[:, UV0 + WA:UV0 + 2 * WA] = (dv * dgelu[:, WA:]).astype(dp_ref.dtype)

        cb = pf(CB0, WA)
        cc = pf(CC0, WA)
        cx = pf(CX0, WA)
        z = cc * cx
        dp_ref[:, CB0:CB0 + WA] = (dyc * pz_ref[:, WA:2 * WA].astype(F32)).astype(dp_ref.dtype)
        dcz = dyc * cb
        extz = jnp.concatenate([dcz, carry_cz[...]], axis=0)
        carry_cz[...] = dcz[:HALO_CONV, :]
        up1 = _shift_up(extz, 1)[:TM, :]
        up2 = _shift_up(extz, 2)[:TM, :]
        dz = cc_ref[2:3, :] * dcz + cc_ref[1:2, :] * up1 + cc_ref[0:1, :] * up2
        small_ref[2:3, :] += _colsum(z * up2)
        small_ref[3:4, :] += _colsum(z * up1)
        small_ref[4:5, :] += _colsum(z * dcz)
        dp_ref[:, CC0:CC0 + WA] = (dz * cx).astype(dp_ref.dtype)
        dp_ref[:, CX0:CX0 + WA] = (dz * cc).astype(dp_ref.dtype)

        @pl.when(step == nt - 1)
        def _():
            ones = jnp.ones((8, CHUNK), F32)
            for g in range(HEADS):
                dws_ref[g] = jnp.where(mask, dws_ref[g], 0.0)
                row = lax.dot_general(ones, dbs_acc[g], (((1,), (1,)), ((), ())), preferred_element_type=F32,
                                      precision=lax.Precision.HIGHEST)
                dbs_ref[g:g + 1, :] = row[0:1, :]

    row = lambda n: pl.BlockSpec((TM, n), lambda i: (nt - 1 - i, 0))
    br_spec = _const((N_DEV, WA, NB_BR), (0, 0, 0))
    acc_spec = lambda shape: pl.BlockSpec(shape, lambda i: (0,) * len(shape))
    return _call(
        body, comms, *_grid_marks(nt),
        (dxmid, p, yabc, pacz, babc, wpool, pscale, gsgu, wsp, bsp_t, convc, wa_all, wb_all, wc_all, wo_all),
        name=name, grid=(nt,),
        in_specs=[row(D), row(NCOL), row(3 * WA), row(2 * WA), row(3 * D),
                  _const((HEADS, CHUNK, CHUNK), (0, 0, 0)), _const((1, WA), (0, 0)), _const((1, WA), (0, 0)),
                  _const((HEADS, CHUNK, CHUNK), (0, 0, 0)), _const((CHUNK, HEADS), (0, 0)), _const((3, WA), (0, 0)),
                  br_spec, br_spec, br_spec, _const((N_DEV, ROWS_O, D), (0, 0, 0))],
        out_specs=[row(NCOL), row(3 * D), acc_spec((HEADS, CHUNK, CHUNK)), acc_spec((HEADS, CHUNK, CHUNK)),
                   acc_spec((8, WA)), acc_spec((8, CHUNK))],
        out_shape=[_sds((s, NCOL), MXU), _sds((s, 3 * D), MXU), _sds((HEADS, CHUNK, CHUNK), F32),
                   _sds((HEADS, CHUNK, CHUNK), F32), _sds((8, WA), F32), _sds((8, CHUNK), F32)],
        scratch_shapes=[pltpu.VMEM((HALO_POOL, WA), F32), pltpu.VMEM((HALO_CONV, WA), F32),
                        pltpu.VMEM((HEADS, CHUNK, CHUNK), F32), pltpu.VMEM((TM, WA), F32), pltpu.VMEM((TM, WA), F32)],
        compiler_params=_params(),
    )


def _wgrad(a, b, a_spec, b_spec, n_out, acc_shape, out_shape, out_spec, store, name, comms=None):
    s = a.shape[-2]
    ts = min(TS_WGRAD if b.dtype == MXU else TS_WGRAD // 2, s)
    n_steps = s // ts

    def body(a_ref, b_ref, o_ref, acc_ref):
        k = pl.program_id(1)

        @pl.when(k == 0)
        def _():
            acc_ref[...] = jnp.zeros_like(acc_ref)

        acc_ref[...] += _dot_tn(a_ref[...], b_ref[...].astype(MXU))

        @pl.when(k == n_steps - 1)
        def _():
            store(o_ref, acc_ref)

    (res,), extra = _call(
        body, comms, lambda: (pl.program_id(0) == 0) & (pl.program_id(1) == 0),
        lambda: (pl.program_id(0) == (3 * n_out) // 4) & (pl.program_id(1) == 0),
        lambda: (pl.program_id(0) == n_out - 1) & (pl.program_id(1) == n_steps - 1), (a, b),
        name=name, grid=(n_out, n_steps),
        in_specs=[a_spec(ts), b_spec(ts)], out_specs=[out_spec], out_shape=[_sds(out_shape, WIRE)],
        scratch_shapes=[pltpu.VMEM(acc_shape, F32)],
        compiler_params=_params(2),
    )
    return res, extra


def _store_plain(o_ref, acc_ref):
    o_ref[...] = acc_ref[...].astype(o_ref.dtype)


def _store_lane_blocks(o_ref, acc_ref):
    for d in range(N_DEV):
        o_ref[d] = acc_ref[:, d * NB_BR:(d + 1) * NB_BR].astype(o_ref.dtype)


def _wgrad_in(h, dp, name, comm=None):
    return _wgrad(h, dp, lambda ts: pl.BlockSpec((ts, D), lambda j, k: (k, 0)),
                  lambda ts: pl.BlockSpec((ts, NB_IN), lambda j, k: (k, j)), N_DEV, (D, NB_IN),
                  (N_DEV, D, NB_IN), pl.BlockSpec((None, D, NB_IN), lambda j, k: (j, 0, 0)), _store_plain, name, comm)


def _wgrad_up(h, du, name, comm=None):
    return _wgrad(h, du, lambda ts: pl.BlockSpec((ts, D), lambda j, k: (k, 0)),
                  lambda ts: pl.BlockSpec((None, ts, NB_UP), lambda j, k: (j, k, 0)), N_DEV, (D, NB_UP),
                  (N_DEV, D, NB_UP), pl.BlockSpec((None, D, NB_UP), lambda j, k: (j, 0, 0)), _store_plain, name, comm)


def _wgrad_down(act, dxo, name, comm=None):
    return _wgrad(act, dxo, lambda ts: pl.BlockSpec((None, ts, NB_UP), lambda j, k: (j, k, 0)),
                  lambda ts: pl.BlockSpec((ts, D), lambda j, k: (k, 0)), N_DEV // 2, (NB_UP, D),
                  (DFF, D), pl.BlockSpec((NB_UP, D), lambda j, k: (j, 0)), _store_plain, name, comm)


def _wgrad_o(merged, dxmid, name, comm=None):
    return _wgrad(merged, dxmid, lambda ts: pl.BlockSpec((ts, D), lambda j, k: (k, 0)),
                  lambda ts: pl.BlockSpec((ts, D), lambda j, k: (k, 0)), 1, (D, D),
                  (D, D), pl.BlockSpec((D, D), lambda j, k: (0, 0)), _store_plain, name, comm)


def _wgrad_branches(yabc, dbabc, name, comm=None):
    return _wgrad(yabc, dbabc, lambda ts: pl.BlockSpec((ts, WA), lambda j, k: (k, j)),
                  lambda ts: pl.BlockSpec((ts, D), lambda j, k: (k, j)), 3, (WA, D),
                  (N_DEV, 3, WA, NB_BR), pl.BlockSpec((N_DEV, None, WA, NB_BR), lambda j, k: (0, j, 0, 0)),
                  _store_lane_blocks, name, comm)


def _adamw_math(g, w, m, v):
    m = ADAM_B1 * m + (1.0 - ADAM_B1) * g
    v = ADAM_B2 * v + (1.0 - ADAM_B2) * (g * g)
    m_hat = m / (1.0 - ADAM_B1 ** ADAM_STEP)
    v_hat = v / (1.0 - ADAM_B2 ** ADAM_STEP)
    delta = -ADAM_LR * (m_hat / (jnp.sqrt(v_hat) + ADAM_EPS) + ADAM_WD * w)
    return delta, m, v


def _adamw_sum(parts, mid, w, m, v, layer, prev, tr, name):
    n_layers, r, c = w.shape

    def body(p_ref, w_ref, m_ref, v_ref, *rest):
        g_ref, d_ref, mo_ref, vo_ref = rest[-4:]
        g = p_ref[0].astype(F32)
        for k in range(1, N_DEV):
            g = g + p_ref[k].astype(F32)
        g_ref[...] = g
        d_ref[...], mo_ref[...], vo_ref[...] = _adamw_math(g, w_ref[...], m_ref[...], v_ref[...])

    blk = pl.BlockSpec((None, tr, c), lambda i: (layer, i, 0))
    extra = [] if prev is None else list(prev)
    return pl.pallas_call(
        body, name=name, grid=(r // tr,),
        in_specs=[pl.BlockSpec((N_DEV, None, tr, c), lambda i: (0, mid, i, 0)), blk, blk, blk]
        + [pl.BlockSpec(memory_space=pl.ANY)] * len(extra),
        out_specs=[blk] * 4, out_shape=[_sds((n_layers, r, c), F32)] * 4,
        input_output_aliases={4 + k: k for k in range(len(extra))},
        compiler_params=_params(),
    )(parts, w, m, v, *extra)


def _sum_parts(parts, name):
    _, r, c = parts.shape

    def body(p_ref, o_ref):
        g = p_ref[0]
        for k in range(1, N_DEV):
            g = g + p_ref[k]
        o_ref[...] = g

    return pl.pallas_call(body, name=name, out_shape=_sds((r, c), F32),
                          compiler_params=pltpu.CompilerParams(vmem_limit_bytes=VMEM_LIMIT))(parts)


def _adamw_small(g, w, m, v, name):
    def body(g_ref, w_ref, m_ref, v_ref, d_ref, mo_ref, vo_ref):
        d_ref[...], mo_ref[...], vo_ref[...] = _adamw_math(g_ref[...], w_ref[...], m_ref[...], v_ref[...])

    return pl.pallas_call(body, name=name, out_shape=[_sds(w.shape, F32)] * 3)(g, w, m, v)


HBM_SPEC = pl.BlockSpec(memory_space=pltpu.HBM)


def _position():
    return lax.axis_index("x"), lax.axis_index("y"), lax.axis_index("c")


def _device_index(chip, core):
    return 4 * chip[0] + 2 * chip[1] + core


def _gather(shards, layer):
    n = len(shards)
    per = 8

    def first_copies(ins, outs, send, recv):
        x, y, c = _position()
        me = 4 * x + 2 * y + c
        targets = [(x, y, 1 - c), (1 - x, y, c), (x, 1 - y, c), (1 - x, 1 - y, c)]
        remote = [pltpu.make_async_remote_copy(
            src_ref=ins[t].at[layer], dst_ref=outs[t].at[me], send_sem=send.at[per * t + k],
            recv_sem=recv.at[per * t + k], device_id=to, device_id_type=MESH)
            for t in range(n) for k, to in enumerate(targets)]
        local = [pltpu.make_async_copy(ins[t].at[layer], outs[t].at[me], send.at[per * t + 4]) for t in range(n)]
        return remote, local

    def passed_on(outs, send, recv):
        x, y, c = _position()
        chips = [(1 - x, y), (x, 1 - y), (1 - x, 1 - y)]
        return [pltpu.make_async_remote_copy(
            src_ref=outs[t].at[_device_index(chip, c)], dst_ref=outs[t].at[_device_index(chip, c)],
            send_sem=send.at[per * t + 5 + j], recv_sem=recv.at[per * t + 5 + j], device_id=(x, y, 1 - c),
            device_id_type=MESH)
            for t in range(n) for j, chip in enumerate(chips)]

    def start(ins, outs, send, recv):
        remote, local = first_copies(ins, outs, send, recv)
        for cp in local + remote:
            cp.start()

    def mid(ins, outs, send, recv):
        remote, local = first_copies(ins, outs, send, recv)
        for cp in remote:
            cp.wait()
        for cp in local:
            cp.wait()
        for cp in passed_on(outs, send, recv):
            cp.start()

    def finish(ins, outs, send, recv):
        for cp in passed_on(outs, send, recv):
            cp.wait()

    return _Comm(shards, [_sds((N_DEV,) + a.shape[1:], a.dtype) for a in shards], per * n, start, finish, mid)


def _run_comms(comms, name):
    def body():
        pass

    _, extra = _call(body, comms, None, None, None, (), name=name, in_specs=[], out_specs=[], out_shape=[])
    return extra


def _exchange(parts):
    n = len(parts)
    per = 8
    flips = [(0, 0, 1), (1, 0, 0), (1, 0, 1), (0, 1, 0), (0, 1, 1), (1, 1, 0), (1, 1, 1)]

    def copies(ins, outs, send, recv):
        x, y, c = _position()
        me = 4 * x + 2 * y + c
        remote = []
        for t in range(n):
            for k, (fx, fy, fc) in enumerate(flips):
                peer = ((1 - x if fx else x), (1 - y if fy else y), (1 - c if fc else c))
                remote.append(pltpu.make_async_remote_copy(
                    src_ref=ins[t].at[_device_index(peer[:2], peer[2])], dst_ref=outs[t].at[me],
                    send_sem=send.at[per * t + k], recv_sem=recv.at[per * t + k], device_id=peer, device_id_type=MESH))
        local = [pltpu.make_async_copy(ins[t].at[me], outs[t].at[me], send.at[per * t + 7]) for t in range(n)]
        return remote, local

    def start(ins, outs, send, recv):
        remote, local = copies(ins, outs, send, recv)
        for cp in local + remote:
            cp.start()

    def finish(ins, outs, send, recv):
        remote, local = copies(ins, outs, send, recv)
        for cp in remote:
            cp.wait()
        for cp in local:
            cp.wait()

    return _Comm(parts, [_sds(a.shape, a.dtype) for a in parts], per * n, start, finish)


def _rows128(a):
    return a.reshape(-1, 128)


def kernel(x, g_mix, w_in, w_pool, pool_scale, g_sgu, w_spatial, b_spatial, conv_c, w_branch_a, w_branch_b, w_branch_c, w_o, g_ffn, w_up, conv_ffn, conv_ffn_b, w_down, g_final, loss_target, m_g_mix, m_w_in, m_w_pool, m_pool_scale, m_g_sgu, m_w_spatial, m_b_spatial, m_conv_c, m_w_branch_a, m_w_branch_b, m_w_branch_c, m_w_o, m_g_ffn, m_w_up, m_conv_ffn, m_conv_ffn_b, m_w_down, m_g_final, v_g_mix, v_w_in, v_w_pool, v_pool_scale, v_g_sgu, v_w_spatial, v_b_spatial, v_conv_c, v_w_branch_a, v_w_branch_b, v_w_branch_c, v_w_o, v_g_ffn, v_w_up, v_conv_ffn, v_conv_ffn_b, v_w_down, v_g_final):
    s = x.shape[1]
    n_layers = g_mix.shape[0]
    x0 = x.reshape(s, D)
    target = loss_target.reshape(s, D)
    me = 4 * lax.axis_index("x") + 2 * lax.axis_index("y") + lax.axis_index("c")

    first_shards = [w_in.astype(MXU), conv_c]
    mix_shards = [w_branch_a.astype(MXU), w_branch_b.astype(MXU), w_branch_c.astype(MXU), w_o.astype(MXU),
                  w_down.astype(MXU)]
    up_shards = [w_up.astype(MXU), conv_ffn]
    (first_now,) = _run_comms([_gather(first_shards, 0)], "gather_first_0")
    mix_now = up_now = None
    wpool_b = w_pool.astype(MXU)
    bsp_t = jnp.swapaxes(b_spatial, 1, 2)
    convb_blk = conv_ffn_b.reshape(n_layers, N_DEV, NB_UP)

    saved = []
    weights = []
    xl = x0
    for l in range(n_layers):
        win8, convc8 = first_now
        convc_full = jnp.transpose(convc8, (1, 0, 2)).reshape(3, WA)
        more = l + 1 < n_layers
        (p, h), got = _rms_proj(xl, g_mix[l:l + 1], win8, f"in_proj_{l}",
                                [_gather(mix_shards if l == 0 else up_shards, l)])
        if l == 0:
            (mix_now,) = got
        else:
            (up_now,) = got
        wa8, wb8, wc8, wo8, wd8 = mix_now
        (xmid, yabc, pacz, babc, merged), got = _mixer_fwd(
            xl, p, wpool_b[l], pool_scale[l:l + 1], g_sgu[l:l + 1], w_spatial[l], bsp_t[l], convc_full,
            wa8, wb8, wc8, wo8, f"mixer_fwd_{l}", [_gather(up_shards, l)] if l == 0 else None)
        if l == 0:
            (up_now,) = got
        wup8, convf8 = up_now
        weights.append((win8, wa8, wb8, wc8, wo8, wup8, wd8, convc_full, convf8))
        (xout, h2, upre, up, act), got = _ffn_block_fwd(
            xmid, g_ffn[l:l + 1], wup8, convf8, convb_blk[l], wd8, f"ffn_fwd_{l}",
            [_gather(first_shards, l + 1), _gather(mix_shards, l + 1)] if more else None)
        if more:
            first_now, mix_now = got
        saved.append((xl, p, h, xmid, yabc, pacz, babc, merged, upre, h2, act, up))
        xl = xout

    dx, dg_final, loss_local = _loss_head(xl, g_final.reshape(1, D), target, "loss_head")

    received = [dict() for _ in range(n_layers)]
    small = {("final", "g_final"): dg_final}
    small_sums = {}
    waiting = None

    def exchange_of(named):
        return [_exchange([a for _, a in named])]

    def land(layer, named, got):
        received[layer].update({k: a for (k, _), a in zip(named, got[0])})

    def gather_small(keys):
        packed = jnp.concatenate([_rows128(small[k]) for k in keys], axis=0)[None]
        return _gather([packed], 0)

    def sum_small(keys, gathered, name):
        summed = _sum_parts(gathered, name)
        row = 0
        for k in keys:
            n_rows = small[k].size // 128
            small_sums[k] = summed[row:row + n_rows].reshape(small[k].shape)
            row += n_rows

    for l in reversed(range(n_layers)):
        xin, p, h, xmid, yabc, pacz, babc, merged, upre, h2, act, up = saved[l]
        win8, wa8, wb8, wc8, wo8, wup8, wd8, convc_full, convf8 = weights[l]
        last = l == 0
        (dupre, dconvf, dxmid, dg_ffn), got = _ffn_block_bwd(
            dx, upre, up, convf8, wd8, wup8, xmid, g_ffn[l:l + 1], f"ffn_bwd_{l}",
            None if waiting is None else exchange_of(waiting[1]))
        if waiting is not None:
            land(waiting[0], waiting[1], got)
        small[(l, "conv_ffn")] = dconvf
        small[(l, "g_ffn")] = dg_ffn
        keys_a = [k for k in small if k not in small_sums]
        g_wdown, got = _wgrad_down(act, dx, f"wgrad_down_{l}", [gather_small(keys_a)] if last else None)
        if last:
            sum_small(keys_a, got[0][0], "sum_small_grads_a")
        down = [("w_down", g_wdown.reshape(N_DEV, ROWS_DN, D))]
        g_wup, got = _wgrad_up(h2, dupre, f"wgrad_up_{l}", exchange_of(down) if last else None)
        if last:
            land(l, down, got)
        upw = [("w_up", g_wup)]
        (dp, dbabc, dwp, dws, mixer_small, dbs), got = _mixer_bwd(
            dxmid, p, yabc, pacz, babc, wpool_b[l], pool_scale[l:l + 1], g_sgu[l:l + 1], w_spatial[l], bsp_t[l],
            convc_full, wa8, wb8, wc8, wo8, f"mixer_bwd_{l}", exchange_of(upw if last else down))
        land(l, upw if last else down, got)
        small.update({(l, "w_pool"): dwp, (l, "mixer_small"): mixer_small, (l, "w_spatial"): dws,
                      (l, "b_spatial"): dbs})
        g_wo, _ = _wgrad_o(merged, dxmid, f"wgrad_o_{l}")
        g_br, _ = _wgrad_branches(yabc, dbabc, f"wgrad_branches_{l}")
        mixer_w = [("branches", g_br), ("w_o", g_wo.reshape(N_DEV, ROWS_O, D))]
        keys_b = [k for k in small if k not in small_sums]
        g_win, got = _wgrad_in(h, dp, f"wgrad_in_{l}",
                               exchange_of(mixer_w) + [gather_small(keys_b)] if last else exchange_of(upw))
        if last:
            land(l, mixer_w, got)
            sum_small(keys_b, got[1][0], "sum_small_grads_b")
        else:
            land(l, upw, got)
        inw = [("w_in", g_win)]
        (dx, dg_mix), got = _proj_bwd(dp, win8, xin, g_mix[l:l + 1], dxmid, f"in_proj_bwd_{l}",
                                      exchange_of(inw if last else mixer_w))
        land(l, inw if last else mixer_w, got)
        waiting = None if last else (l, inw)
        small[(l, "g_mix")] = dg_mix
    grad_x = dx.reshape(1, s, D)
    late_keys = [k for k in small if k not in small_sums]
    (gathered_late,) = _run_comms([gather_small(late_keys)], "gather_last_small_grads")[0]
    sum_small(late_keys, gathered_late, "sum_last_small_grads")

    def update_big(key, mid, w, m, v, tr, tag):
        outs = None
        for l in range(n_layers):
            parts = received[l][key]
            if parts.ndim == 3:
                parts = parts.reshape(N_DEV, 1, *parts.shape[1:])
            outs = _adamw_sum(parts, mid, w, m, v, l, outs, tr, f"adamw_{tag}_{l}")
        return outs

    up_in = update_big("w_in", 0, w_in, m_w_in, v_w_in, 256, "w_in")
    up_a = update_big("branches", 0, w_branch_a, m_w_branch_a, v_w_branch_a, WA, "w_branch_a")
    up_b = update_big("branches", 1, w_branch_b, m_w_branch_b, v_w_branch_b, WA, "w_branch_b")
    up_c = update_big("branches", 2, w_branch_c, m_w_branch_c, v_w_branch_c, WA, "w_branch_c")
    up_o = update_big("w_o", 0, w_o, m_w_o, v_w_o, ROWS_O, "w_o")
    up_up = update_big("w_up", 0, w_up, m_w_up, v_w_up, 256, "w_up")
    up_down = update_big("w_down", 0, w_down, m_w_down, v_w_down, ROWS_DN, "w_down")

    stack = lambda kind: jnp.stack([small_sums[(l, kind)] for l in range(n_layers)], axis=0)
    grad_g_mix = stack("g_mix")[:, 0, :]
    grad_w_pool = stack("w_pool")
    mixer_sums = stack("mixer_small")
    grad_pool_scale = mixer_sums[:, 0, :]
    grad_g_sgu = mixer_sums[:, 1, :]
    grad_conv_c = lax.dynamic_slice_in_dim(mixer_sums[:, 2:5, :], me * (WA // N_DEV), WA // N_DEV, axis=2)
    grad_w_spatial = stack("w_spatial")
    grad_b_spatial = stack("b_spatial")[:, 0:HEADS, :]
    grad_g_ffn = stack("g_ffn")[:, 0, :]
    conv_grads = stack("conv_ffn")
    grad_conv_ffn = lax.dynamic_index_in_dim(conv_grads, me, axis=1, keepdims=False)[:, 0:3, :]
    grad_conv_ffn_b = conv_grads[:, :, 3, :].reshape(n_layers, 2 * DFF)
    grad_g_final = small_sums[("final", "g_final")][0]

    def update_small(g, w, m, v, tag):
        shape2 = (-1, w.shape[-1])
        outs = _adamw_small(g.reshape(shape2), w.reshape(shape2), m.reshape(shape2), v.reshape(shape2), f"adamw_{tag}")
        return [g] + [o.reshape(w.shape) for o in outs]

    up = {
        "g_mix": update_small(grad_g_mix, g_mix, m_g_mix, v_g_mix, "g_mix"),
        "w_in": up_in,
        "w_pool": update_small(grad_w_pool, w_pool, m_w_pool, v_w_pool, "w_pool"),
        "pool_scale": update_small(grad_pool_scale, pool_scale, m_pool_scale, v_pool_scale, "pool_scale"),
        "g_sgu": update_small(grad_g_sgu, g_sgu, m_g_sgu, v_g_sgu, "g_sgu"),
        "w_spatial": update_small(grad_w_spatial, w_spatial, m_w_spatial, v_w_spatial, "w_spatial"),
        "b_spatial": update_small(grad_b_spatial, b_spatial, m_b_spatial, v_b_spatial, "b_spatial"),
        "conv_c": update_small(grad_conv_c, conv_c, m_conv_c, v_conv_c, "conv_c"),
        "w_branch_a": up_a,
        "w_branch_b": up_b,
        "w_branch_c": up_c,
        "w_o": up_o,
        "g_ffn": update_small(grad_g_ffn, g_ffn, m_g_ffn, v_g_ffn, "g_ffn"),
        "w_up": up_up,
        "conv_ffn": update_small(grad_conv_ffn, conv_ffn, m_conv_ffn, v_conv_ffn, "conv_ffn"),
        "conv_ffn_b": update_small(grad_conv_ffn_b, conv_ffn_b, m_conv_ffn_b, v_conv_ffn_b, "conv_ffn_b"),
        "w_down": up_down,
        "g_final": update_small(grad_g_final, g_final, m_g_final, v_g_final, "g_final"),
    }
    loss = lax.psum(loss_local[0, 0], AXES)
    order = list(up)
    return (loss, grad_x, *[up[k][0] for k in order], *[up[k][1] for k in order], *[up[k][2] for k in order],
            *[up[k][3] for k in order])
```

```python
import functools

import jax
import jax.numpy as jnp
from jax import lax
from jax.experimental import pallas as pl
from jax.experimental.pallas import tpu as pltpu

F32 = jnp.float32
BF16 = jnp.bfloat16
MXU = BF16
ACT = BF16
WIRE = BF16

N_DEV = 8
D = 1024
WA = 512
NCOL = 6144
DFF = 2816
NB_IN = NCOL // N_DEV
NB_UP = 2 * DFF // N_DEV
NB_BR = D // N_DEV
ROWS_O = D // N_DEV
ROWS_DN = DFF // N_DEV
CHUNK = 128
HEADS = 4
POOL_WINDOWS = (2, 4, 8, 16)
EPS = 1e-6
A0, UV0, CB0, CC0, CX0, GA0, GB0, GC0 = 0, 512, 1536, 2048, 2560, 3072, 4096, 5120

ADAM_LR = 0.001
ADAM_B1 = 0.9
ADAM_B2 = 0.999
ADAM_EPS = 1e-08
ADAM_WD = 0.01
ADAM_STEP = 10

TM = 256
TS_WGRAD = 4096
HALO_POOL = 16
HALO_CONV = 8
VMEM_LIMIT = 56 * 1024 * 1024
MESH = pl.DeviceIdType.MESH
AXES = ("x", "y", "c")


def _sds(shape, dtype):
    return jax.ShapeDtypeStruct(tuple(shape), dtype)


def _params(n_grid=1):
    return pltpu.CompilerParams(dimension_semantics=("arbitrary",) * n_grid, vmem_limit_bytes=VMEM_LIMIT)


def _const(block, index):
    return pl.BlockSpec(block, lambda *_: index, pipeline_mode=pl.Buffered(1))


def _dot(a, b):
    return jnp.dot(a, b, preferred_element_type=F32)


def _dot_nt(a, b):
    return lax.dot_general(a, b, (((1,), (1,)), ((), ())), preferred_element_type=F32)


def _dot_tn(a, b):
    return lax.dot_general(a, b, (((0,), (0,)), ((), ())), preferred_element_type=F32)


def _sigmoid(v):
    return 0.5 * jnp.tanh(0.5 * v) + 0.5


def _shift_down(v, k):
    return pltpu.roll(v, k, axis=0)


def _shift_up(v, k):
    return pltpu.roll(v, v.shape[0] - k, axis=0)


def _colsum(v):
    return jnp.sum(v, axis=0, keepdims=True)


def _lane_cat(ref):
    return jnp.concatenate([ref[d] for d in range(N_DEV)], axis=1)


class _Comm:
    def __init__(self, operands, out_shapes, n_sems, start, finish, mid=None):
        self.operands = list(operands)
        self.out_shapes = list(out_shapes)
        self.n_sems = n_sems
        self.start = start
        self.mid = mid
        self.finish = finish


def _call(body, comms, is_first, is_mid, is_last, operands, *, in_specs, out_specs, out_shape, scratch_shapes=(), **kw):
    n_in, n_out, n_scr = len(in_specs), len(out_specs), len(scratch_shapes)
    comms = [c for c in (comms or []) if c is not None]
    if not comms:
        res = pl.pallas_call(body, in_specs=in_specs, out_specs=out_specs, out_shape=out_shape,
                             scratch_shapes=list(scratch_shapes), **kw)(*operands)
        return res, []
    nci = [len(c.operands) for c in comms]
    nco = [len(c.out_shapes) for c in comms]

    def split(refs, sizes):
        parts = []
        for n in sizes:
            parts.append(refs[:n])
            refs = refs[n:]
        return parts, refs

    def carrier(*refs):
        ins, refs = refs[:n_in], refs[n_in:]
        cins, refs = split(refs, nci)
        outs, refs = refs[:n_out], refs[n_out:]
        couts, refs = split(refs, nco)
        scr, sems = refs[:n_scr], refs[n_scr:]

        def run(step):
            for k, c in enumerate(comms):
                if getattr(c, step) is not None:
                    getattr(c, step)(cins[k], couts[k], sems[2 * k], sems[2 * k + 1])

        def at(mark, step):
            if mark is None:
                run(step)
            else:
                pl.when(mark())(lambda: run(step))

        at(is_first, "start")
        body(*ins, *outs, *scr)
        at(is_mid, "mid")
        at(is_last, "finish")

    res = pl.pallas_call(
        carrier, in_specs=list(in_specs) + [HBM_SPEC] * sum(nci), out_specs=list(out_specs) + [HBM_SPEC] * sum(nco),
        out_shape=list(out_shape) + [s for c in comms for s in c.out_shapes],
        scratch_shapes=list(scratch_shapes) + [pltpu.SemaphoreType.DMA((c.n_sems,)) for c in comms for _ in range(2)],
        **kw,
    )(*operands, *[a for c in comms for a in c.operands])
    extra, _ = split(res[n_out:], nco)
    return res[:n_out], extra


def _grid_marks(nt):
    return (lambda: pl.program_id(0) == 0), (lambda: pl.program_id(0) == (3 * nt) // 4), (lambda: pl.program_id(0) == nt - 1)


def _rms_proj(x, g, w_all, name, comms=None):
    s = x.shape[0]
    nb = w_all.shape[-1]
    nt = s // TM

    def body(x_ref, g_ref, w_ref, p_ref, h_ref):
        xf = x_ref[...]
        r = lax.rsqrt(jnp.mean(xf * xf, axis=-1, keepdims=True) + EPS)
        h = (xf * r * g_ref[...]).astype(MXU)
        h_ref[...] = h
        for j in range(N_DEV):
            p_ref[:, j * nb:(j + 1) * nb] = _dot(h, w_ref[j]).astype(p_ref.dtype)

    row = lambda n: pl.BlockSpec((TM, n), lambda i: (i, 0))
    return _call(
        body, comms, *_grid_marks(nt), (x, g, w_all),
        name=name, grid=(nt,),
        in_specs=[row(D), _const((1, D), (0, 0)), _const((N_DEV, D, nb), (0, 0, 0))],
        out_specs=[row(N_DEV * nb), row(D)],
        out_shape=[_sds((s, N_DEV * nb), ACT), _sds((s, D), MXU)],
        compiler_params=_params(),
    )


def _tril_mask():
    r = lax.broadcasted_iota(jnp.int32, (CHUNK, CHUNK), 0)
    c = lax.broadcasted_iota(jnp.int32, (CHUNK, CHUNK), 1)
    return r >= c


def _gelu_parts(v):
    c0 = 0.7978845608028654
    th = jnp.tanh(c0 * (v + 0.044715 * (v * v * v)))
    cdf = 0.5 * (1.0 + th)
    dgelu = cdf + v * (0.5 * c0) * (1.0 - th * th) * (1.0 + 3.0 * 0.044715 * (v * v))
    return v * cdf, dgelu


def _mixer_fwd(x, p, wpool, pscale, gsgu, wsp, bsp_t, convc, wa_all, wb_all, wc_all, wo_all, name, comms=None,
               in_proj=None):
    s = x.shape[0]
    nt = s // TM
    fused = in_proj is not None

    def body(*refs):
        if fused:
            (x_ref, g_ref, win_ref, wpool_ref, ps_ref, gs_ref, wsp_ref, bsp_ref, cc_ref, wa_ref, wb_ref, wc_ref, wo_ref,
             xmid_ref, y_ref, pz_ref, b_ref, m_ref, p_ref, h_ref, carry_a, carry_z) = refs
        else:
            (x_ref, p_ref, wpool_ref, ps_ref, gs_ref, wsp_ref, bsp_ref, cc_ref, wa_ref, wb_ref, wc_ref, wo_ref,
             xmid_ref, y_ref, pz_ref, b_ref, m_ref, carry_a, carry_z) = refs
        i = pl.program_id(0)

        @pl.when(i == 0)
        def _():
            carry_a[...] = jnp.zeros_like(carry_a)
            carry_z[...] = jnp.zeros_like(carry_z)

        def pf(lo, n):
            return p_ref[:, lo:lo + n].astype(F32)

        def project(blocks):
            if fused:
                for j in blocks:
                    p_ref[:, j * NB_IN:(j + 1) * NB_IN] = _dot(h_ref[...], win_ref[j]).astype(ACT)

        if fused:
            xf = x_ref[...]
            r = lax.rsqrt(jnp.mean(xf * xf, axis=-1, keepdims=True) + EPS)
            h_ref[...] = (xf * r * g_ref[...]).astype(MXU)
        project((0, 1, 2))

        a = pf(A0, WA)
        ext = jnp.concatenate([carry_a[...], a], axis=0)
        carry_a[...] = a[TM - HALO_POOL:, :]
        t_pos = (i * TM + lax.broadcasted_iota(jnp.int32, (TM, 1), 0)).astype(F32)
        for g, win in enumerate(POOL_WINDOWS):
            cols = slice(g * CHUNK, (g + 1) * CHUNK)
            acc = ext[:, cols]
            k = 1
            while k < win:
                acc = acc + _shift_down(acc, k)
                k *= 2
            cnt = jnp.minimum(t_pos + 1.0, float(win))
            pa_g = (acc[HALO_POOL:, :] / cnt - a[:, cols]).astype(MXU)
            pz_ref[:, cols] = pa_g
            y_ref[:, cols] = (_dot(pa_g, wpool_ref[g]) * ps_ref[:, cols]).astype(ACT)

        project((3,))
        uvg, _ = _gelu_parts(pf(UV0, 2 * WA))
        u = uvg[:, :WA]
        v = uvg[:, WA:]
        rv = lax.rsqrt(jnp.mean(v * v, axis=-1, keepdims=True) + EPS)
        vn = (v * rv * gs_ref[...]).astype(MXU)
        mask = _tril_mask()
        for g in range(HEADS):
            cols = slice(g * CHUNK, (g + 1) * CHUNK)
            wt = jnp.where(mask, wsp_ref[g], 0.0).astype(MXU)
            bcol = bsp_ref[:, g:g + 1]
            for c in range(TM // CHUNK):
                rows = slice(c * CHUNK, (c + 1) * CHUNK)
                sv = _dot(wt, vn[rows, cols]) + bcol
                y_ref[rows, WA + g * CHUNK:WA + (g + 1) * CHUNK] = (u[rows, cols] * sv).astype(ACT)

        project((4, 5))
        z = pf(CC0, WA) * pf(CX0, WA)
        extz = jnp.concatenate([carry_z[...], z], axis=0)
        carry_z[...] = z[TM - HALO_CONV:, :]
        cz = (cc_ref[0:1, :] * _shift_down(extz, 2)[HALO_CONV:, :]
              + cc_ref[1:2, :] * _shift_down(extz, 1)[HALO_CONV:, :] + cc_ref[2:3, :] * z)
        pz_ref[:, WA:2 * WA] = cz.astype(ACT)
        y_ref[:, 2 * WA:3 * WA] = (pf(CB0, WA) * cz).astype(ACT)

        project((6, 7))
        merged = jnp.zeros((TM, D), F32)
        for k, (w_ref, glo) in enumerate(((wa_ref, GA0), (wb_ref, GB0), (wc_ref, GC0))):
            br = _dot(y_ref[:, k * WA:(k + 1) * WA], _lane_cat(w_ref))
            b_ref[:, k * D:(k + 1) * D] = br.astype(ACT)
            merged = merged + _sigmoid(pf(glo, D)) * br
        mb = merged.astype(MXU)
        m_ref[...] = mb
        xmid_ref[...] = x_ref[...] + _dot(mb, wo_ref[...].reshape(D, D))

    row = lambda n: pl.BlockSpec((TM, n), lambda i: (i, 0))
    br_spec = _const((N_DEV, WA, NB_BR), (0, 0, 0))
    if fused:
        lead_specs = [row(D), _const((1, D), (0, 0)), _const((N_DEV, D, NB_IN), (0, 0, 0))]
        lead = (x,) + tuple(in_proj)
    else:
        lead_specs = [row(D), row(NCOL)]
        lead = (x, p)
    more_specs = [row(NCOL), row(D)] if fused else []
    more_shapes = [_sds((s, NCOL), ACT), _sds((s, D), MXU)] if fused else []
    return _call(
        body, comms, *_grid_marks(nt), lead + (wpool, pscale, gsgu, wsp, bsp_t, convc, wa_all, wb_all, wc_all, wo_all),
        name=name, grid=(nt,),
        in_specs=lead_specs + [_const((HEADS, CHUNK, CHUNK), (0, 0, 0)), _const((1, WA), (0, 0)),
                               _const((1, WA), (0, 0)), _const((HEADS, CHUNK, CHUNK), (0, 0, 0)),
                               _const((CHUNK, HEADS), (0, 0)), _const((3, WA), (0, 0)), br_spec, br_spec, br_spec,
                               _const((N_DEV, ROWS_O, D), (0, 0, 0))],
        out_specs=[row(D), row(3 * WA), row(2 * WA), row(3 * D), row(D)] + more_specs,
        out_shape=[_sds((s, D), F32), _sds((s, 3 * WA), ACT), _sds((s, 2 * WA), ACT), _sds((s, 3 * D), ACT),
                   _sds((s, D), MXU)] + more_shapes,
        scratch_shapes=[pltpu.VMEM((HALO_POOL, WA), F32), pltpu.VMEM((HALO_CONV, WA), F32)],
        compiler_params=_params(),
    )


def _conv_up(ext, cur, w_ref, j, b_row):
    return (w_ref[j, 0:1, :] * _shift_down(ext, 2)[HALO_CONV:, :] + w_ref[j, 1:2, :] * _shift_down(ext, 1)[HALO_CONV:, :]
            + w_ref[j, 2:3, :] * cur + b_row)


def _ffn_block_fwd(xmid, g, wup_all, convf_all, convb, wd_all, name, comms=None):
    s = xmid.shape[0]
    nt = s // TM
    half = N_DEV // 2

    def body(x_ref, g_ref, wup_ref, cw_ref, cb_ref, wd_ref, xo_ref, h_ref, u_ref, up_ref, act_ref, carry):
        i = pl.program_id(0)

        @pl.when(i == 0)
        def _():
            carry[...] = jnp.zeros_like(carry)

        xf = x_ref[...]
        r = lax.rsqrt(jnp.mean(xf * xf, axis=-1, keepdims=True) + EPS)
        h = (xf * r * g_ref[...]).astype(MXU)
        h_ref[...] = h

        def project(j):
            return _dot(h, wup_ref[j]).astype(ACT)

        def conv(j, pre):
            u_ref[j] = pre
            cur = pre.astype(F32)
            ext = jnp.concatenate([carry[j], cur], axis=0)
            carry[j] = cur[TM - HALO_CONV:, :]
            up = _conv_up(ext, cur, cw_ref, j, cb_ref[j:j + 1, :])
            up_ref[j] = up.astype(ACT)
            return up

        order = [j + k * half for j in range(half) for k in range(2)]
        acc = xf
        ahead = 2
        pres = {n: project(order[n]) for n in range(ahead)}
        ups = {}
        for n, j in enumerate(order):
            if n + ahead < N_DEV:
                pres[n + ahead] = project(order[n + ahead])
            ups[j] = conv(j, pres.pop(n))
            if j >= half:
                gate, val = ups.pop(j - half), ups.pop(j)
                act = (gate * _sigmoid(gate) * val).astype(MXU)
                act_ref[j - half] = act
                wd = jnp.concatenate([wd_ref[2 * (j - half)], wd_ref[2 * (j - half) + 1]], axis=0)
                acc = acc + _dot(act, wd)
        xo_ref[...] = acc

    row = pl.BlockSpec((TM, D), lambda i: (i, 0))
    blocks = pl.BlockSpec((N_DEV, TM, NB_UP), lambda i: (0, i, 0))
    return _call(
        body, comms, *_grid_marks(nt), (xmid, g, wup_all, convf_all, convb, wd_all),
        name=name, grid=(nt,),
        in_specs=[row, _const((1, D), (0, 0)), _const((N_DEV, D, NB_UP), (0, 0, 0)),
                  _const((N_DEV, 3, NB_UP), (0, 0, 0)), _const((N_DEV, NB_UP), (0, 0)),
                  _const((N_DEV, ROWS_DN, D), (0, 0, 0))],
        out_specs=[row, row, blocks, blocks, pl.BlockSpec((half, TM, NB_UP), lambda i: (0, i, 0))],
        out_shape=[_sds((s, D), F32), _sds((s, D), MXU), _sds((N_DEV, s, NB_UP), ACT), _sds((N_DEV, s, NB_UP), ACT),
                   _sds((half, s, NB_UP), MXU)],
        scratch_shapes=[pltpu.VMEM((N_DEV, HALO_CONV, NB_UP), F32)],
        compiler_params=_params(),
    )


def _loss_head(x, g, target, name):
    s = x.shape[0]
    nt = s // TM

    def body(x_ref, g_ref, t_ref, dx_ref, dg_ref, loss_ref):
        i = pl.program_id(0)

        @pl.when(i == 0)
        def _():
            dg_ref[...] = jnp.zeros_like(dg_ref)
            loss_ref[...] = jnp.zeros_like(loss_ref)

        xf = x_ref[...]
        r = lax.rsqrt(jnp.mean(xf * xf, axis=-1, keepdims=True) + EPS)
        xn = xf * r
        err = xn * g_ref[...] - t_ref[...]
        loss_ref[...] += 0.5 * jnp.sum(jnp.mean(err * err, axis=-1, keepdims=True), axis=0, keepdims=True)
        dy = err * (1.0 / D)
        dg_ref[0:1, :] += _colsum(dy * xn)
        dyg = dy * g_ref[...]
        dx_ref[...] = r * (dyg - xn * jnp.mean(dyg * xn, axis=-1, keepdims=True))

    return pl.pallas_call(
        body, name=name, grid=(nt,),
        in_specs=[pl.BlockSpec((TM, D), lambda i: (i, 0)), _const((1, D), (0, 0)), pl.BlockSpec((TM, D), lambda i: (i, 0))],
        out_specs=[pl.BlockSpec((TM, D), lambda i: (i, 0)), pl.BlockSpec((8, D), lambda i: (0, 0)),
                   pl.BlockSpec((1, 1), lambda i: (0, 0))],
        out_shape=[_sds((s, D), F32), _sds((8, D), F32), _sds((1, 1), F32)],
        compiler_params=_params(),
    )(x, g, target)


def _ffn_block_bwd(dxo, upre, up, convf_all, wd_all, wup_all, xmid, g, name, comms=None):
    s = dxo.shape[0]
    nt = s // TM
    half = N_DEV // 2

    def body(dx_ref, u_ref, up_ref, cw_ref, wd_ref, wup_ref, x_ref, g_ref, du_ref, dc_ref, dxm_ref, dg_ref, carry):
        step = pl.program_id(0)

        @pl.when(step == 0)
        def _():
            carry[...] = jnp.zeros_like(carry)
            dc_ref[...] = jnp.zeros_like(dc_ref)
            dg_ref[...] = jnp.zeros_like(dg_ref)

        dxo_t = dx_ref[...]
        dxb = dxo_t.astype(MXU)

        def adjoint(j, d_up):
            cur = u_ref[j].astype(F32)
            ext = jnp.concatenate([d_up, carry[j]], axis=0)
            carry[j] = d_up[:HALO_CONV, :]
            up1 = _shift_up(ext, 1)[:TM, :]
            up2 = _shift_up(ext, 2)[:TM, :]
            du = (cw_ref[j, 2:3, :] * d_up + cw_ref[j, 1:2, :] * up1 + cw_ref[j, 0:1, :] * up2).astype(du_ref.dtype)
            du_ref[j] = du
            dc_ref[j, 0:1, :] += _colsum(cur * up2)
            dc_ref[j, 1:2, :] += _colsum(cur * up1)
            dc_ref[j, 2:3, :] += _colsum(cur * d_up)
            dc_ref[j, 3:4, :] += _colsum(d_up)
            return _dot_nt(du, wup_ref[j])

        def d_act(j):
            return _dot_nt(dxb, jnp.concatenate([wd_ref[2 * j], wd_ref[2 * j + 1]], axis=0))

        dh = jnp.zeros((TM, D), F32)
        dact = d_act(0)
        for j in range(half):
            nxt = d_act(j + 1) if j + 1 < half else None
            gate = up_ref[j].astype(F32)
            val = up_ref[j + half].astype(F32)
            sg = _sigmoid(gate)
            dh = dh + adjoint(j, dact * val * sg * (1.0 + gate * (1.0 - sg)))
            dh = dh + adjoint(j + half, dact * gate * sg)
            dact = nxt

        xf = x_ref[...]
        r = lax.rsqrt(jnp.mean(xf * xf, axis=-1, keepdims=True) + EPS)
        xn = xf * r
        dg_ref[0:1, :] += _colsum(dh * xn)
        dhg = dh * g_ref[...]
        dxm_ref[...] = dxo_t + r * (dhg - xn * jnp.mean(dhg * xn, axis=-1, keepdims=True))

    blocks = pl.BlockSpec((N_DEV, TM, NB_UP), lambda i: (0, nt - 1 - i, 0))
    row = pl.BlockSpec((TM, D), lambda i: (nt - 1 - i, 0))
    return _call(
        body, comms, *_grid_marks(nt), (dxo, upre, up, convf_all, wd_all, wup_all, xmid, g),
        name=name, grid=(nt,),
        in_specs=[row, blocks, blocks, _const((N_DEV, 3, NB_UP), (0, 0, 0)), _const((N_DEV, ROWS_DN, D), (0, 0, 0)),
                  _const((N_DEV, D, NB_UP), (0, 0, 0)), row, _const((1, D), (0, 0))],
        out_specs=[blocks, pl.BlockSpec((N_DEV, 8, NB_UP), lambda i: (0, 0, 0)), row,
                   pl.BlockSpec((8, D), lambda i: (0, 0))],
        out_shape=[_sds((N_DEV, s, NB_UP), MXU), _sds((N_DEV, 8, NB_UP), F32), _sds((s, D), F32), _sds((8, D), F32)],
        scratch_shapes=[pltpu.VMEM((N_DEV, HALO_CONV, NB_UP), F32)],
        compiler_params=_params(),
    )


def _proj_bwd(dy, w_all, x, g, dres, name, comms=None):
    s = x.shape[0]
    nb = w_all.shape[-1]
    nt = s // TM

    def body(dy_ref, w_ref, x_ref, g_ref, dres_ref, dx_ref, dg_ref):
        i = pl.program_id(0)

        @pl.when(i == 0)
        def _():
            dg_ref[...] = jnp.zeros_like(dg_ref)

        dh = jnp.zeros((TM, D), F32)
        for j in range(N_DEV):
            dh = dh + _dot_nt(dy_ref[:, j * nb:(j + 1) * nb], w_ref[j])
        xf = x_ref[...]
        r = lax.rsqrt(jnp.mean(xf * xf, axis=-1, keepdims=True) + EPS)
        xn = xf * r
        dg_ref[0:1, :] += _colsum(dh * xn)
        dhg = dh * g_ref[...]
        dx_ref[...] = dres_ref[...] + r * (dhg - xn * jnp.mean(dhg * xn, axis=-1, keepdims=True))

    dy_spec = pl.BlockSpec((TM, N_DEV * nb), lambda i: (i, 0))
    row = pl.BlockSpec((TM, D), lambda i: (i, 0))
    return _call(
        body, comms, *_grid_marks(nt), (dy, w_all, x, g, dres),
        name=name, grid=(nt,),
        in_specs=[dy_spec, _const((N_DEV, D, nb), (0, 0, 0)), row, _const((1, D), (0, 0)), row],
        out_specs=[row, pl.BlockSpec((8, D), lambda i: (0, 0))],
        out_shape=[_sds((s, D), F32), _sds((8, D), F32)],
        compiler_params=_params(),
    )


def _mixer_bwd(dxmid, p, yabc, pacz, babc, wpool, pscale, gsgu, wsp, bsp_t, convc, wa_all, wb_all, wc_all, wo_all,
               name, comms=None, in_proj=None):
    s = dxmid.shape[0]
    nt = s // TM
    fused = in_proj is not None

    def body(*refs):
        (dx_ref, p_ref, y_ref, pz_ref, b_ref, wpool_ref, ps_ref, gs_ref, wsp_ref, bsp_ref, cc_ref,
         wa_ref, wb_ref, wc_ref, wo_ref) = refs[:15]
        refs = refs[15:]
        if fused:
            (win_ref, x_ref, g_ref), refs = refs[:3], refs[3:]
        (dp_ref, db_ref, dwp_ref, dws_ref, small_ref, dbs_ref), refs = refs[:6], refs[6:]
        if fused:
            (dxin_ref, dg_ref), refs = refs[:2], refs[2:]
        carry_pa, carry_cz, dbs_acc, du_s, dvn_s = refs
        step = pl.program_id(0)
        tile = nt - 1 - step

        @pl.when(step == 0)
        def _():
            for ref in (carry_pa, carry_cz, dbs_acc, dwp_ref, dws_ref, small_ref, dbs_ref) + ((dg_ref,) if fused else ()):
                ref[...] = jnp.zeros_like(ref)

        def pf(lo, n):
            return p_ref[:, lo:lo + n].astype(F32)

        def back_project(blocks):
            return sum(_dot_nt(dp_ref[:, j * NB_IN:(j + 1) * NB_IN], win_ref[j]) for j in blocks)

        dxm = dx_ref[...]
        dm = _dot_nt(dxm.astype(MXU), wo_ref[...].reshape(D, D))

        def through_gate(k, glo, w_ref):
            sg = _sigmoid(pf(glo, D))
            br = b_ref[:, k * D:(k + 1) * D].astype(F32)
            dp_ref[:, glo:glo + D] = (dm * br * sg * (1.0 - sg)).astype(dp_ref.dtype)
            dbr = (dm * sg).astype(MXU)
            db_ref[:, k * D:(k + 1) * D] = dbr
            return _dot_nt(dbr, _lane_cat(w_ref))

        dya = through_gate(0, GA0, wa_ref)
        dyb = through_gate(1, GB0, wb_ref)
        dyc = through_gate(2, GC0, wc_ref)
        if fused:
            dh = back_project((4, 5, 6, 7))

        t_pos = (tile * TM + lax.broadcasted_iota(jnp.int32, (TM, 1), 0)).astype(F32)
        for g, win in enumerate(POOL_WINDOWS):
            cols = slice(g * CHUNK, (g + 1) * CHUNK)
            pa_g = pz_ref[:, cols]
            q = _dot(pa_g, wpool_ref[g])
            dya_g = dya[:, cols]
            small_ref[0:1, cols] += _colsum(dya_g * q)
            dq = (dya_g * ps_ref[:, cols]).astype(MXU)
            dpa_g = _dot_nt(dq, wpool_ref[g])
            dwp_ref[g] += _dot_tn(pa_g, dq)
            dpw = dpa_g / jnp.minimum(t_pos + 1.0, float(win))
            acc = jnp.concatenate([dpw, carry_pa[:, cols]], axis=0)
            carry_pa[:, cols] = dpw[:HALO_POOL, :]
            k = 1
            while k < win:
                acc = acc + _shift_up(acc, k)
                k *= 2
            dp_ref[:, cols] = (acc[:TM, :] - dpa_g).astype(dp_ref.dtype)

        uvp = pf(UV0, 2 * WA)
        uvg, dgelu = _gelu_parts(uvp)
        u = uvg[:, :WA]
        v = uvg[:, WA:]
        rv = lax.rsqrt(jnp.mean(v * v, axis=-1, keepdims=True) + EPS)
        vh = v * rv
        vn = (vh * gs_ref[...]).astype(MXU)
        mask = _tril_mask()
        for g in range(HEADS):
            cols = slice(g * CHUNK, (g + 1) * CHUNK)
            wt32 = jnp.where(mask, wsp_ref[g], 0.0)
            wt = wt32.astype(MXU)
            wt_t = wt32.T.astype(MXU)
            bcol = bsp_ref[:, g:g + 1]
            for c in range(TM // CHUNK):
                rows = slice(c * CHUNK, (c + 1) * CHUNK)
                vn_cg = vn[rows, cols]
                sv = _dot(wt, vn_cg) + bcol
                dyb_cg = dyb[rows, cols]
                du_s[rows, cols] = dyb_cg * sv
                dsv = dyb_cg * u[rows, cols]
                dbs_acc[g] += dsv
                dsv_b = dsv.astype(MXU)
                dws_ref[g] += _dot_nt(dsv_b, vn_cg)
                dvn_s[rows, cols] = _dot(wt_t, dsv_b)
        dvn = dvn_s[...]
        small_ref[1:2, :] += _colsum(dvn * vh)
        dvg = dvn * gs_ref[...]
        dv = rv * (dvg - vh * jnp.mean(dvg * vh, axis=-1, keepdims=True))
        dp_ref[:, UV0:UV0 + WA] = (du_s[...] * dgelu[:, :WA]).astype(dp_ref.dtype)
        dp_ref[:, UV0 + WA:UV0 + 2 * WA] = (dv * dgelu[:, WA:]).astype(dp_ref.dtype)
        if fused:
            dh = dh + back_project((0, 1))

        cb = pf(CB0, WA)
        cc = pf(CC0, WA)
        cx = pf(CX0, WA)
        z = cc * cx
        dp_ref[:, CB0:CB0 + WA] = (dyc * pz_ref[:, WA:2 * WA].astype(F32)).astype(dp_ref.dtype)
        dcz = dyc * cb
        extz = jnp.concatenate([dcz, carry_cz[...]], axis=0)
        carry_cz[...] = dcz[:HALO_CONV, :]
        up1 = _shift_up(extz, 1)[:TM, :]
        up2 = _shift_up(extz, 2)[:TM, :]
        dz = cc_ref[2:3, :] * dcz + cc_ref[1:2, :] * up1 + cc_ref[0:1, :] * up2
        small_ref[2:3, :] += _colsum(z * up2)
        small_ref[3:4, :] += _colsum(z * up1)
        small_ref[4:5, :] += _colsum(z * dcz)
        dp_ref[:, CC0:CC0 + WA] = (dz * cx).astype(dp_ref.dtype)
        dp_ref[:, CX0:CX0 + WA] = (dz * cc).astype(dp_ref.dtype)

        if fused:
            dh = dh + back_project((2, 3))
            xf = x_ref[...]
            r = lax.rsqrt(jnp.mean(xf * xf, axis=-1, keepdims=True) + EPS)
            xn = xf * r
            dg_ref[0:1, :] += _colsum(dh * xn)
            dhg = dh * g_ref[...]
            dxin_ref[...] = dxm + r * (dhg - xn * jnp.mean(dhg * xn, axis=-1, keepdims=True))

        @pl.when(step == nt - 1)
        def _():
            ones = jnp.ones((8, CHUNK), F32)
            for g in range(HEADS):
                dws_ref[g] = jnp.where(mask, dws_ref[g], 0.0)
                row = lax.dot_general(ones, dbs_acc[g], (((1,), (1,)), ((), ())), preferred_element_type=F32,
                                      precision=lax.Precision.HIGHEST)
                dbs_ref[g:g + 1, :] = row[0:1, :]

    row = lambda n: pl.BlockSpec((TM, n), lambda i: (nt - 1 - i, 0))
    br_spec = _const((N_DEV, WA, NB_BR), (0, 0, 0))
    acc_spec = lambda shape: pl.BlockSpec(shape, lambda i: (0,) * len(shape))
    more_in = tuple(in_proj) if fused else ()
    more_in_specs = [_const((N_DEV, D, NB_IN), (0, 0, 0)), row(D), _const((1, D), (0, 0))] if fused else []
    more_out_specs = [row(D), acc_spec((8, D))] if fused else []
    more_out_shapes = [_sds((s, D), F32), _sds((8, D), F32)] if fused else []
    return _call(
        body, comms, *_grid_marks(nt),
        (dxmid, p, yabc, pacz, babc, wpool, pscale, gsgu, wsp, bsp_t, convc, wa_all, wb_all, wc_all, wo_all) + more_in,
        name=name, grid=(nt,),
        in_specs=[row(D), row(NCOL), row(3 * WA), row(2 * WA), row(3 * D),
                  _const((HEADS, CHUNK, CHUNK), (0, 0, 0)), _const((1, WA), (0, 0)), _const((1, WA), (0, 0)),
                  _const((HEADS, CHUNK, CHUNK), (0, 0, 0)), _const((CHUNK, HEADS), (0, 0)), _const((3, WA), (0, 0)),
                  br_spec, br_spec, br_spec, _const((N_DEV, ROWS_O, D), (0, 0, 0))] + more_in_specs,
        out_specs=[row(NCOL), row(3 * D), acc_spec((HEADS, CHUNK, CHUNK)), acc_spec((HEADS, CHUNK, CHUNK)),
                   acc_spec((8, WA)), acc_spec((8, CHUNK))] + more_out_specs,
        out_shape=[_sds((s, NCOL), MXU), _sds((s, 3 * D), MXU), _sds((HEADS, CHUNK, CHUNK), F32),
                   _sds((HEADS, CHUNK, CHUNK), F32), _sds((8, WA), F32), _sds((8, CHUNK), F32)] + more_out_shapes,
        scratch_shapes=[pltpu.VMEM((HALO_POOL, WA), F32), pltpu.VMEM((HALO_CONV, WA), F32),
                        pltpu.VMEM((HEADS, CHUNK, CHUNK), F32), pltpu.VMEM((TM, WA), F32), pltpu.VMEM((TM, WA), F32)],
        compiler_params=_params(),
    )


def _wgrad(a, b, a_spec, b_spec, n_out, acc_shape, out_shape, out_spec, store, name, comms=None):
    s = a.shape[-2]
    ts = min(TS_WGRAD if b.dtype == MXU else TS_WGRAD // 2, s)
    n_steps = s // ts

    def body(a_ref, b_ref, o_ref, acc_ref):
        k = pl.program_id(1)

        @pl.when(k == 0)
        def _():
            acc_ref[...] = jnp.zeros_like(acc_ref)

        acc_ref[...] += _dot_tn(a_ref[...], b_ref[...].astype(MXU))

        @pl.when(k == n_steps - 1)
        def _():
            store(o_ref, acc_ref)

    (res,), extra = _call(
        body, comms, lambda: (pl.program_id(0) == 0) & (pl.program_id(1) == 0),
        lambda: (pl.program_id(0) == (3 * n_out) // 4) & (pl.program_id(1) == 0),
        lambda: (pl.program_id(0) == n_out - 1) & (pl.program_id(1) == n_steps - 1), (a, b),
        name=name, grid=(n_out, n_steps),
        in_specs=[a_spec(ts), b_spec(ts)], out_specs=[out_spec], out_shape=[_sds(out_shape, WIRE)],
        scratch_shapes=[pltpu.VMEM(acc_shape, F32)],
        compiler_params=_params(2),
    )
    return res, extra


def _store_plain(o_ref, acc_ref):
    o_ref[...] = acc_ref[...].astype(o_ref.dtype)


def _store_lane_blocks(o_ref, acc_ref):
    for d in range(N_DEV):
        o_ref[d] = acc_ref[:, d * NB_BR:(d + 1) * NB_BR].astype(o_ref.dtype)


def _wgrad_in(h, dp, name, comm=None):
    return _wgrad(h, dp, lambda ts: pl.BlockSpec((ts, D), lambda j, k: (k, 0)),
                  lambda ts: pl.BlockSpec((ts, NB_IN), lambda j, k: (k, j)), N_DEV, (D, NB_IN),
                  (N_DEV, D, NB_IN), pl.BlockSpec((None, D, NB_IN), lambda j, k: (j, 0, 0)), _store_plain, name, comm)


def _wgrad_up(h, du, name, comm=None):
    return _wgrad(h, du, lambda ts: pl.BlockSpec((ts, D), lambda j, k: (k, 0)),
                  lambda ts: pl.BlockSpec((None, ts, NB_UP), lambda j, k: (j, k, 0)), N_DEV, (D, NB_UP),
                  (N_DEV, D, NB_UP), pl.BlockSpec((None, D, NB_UP), lambda j, k: (j, 0, 0)), _store_plain, name, comm)


def _wgrad_down(act, dxo, name, comm=None):
    return _wgrad(act, dxo, lambda ts: pl.BlockSpec((None, ts, NB_UP), lambda j, k: (j, k, 0)),
                  lambda ts: pl.BlockSpec((ts, D), lambda j, k: (k, 0)), N_DEV // 2, (NB_UP, D),
                  (DFF, D), pl.BlockSpec((NB_UP, D), lambda j, k: (j, 0)), _store_plain, name, comm)


def _wgrad_o(merged, dxmid, name, comm=None):
    return _wgrad(merged, dxmid, lambda ts: pl.BlockSpec((ts, D), lambda j, k: (k, 0)),
                  lambda ts: pl.BlockSpec((ts, D), lambda j, k: (k, 0)), 1, (D, D),
                  (D, D), pl.BlockSpec((D, D), lambda j, k: (0, 0)), _store_plain, name, comm)


def _wgrad_branches(yabc, dbabc, name, comm=None):
    return _wgrad(yabc, dbabc, lambda ts: pl.BlockSpec((ts, WA), lambda j, k: (k, j)),
                  lambda ts: pl.BlockSpec((ts, D), lambda j, k: (k, j)), 3, (WA, D),
                  (N_DEV, 3, WA, NB_BR), pl.BlockSpec((N_DEV, None, WA, NB_BR), lambda j, k: (0, j, 0, 0)),
                  _store_lane_blocks, name, comm)


def _adamw_math(g, w, m, v):
    m = ADAM_B1 * m + (1.0 - ADAM_B1) * g
    v = ADAM_B2 * v + (1.0 - ADAM_B2) * (g * g)
    m_hat = m / (1.0 - ADAM_B1 ** ADAM_STEP)
    v_hat = v / (1.0 - ADAM_B2 ** ADAM_STEP)
    delta = -ADAM_LR * (m_hat / (jnp.sqrt(v_hat) + ADAM_EPS) + ADAM_WD * w)
    return delta, m, v


def _adamw_sum(parts, mid, w, m, v, layer, prev, tr, name):
    n_layers, r, c = w.shape

    def body(p_ref, w_ref, m_ref, v_ref, *rest):
        g_ref, d_ref, mo_ref, vo_ref = rest[-4:]
        g = p_ref[0].astype(F32)
        for k in range(1, N_DEV):
            g = g + p_ref[k].astype(F32)
        g_ref[...] = g
        d_ref[...], mo_ref[...], vo_ref[...] = _adamw_math(g, w_ref[...], m_ref[...], v_ref[...])

    blk = pl.BlockSpec((None, tr, c), lambda i: (layer, i, 0))
    extra = [] if prev is None else list(prev)
    return pl.pallas_call(
        body, name=name, grid=(r // tr,),
        in_specs=[pl.BlockSpec((N_DEV, None, tr, c), lambda i: (0, mid, i, 0)), blk, blk, blk]
        + [pl.BlockSpec(memory_space=pl.ANY)] * len(extra),
        out_specs=[blk] * 4, out_shape=[_sds((n_layers, r, c), F32)] * 4,
        input_output_aliases={4 + k: k for k in range(len(extra))},
        compiler_params=_params(),
    )(parts, w, m, v, *extra)


def _sum_parts(parts, name):
    _, r, c = parts.shape

    def body(p_ref, o_ref):
        g = p_ref[0]
        for k in range(1, N_DEV):
            g = g + p_ref[k]
        o_ref[...] = g

    return pl.pallas_call(body, name=name, out_shape=_sds((r, c), F32),
                          compiler_params=pltpu.CompilerParams(vmem_limit_bytes=VMEM_LIMIT))(parts)


def _adamw_small(g, w, m, v, name):
    def body(g_ref, w_ref, m_ref, v_ref, d_ref, mo_ref, vo_ref):
        d_ref[...], mo_ref[...], vo_ref[...] = _adamw_math(g_ref[...], w_ref[...], m_ref[...], v_ref[...])

    return pl.pallas_call(body, name=name, out_shape=[_sds(w.shape, F32)] * 3)(g, w, m, v)


HBM_SPEC = pl.BlockSpec(memory_space=pltpu.HBM)


def _position():
    return lax.axis_index("x"), lax.axis_index("y"), lax.axis_index("c")


def _device_index(chip, core):
    return 4 * chip[0] + 2 * chip[1] + core


def _gather(shards, layer):
    n = len(shards)
    per = 8

    def first_copies(ins, outs, send, recv):
        x, y, c = _position()
        me = 4 * x + 2 * y + c
        targets = [(x, y, 1 - c), (1 - x, y, c), (x, 1 - y, c), (1 - x, 1 - y, c)]
        remote = [pltpu.make_async_remote_copy(
            src_ref=ins[t].at[layer], dst_ref=outs[t].at[me], send_sem=send.at[per * t + k],
            recv_sem=recv.at[per * t + k], device_id=to, device_id_type=MESH)
            for t in range(n) for k, to in enumerate(targets)]
        local = [pltpu.make_async_copy(ins[t].at[layer], outs[t].at[me], send.at[per * t + 4]) for t in range(n)]
        return remote, local

    def passed_on(outs, send, recv):
        x, y, c = _position()
        chips = [(1 - x, y), (x, 1 - y), (1 - x, 1 - y)]
        return [pltpu.make_async_remote_copy(
            src_ref=outs[t].at[_device_index(chip, c)], dst_ref=outs[t].at[_device_index(chip, c)],
            send_sem=send.at[per * t + 5 + j], recv_sem=recv.at[per * t + 5 + j], device_id=(x, y, 1 - c),
            device_id_type=MESH)
            for t in range(n) for j, chip in enumerate(chips)]

    def start(ins, outs, send, recv):
        remote, local = first_copies(ins, outs, send, recv)
        for cp in local + remote:
            cp.start()

    def mid(ins, outs, send, recv):
        remote, local = first_copies(ins, outs, send, recv)
        for cp in remote:
            cp.wait()
        for cp in local:
            cp.wait()
        for cp in passed_on(outs, send, recv):
            cp.start()

    def finish(ins, outs, send, recv):
        for cp in passed_on(outs, send, recv):
            cp.wait()

    return _Comm(shards, [_sds((N_DEV,) + a.shape[1:], a.dtype) for a in shards], per * n, start, finish, mid)


def _run_comms(comms, name):
    def body():
        pass

    _, extra = _call(body, comms, None, None, None, (), name=name, in_specs=[], out_specs=[], out_shape=[])
    return extra


def _exchange(parts):
    n = len(parts)
    per = 8
    flips = [(0, 0, 1), (1, 0, 0), (1, 0, 1), (0, 1, 0), (0, 1, 1), (1, 1, 0), (1, 1, 1)]

    def copies(ins, outs, send, recv):
        x, y, c = _position()
        me = 4 * x + 2 * y + c
        remote = []
        for t in range(n):
            for k, (fx, fy, fc) in enumerate(flips):
                peer = ((1 - x if fx else x), (1 - y if fy else y), (1 - c if fc else c))
                remote.append(pltpu.make_async_remote_copy(
                    src_ref=ins[t].at[_device_index(peer[:2], peer[2])], dst_ref=outs[t].at[me],
                    send_sem=send.at[per * t + k], recv_sem=recv.at[per * t + k], device_id=peer, device_id_type=MESH))
        local = [pltpu.make_async_copy(ins[t].at[me], outs[t].at[me], send.at[per * t + 7]) for t in range(n)]
        return remote, local

    def start(ins, outs, send, recv):
        remote, local = copies(ins, outs, send, recv)
        for cp in local + remote:
            cp.start()

    def finish(ins, outs, send, recv):
        remote, local = copies(ins, outs, send, recv)
        for cp in remote:
            cp.wait()
        for cp in local:
            cp.wait()

    return _Comm(parts, [_sds(a.shape, a.dtype) for a in parts], per * n, start, finish)


def _rows128(a):
    return a.reshape(-1, 128)


def kernel(x, g_mix, w_in, w_pool, pool_scale, g_sgu, w_spatial, b_spatial, conv_c, w_branch_a, w_branch_b, w_branch_c, w_o, g_ffn, w_up, conv_ffn, conv_ffn_b, w_down, g_final, loss_target, m_g_mix, m_w_in, m_w_pool, m_pool_scale, m_g_sgu, m_w_spatial, m_b_spatial, m_conv_c, m_w_branch_a, m_w_branch_b, m_w_branch_c, m_w_o, m_g_ffn, m_w_up, m_conv_ffn, m_conv_ffn_b, m_w_down, m_g_final, v_g_mix, v_w_in, v_w_pool, v_pool_scale, v_g_sgu, v_w_spatial, v_b_spatial, v_conv_c, v_w_branch_a, v_w_branch_b, v_w_branch_c, v_w_o, v_g_ffn, v_w_up, v_conv_ffn, v_conv_ffn_b, v_w_down, v_g_final):
    s = x.shape[1]
    n_layers = g_mix.shape[0]
    x0 = x.reshape(s, D)
    target = loss_target.reshape(s, D)
    me = 4 * lax.axis_index("x") + 2 * lax.axis_index("y") + lax.axis_index("c")

    first_shards = [w_in.astype(MXU), conv_c]
    mix_shards = [w_branch_a.astype(MXU), w_branch_b.astype(MXU), w_branch_c.astype(MXU), w_o.astype(MXU),
                  w_down.astype(MXU)]
    up_shards = [w_up.astype(MXU), conv_ffn]
    (first_now,) = _run_comms([_gather(first_shards, 0)], "gather_first_0")
    mix_now = up_now = None
    wpool_b = w_pool.astype(MXU)
    bsp_t = jnp.swapaxes(b_spatial, 1, 2)
    convb_blk = conv_ffn_b.reshape(n_layers, N_DEV, NB_UP)

    saved = []
    weights = []
    xl = x0
    for l in range(n_layers):
        win8, convc8 = first_now
        convc_full = jnp.transpose(convc8, (1, 0, 2)).reshape(3, WA)
        more = l + 1 < n_layers
        mixer_args = (wpool_b[l], pool_scale[l:l + 1], g_sgu[l:l + 1], w_spatial[l], bsp_t[l], convc_full)
        if l == 0:
            (p, h), (mix_now,) = _rms_proj(xl, g_mix[l:l + 1], win8, f"in_proj_{l}", [_gather(mix_shards, l)])
            wa8, wb8, wc8, wo8, wd8 = mix_now
            (xmid, yabc, pacz, babc, merged), (up_now,) = _mixer_fwd(
                xl, p, *mixer_args, wa8, wb8, wc8, wo8, f"mixer_fwd_{l}", [_gather(up_shards, l)])
        else:
            wa8, wb8, wc8, wo8, wd8 = mix_now
            (xmid, yabc, pacz, babc, merged, p, h), (up_now,) = _mixer_fwd(
                xl, None, *mixer_args, wa8, wb8, wc8, wo8, f"mixer_fwd_{l}", [_gather(up_shards, l)],
                in_proj=(g_mix[l:l + 1], win8))
        wup8, convf8 = up_now
        weights.append((win8, wa8, wb8, wc8, wo8, wup8, wd8, convc_full, convf8))
        (xout, h2, upre, up, act), got = _ffn_block_fwd(
            xmid, g_ffn[l:l + 1], wup8, convf8, convb_blk[l], wd8, f"ffn_fwd_{l}",
            [_gather(first_shards, l + 1), _gather(mix_shards, l + 1)] if more else None)
        if more:
            first_now, mix_now = got
        saved.append((xl, p, h, xmid, yabc, pacz, babc, merged, upre, h2, act, up))
        xl = xout

    dx, dg_final, loss_local = _loss_head(xl, g_final.reshape(1, D), target, "loss_head")

    received = [dict() for _ in range(n_layers)]
    small = {("final", "g_final"): dg_final}
    small_sums = {}
    waiting = None

    def exchange_of(named):
        return [_exchange([a for _, a in named])]

    def land(layer, named, got):
        received[layer].update({k: a for (k, _), a in zip(named, got[0])})

    def gather_small(keys):
        packed = jnp.concatenate([_rows128(small[k]) for k in keys], axis=0)[None]
        return _gather([packed], 0)

    def sum_small(keys, gathered, name):
        summed = _sum_parts(gathered, name)
        row = 0
        for k in keys:
            n_rows = small[k].size // 128
            small_sums[k] = summed[row:row + n_rows].reshape(small[k].shape)
            row += n_rows

    for l in reversed(range(n_layers)):
        xin, p, h, xmid, yabc, pacz, babc, merged, upre, h2, act, up = saved[l]
        win8, wa8, wb8, wc8, wo8, wup8, wd8, convc_full, convf8 = weights[l]
        last = l == 0
        (dupre, dconvf, dxmid, dg_ffn), got = _ffn_block_bwd(
            dx, upre, up, convf8, wd8, wup8, xmid, g_ffn[l:l + 1], f"ffn_bwd_{l}",
            None if waiting is None else exchange_of(waiting[1]))
        if waiting is not None:
            land(waiting[0], waiting[1], got)
        small[(l, "conv_ffn")] = dconvf
        small[(l, "g_ffn")] = dg_ffn
        keys_a = [k for k in small if k not in small_sums]
        g_wdown, got = _wgrad_down(act, dx, f"wgrad_down_{l}", [gather_small(keys_a)] if last else None)
        if last:
            sum_small(keys_a, got[0][0], "sum_small_grads_a")
        down = [("w_down", g_wdown.reshape(N_DEV, ROWS_DN, D))]
        g_wup, got = _wgrad_up(h2, dupre, f"wgrad_up_{l}", exchange_of(down) if last else None)
        if last:
            land(l, down, got)
        upw = [("w_up", g_wup)]
        mixer_args = (dxmid, p, yabc, pacz, babc, wpool_b[l], pool_scale[l:l + 1], g_sgu[l:l + 1], w_spatial[l],
                      bsp_t[l], convc_full, wa8, wb8, wc8, wo8, f"mixer_bwd_{l}")
        if last:
            (dp, dbabc, dwp, dws, mixer_small, dbs), got = _mixer_bwd(*mixer_args, exchange_of(upw))
            land(l, upw, got)
        else:
            (dp, dbabc, dwp, dws, mixer_small, dbs, dx, dg_mix), got = _mixer_bwd(
                *mixer_args, exchange_of(down), in_proj=(win8, xin, g_mix[l:l + 1]))
            land(l, down, got)
        small.update({(l, "w_pool"): dwp, (l, "mixer_small"): mixer_small, (l, "w_spatial"): dws,
                      (l, "b_spatial"): dbs})
        g_wo, _ = _wgrad_o(merged, dxmid, f"wgrad_o_{l}")
        g_br, _ = _wgrad_branches(yabc, dbabc, f"wgrad_branches_{l}")
        mixer_w = [("branches", g_br), ("w_o", g_wo.reshape(N_DEV, ROWS_O, D))]
        keys_b = [k for k in small if k not in small_sums]
        g_win, got = _wgrad_in(h, dp, f"wgrad_in_{l}",
                               exchange_of(mixer_w) + [gather_small(keys_b)] if last else exchange_of(upw))
        inw = [("w_in", g_win)]
        if last:
            land(l, mixer_w, got)
            sum_small(keys_b, got[1][0], "sum_small_grads_b")
            (dx, dg_mix), got = _proj_bwd(dp, win8, xin, g_mix[l:l + 1], dxmid, f"in_proj_bwd_{l}", exchange_of(inw))
            land(l, inw, got)
        else:
            land(l, upw, got)
            waiting = (l, mixer_w + inw)
        small[(l, "g_mix")] = dg_mix
    grad_x = dx.reshape(1, s, D)
    late_keys = [k for k in small if k not in small_sums]
    (gathered_late,) = _run_comms([gather_small(late_keys)], "gather_last_small_grads")[0]
    sum_small(late_keys, gathered_late, "sum_last_small_grads")

    def update_big(key, mid, w, m, v, tr, tag):
        outs = None
        for l in range(n_layers):
            parts = received[l][key]
            if parts.ndim == 3:
                parts = parts.reshape(N_DEV, 1, *parts.shape[1:])
            outs = _adamw_sum(parts, mid, w, m, v, l, outs, tr, f"adamw_{tag}_{l}")
        return outs

    up_in = update_big("w_in", 0, w_in, m_w_in, v_w_in, 256, "w_in")
    up_a = update_big("branches", 0, w_branch_a, m_w_branch_a, v_w_branch_a, WA, "w_branch_a")
    up_b = update_big("branches", 1, w_branch_b, m_w_branch_b, v_w_branch_b, WA, "w_branch_b")
    up_c = update_big("branches", 2, w_branch_c, m_w_branch_c, v_w_branch_c, WA, "w_branch_c")
    up_o = update_big("w_o", 0, w_o, m_w_o, v_w_o, ROWS_O, "w_o")
    up_up = update_big("w_up", 0, w_up, m_w_up, v_w_up, 256, "w_up")
    up_down = update_big("w_down", 0, w_down, m_w_down, v_w_down, ROWS_DN, "w_down")

    stack = lambda kind: jnp.stack([small_sums[(l, kind)] for l in range(n_layers)], axis=0)
    grad_g_mix = stack("g_mix")[:, 0, :]
    grad_w_pool = stack("w_pool")
    mixer_sums = stack("mixer_small")
    grad_pool_scale = mixer_sums[:, 0, :]
    grad_g_sgu = mixer_sums[:, 1, :]
    grad_conv_c = lax.dynamic_slice_in_dim(mixer_sums[:, 2:5, :], me * (WA // N_DEV), WA // N_DEV, axis=2)
    grad_w_spatial = stack("w_spatial")
    grad_b_spatial = stack("b_spatial")[:, 0:HEADS, :]
    grad_g_ffn = stack("g_ffn")[:, 0, :]
    conv_grads = stack("conv_ffn")
    grad_conv_ffn = lax.dynamic_index_in_dim(conv_grads, me, axis=1, keepdims=False)[:, 0:3, :]
    grad_conv_ffn_b = conv_grads[:, :, 3, :].reshape(n_layers, 2 * DFF)
    grad_g_final = small_sums[("final", "g_final")][0]

    def update_small(g, w, m, v, tag):
        shape2 = (-1, w.shape[-1])
        outs = _adamw_small(g.reshape(shape2), w.reshape(shape2), m.reshape(shape2), v.reshape(shape2), f"adamw_{tag}")
        return [g] + [o.reshape(w.shape) for o in outs]

    up = {
        "g_mix": update_small(grad_g_mix, g_mix, m_g_mix, v_g_mix, "g_mix"),
        "w_in": up_in,
        "w_pool": update_small(grad_w_pool, w_pool, m_w_pool, v_w_pool, "w_pool"),
        "pool_scale": update_small(grad_pool_scale, pool_scale, m_pool_scale, v_pool_scale, "pool_scale"),
        "g_sgu": update_small(grad_g_sgu, g_sgu, m_g_sgu, v_g_sgu, "g_sgu"),
        "w_spatial": update_small(grad_w_spatial, w_spatial, m_w_spatial, v_w_spatial, "w_spatial"),
        "b_spatial": update_small(grad_b_spatial, b_spatial, m_b_spatial, v_b_spatial, "b_spatial"),
        "conv_c": update_small(grad_conv_c, conv_c, m_conv_c, v_conv_c, "conv_c"),
        "w_branch_a": up_a,
        "w_branch_b": up_b,
        "w_branch_c": up_c,
        "w_o": up_o,
        "g_ffn": update_small(grad_g_ffn, g_ffn, m_g_ffn, v_g_ffn, "g_ffn"),
        "w_up": up_up,
        "conv_ffn": update_small(grad_conv_ffn, conv_ffn, m_conv_ffn, v_conv_ffn, "conv_ffn"),
        "conv_ffn_b": update_small(grad_conv_ffn_b, conv_ffn_b, m_conv_ffn_b, v_conv_ffn_b, "conv_ffn_b"),
        "w_down": up_down,
        "g_final": update_small(grad_g_final, g_final, m_g_final, v_g_final, "g_final"),
    }
    loss = lax.psum(loss_local[0, 0], AXES)
    order = list(up)
    return (loss, grad_x, *[up[k][0] for k in order], *[up[k][1] for k in order], *[up[k][2] for k in order],
            *[up[k][3] for k in order])
```

```python
import functools

import jax
import jax.numpy as jnp
from jax import lax
from jax.experimental import pallas as pl
from jax.experimental.pallas import tpu as pltpu

F32 = jnp.float32
BF16 = jnp.bfloat16
MXU = BF16
ACT = BF16
WIRE = BF16

N_DEV = 8
D = 1024
WA = 512
NCOL = 6144
DFF = 2816
NB_IN = NCOL // N_DEV
NB_UP = 2 * DFF // N_DEV
NB_BR = D // N_DEV
ROWS_O = D // N_DEV
ROWS_DN = DFF // N_DEV
CHUNK = 128
HEADS = 4
POOL_WINDOWS = (2, 4, 8, 16)
EPS = 1e-6
A0, UV0, CB0, CC0, CX0, GA0, GB0, GC0 = 0, 512, 1536, 2048, 2560, 3072, 4096, 5120

ADAM_LR = 0.001
ADAM_B1 = 0.9
ADAM_B2 = 0.999
ADAM_EPS = 1e-08
ADAM_WD = 0.01
ADAM_STEP = 10

TM = 256
TS_WGRAD = 4096
HALO_POOL = 16
HALO_CONV = 8
VMEM_LIMIT = 56 * 1024 * 1024
MESH = pl.DeviceIdType.MESH
AXES = ("x", "y", "c")


def _sds(shape, dtype):
    return jax.ShapeDtypeStruct(tuple(shape), dtype)


def _params(n_grid=1):
    return pltpu.CompilerParams(dimension_semantics=("arbitrary",) * n_grid, vmem_limit_bytes=VMEM_LIMIT)


def _const(block, index):
    return pl.BlockSpec(block, lambda *_: index, pipeline_mode=pl.Buffered(1))


def _dot(a, b):
    return jnp.dot(a, b, preferred_element_type=F32)


def _dot_nt(a, b):
    return lax.dot_general(a, b, (((1,), (1,)), ((), ())), preferred_element_type=F32)


def _dot_tn(a, b):
    return lax.dot_general(a, b, (((0,), (0,)), ((), ())), preferred_element_type=F32)


def _sigmoid(v):
    return 0.5 * jnp.tanh(0.5 * v) + 0.5


def _shift_down(v, k):
    return pltpu.roll(v, k, axis=0)


def _shift_up(v, k):
    return pltpu.roll(v, v.shape[0] - k, axis=0)


def _colsum(v):
    return jnp.sum(v, axis=0, keepdims=True)


def _lane_cat(ref):
    return jnp.concatenate([ref[d] for d in range(N_DEV)], axis=1)


class _Comm:
    def __init__(self, operands, out_shapes, n_sems, start, finish, mid=None):
        self.operands = list(operands)
        self.out_shapes = list(out_shapes)
        self.n_sems = n_sems
        self.start = start
        self.mid = mid
        self.finish = finish


def _call(body, comms, is_first, is_mid, is_last, operands, *, in_specs, out_specs, out_shape, scratch_shapes=(), **kw):
    n_in, n_out, n_scr = len(in_specs), len(out_specs), len(scratch_shapes)
    comms = [c for c in (comms or []) if c is not None]
    if not comms:
        res = pl.pallas_call(body, in_specs=in_specs, out_specs=out_specs, out_shape=out_shape,
                             scratch_shapes=list(scratch_shapes), **kw)(*operands)
        return res, []
    nci = [len(c.operands) for c in comms]
    nco = [len(c.out_shapes) for c in comms]

    def split(refs, sizes):
        parts = []
        for n in sizes:
            parts.append(refs[:n])
            refs = refs[n:]
        return parts, refs

    def carrier(*refs):
        ins, refs = refs[:n_in], refs[n_in:]
        cins, refs = split(refs, nci)
        outs, refs = refs[:n_out], refs[n_out:]
        couts, refs = split(refs, nco)
        scr, sems = refs[:n_scr], refs[n_scr:]

        def run(step):
            for k, c in enumerate(comms):
                if getattr(c, step) is not None:
                    getattr(c, step)(cins[k], couts[k], sems[2 * k], sems[2 * k + 1])

        def at(mark, step):
            if mark is None:
                run(step)
            else:
                pl.when(mark())(lambda: run(step))

        at(is_first, "start")
        body(*ins, *outs, *scr)
        at(is_mid, "mid")
        at(is_last, "finish")

    res = pl.pallas_call(
        carrier, in_specs=list(in_specs) + [HBM_SPEC] * sum(nci), out_specs=list(out_specs) + [HBM_SPEC] * sum(nco),
        out_shape=list(out_shape) + [s for c in comms for s in c.out_shapes],
        scratch_shapes=list(scratch_shapes) + [pltpu.SemaphoreType.DMA((c.n_sems,)) for c in comms for _ in range(2)],
        **kw,
    )(*operands, *[a for c in comms for a in c.operands])
    extra, _ = split(res[n_out:], nco)
    return res[:n_out], extra


def _grid_marks(nt):
    return (lambda: pl.program_id(0) == 0), (lambda: pl.program_id(0) == (3 * nt) // 4), (lambda: pl.program_id(0) == nt - 1)


def _rms_proj(x, g, w_all, name, comms=None):
    s = x.shape[0]
    nb = w_all.shape[-1]
    nt = s // TM

    def body(x_ref, g_ref, w_ref, p_ref, h_ref):
        xf = x_ref[...]
        r = lax.rsqrt(jnp.mean(xf * xf, axis=-1, keepdims=True) + EPS)
        h = (xf * r * g_ref[...]).astype(MXU)
        h_ref[...] = h
        for j in range(N_DEV):
            p_ref[:, j * nb:(j + 1) * nb] = _dot(h, w_ref[j]).astype(p_ref.dtype)

    row = lambda n: pl.BlockSpec((TM, n), lambda i: (i, 0))
    return _call(
        body, comms, *_grid_marks(nt), (x, g, w_all),
        name=name, grid=(nt,),
        in_specs=[row(D), _const((1, D), (0, 0)), _const((N_DEV, D, nb), (0, 0, 0))],
        out_specs=[row(N_DEV * nb), row(D)],
        out_shape=[_sds((s, N_DEV * nb), ACT), _sds((s, D), MXU)],
        compiler_params=_params(),
    )


def _tril_mask():
    r = lax.broadcasted_iota(jnp.int32, (CHUNK, CHUNK), 0)
    c = lax.broadcasted_iota(jnp.int32, (CHUNK, CHUNK), 1)
    return r >= c


def _gelu_parts(v):
    c0 = 0.7978845608028654
    th = jnp.tanh(c0 * (v + 0.044715 * (v * v * v)))
    cdf = 0.5 * (1.0 + th)
    dgelu = cdf + v * (0.5 * c0) * (1.0 - th * th) * (1.0 + 3.0 * 0.044715 * (v * v))
    return v * cdf, dgelu


def _mixer_fwd(x, p, wpool, pscale, gsgu, wsp, bsp_t, convc, wa_all, wb_all, wc_all, wo_all, name, comms=None,
               in_proj=None):
    s = x.shape[0]
    nt = s // TM
    fused = in_proj is not None

    def body(*refs):
        if fused:
            (x_ref, g_ref, win_ref, wpool_ref, ps_ref, gs_ref, wsp_ref, bsp_ref, cc_ref, wa_ref, wb_ref, wc_ref, wo_ref,
             xmid_ref, y_ref, pz_ref, b_ref, m_ref, p_ref, h_ref, carry_a, carry_z) = refs
        else:
            (x_ref, p_ref, wpool_ref, ps_ref, gs_ref, wsp_ref, bsp_ref, cc_ref, wa_ref, wb_ref, wc_ref, wo_ref,
             xmid_ref, y_ref, pz_ref, b_ref, m_ref, carry_a, carry_z) = refs
        i = pl.program_id(0)

        @pl.when(i == 0)
        def _():
            carry_a[...] = jnp.zeros_like(carry_a)
            carry_z[...] = jnp.zeros_like(carry_z)

        def pf(lo, n):
            return p_ref[:, lo:lo + n].astype(F32)

        def project(blocks):
            if fused:
                for j in blocks:
                    p_ref[:, j * NB_IN:(j + 1) * NB_IN] = _dot(h_ref[...], win_ref[j]).astype(ACT)

        if fused:
            xf = x_ref[...]
            r = lax.rsqrt(jnp.mean(xf * xf, axis=-1, keepdims=True) + EPS)
            h_ref[...] = (xf * r * g_ref[...]).astype(MXU)
        project((0, 1, 2))

        a = pf(A0, WA)
        ext = jnp.concatenate([carry_a[...], a], axis=0)
        carry_a[...] = a[TM - HALO_POOL:, :]
        t_pos = (i * TM + lax.broadcasted_iota(jnp.int32, (TM, 1), 0)).astype(F32)
        for g, win in enumerate(POOL_WINDOWS):
            cols = slice(g * CHUNK, (g + 1) * CHUNK)
            acc = ext[:, cols]
            k = 1
            while k < win:
                acc = acc + _shift_down(acc, k)
                k *= 2
            cnt = jnp.minimum(t_pos + 1.0, float(win))
            pa_g = (acc[HALO_POOL:, :] / cnt - a[:, cols]).astype(MXU)
            pz_ref[:, cols] = pa_g
            y_ref[:, cols] = (_dot(pa_g, wpool_ref[g]) * ps_ref[:, cols]).astype(ACT)

        project((3,))
        uvg, _ = _gelu_parts(pf(UV0, 2 * WA))
        u = uvg[:, :WA]
        v = uvg[:, WA:]
        rv = lax.rsqrt(jnp.mean(v * v, axis=-1, keepdims=True) + EPS)
        vn = (v * rv * gs_ref[...]).astype(MXU)
        mask = _tril_mask()
        for g in range(HEADS):
            cols = slice(g * CHUNK, (g + 1) * CHUNK)
            wt = jnp.where(mask, wsp_ref[g], 0.0).astype(MXU)
            bcol = bsp_ref[:, g:g + 1]
            for c in range(TM // CHUNK):
                rows = slice(c * CHUNK, (c + 1) * CHUNK)
                sv = _dot(wt, vn[rows, cols]) + bcol
                y_ref[rows, WA + g * CHUNK:WA + (g + 1) * CHUNK] = (u[rows, cols] * sv).astype(ACT)

        project((4, 5))
        z = pf(CC0, WA) * pf(CX0, WA)
        extz = jnp.concatenate([carry_z[...], z], axis=0)
        carry_z[...] = z[TM - HALO_CONV:, :]
        cz = (cc_ref[0:1, :] * _shift_down(extz, 2)[HALO_CONV:, :]
              + cc_ref[1:2, :] * _shift_down(extz, 1)[HALO_CONV:, :] + cc_ref[2:3, :] * z)
        pz_ref[:, WA:2 * WA] = cz.astype(ACT)
        y_ref[:, 2 * WA:3 * WA] = (pf(CB0, WA) * cz).astype(ACT)

        project((6, 7))
        merged = jnp.zeros((TM, D), F32)
        for k, (w_ref, glo) in enumerate(((wa_ref, GA0), (wb_ref, GB0), (wc_ref, GC0))):
            br = _dot(y_ref[:, k * WA:(k + 1) * WA], _lane_cat(w_ref))
            b_ref[:, k * D:(k + 1) * D] = br.astype(ACT)
            merged = merged + _sigmoid(pf(glo, D)) * br
        mb = merged.astype(MXU)
        m_ref[...] = mb
        xmid_ref[...] = x_ref[...] + _dot(mb, wo_ref[...].reshape(D, D))

    row = lambda n: pl.BlockSpec((TM, n), lambda i: (i, 0))
    br_spec = _const((N_DEV, WA, NB_BR), (0, 0, 0))
    if fused:
        lead_specs = [row(D), _const((1, D), (0, 0)), _const((N_DEV, D, NB_IN), (0, 0, 0))]
        lead = (x,) + tuple(in_proj)
    else:
        lead_specs = [row(D), row(NCOL)]
        lead = (x, p)
    more_specs = [row(NCOL), row(D)] if fused else []
    more_shapes = [_sds((s, NCOL), ACT), _sds((s, D), MXU)] if fused else []
    return _call(
        body, comms, *_grid_marks(nt), lead + (wpool, pscale, gsgu, wsp, bsp_t, convc, wa_all, wb_all, wc_all, wo_all),
        name=name, grid=(nt,),
        in_specs=lead_specs + [_const((HEADS, CHUNK, CHUNK), (0, 0, 0)), _const((1, WA), (0, 0)),
                               _const((1, WA), (0, 0)), _const((HEADS, CHUNK, CHUNK), (0, 0, 0)),
                               _const((CHUNK, HEADS), (0, 0)), _const((3, WA), (0, 0)), br_spec, br_spec, br_spec,
                               _const((N_DEV, ROWS_O, D), (0, 0, 0))],
        out_specs=[row(D), row(3 * WA), row(2 * WA), row(3 * D), row(D)] + more_specs,
        out_shape=[_sds((s, D), F32), _sds((s, 3 * WA), ACT), _sds((s, 2 * WA), ACT), _sds((s, 3 * D), ACT),
                   _sds((s, D), MXU)] + more_shapes,
        scratch_shapes=[pltpu.VMEM((HALO_POOL, WA), F32), pltpu.VMEM((HALO_CONV, WA), F32)],
        compiler_params=_params(),
    )


def _conv_up(ext, cur, w_ref, j, b_row):
    return (w_ref[j, 0:1, :] * _shift_down(ext, 2)[HALO_CONV:, :] + w_ref[j, 1:2, :] * _shift_down(ext, 1)[HALO_CONV:, :]
            + w_ref[j, 2:3, :] * cur + b_row)


def _ffn_block_fwd(xmid, g, wup_all, convf_all, convb, wd_all, name, comms=None, loss_head=None):
    s = xmid.shape[0]
    nt = s // TM
    half = N_DEV // 2
    with_loss = loss_head is not None

    def body(*refs):
        (x_ref, g_ref, wup_ref, cw_ref, cb_ref, wd_ref), refs = refs[:6], refs[6:]
        if with_loss:
            (gf_ref, t_ref), refs = refs[:2], refs[2:]
        (xo_ref, h_ref, u_ref, up_ref, act_ref), refs = refs[:5], refs[5:]
        if with_loss:
            (dgf_ref, loss_ref), refs = refs[:2], refs[2:]
        (carry,) = refs
        i = pl.program_id(0)

        @pl.when(i == 0)
        def _():
            carry[...] = jnp.zeros_like(carry)
            if with_loss:
                dgf_ref[...] = jnp.zeros_like(dgf_ref)
                loss_ref[...] = jnp.zeros_like(loss_ref)

        xf = x_ref[...]
        r = lax.rsqrt(jnp.mean(xf * xf, axis=-1, keepdims=True) + EPS)
        h = (xf * r * g_ref[...]).astype(MXU)
        h_ref[...] = h

        def project(j):
            return _dot(h, wup_ref[j]).astype(ACT)

        def conv(j, pre):
            u_ref[j] = pre
            cur = pre.astype(F32)
            ext = jnp.concatenate([carry[j], cur], axis=0)
            carry[j] = cur[TM - HALO_CONV:, :]
            up = _conv_up(ext, cur, cw_ref, j, cb_ref[j:j + 1, :])
            up_ref[j] = up.astype(ACT)
            return up

        order = [j + k * half for j in range(half) for k in range(2)]
        acc = xf
        ahead = 2
        pres = {n: project(order[n]) for n in range(ahead)}
        ups = {}
        for n, j in enumerate(order):
            if n + ahead < N_DEV:
                pres[n + ahead] = project(order[n + ahead])
            ups[j] = conv(j, pres.pop(n))
            if j >= half:
                gate, val = ups.pop(j - half), ups.pop(j)
                act = (gate * _sigmoid(gate) * val).astype(MXU)
                act_ref[j - half] = act
                wd = jnp.concatenate([wd_ref[2 * (j - half)], wd_ref[2 * (j - half) + 1]], axis=0)
                acc = acc + _dot(act, wd)
        if not with_loss:
            xo_ref[...] = acc
        else:
            ro = lax.rsqrt(jnp.mean(acc * acc, axis=-1, keepdims=True) + EPS)
            xn = acc * ro
            err = xn * gf_ref[...] - t_ref[...]
            loss_ref[...] += 0.5 * jnp.sum(jnp.mean(err * err, axis=-1, keepdims=True), axis=0, keepdims=True)
            dy = err * (1.0 / D)
            dgf_ref[0:1, :] += _colsum(dy * xn)
            dyg = dy * gf_ref[...]
            xo_ref[...] = ro * (dyg - xn * jnp.mean(dyg * xn, axis=-1, keepdims=True))

    row = pl.BlockSpec((TM, D), lambda i: (i, 0))
    blocks = pl.BlockSpec((N_DEV, TM, NB_UP), lambda i: (0, i, 0))
    return _call(
        body, comms, *_grid_marks(nt), (xmid, g, wup_all, convf_all, convb, wd_all) + (tuple(loss_head) if with_loss else ()),
        name=name, grid=(nt,),
        in_specs=[row, _const((1, D), (0, 0)), _const((N_DEV, D, NB_UP), (0, 0, 0)),
                  _const((N_DEV, 3, NB_UP), (0, 0, 0)), _const((N_DEV, NB_UP), (0, 0)),
                  _const((N_DEV, ROWS_DN, D), (0, 0, 0))] + ([_const((1, D), (0, 0)), row] if with_loss else []),
        out_specs=[row, row, blocks, blocks, pl.BlockSpec((half, TM, NB_UP), lambda i: (0, i, 0))]
        + ([pl.BlockSpec((8, D), lambda i: (0, 0)), pl.BlockSpec((1, 1), lambda i: (0, 0))] if with_loss else []),
        out_shape=[_sds((s, D), F32), _sds((s, D), MXU), _sds((N_DEV, s, NB_UP), ACT), _sds((N_DEV, s, NB_UP), ACT),
                   _sds((half, s, NB_UP), MXU)] + ([_sds((8, D), F32), _sds((1, 1), F32)] if with_loss else []),
        scratch_shapes=[pltpu.VMEM((N_DEV, HALO_CONV, NB_UP), F32)],
        compiler_params=_params(),
    )


def _ffn_block_bwd(dxo, upre, up, convf_all, wd_all, wup_all, xmid, g, name, comms=None):
    s = dxo.shape[0]
    nt = s // TM
    half = N_DEV // 2

    def body(dx_ref, u_ref, up_ref, cw_ref, wd_ref, wup_ref, x_ref, g_ref, du_ref, dc_ref, dxm_ref, dg_ref, carry):
        step = pl.program_id(0)

        @pl.when(step == 0)
        def _():
            carry[...] = jnp.zeros_like(carry)
            dc_ref[...] = jnp.zeros_like(dc_ref)
            dg_ref[...] = jnp.zeros_like(dg_ref)

        dxo_t = dx_ref[...]
        dxb = dxo_t.astype(MXU)

        def adjoint(j, d_up):
            cur = u_ref[j].astype(F32)
            ext = jnp.concatenate([d_up, carry[j]], axis=0)
            carry[j] = d_up[:HALO_CONV, :]
            up1 = _shift_up(ext, 1)[:TM, :]
            up2 = _shift_up(ext, 2)[:TM, :]
            du = (cw_ref[j, 2:3, :] * d_up + cw_ref[j, 1:2, :] * up1 + cw_ref[j, 0:1, :] * up2).astype(du_ref.dtype)
            du_ref[j] = du
            dc_ref[j, 0:1, :] += _colsum(cur * up2)
            dc_ref[j, 1:2, :] += _colsum(cur * up1)
            dc_ref[j, 2:3, :] += _colsum(cur * d_up)
            dc_ref[j, 3:4, :] += _colsum(d_up)
            return _dot_nt(du, wup_ref[j])

        def d_act(j):
            return _dot_nt(dxb, jnp.concatenate([wd_ref[2 * j], wd_ref[2 * j + 1]], axis=0))

        dh = jnp.zeros((TM, D), F32)
        dact = d_act(0)
        for j in range(half):
            nxt = d_act(j + 1) if j + 1 < half else None
            gate = up_ref[j].astype(F32)
            val = up_ref[j + half].astype(F32)
            sg = _sigmoid(gate)
            dh = dh + adjoint(j, dact * val * sg * (1.0 + gate * (1.0 - sg)))
            dh = dh + adjoint(j + half, dact * gate * sg)
            dact = nxt

        xf = x_ref[...]
        r = lax.rsqrt(jnp.mean(xf * xf, axis=-1, keepdims=True) + EPS)
        xn = xf * r
        dg_ref[0:1, :] += _colsum(dh * xn)
        dhg = dh * g_ref[...]
        dxm_ref[...] = dxo_t + r * (dhg - xn * jnp.mean(dhg * xn, axis=-1, keepdims=True))

    blocks = pl.BlockSpec((N_DEV, TM, NB_UP), lambda i: (0, nt - 1 - i, 0))
    row = pl.BlockSpec((TM, D), lambda i: (nt - 1 - i, 0))
    return _call(
        body, comms, *_grid_marks(nt), (dxo, upre, up, convf_all, wd_all, wup_all, xmid, g),
        name=name, grid=(nt,),
        in_specs=[row, blocks, blocks, _const((N_DEV, 3, NB_UP), (0, 0, 0)), _const((N_DEV, ROWS_DN, D), (0, 0, 0)),
                  _const((N_DEV, D, NB_UP), (0, 0, 0)), row, _const((1, D), (0, 0))],
        out_specs=[blocks, pl.BlockSpec((N_DEV, 8, NB_UP), lambda i: (0, 0, 0)), row,
                   pl.BlockSpec((8, D), lambda i: (0, 0))],
        out_shape=[_sds((N_DEV, s, NB_UP), MXU), _sds((N_DEV, 8, NB_UP), F32), _sds((s, D), F32), _sds((8, D), F32)],
        scratch_shapes=[pltpu.VMEM((N_DEV, HALO_CONV, NB_UP), F32)],
        compiler_params=_params(),
    )


def _proj_bwd(dy, w_all, x, g, dres, name, comms=None):
    s = x.shape[0]
    nb = w_all.shape[-1]
    nt = s // TM

    def body(dy_ref, w_ref, x_ref, g_ref, dres_ref, dx_ref, dg_ref):
        i = pl.program_id(0)

        @pl.when(i == 0)
        def _():
            dg_ref[...] = jnp.zeros_like(dg_ref)

        dh = jnp.zeros((TM, D), F32)
        for j in range(N_DEV):
            dh = dh + _dot_nt(dy_ref[:, j * nb:(j + 1) * nb], w_ref[j])
        xf = x_ref[...]
        r = lax.rsqrt(jnp.mean(xf * xf, axis=-1, keepdims=True) + EPS)
        xn = xf * r
        dg_ref[0:1, :] += _colsum(dh * xn)
        dhg = dh * g_ref[...]
        dx_ref[...] = dres_ref[...] + r * (dhg - xn * jnp.mean(dhg * xn, axis=-1, keepdims=True))

    dy_spec = pl.BlockSpec((TM, N_DEV * nb), lambda i: (i, 0))
    row = pl.BlockSpec((TM, D), lambda i: (i, 0))
    return _call(
        body, comms, *_grid_marks(nt), (dy, w_all, x, g, dres),
        name=name, grid=(nt,),
        in_specs=[dy_spec, _const((N_DEV, D, nb), (0, 0, 0)), row, _const((1, D), (0, 0)), row],
        out_specs=[row, pl.BlockSpec((8, D), lambda i: (0, 0))],
        out_shape=[_sds((s, D), F32), _sds((8, D), F32)],
        compiler_params=_params(),
    )


def _mixer_bwd(dxmid, p, yabc, pacz, babc, wpool, pscale, gsgu, wsp, bsp_t, convc, wa_all, wb_all, wc_all, wo_all,
               name, comms=None, in_proj=None):
    s = dxmid.shape[0]
    nt = s // TM
    fused = in_proj is not None

    def body(*refs):
        (dx_ref, p_ref, y_ref, pz_ref, b_ref, wpool_ref, ps_ref, gs_ref, wsp_ref, bsp_ref, cc_ref,
         wa_ref, wb_ref, wc_ref, wo_ref) = refs[:15]
        refs = refs[15:]
        if fused:
            (win_ref, x_ref, g_ref), refs = refs[:3], refs[3:]
        (dp_ref, db_ref, dwp_ref, dws_ref, small_ref, dbs_ref), refs = refs[:6], refs[6:]
        if fused:
            (dxin_ref, dg_ref), refs = refs[:2], refs[2:]
        carry_pa, carry_cz, dbs_acc, du_s, dvn_s = refs
        step = pl.program_id(0)
        tile = nt - 1 - step

        @pl.when(step == 0)
        def _():
            for ref in (carry_pa, carry_cz, dbs_acc, dwp_ref, dws_ref, small_ref, dbs_ref) + ((dg_ref,) if fused else ()):
                ref[...] = jnp.zeros_like(ref)

        def pf(lo, n):
            return p_ref[:, lo:lo + n].astype(F32)

        def back_project(blocks):
            return sum(_dot_nt(dp_ref[:, j * NB_IN:(j + 1) * NB_IN], win_ref[j]) for j in blocks)

        dxm = dx_ref[...]
        dm = _dot_nt(dxm.astype(MXU), wo_ref[...].reshape(D, D))

        def through_gate(k, glo, w_ref):
            sg = _sigmoid(pf(glo, D))
            br = b_ref[:, k * D:(k + 1) * D].astype(F32)
            dp_ref[:, glo:glo + D] = (dm * br * sg * (1.0 - sg)).astype(dp_ref.dtype)
            dbr = (dm * sg).astype(MXU)
            db_ref[:, k * D:(k + 1) * D] = dbr
            return _dot_nt(dbr, _lane_cat(w_ref))

        dya = through_gate(0, GA0, wa_ref)
        dyb = through_gate(1, GB0, wb_ref)
        dyc = through_gate(2, GC0, wc_ref)
        if fused:
            dh = back_project((4, 5, 6, 7))

        t_pos = (tile * TM + lax.broadcasted_iota(jnp.int32, (TM, 1), 0)).astype(F32)
        for g, win in enumerate(POOL_WINDOWS):
            cols = slice(g * CHUNK, (g + 1) * CHUNK)
            pa_g = pz_ref[:, cols]
            q = _dot(pa_g, wpool_ref[g])
            dya_g = dya[:, cols]
            small_ref[0:1, cols] += _colsum(dya_g * q)
            dq = (dya_g * ps_ref[:, cols]).astype(MXU)
            dpa_g = _dot_nt(dq, wpool_ref[g])
            dwp_ref[g] += _dot_tn(pa_g, dq)
            dpw = dpa_g / jnp.minimum(t_pos + 1.0, float(win))
            acc = jnp.concatenate([dpw, carry_pa[:, cols]], axis=0)
            carry_pa[:, cols] = dpw[:HALO_POOL, :]
            k = 1
            while k < win:
                acc = acc + _shift_up(acc, k)
                k *= 2
            dp_ref[:, cols] = (acc[:TM, :] - dpa_g).astype(dp_ref.dtype)

        uvp = pf(UV0, 2 * WA)
        uvg, dgelu = _gelu_parts(uvp)
        u = uvg[:, :WA]
        v = uvg[:, WA:]
        rv = lax.rsqrt(jnp.mean(v * v, axis=-1, keepdims=True) + EPS)
        vh = v * rv
        vn = (vh * gs_ref[...]).astype(MXU)
        mask = _tril_mask()
        for g in range(HEADS):
            cols = slice(g * CHUNK, (g + 1) * CHUNK)
            wt32 = jnp.where(mask, wsp_ref[g], 0.0)
            wt = wt32.astype(MXU)
            wt_t = wt32.T.astype(MXU)
            bcol = bsp_ref[:, g:g + 1]
            for c in range(TM // CHUNK):
                rows = slice(c * CHUNK, (c + 1) * CHUNK)
                vn_cg = vn[rows, cols]
                sv = _dot(wt, vn_cg) + bcol
                dyb_cg = dyb[rows, cols]
                du_s[rows, cols] = dyb_cg * sv
                dsv = dyb_cg * u[rows, cols]
                dbs_acc[g] += dsv
                dsv_b = dsv.astype(MXU)
                dws_ref[g] += _dot_nt(dsv_b, vn_cg)
                dvn_s[rows, cols] = _dot(wt_t, dsv_b)
        dvn = dvn_s[...]
        small_ref[1:2, :] += _colsum(dvn * vh)
        dvg = dvn * gs_ref[...]
        dv = rv * (dvg - vh * jnp.mean(dvg * vh, axis=-1, keepdims=True))
        dp_ref[:, UV0:UV0 + WA] = (du_s[...] * dgelu[:, :WA]).astype(dp_ref.dtype)
        dp_ref[:, UV0 + WA:UV0 + 2 * WA] = (dv * dgelu[:, WA:]).astype(dp_ref.dtype)
        if fused:
            dh = dh + back_project((0, 1))

        cb = pf(CB0, WA)
        cc = pf(CC0, WA)
        cx = pf(CX0, WA)
        z = cc * cx
        dp_ref[:, CB0:CB0 + WA] = (dyc * pz_ref[:, WA:2 * WA].astype(F32)).astype(dp_ref.dtype)
        dcz = dyc * cb
        extz = jnp.concatenate([dcz, carry_cz[...]], axis=0)
        carry_cz[...] = dcz[:HALO_CONV, :]
        up1 = _shift_up(extz, 1)[:TM, :]
        up2 = _shift_up(extz, 2)[:TM, :]
        dz = cc_ref[2:3, :] * dcz + cc_ref[1:2, :] * up1 + cc_ref[0:1, :] * up2
        small_ref[2:3, :] += _colsum(z * up2)
        small_ref[3:4, :] += _colsum(z * up1)
        small_ref[4:5, :] += _colsum(z * dcz)
        dp_ref[:, CC0:CC0 + WA] = (dz * cx).astype(dp_ref.dtype)
        dp_ref[:, CX0:CX0 + WA] = (dz * cc).astype(dp_ref.dtype)

        if fused:
            dh = dh + back_project((2, 3))
            xf = x_ref[...]
            r = lax.rsqrt(jnp.mean(xf * xf, axis=-1, keepdims=True) + EPS)
            xn = xf * r
            dg_ref[0:1, :] += _colsum(dh * xn)
            dhg = dh * g_ref[...]
            dxin_ref[...] = dxm + r * (dhg - xn * jnp.mean(dhg * xn, axis=-1, keepdims=True))

        @pl.when(step == nt - 1)
        def _():
            ones = jnp.ones((8, CHUNK), F32)
            for g in range(HEADS):
                dws_ref[g] = jnp.where(mask, dws_ref[g], 0.0)
                row = lax.dot_general(ones, dbs_acc[g], (((1,), (1,)), ((), ())), preferred_element_type=F32,
                                      precision=lax.Precision.HIGHEST)
                dbs_ref[g:g + 1, :] = row[0:1, :]

    row = lambda n: pl.BlockSpec((TM, n), lambda i: (nt - 1 - i, 0))
    br_spec = _const((N_DEV, WA, NB_BR), (0, 0, 0))
    acc_spec = lambda shape: pl.BlockSpec(shape, lambda i: (0,) * len(shape))
    more_in = tuple(in_proj) if fused else ()
    more_in_specs = [_const((N_DEV, D, NB_IN), (0, 0, 0)), row(D), _const((1, D), (0, 0))] if fused else []
    more_out_specs = [row(D), acc_spec((8, D))] if fused else []
    more_out_shapes = [_sds((s, D), F32), _sds((8, D), F32)] if fused else []
    return _call(
        body, comms, *_grid_marks(nt),
        (dxmid, p, yabc, pacz, babc, wpool, pscale, gsgu, wsp, bsp_t, convc, wa_all, wb_all, wc_all, wo_all) + more_in,
        name=name, grid=(nt,),
        in_specs=[row(D), row(NCOL), row(3 * WA), row(2 * WA), row(3 * D),
                  _const((HEADS, CHUNK, CHUNK), (0, 0, 0)), _const((1, WA), (0, 0)), _const((1, WA), (0, 0)),
                  _const((HEADS, CHUNK, CHUNK), (0, 0, 0)), _const((CHUNK, HEADS), (0, 0)), _const((3, WA), (0, 0)),
                  br_spec, br_spec, br_spec, _const((N_DEV, ROWS_O, D), (0, 0, 0))] + more_in_specs,
        out_specs=[row(NCOL), row(3 * D), acc_spec((HEADS, CHUNK, CHUNK)), acc_spec((HEADS, CHUNK, CHUNK)),
                   acc_spec((8, WA)), acc_spec((8, CHUNK))] + more_out_specs,
        out_shape=[_sds((s, NCOL), MXU), _sds((s, 3 * D), MXU), _sds((HEADS, CHUNK, CHUNK), F32),
                   _sds((HEADS, CHUNK, CHUNK), F32), _sds((8, WA), F32), _sds((8, CHUNK), F32)] + more_out_shapes,
        scratch_shapes=[pltpu.VMEM((HALO_POOL, WA), F32), pltpu.VMEM((HALO_CONV, WA), F32),
                        pltpu.VMEM((HEADS, CHUNK, CHUNK), F32), pltpu.VMEM((TM, WA), F32), pltpu.VMEM((TM, WA), F32)],
        compiler_params=_params(),
    )


def _wgrad(a, b, a_spec, b_spec, n_out, acc_shape, out_shape, out_spec, store, name, comms=None):
    s = a.shape[-2]
    ts = min(TS_WGRAD if b.dtype == MXU else TS_WGRAD // 2, s)
    n_steps = s // ts

    def body(a_ref, b_ref, o_ref, acc_ref):
        k = pl.program_id(1)

        @pl.when(k == 0)
        def _():
            acc_ref[...] = jnp.zeros_like(acc_ref)

        acc_ref[...] += _dot_tn(a_ref[...], b_ref[...].astype(MXU))

        @pl.when(k == n_steps - 1)
        def _():
            store(o_ref, acc_ref)

    (res,), extra = _call(
        body, comms, lambda: (pl.program_id(0) == 0) & (pl.program_id(1) == 0),
        lambda: (pl.program_id(0) == (3 * n_out) // 4) & (pl.program_id(1) == 0),
        lambda: (pl.program_id(0) == n_out - 1) & (pl.program_id(1) == n_steps - 1), (a, b),
        name=name, grid=(n_out, n_steps),
        in_specs=[a_spec(ts), b_spec(ts)], out_specs=[out_spec], out_shape=[_sds(out_shape, WIRE)],
        scratch_shapes=[pltpu.VMEM(acc_shape, F32)],
        compiler_params=_params(2),
    )
    return res, extra


def _store_plain(o_ref, acc_ref):
    o_ref[...] = acc_ref[...].astype(o_ref.dtype)


def _store_lane_blocks(o_ref, acc_ref):
    for d in range(N_DEV):
        o_ref[d] = acc_ref[:, d * NB_BR:(d + 1) * NB_BR].astype(o_ref.dtype)


def _wgrad_in(h, dp, name, comm=None):
    return _wgrad(h, dp, lambda ts: pl.BlockSpec((ts, D), lambda j, k: (k, 0)),
                  lambda ts: pl.BlockSpec((ts, NB_IN), lambda j, k: (k, j)), N_DEV, (D, NB_IN),
                  (N_DEV, D, NB_IN), pl.BlockSpec((None, D, NB_IN), lambda j, k: (j, 0, 0)), _store_plain, name, comm)


def _wgrad_up(h, du, name, comm=None):
    return _wgrad(h, du, lambda ts: pl.BlockSpec((ts, D), lambda j, k: (k, 0)),
                  lambda ts: pl.BlockSpec((None, ts, NB_UP), lambda j, k: (j, k, 0)), N_DEV, (D, NB_UP),
                  (N_DEV, D, NB_UP), pl.BlockSpec((None, D, NB_UP), lambda j, k: (j, 0, 0)), _store_plain, name, comm)


def _wgrad_down(act, dxo, name, comm=None):
    return _wgrad(act, dxo, lambda ts: pl.BlockSpec((None, ts, NB_UP), lambda j, k: (j, k, 0)),
                  lambda ts: pl.BlockSpec((ts, D), lambda j, k: (k, 0)), N_DEV // 2, (NB_UP, D),
                  (DFF, D), pl.BlockSpec((NB_UP, D), lambda j, k: (j, 0)), _store_plain, name, comm)


def _wgrad_o(merged, dxmid, name, comm=None):
    return _wgrad(merged, dxmid, lambda ts: pl.BlockSpec((ts, D), lambda j, k: (k, 0)),
                  lambda ts: pl.BlockSpec((ts, D), lambda j, k: (k, 0)), 1, (D, D),
                  (D, D), pl.BlockSpec((D, D), lambda j, k: (0, 0)), _store_plain, name, comm)


def _wgrad_branches(yabc, dbabc, name, comm=None):
    return _wgrad(yabc, dbabc, lambda ts: pl.BlockSpec((ts, WA), lambda j, k: (k, j)),
                  lambda ts: pl.BlockSpec((ts, D), lambda j, k: (k, j)), 3, (WA, D),
                  (N_DEV, 3, WA, NB_BR), pl.BlockSpec((N_DEV, None, WA, NB_BR), lambda j, k: (0, j, 0, 0)),
                  _store_lane_blocks, name, comm)


def _adamw_math(g, w, m, v):
    m = ADAM_B1 * m + (1.0 - ADAM_B1) * g
    v = ADAM_B2 * v + (1.0 - ADAM_B2) * (g * g)
    m_hat = m / (1.0 - ADAM_B1 ** ADAM_STEP)
    v_hat = v / (1.0 - ADAM_B2 ** ADAM_STEP)
    delta = -ADAM_LR * (m_hat / (jnp.sqrt(v_hat) + ADAM_EPS) + ADAM_WD * w)
    return delta, m, v


def _adamw_sum(parts, mid, w, m, v, layer, prev, tr, name):
    n_layers, r, c = w.shape

    def body(p_ref, w_ref, m_ref, v_ref, *rest):
        g_ref, d_ref, mo_ref, vo_ref = rest[-4:]
        g = p_ref[0].astype(F32)
        for k in range(1, N_DEV):
            g = g + p_ref[k].astype(F32)
        g_ref[...] = g
        d_ref[...], mo_ref[...], vo_ref[...] = _adamw_math(g, w_ref[...], m_ref[...], v_ref[...])

    blk = pl.BlockSpec((None, tr, c), lambda i: (layer, i, 0))
    extra = [] if prev is None else list(prev)
    return pl.pallas_call(
        body, name=name, grid=(r // tr,),
        in_specs=[pl.BlockSpec((N_DEV, None, tr, c), lambda i: (0, mid, i, 0)), blk, blk, blk]
        + [pl.BlockSpec(memory_space=pl.ANY)] * len(extra),
        out_specs=[blk] * 4, out_shape=[_sds((n_layers, r, c), F32)] * 4,
        input_output_aliases={4 + k: k for k in range(len(extra))},
        compiler_params=_params(),
    )(parts, w, m, v, *extra)


def _sum_parts(parts, name):
    _, r, c = parts.shape

    def body(p_ref, o_ref):
        g = p_ref[0]
        for k in range(1, N_DEV):
            g = g + p_ref[k]
        o_ref[...] = g

    return pl.pallas_call(body, name=name, out_shape=_sds((r, c), F32),
                          compiler_params=pltpu.CompilerParams(vmem_limit_bytes=VMEM_LIMIT))(parts)


def _adamw_small(g, w, m, v, name):
    def body(g_ref, w_ref, m_ref, v_ref, d_ref, mo_ref, vo_ref):
        d_ref[...], mo_ref[...], vo_ref[...] = _adamw_math(g_ref[...], w_ref[...], m_ref[...], v_ref[...])

    return pl.pallas_call(body, name=name, out_shape=[_sds(w.shape, F32)] * 3)(g, w, m, v)


HBM_SPEC = pl.BlockSpec(memory_space=pltpu.HBM)


def _position():
    return lax.axis_index("x"), lax.axis_index("y"), lax.axis_index("c")


def _device_index(chip, core):
    return 4 * chip[0] + 2 * chip[1] + core


def _gather(shards, layer):
    n = len(shards)
    per = 8

    def first_copies(ins, outs, send, recv):
        x, y, c = _position()
        me = 4 * x + 2 * y + c
        targets = [(x, y, 1 - c), (1 - x, y, c), (x, 1 - y, c), (1 - x, 1 - y, c)]
        remote = [pltpu.make_async_remote_copy(
            src_ref=ins[t].at[layer], dst_ref=outs[t].at[me], send_sem=send.at[per * t + k],
            recv_sem=recv.at[per * t + k], device_id=to, device_id_type=MESH)
            for t in range(n) for k, to in enumerate(targets)]
        local = [pltpu.make_async_copy(ins[t].at[layer], outs[t].at[me], send.at[per * t + 4]) for t in range(n)]
        return remote, local

    def passed_on(outs, send, recv):
        x, y, c = _position()
        chips = [(1 - x, y), (x, 1 - y), (1 - x, 1 - y)]
        return [pltpu.make_async_remote_copy(
            src_ref=outs[t].at[_device_index(chip, c)], dst_ref=outs[t].at[_device_index(chip, c)],
            send_sem=send.at[per * t + 5 + j], recv_sem=recv.at[per * t + 5 + j], device_id=(x, y, 1 - c),
            device_id_type=MESH)
            for t in range(n) for j, chip in enumerate(chips)]

    def start(ins, outs, send, recv):
        remote, local = first_copies(ins, outs, send, recv)
        for cp in local + remote:
            cp.start()

    def mid(ins, outs, send, recv):
        remote, local = first_copies(ins, outs, send, recv)
        for cp in remote:
            cp.wait()
        for cp in local:
            cp.wait()
        for cp in passed_on(outs, send, recv):
            cp.start()

    def finish(ins, outs, send, recv):
        for cp in passed_on(outs, send, recv):
            cp.wait()

    return _Comm(shards, [_sds((N_DEV,) + a.shape[1:], a.dtype) for a in shards], per * n, start, finish, mid)


def _run_comms(comms, name):
    def body():
        pass

    _, extra = _call(body, comms, None, None, None, (), name=name, in_specs=[], out_specs=[], out_shape=[])
    return extra


def _exchange(parts):
    n = len(parts)
    per = 8
    flips = [(0, 0, 1), (1, 0, 0), (1, 0, 1), (0, 1, 0), (0, 1, 1), (1, 1, 0), (1, 1, 1)]

    def copies(ins, outs, send, recv):
        x, y, c = _position()
        me = 4 * x + 2 * y + c
        remote = []
        for t in range(n):
            for k, (fx, fy, fc) in enumerate(flips):
                peer = ((1 - x if fx else x), (1 - y if fy else y), (1 - c if fc else c))
                remote.append(pltpu.make_async_remote_copy(
                    src_ref=ins[t].at[_device_index(peer[:2], peer[2])], dst_ref=outs[t].at[me],
                    send_sem=send.at[per * t + k], recv_sem=recv.at[per * t + k], device_id=peer, device_id_type=MESH))
        local = [pltpu.make_async_copy(ins[t].at[me], outs[t].at[me], send.at[per * t + 7]) for t in range(n)]
        return remote, local

    def start(ins, outs, send, recv):
        remote, local = copies(ins, outs, send, recv)
        for cp in local + remote:
            cp.start()

    def finish(ins, outs, send, recv):
        remote, local = copies(ins, outs, send, recv)
        for cp in remote:
            cp.wait()
        for cp in local:
            cp.wait()

    return _Comm(parts, [_sds(a.shape, a.dtype) for a in parts], per * n, start, finish)


def _rows128(a):
    return a.reshape(-1, 128)


def kernel(x, g_mix, w_in, w_pool, pool_scale, g_sgu, w_spatial, b_spatial, conv_c, w_branch_a, w_branch_b, w_branch_c, w_o, g_ffn, w_up, conv_ffn, conv_ffn_b, w_down, g_final, loss_target, m_g_mix, m_w_in, m_w_pool, m_pool_scale, m_g_sgu, m_w_spatial, m_b_spatial, m_conv_c, m_w_branch_a, m_w_branch_b, m_w_branch_c, m_w_o, m_g_ffn, m_w_up, m_conv_ffn, m_conv_ffn_b, m_w_down, m_g_final, v_g_mix, v_w_in, v_w_pool, v_pool_scale, v_g_sgu, v_w_spatial, v_b_spatial, v_conv_c, v_w_branch_a, v_w_branch_b, v_w_branch_c, v_w_o, v_g_ffn, v_w_up, v_conv_ffn, v_conv_ffn_b, v_w_down, v_g_final):
    s = x.shape[1]
    n_layers = g_mix.shape[0]
    x0 = x.reshape(s, D)
    target = loss_target.reshape(s, D)
    me = 4 * lax.axis_index("x") + 2 * lax.axis_index("y") + lax.axis_index("c")

    first_shards = [w_in.astype(MXU), conv_c]
    mix_shards = [w_branch_a.astype(MXU), w_branch_b.astype(MXU), w_branch_c.astype(MXU), w_o.astype(MXU),
                  w_down.astype(MXU)]
    up_shards = [w_up.astype(MXU), conv_ffn]
    (first_now,) = _run_comms([_gather(first_shards, 0)], "gather_first_0")
    mix_now = up_now = None
    wpool_b = w_pool.astype(MXU)
    bsp_t = jnp.swapaxes(b_spatial, 1, 2)
    convb_blk = conv_ffn_b.reshape(n_layers, N_DEV, NB_UP)

    saved = []
    weights = []
    xl = x0
    for l in range(n_layers):
        win8, convc8 = first_now
        convc_full = jnp.transpose(convc8, (1, 0, 2)).reshape(3, WA)
        more = l + 1 < n_layers
        mixer_args = (wpool_b[l], pool_scale[l:l + 1], g_sgu[l:l + 1], w_spatial[l], bsp_t[l], convc_full)
        if l == 0:
            (p, h), (mix_now,) = _rms_proj(xl, g_mix[l:l + 1], win8, f"in_proj_{l}", [_gather(mix_shards, l)])
            wa8, wb8, wc8, wo8, wd8 = mix_now
            (xmid, yabc, pacz, babc, merged), (up_now,) = _mixer_fwd(
                xl, p, *mixer_args, wa8, wb8, wc8, wo8, f"mixer_fwd_{l}", [_gather(up_shards, l)])
        else:
            wa8, wb8, wc8, wo8, wd8 = mix_now
            (xmid, yabc, pacz, babc, merged, p, h), (up_now,) = _mixer_fwd(
                xl, None, *mixer_args, wa8, wb8, wc8, wo8, f"mixer_fwd_{l}", [_gather(up_shards, l)],
                in_proj=(g_mix[l:l + 1], win8))
        wup8, convf8 = up_now
        weights.append((win8, wa8, wb8, wc8, wo8, wup8, wd8, convc_full, convf8))
        if more:
            (xout, h2, upre, up, act), (first_now, mix_now) = _ffn_block_fwd(
                xmid, g_ffn[l:l + 1], wup8, convf8, convb_blk[l], wd8, f"ffn_fwd_{l}",
                [_gather(first_shards, l + 1), _gather(mix_shards, l + 1)])
        else:
            (xout, h2, upre, up, act, dg_final, loss_local), _ = _ffn_block_fwd(
                xmid, g_ffn[l:l + 1], wup8, convf8, convb_blk[l], wd8, f"ffn_fwd_{l}",
                loss_head=(g_final.reshape(1, D), target))
        saved.append((xl, p, h, xmid, yabc, pacz, babc, merged, upre, h2, act, up))
        xl = xout

    dx = xl

    received = [dict() for _ in range(n_layers)]
    small = {("final", "g_final"): dg_final}
    small_sums = {}
    waiting = None

    def exchange_of(named):
        return [_exchange([a for _, a in named])]

    def land(layer, named, got):
        received[layer].update({k: a for (k, _), a in zip(named, got[0])})

    def gather_small(keys):
        packed = jnp.concatenate([_rows128(small[k]) for k in keys], axis=0)[None]
        return _gather([packed], 0)

    def sum_small(keys, gathered, name):
        summed = _sum_parts(gathered, name)
        row = 0
        for k in keys:
            n_rows = small[k].size // 128
            small_sums[k] = summed[row:row + n_rows].reshape(small[k].shape)
            row += n_rows

    for l in reversed(range(n_layers)):
        xin, p, h, xmid, yabc, pacz, babc, merged, upre, h2, act, up = saved[l]
        win8, wa8, wb8, wc8, wo8, wup8, wd8, convc_full, convf8 = weights[l]
        last = l == 0
        (dupre, dconvf, dxmid, dg_ffn), got = _ffn_block_bwd(
            dx, upre, up, convf8, wd8, wup8, xmid, g_ffn[l:l + 1], f"ffn_bwd_{l}",
            None if waiting is None else exchange_of(waiting[1]))
        if waiting is not None:
            land(waiting[0], waiting[1], got)
        small[(l, "conv_ffn")] = dconvf
        small[(l, "g_ffn")] = dg_ffn
        keys_a = [k for k in small if k not in small_sums]
        g_wdown, got = _wgrad_down(act, dx, f"wgrad_down_{l}", [gather_small(keys_a)] if last else None)
        if last:
            sum_small(keys_a, got[0][0], "sum_small_grads_a")
        down = [("w_down", g_wdown.reshape(N_DEV, ROWS_DN, D))]
        g_wup, got = _wgrad_up(h2, dupre, f"wgrad_up_{l}", exchange_of(down) if last else None)
        if last:
            land(l, down, got)
        upw = [("w_up", g_wup)]
        mixer_args = (dxmid, p, yabc, pacz, babc, wpool_b[l], pool_scale[l:l + 1], g_sgu[l:l + 1], w_spatial[l],
                      bsp_t[l], convc_full, wa8, wb8, wc8, wo8, f"mixer_bwd_{l}")
        if last:
            (dp, dbabc, dwp, dws, mixer_small, dbs), got = _mixer_bwd(*mixer_args, exchange_of(upw))
            land(l, upw, got)
        else:
            (dp, dbabc, dwp, dws, mixer_small, dbs, dx, dg_mix), got = _mixer_bwd(
                *mixer_args, exchange_of(down + upw), in_proj=(win8, xin, g_mix[l:l + 1]))
            land(l, down + upw, got)
        small.update({(l, "w_pool"): dwp, (l, "mixer_small"): mixer_small, (l, "w_spatial"): dws,
                      (l, "b_spatial"): dbs})
        g_wo, _ = _wgrad_o(merged, dxmid, f"wgrad_o_{l}")
        g_br, _ = _wgrad_branches(yabc, dbabc, f"wgrad_branches_{l}")
        mixer_w = [("branches", g_br), ("w_o", g_wo.reshape(N_DEV, ROWS_O, D))]
        keys_b = [k for k in small if k not in small_sums]
        g_win, got = _wgrad_in(h, dp, f"wgrad_in_{l}",
                               exchange_of(mixer_w) + ([gather_small(keys_b)] if last else []))
        land(l, mixer_w, got)
        inw = [("w_in", g_win)]
        if last:
            sum_small(keys_b, got[1][0], "sum_small_grads_b")
            (dx, dg_mix), got = _proj_bwd(dp, win8, xin, g_mix[l:l + 1], dxmid, f"in_proj_bwd_{l}", exchange_of(inw))
            land(l, inw, got)
        else:
            waiting = (l, inw)
        small[(l, "g_mix")] = dg_mix
    grad_x = dx.reshape(1, s, D)
    late_keys = [k for k in small if k not in small_sums]
    (gathered_late,) = _run_comms([gather_small(late_keys)], "gather_last_small_grads")[0]
    sum_small(late_keys, gathered_late, "sum_last_small_grads")

    def update_big(key, mid, w, m, v, tr, tag):
        outs = None
        for l in range(n_layers):
            parts = received[l][key]
            if parts.ndim == 3:
                parts = parts.reshape(N_DEV, 1, *parts.shape[1:])
            outs = _adamw_sum(parts, mid, w, m, v, l, outs, tr, f"adamw_{tag}_{l}")
        return outs

    up_in = update_big("w_in", 0, w_in, m_w_in, v_w_in, 256, "w_in")
    up_a = update_big("branches", 0, w_branch_a, m_w_branch_a, v_w_branch_a, WA, "w_branch_a")
    up_b = update_big("branches", 1, w_branch_b, m_w_branch_b, v_w_branch_b, WA, "w_branch_b")
    up_c = update_big("branches", 2, w_branch_c, m_w_branch_c, v_w_branch_c, WA, "w_branch_c")
    up_o = update_big("w_o", 0, w_o, m_w_o, v_w_o, ROWS_O, "w_o")
    up_up = update_big("w_up", 0, w_up, m_w_up, v_w_up, 256, "w_up")
    up_down = update_big("w_down", 0, w_down, m_w_down, v_w_down, ROWS_DN, "w_down")

    stack = lambda kind: jnp.stack([small_sums[(l, kind)] for l in range(n_layers)], axis=0)
    grad_g_mix = stack("g_mix")[:, 0, :]
    grad_w_pool = stack("w_pool")
    mixer_sums = stack("mixer_small")
    grad_pool_scale = mixer_sums[:, 0, :]
    grad_g_sgu = mixer_sums[:, 1, :]
    grad_conv_c = lax.dynamic_slice_in_dim(mixer_sums[:, 2:5, :], me * (WA // N_DEV), WA // N_DEV, axis=2)
    grad_w_spatial = stack("w_spatial")
    grad_b_spatial = stack("b_spatial")[:, 0:HEADS, :]
    grad_g_ffn = stack("g_ffn")[:, 0, :]
    conv_grads = stack("conv_ffn")
    grad_conv_ffn = lax.dynamic_index_in_dim(conv_grads, me, axis=1, keepdims=False)[:, 0:3, :]
    grad_conv_ffn_b = conv_grads[:, :, 3, :].reshape(n_layers, 2 * DFF)
    grad_g_final = small_sums[("final", "g_final")][0]

    def update_small(g, w, m, v, tag):
        shape2 = (-1, w.shape[-1])
        outs = _adamw_small(g.reshape(shape2), w.reshape(shape2), m.reshape(shape2), v.reshape(shape2), f"adamw_{tag}")
        return [g] + [o.reshape(w.shape) for o in outs]

    up = {
        "g_mix": update_small(grad_g_mix, g_mix, m_g_mix, v_g_mix, "g_mix"),
        "w_in": up_in,
        "w_pool": update_small(grad_w_pool, w_pool, m_w_pool, v_w_pool, "w_pool"),
        "pool_scale": update_small(grad_pool_scale, pool_scale, m_pool_scale, v_pool_scale, "pool_scale"),
        "g_sgu": update_small(grad_g_sgu, g_sgu, m_g_sgu, v_g_sgu, "g_sgu"),
        "w_spatial": update_small(grad_w_spatial, w_spatial, m_w_spatial, v_w_spatial, "w_spatial"),
        "b_spatial": update_small(grad_b_spatial, b_spatial, m_b_spatial, v_b_spatial, "b_spatial"),
        "conv_c": update_small(grad_conv_c, conv_c, m_conv_c, v_conv_c, "conv_c"),
        "w_branch_a": up_a,
        "w_branch_b": up_b,
        "w_branch_c": up_c,
        "w_o": up_o,
        "g_ffn": update_small(grad_g_ffn, g_ffn, m_g_ffn, v_g_ffn, "g_ffn"),
        "w_up": up_up,
        "conv_ffn": update_small(grad_conv_ffn, conv_ffn, m_conv_ffn, v_conv_ffn, "conv_ffn"),
        "conv_ffn_b": update_small(grad_conv_ffn_b, conv_ffn_b, m_conv_ffn_b, v_conv_ffn_b, "conv_ffn_b"),
        "w_down": up_down,
        "g_final": update_small(grad_g_final, g_final, m_g_final, v_g_final, "g_final"),
    }
    loss = lax.psum(loss_local[0, 0], AXES)
    order = list(up)
    return (loss, grad_x, *[up[k][0] for k in order], *[up[k][1] for k in order], *[up[k][2] for k in order],
            *[up[k][3] for k in order])
```

```python
import functools

import jax
import jax.numpy as jnp
from jax import lax
from jax.experimental import pallas as pl
from jax.experimental.pallas import tpu as pltpu

F32 = jnp.float32
BF16 = jnp.bfloat16
MXU = BF16
ACT = BF16
WIRE = BF16

N_DEV = 8
D = 1024
WA = 512
NCOL = 6144
DFF = 2816
NB_IN = NCOL // N_DEV
NB_UP = 2 * DFF // N_DEV
NB_BR = D // N_DEV
ROWS_O = D // N_DEV
ROWS_DN = DFF // N_DEV
CHUNK = 128
HEADS = 4
POOL_WINDOWS = (2, 4, 8, 16)
EPS = 1e-6
A0, UV0, CB0, CC0, CX0, GA0, GB0, GC0 = 0, 512, 1536, 2048, 2560, 3072, 4096, 5120

ADAM_LR = 0.001
ADAM_B1 = 0.9
ADAM_B2 = 0.999
ADAM_EPS = 1e-08
ADAM_WD = 0.01
ADAM_STEP = 10

TM = 256
TS_WGRAD = 4096
HALO_POOL = 16
HALO_CONV = 8
VMEM_LIMIT = 56 * 1024 * 1024
MESH = pl.DeviceIdType.MESH
AXES = ("x", "y", "c")


def _sds(shape, dtype):
    return jax.ShapeDtypeStruct(tuple(shape), dtype)


def _params(n_grid=1):
    return pltpu.CompilerParams(dimension_semantics=("arbitrary",) * n_grid, vmem_limit_bytes=VMEM_LIMIT)


def _const(block, index):
    return pl.BlockSpec(block, lambda *_: index, pipeline_mode=pl.Buffered(1))


def _dot(a, b):
    return jnp.dot(a, b, preferred_element_type=F32)


def _dot_nt(a, b):
    return lax.dot_general(a, b, (((1,), (1,)), ((), ())), preferred_element_type=F32)


def _dot_tn(a, b):
    return lax.dot_general(a, b, (((0,), (0,)), ((), ())), preferred_element_type=F32)


def _sigmoid(v):
    return 0.5 * jnp.tanh(0.5 * v) + 0.5


def _shift_down(v, k):
    return pltpu.roll(v, k, axis=0)


def _shift_up(v, k):
    return pltpu.roll(v, v.shape[0] - k, axis=0)


def _colsum(v):
    return jnp.sum(v, axis=0, keepdims=True)


def _lane_cat(ref):
    return jnp.concatenate([ref[d] for d in range(N_DEV)], axis=1)


class _Comm:
    def __init__(self, operands, out_shapes, n_sems, start, finish, mid=None):
        self.operands = list(operands)
        self.out_shapes = list(out_shapes)
        self.n_sems = n_sems
        self.start = start
        self.mid = mid
        self.finish = finish


def _call(body, comms, is_first, is_mid, is_last, operands, *, in_specs, out_specs, out_shape, scratch_shapes=(), **kw):
    n_in, n_out, n_scr = len(in_specs), len(out_specs), len(scratch_shapes)
    comms = [c for c in (comms or []) if c is not None]
    if not comms:
        res = pl.pallas_call(body, in_specs=in_specs, out_specs=out_specs, out_shape=out_shape,
                             scratch_shapes=list(scratch_shapes), **kw)(*operands)
        return res, []
    nci = [len(c.operands) for c in comms]
    nco = [len(c.out_shapes) for c in comms]

    def split(refs, sizes):
        parts = []
        for n in sizes:
            parts.append(refs[:n])
            refs = refs[n:]
        return parts, refs

    def carrier(*refs):
        ins, refs = refs[:n_in], refs[n_in:]
        cins, refs = split(refs, nci)
        outs, refs = refs[:n_out], refs[n_out:]
        couts, refs = split(refs, nco)
        scr, sems = refs[:n_scr], refs[n_scr:]

        def run(step):
            for k, c in enumerate(comms):
                if getattr(c, step) is not None:
                    getattr(c, step)(cins[k], couts[k], sems[2 * k], sems[2 * k + 1])

        def at(mark, step):
            if mark is None:
                run(step)
            else:
                pl.when(mark())(lambda: run(step))

        at(is_first, "start")
        body(*ins, *outs, *scr)
        at(is_mid, "mid")
        at(is_last, "finish")

    res = pl.pallas_call(
        carrier, in_specs=list(in_specs) + [HBM_SPEC] * sum(nci), out_specs=list(out_specs) + [HBM_SPEC] * sum(nco),
        out_shape=list(out_shape) + [s for c in comms for s in c.out_shapes],
        scratch_shapes=list(scratch_shapes) + [pltpu.SemaphoreType.DMA((c.n_sems,)) for c in comms for _ in range(2)],
        **kw,
    )(*operands, *[a for c in comms for a in c.operands])
    extra, _ = split(res[n_out:], nco)
    return res[:n_out], extra


def _grid_marks(nt):
    return (lambda: pl.program_id(0) == 0), (lambda: pl.program_id(0) == (3 * nt) // 4), (lambda: pl.program_id(0) == nt - 1)


def _rms_proj(x, g, w_all, name, comms=None):
    s = x.shape[0]
    nb = w_all.shape[-1]
    nt = s // TM

    def body(x_ref, g_ref, w_ref, p_ref, h_ref):
        xf = x_ref[...]
        r = lax.rsqrt(jnp.mean(xf * xf, axis=-1, keepdims=True) + EPS)
        h = (xf * r * g_ref[...]).astype(MXU)
        h_ref[...] = h
        for j in range(N_DEV):
            p_ref[:, j * nb:(j + 1) * nb] = _dot(h, w_ref[j]).astype(p_ref.dtype)

    row = lambda n: pl.BlockSpec((TM, n), lambda i: (i, 0))
    return _call(
        body, comms, *_grid_marks(nt), (x, g, w_all),
        name=name, grid=(nt,),
        in_specs=[row(D), _const((1, D), (0, 0)), _const((N_DEV, D, nb), (0, 0, 0))],
        out_specs=[row(N_DEV * nb), row(D)],
        out_shape=[_sds((s, N_DEV * nb), ACT), _sds((s, D), MXU)],
        compiler_params=_params(),
    )


def _tril_mask():
    r = lax.broadcasted_iota(jnp.int32, (CHUNK, CHUNK), 0)
    c = lax.broadcasted_iota(jnp.int32, (CHUNK, CHUNK), 1)
    return r >= c


def _gelu_parts(v):
    c0 = 0.7978845608028654
    th = jnp.tanh(c0 * (v + 0.044715 * (v * v * v)))
    cdf = 0.5 * (1.0 + th)
    dgelu = cdf + v * (0.5 * c0) * (1.0 - th * th) * (1.0 + 3.0 * 0.044715 * (v * v))
    return v * cdf, dgelu


def _mixer_fwd(x, p, wpool, pscale, gsgu, wsp, bsp_t, convc, wa_all, wb_all, wc_all, wo_all, name, comms=None,
               in_proj=None):
    s = x.shape[0]
    nt = s // TM
    fused = in_proj is not None

    def body(*refs):
        if fused:
            (x_ref, g_ref, win_ref, wpool_ref, ps_ref, gs_ref, wsp_ref, bsp_ref, cc_ref, wa_ref, wb_ref, wc_ref, wo_ref,
             xmid_ref, y_ref, pz_ref, b_ref, m_ref, p_ref, h_ref, carry_a, carry_z) = refs
        else:
            (x_ref, p_ref, wpool_ref, ps_ref, gs_ref, wsp_ref, bsp_ref, cc_ref, wa_ref, wb_ref, wc_ref, wo_ref,
             xmid_ref, y_ref, pz_ref, b_ref, m_ref, carry_a, carry_z) = refs
        i = pl.program_id(0)

        @pl.when(i == 0)
        def _():
            carry_a[...] = jnp.zeros_like(carry_a)
            carry_z[...] = jnp.zeros_like(carry_z)

        def pf(lo, n):
            return p_ref[:, lo:lo + n].astype(F32)

        def project(blocks):
            if fused:
                for j in blocks:
                    p_ref[:, j * NB_IN:(j + 1) * NB_IN] = _dot(h_ref[...], win_ref[j]).astype(ACT)

        if fused:
            xf = x_ref[...]
            r = lax.rsqrt(jnp.mean(xf * xf, axis=-1, keepdims=True) + EPS)
            h_ref[...] = (xf * r * g_ref[...]).astype(MXU)
        project((0, 1, 2))

        a = pf(A0, WA)
        ext = jnp.concatenate([carry_a[...], a], axis=0)
        carry_a[...] = a[TM - HALO_POOL:, :]
        t_pos = (i * TM + lax.broadcasted_iota(jnp.int32, (TM, 1), 0)).astype(F32)
        for g, win in enumerate(POOL_WINDOWS):
            cols = slice(g * CHUNK, (g + 1) * CHUNK)
            acc = ext[:, cols]
            k = 1
            while k < win:
                acc = acc + _shift_down(acc, k)
                k *= 2
            cnt = jnp.minimum(t_pos + 1.0, float(win))
            pa_g = (acc[HALO_POOL:, :] / cnt - a[:, cols]).astype(MXU)
            pz_ref[:, cols] = pa_g
            y_ref[:, cols] = (_dot(pa_g, wpool_ref[g]) * ps_ref[:, cols]).astype(ACT)

        project((3,))
        uvg, _ = _gelu_parts(pf(UV0, 2 * WA))
        u = uvg[:, :WA]
        v = uvg[:, WA:]
        rv = lax.rsqrt(jnp.mean(v * v, axis=-1, keepdims=True) + EPS)
        vn = (v * rv * gs_ref[...]).astype(MXU)
        mask = _tril_mask()
        for g in range(HEADS):
            cols = slice(g * CHUNK, (g + 1) * CHUNK)
            wt = jnp.where(mask, wsp_ref[g], 0.0).astype(MXU)
            bcol = bsp_ref[:, g:g + 1]
            for c in range(TM // CHUNK):
                rows = slice(c * CHUNK, (c + 1) * CHUNK)
                sv = _dot(wt, vn[rows, cols]) + bcol
                y_ref[rows, WA + g * CHUNK:WA + (g + 1) * CHUNK] = (u[rows, cols] * sv).astype(ACT)

        project((4, 5))
        z = pf(CC0, WA) * pf(CX0, WA)
        extz = jnp.concatenate([carry_z[...], z], axis=0)
        carry_z[...] = z[TM - HALO_CONV:, :]
        cz = (cc_ref[0:1, :] * _shift_down(extz, 2)[HALO_CONV:, :]
              + cc_ref[1:2, :] * _shift_down(extz, 1)[HALO_CONV:, :] + cc_ref[2:3, :] * z)
        pz_ref[:, WA:2 * WA] = cz.astype(ACT)
        y_ref[:, 2 * WA:3 * WA] = (pf(CB0, WA) * cz).astype(ACT)

        project((6, 7))
        merged = jnp.zeros((TM, D), F32)
        for k, (w_ref, glo) in enumerate(((wa_ref, GA0), (wb_ref, GB0), (wc_ref, GC0))):
            br = _dot(y_ref[:, k * WA:(k + 1) * WA], _lane_cat(w_ref))
            b_ref[:, k * D:(k + 1) * D] = br.astype(ACT)
            merged = merged + _sigmoid(pf(glo, D)) * br
        mb = merged.astype(MXU)
        m_ref[...] = mb
        xmid_ref[...] = x_ref[...] + _dot(mb, wo_ref[...].reshape(D, D))

    row = lambda n: pl.BlockSpec((TM, n), lambda i: (i, 0))
    br_spec = _const((N_DEV, WA, NB_BR), (0, 0, 0))
    if fused:
        lead_specs = [row(D), _const((1, D), (0, 0)), _const((N_DEV, D, NB_IN), (0, 0, 0))]
        lead = (x,) + tuple(in_proj)
    else:
        lead_specs = [row(D), row(NCOL)]
        lead = (x, p)
    more_specs = [row(NCOL), row(D)] if fused else []
    more_shapes = [_sds((s, NCOL), ACT), _sds((s, D), MXU)] if fused else []
    return _call(
        body, comms, *_grid_marks(nt), lead + (wpool, pscale, gsgu, wsp, bsp_t, convc, wa_all, wb_all, wc_all, wo_all),
        name=name, grid=(nt,),
        in_specs=lead_specs + [_const((HEADS, CHUNK, CHUNK), (0, 0, 0)), _const((1, WA), (0, 0)),
                               _const((1, WA), (0, 0)), _const((HEADS, CHUNK, CHUNK), (0, 0, 0)),
                               _const((CHUNK, HEADS), (0, 0)), _const((3, WA), (0, 0)), br_spec, br_spec, br_spec,
                               _const((N_DEV, ROWS_O, D), (0, 0, 0))],
        out_specs=[row(D), row(3 * WA), row(2 * WA), row(3 * D), row(D)] + more_specs,
        out_shape=[_sds((s, D), F32), _sds((s, 3 * WA), ACT), _sds((s, 2 * WA), ACT), _sds((s, 3 * D), ACT),
                   _sds((s, D), MXU)] + more_shapes,
        scratch_shapes=[pltpu.VMEM((HALO_POOL, WA), F32), pltpu.VMEM((HALO_CONV, WA), F32)],
        compiler_params=_params(),
    )


def _conv_up(ext, cur, w_ref, j, b_row):
    return (w_ref[j, 0:1, :] * _shift_down(ext, 2)[HALO_CONV:, :] + w_ref[j, 1:2, :] * _shift_down(ext, 1)[HALO_CONV:, :]
            + w_ref[j, 2:3, :] * cur + b_row)


def _ffn_block_fwd(xmid, g, wup_all, convf_all, convb, wd_all, name, comms=None, loss_head=None):
    s = xmid.shape[0]
    nt = s // TM
    half = N_DEV // 2
    with_loss = loss_head is not None

    def body(*refs):
        (x_ref, g_ref, wup_ref, cw_ref, cb_ref, wd_ref), refs = refs[:6], refs[6:]
        if with_loss:
            (gf_ref, t_ref), refs = refs[:2], refs[2:]
        (xo_ref, h_ref, u_ref, up_ref, act_ref), refs = refs[:5], refs[5:]
        if with_loss:
            (dgf_ref, loss_ref), refs = refs[:2], refs[2:]
        (carry,) = refs
        i = pl.program_id(0)

        @pl.when(i == 0)
        def _():
            carry[...] = jnp.zeros_like(carry)
            if with_loss:
                dgf_ref[...] = jnp.zeros_like(dgf_ref)
                loss_ref[...] = jnp.zeros_like(loss_ref)

        xf = x_ref[...]
        r = lax.rsqrt(jnp.mean(xf * xf, axis=-1, keepdims=True) + EPS)
        h = (xf * r * g_ref[...]).astype(MXU)
        h_ref[...] = h

        def project(j):
            return _dot(h, wup_ref[j]).astype(ACT)

        def conv(j, pre):
            u_ref[j] = pre
            cur = pre.astype(F32)
            ext = jnp.concatenate([carry[j], cur], axis=0)
            carry[j] = cur[TM - HALO_CONV:, :]
            up = _conv_up(ext, cur, cw_ref, j, cb_ref[j:j + 1, :])
            up_ref[j] = up.astype(ACT)
            return up

        order = [j + k * half for j in range(half) for k in range(2)]
        acc = xf
        ahead = 2
        pres = {n: project(order[n]) for n in range(ahead)}
        ups = {}
        for n, j in enumerate(order):
            if n + ahead < N_DEV:
                pres[n + ahead] = project(order[n + ahead])
            ups[j] = conv(j, pres.pop(n))
            if j >= half:
                gate, val = ups.pop(j - half), ups.pop(j)
                act = (gate * _sigmoid(gate) * val).astype(MXU)
                act_ref[j - half] = act
                wd = jnp.concatenate([wd_ref[2 * (j - half)], wd_ref[2 * (j - half) + 1]], axis=0)
                acc = acc + _dot(act, wd)
        if not with_loss:
            xo_ref[...] = acc
        else:
            ro = lax.rsqrt(jnp.mean(acc * acc, axis=-1, keepdims=True) + EPS)
            xn = acc * ro
            err = xn * gf_ref[...] - t_ref[...]
            loss_ref[...] += 0.5 * jnp.sum(jnp.mean(err * err, axis=-1, keepdims=True), axis=0, keepdims=True)
            dy = err * (1.0 / D)
            dgf_ref[0:1, :] += _colsum(dy * xn)
            dyg = dy * gf_ref[...]
            xo_ref[...] = ro * (dyg - xn * jnp.mean(dyg * xn, axis=-1, keepdims=True))

    row = pl.BlockSpec((TM, D), lambda i: (i, 0))
    blocks = pl.BlockSpec((N_DEV, TM, NB_UP), lambda i: (0, i, 0))
    return _call(
        body, comms, *_grid_marks(nt), (xmid, g, wup_all, convf_all, convb, wd_all) + (tuple(loss_head) if with_loss else ()),
        name=name, grid=(nt,),
        in_specs=[row, _const((1, D), (0, 0)), _const((N_DEV, D, NB_UP), (0, 0, 0)),
                  _const((N_DEV, 3, NB_UP), (0, 0, 0)), _const((N_DEV, NB_UP), (0, 0)),
                  _const((N_DEV, ROWS_DN, D), (0, 0, 0))] + ([_const((1, D), (0, 0)), row] if with_loss else []),
        out_specs=[row, row, blocks, blocks, pl.BlockSpec((half, TM, NB_UP), lambda i: (0, i, 0))]
        + ([pl.BlockSpec((8, D), lambda i: (0, 0)), pl.BlockSpec((1, 1), lambda i: (0, 0))] if with_loss else []),
        out_shape=[_sds((s, D), F32), _sds((s, D), MXU), _sds((N_DEV, s, NB_UP), ACT), _sds((N_DEV, s, NB_UP), ACT),
                   _sds((half, s, NB_UP), MXU)] + ([_sds((8, D), F32), _sds((1, 1), F32)] if with_loss else []),
        scratch_shapes=[pltpu.VMEM((N_DEV, HALO_CONV, NB_UP), F32)],
        compiler_params=_params(),
    )


def _ffn_block_bwd(dxo, upre, up, convf_all, wd_all, wup_all, xmid, g, name, comms=None):
    s = dxo.shape[0]
    nt = s // TM
    half = N_DEV // 2

    def body(dx_ref, u_ref, up_ref, cw_ref, wd_ref, wup_ref, x_ref, g_ref, du_ref, dc_ref, dxm_ref, dg_ref, carry):
        step = pl.program_id(0)

        @pl.when(step == 0)
        def _():
            carry[...] = jnp.zeros_like(carry)
            dc_ref[...] = jnp.zeros_like(dc_ref)
            dg_ref[...] = jnp.zeros_like(dg_ref)

        dxo_t = dx_ref[...]
        dxb = dxo_t.astype(MXU)

        def adjoint(j, d_up):
            cur = u_ref[j].astype(F32)
            ext = jnp.concatenate([d_up, carry[j]], axis=0)
            carry[j] = d_up[:HALO_CONV, :]
            up1 = _shift_up(ext, 1)[:TM, :]
            up2 = _shift_up(ext, 2)[:TM, :]
            du = (cw_ref[j, 2:3, :] * d_up + cw_ref[j, 1:2, :] * up1 + cw_ref[j, 0:1, :] * up2).astype(du_ref.dtype)
            du_ref[j] = du
            dc_ref[j, 0:1, :] += _colsum(cur * up2)
            dc_ref[j, 1:2, :] += _colsum(cur * up1)
            dc_ref[j, 2:3, :] += _colsum(cur * d_up)
            dc_ref[j, 3:4, :] += _colsum(d_up)
            return _dot_nt(du, wup_ref[j])

        def d_act(j):
            return _dot_nt(dxb, jnp.concatenate([wd_ref[2 * j], wd_ref[2 * j + 1]], axis=0))

        dh = jnp.zeros((TM, D), F32)
        dact = d_act(0)
        for j in range(half):
            nxt = d_act(j + 1) if j + 1 < half else None
            gate = up_ref[j].astype(F32)
            val = up_ref[j + half].astype(F32)
            sg = _sigmoid(gate)
            dh = dh + adjoint(j, dact * val * sg * (1.0 + gate * (1.0 - sg)))
            dh = dh + adjoint(j + half, dact * gate * sg)
            dact = nxt

        xf = x_ref[...]
        r = lax.rsqrt(jnp.mean(xf * xf, axis=-1, keepdims=True) + EPS)
        xn = xf * r
        dg_ref[0:1, :] += _colsum(dh * xn)
        dhg = dh * g_ref[...]
        dxm_ref[...] = dxo_t + r * (dhg - xn * jnp.mean(dhg * xn, axis=-1, keepdims=True))

    blocks = pl.BlockSpec((N_DEV, TM, NB_UP), lambda i: (0, nt - 1 - i, 0))
    row = pl.BlockSpec((TM, D), lambda i: (nt - 1 - i, 0))
    return _call(
        body, comms, *_grid_marks(nt), (dxo, upre, up, convf_all, wd_all, wup_all, xmid, g),
        name=name, grid=(nt,),
        in_specs=[row, blocks, blocks, _const((N_DEV, 3, NB_UP), (0, 0, 0)), _const((N_DEV, ROWS_DN, D), (0, 0, 0)),
                  _const((N_DEV, D, NB_UP), (0, 0, 0)), row, _const((1, D), (0, 0))],
        out_specs=[blocks, pl.BlockSpec((N_DEV, 8, NB_UP), lambda i: (0, 0, 0)), row,
                   pl.BlockSpec((8, D), lambda i: (0, 0))],
        out_shape=[_sds((N_DEV, s, NB_UP), MXU), _sds((N_DEV, 8, NB_UP), F32), _sds((s, D), F32), _sds((8, D), F32)],
        scratch_shapes=[pltpu.VMEM((N_DEV, HALO_CONV, NB_UP), F32)],
        compiler_params=_params(),
    )


def _proj_bwd(dy, w_all, x, g, dres, name, comms=None):
    s = x.shape[0]
    nb = w_all.shape[-1]
    nt = s // TM

    def body(dy_ref, w_ref, x_ref, g_ref, dres_ref, dx_ref, dg_ref):
        i = pl.program_id(0)

        @pl.when(i == 0)
        def _():
            dg_ref[...] = jnp.zeros_like(dg_ref)

        dh = jnp.zeros((TM, D), F32)
        for j in range(N_DEV):
            dh = dh + _dot_nt(dy_ref[:, j * nb:(j + 1) * nb], w_ref[j])
        xf = x_ref[...]
        r = lax.rsqrt(jnp.mean(xf * xf, axis=-1, keepdims=True) + EPS)
        xn = xf * r
        dg_ref[0:1, :] += _colsum(dh * xn)
        dhg = dh * g_ref[...]
        dx_ref[...] = dres_ref[...] + r * (dhg - xn * jnp.mean(dhg * xn, axis=-1, keepdims=True))

    dy_spec = pl.BlockSpec((TM, N_DEV * nb), lambda i: (i, 0))
    row = pl.BlockSpec((TM, D), lambda i: (i, 0))
    return _call(
        body, comms, *_grid_marks(nt), (dy, w_all, x, g, dres),
        name=name, grid=(nt,),
        in_specs=[dy_spec, _const((N_DEV, D, nb), (0, 0, 0)), row, _const((1, D), (0, 0)), row],
        out_specs=[row, pl.BlockSpec((8, D), lambda i: (0, 0))],
        out_shape=[_sds((s, D), F32), _sds((8, D), F32)],
        compiler_params=_params(),
    )


def _mixer_bwd(dxmid, p, yabc, pacz, babc, wpool, pscale, gsgu, wsp, bsp_t, convc, wa_all, wb_all, wc_all, wo_all,
               name, comms=None, in_proj=None):
    s = dxmid.shape[0]
    nt = s // TM
    fused = in_proj is not None

    def body(*refs):
        (dx_ref, p_ref, y_ref, pz_ref, b_ref, wpool_ref, ps_ref, gs_ref, wsp_ref, bsp_ref, cc_ref,
         wa_ref, wb_ref, wc_ref, wo_ref) = refs[:15]
        refs = refs[15:]
        if fused:
            (win_ref, x_ref, g_ref), refs = refs[:3], refs[3:]
        (dp_ref, db_ref, dwp_ref, dws_ref, small_ref, dbs_ref), refs = refs[:6], refs[6:]
        if fused:
            (dxin_ref, dg_ref), refs = refs[:2], refs[2:]
        carry_pa, carry_cz, dbs_acc, du_s, dvn_s = refs
        step = pl.program_id(0)
        tile = nt - 1 - step

        @pl.when(step == 0)
        def _():
            for ref in (carry_pa, carry_cz, dbs_acc, dwp_ref, dws_ref, small_ref, dbs_ref) + ((dg_ref,) if fused else ()):
                ref[...] = jnp.zeros_like(ref)

        def pf(lo, n):
            return p_ref[:, lo:lo + n].astype(F32)

        def back_project(blocks):
            return sum(_dot_nt(dp_ref[:, j * NB_IN:(j + 1) * NB_IN], win_ref[j]) for j in blocks)

        dxm = dx_ref[...]
        dm = _dot_nt(dxm.astype(MXU), wo_ref[...].reshape(D, D))

        def through_gate(k, glo, w_ref):
            sg = _sigmoid(pf(glo, D))
            br = b_ref[:, k * D:(k + 1) * D].astype(F32)
            dp_ref[:, glo:glo + D] = (dm * br * sg * (1.0 - sg)).astype(dp_ref.dtype)
            dbr = (dm * sg).astype(MXU)
            db_ref[:, k * D:(k + 1) * D] = dbr
            return _dot_nt(dbr, _lane_cat(w_ref))

        dya = through_gate(0, GA0, wa_ref)
        dyb = through_gate(1, GB0, wb_ref)
        dyc = through_gate(2, GC0, wc_ref)
        if fused:
            dh = back_project((4, 5, 6, 7))

        t_pos = (tile * TM + lax.broadcasted_iota(jnp.int32, (TM, 1), 0)).astype(F32)
        for g, win in enumerate(POOL_WINDOWS):
            cols = slice(g * CHUNK, (g + 1) * CHUNK)
            pa_g = pz_ref[:, cols]
            q = _dot(pa_g, wpool_ref[g])
            dya_g = dya[:, cols]
            small_ref[0:1, cols] += _colsum(dya_g * q)
            dq = (dya_g * ps_ref[:, cols]).astype(MXU)
            dpa_g = _dot_nt(dq, wpool_ref[g])
            dwp_ref[g] += _dot_tn(pa_g, dq)
            dpw = dpa_g / jnp.minimum(t_pos + 1.0, float(win))
            acc = jnp.concatenate([dpw, carry_pa[:, cols]], axis=0)
            carry_pa[:, cols] = dpw[:HALO_POOL, :]
            k = 1
            while k < win:
                acc = acc + _shift_up(acc, k)
                k *= 2
            dp_ref[:, cols] = (acc[:TM, :] - dpa_g).astype(dp_ref.dtype)

        uvp = pf(UV0, 2 * WA)
        uvg, dgelu = _gelu_parts(uvp)
        u = uvg[:, :WA]
        v = uvg[:, WA:]
        rv = lax.rsqrt(jnp.mean(v * v, axis=-1, keepdims=True) + EPS)
        vh = v * rv
        vn = (vh * gs_ref[...]).astype(MXU)
        mask = _tril_mask()
        for g in range(HEADS):
            cols = slice(g * CHUNK, (g + 1) * CHUNK)
            wt32 = jnp.where(mask, wsp_ref[g], 0.0)
            wt = wt32.astype(MXU)
            wt_t = wt32.T.astype(MXU)
            bcol = bsp_ref[:, g:g + 1]
            for c in range(TM // CHUNK):
                rows = slice(c * CHUNK, (c + 1) * CHUNK)
                vn_cg = vn[rows, cols]
                sv = _dot(wt, vn_cg) + bcol
                dyb_cg = dyb[rows, cols]
                du_s[rows, cols] = dyb_cg * sv
                dsv = dyb_cg * u[rows, cols]
                dbs_acc[g] += dsv
                dsv_b = dsv.astype(MXU)
                dws_ref[g] += _dot_nt(dsv_b, vn_cg)
                dvn_s[rows, cols] = _dot(wt_t, dsv_b)
        dvn = dvn_s[...]
        small_ref[1:2, :] += _colsum(dvn * vh)
        dvg = dvn * gs_ref[...]
        dv = rv * (dvg - vh * jnp.mean(dvg * vh, axis=-1, keepdims=True))
        dp_ref[:, UV0:UV0 + WA] = (du_s[...] * dgelu[:, :WA]).astype(dp_ref.dtype)
        dp_ref[:, UV0 + WA:UV0 + 2 * WA] = (dv * dgelu[:, WA:]).astype(dp_ref.dtype)
        if fused:
            dh = dh + back_project((0, 1))

        cb = pf(CB0, WA)
        cc = pf(CC0, WA)
        cx = pf(CX0, WA)
        z = cc * cx
        dp_ref[:, CB0:CB0 + WA] = (dyc * pz_ref[:, WA:2 * WA].astype(F32)).astype(dp_ref.dtype)
        dcz = dyc * cb
        extz = jnp.concatenate([dcz, carry_cz[...]], axis=0)
        carry_cz[...] = dcz[:HALO_CONV, :]
        up1 = _shift_up(extz, 1)[:TM, :]
        up2 = _shift_up(extz, 2)[:TM, :]
        dz = cc_ref[2:3, :] * dcz + cc_ref[1:2, :] * up1 + cc_ref[0:1, :] * up2
        small_ref[2:3, :] += _colsum(z * up2)
        small_ref[3:4, :] += _colsum(z * up1)
        small_ref[4:5, :] += _colsum(z * dcz)
        dp_ref[:, CC0:CC0 + WA] = (dz * cx).astype(dp_ref.dtype)
        dp_ref[:, CX0:CX0 + WA] = (dz * cc).astype(dp_ref.dtype)

        if fused:
            dh = dh + back_project((2, 3))
            xf = x_ref[...]
            r = lax.rsqrt(jnp.mean(xf * xf, axis=-1, keepdims=True) + EPS)
            xn = xf * r
            dg_ref[0:1, :] += _colsum(dh * xn)
            dhg = dh * g_ref[...]
            dxin_ref[...] = dxm + r * (dhg - xn * jnp.mean(dhg * xn, axis=-1, keepdims=True))

        @pl.when(step == nt - 1)
        def _():
            ones = jnp.ones((8, CHUNK), F32)
            for g in range(HEADS):
                dws_ref[g] = jnp.where(mask, dws_ref[g], 0.0)
                row = lax.dot_general(ones, dbs_acc[g], (((1,), (1,)), ((), ())), preferred_element_type=F32,
                                      precision=lax.Precision.HIGHEST)
                dbs_ref[g:g + 1, :] = row[0:1, :]

    row = lambda n: pl.BlockSpec((TM, n), lambda i: (nt - 1 - i, 0))
    br_spec = _const((N_DEV, WA, NB_BR), (0, 0, 0))
    acc_spec = lambda shape: pl.BlockSpec(shape, lambda i: (0,) * len(shape))
    more_in = tuple(in_proj) if fused else ()
    more_in_specs = [_const((N_DEV, D, NB_IN), (0, 0, 0)), row(D), _const((1, D), (0, 0))] if fused else []
    more_out_specs = [row(D), acc_spec((8, D))] if fused else []
    more_out_shapes = [_sds((s, D), F32), _sds((8, D), F32)] if fused else []
    return _call(
        body, comms, *_grid_marks(nt),
        (dxmid, p, yabc, pacz, babc, wpool, pscale, gsgu, wsp, bsp_t, convc, wa_all, wb_all, wc_all, wo_all) + more_in,
        name=name, grid=(nt,),
        in_specs=[row(D), row(NCOL), row(3 * WA), row(2 * WA), row(3 * D),
                  _const((HEADS, CHUNK, CHUNK), (0, 0, 0)), _const((1, WA), (0, 0)), _const((1, WA), (0, 0)),
                  _const((HEADS, CHUNK, CHUNK), (0, 0, 0)), _const((CHUNK, HEADS), (0, 0)), _const((3, WA), (0, 0)),
                  br_spec, br_spec, br_spec, _const((N_DEV, ROWS_O, D), (0, 0, 0))] + more_in_specs,
        out_specs=[row(NCOL), row(3 * D), acc_spec((HEADS, CHUNK, CHUNK)), acc_spec((HEADS, CHUNK, CHUNK)),
                   acc_spec((8, WA)), acc_spec((8, CHUNK))] + more_out_specs,
        out_shape=[_sds((s, NCOL), MXU), _sds((s, 3 * D), MXU), _sds((HEADS, CHUNK, CHUNK), F32),
                   _sds((HEADS, CHUNK, CHUNK), F32), _sds((8, WA), F32), _sds((8, CHUNK), F32)] + more_out_shapes,
        scratch_shapes=[pltpu.VMEM((HALO_POOL, WA), F32), pltpu.VMEM((HALO_CONV, WA), F32),
                        pltpu.VMEM((HEADS, CHUNK, CHUNK), F32), pltpu.VMEM((TM, WA), F32), pltpu.VMEM((TM, WA), F32)],
        compiler_params=_params(),
    )


def _wgrad(a, b, a_spec, b_spec, n_out, acc_shape, out_shape, out_spec, store, name, comms=None):
    s = a.shape[-2]
    ts = min(TS_WGRAD if b.dtype == MXU else TS_WGRAD // 2, s)
    n_steps = s // ts

    def body(a_ref, b_ref, o_ref, acc_ref):
        k = pl.program_id(1)

        @pl.when(k == 0)
        def _():
            acc_ref[...] = jnp.zeros_like(acc_ref)

        acc_ref[...] += _dot_tn(a_ref[...], b_ref[...].astype(MXU))

        @pl.when(k == n_steps - 1)
        def _():
            store(o_ref, acc_ref)

    (res,), extra = _call(
        body, comms, lambda: (pl.program_id(0) == 0) & (pl.program_id(1) == 0),
        lambda: (pl.program_id(0) == (3 * n_out) // 4) & (pl.program_id(1) == 0),
        lambda: (pl.program_id(0) == n_out - 1) & (pl.program_id(1) == n_steps - 1), (a, b),
        name=name, grid=(n_out, n_steps),
        in_specs=[a_spec(ts), b_spec(ts)], out_specs=[out_spec], out_shape=[_sds(out_shape, WIRE)],
        scratch_shapes=[pltpu.VMEM(acc_shape, F32)],
        compiler_params=_params(2),
    )
    return res, extra


def _store_plain(o_ref, acc_ref):
    o_ref[...] = acc_ref[...].astype(o_ref.dtype)


def _store_lane_blocks(o_ref, acc_ref):
    for d in range(N_DEV):
        o_ref[d] = acc_ref[:, d * NB_BR:(d + 1) * NB_BR].astype(o_ref.dtype)


def _wgrad_in(h, dp, name, comm=None):
    return _wgrad(h, dp, lambda ts: pl.BlockSpec((ts, D), lambda j, k: (k, 0)),
                  lambda ts: pl.BlockSpec((ts, NB_IN), lambda j, k: (k, j)), N_DEV, (D, NB_IN),
                  (N_DEV, D, NB_IN), pl.BlockSpec((None, D, NB_IN), lambda j, k: (j, 0, 0)), _store_plain, name, comm)


def _wgrad_up(h, du, name, comm=None):
    return _wgrad(h, du, lambda ts: pl.BlockSpec((ts, D), lambda j, k: (k, 0)),
                  lambda ts: pl.BlockSpec((None, ts, NB_UP), lambda j, k: (j, k, 0)), N_DEV, (D, NB_UP),
                  (N_DEV, D, NB_UP), pl.BlockSpec((None, D, NB_UP), lambda j, k: (j, 0, 0)), _store_plain, name, comm)


def _wgrad_down(act, dxo, name, comm=None):
    return _wgrad(act, dxo, lambda ts: pl.BlockSpec((None, ts, NB_UP), lambda j, k: (j, k, 0)),
                  lambda ts: pl.BlockSpec((ts, D), lambda j, k: (k, 0)), N_DEV // 2, (NB_UP, D),
                  (DFF, D), pl.BlockSpec((NB_UP, D), lambda j, k: (j, 0)), _store_plain, name, comm)


def _wgrad_o(merged, dxmid, name, comm=None):
    return _wgrad(merged, dxmid, lambda ts: pl.BlockSpec((ts, D), lambda j, k: (k, 0)),
                  lambda ts: pl.BlockSpec((ts, D), lambda j, k: (k, 0)), 1, (D, D),
                  (D, D), pl.BlockSpec((D, D), lambda j, k: (0, 0)), _store_plain, name, comm)


def _wgrad_branches(yabc, dbabc, name, comm=None):
    return _wgrad(yabc, dbabc, lambda ts: pl.BlockSpec((ts, WA), lambda j, k: (k, j)),
                  lambda ts: pl.BlockSpec((ts, D), lambda j, k: (k, j)), 3, (WA, D),
                  (N_DEV, 3, WA, NB_BR), pl.BlockSpec((N_DEV, None, WA, NB_BR), lambda j, k: (0, j, 0, 0)),
                  _store_lane_blocks, name, comm)


def _adamw_math(g, w, m, v):
    m = ADAM_B1 * m + (1.0 - ADAM_B1) * g
    v = ADAM_B2 * v + (1.0 - ADAM_B2) * (g * g)
    m_hat = m / (1.0 - ADAM_B1 ** ADAM_STEP)
    v_hat = v / (1.0 - ADAM_B2 ** ADAM_STEP)
    delta = -ADAM_LR * (m_hat / (jnp.sqrt(v_hat) + ADAM_EPS) + ADAM_WD * w)
    return delta, m, v


def _adamw_sum(parts, mid, w, m, v, layer, prev, tr, name, transposed=False):
    n_layers, r, c = w.shape[0], parts.shape[2], parts.shape[3]

    def body(p_ref, w_ref, m_ref, v_ref, *rest):
        g_ref, d_ref, mo_ref, vo_ref = rest[-4:]
        g = p_ref[0].astype(F32)
        for k in range(1, N_DEV):
            g = g + p_ref[k].astype(F32)
        if transposed:
            g = g.T
        g_ref[...] = g
        d_ref[...], mo_ref[...], vo_ref[...] = _adamw_math(g, w_ref[...], m_ref[...], v_ref[...])

    if transposed:
        blk = pl.BlockSpec((None, c, tr), lambda i: (layer, 0, i))
    else:
        blk = pl.BlockSpec((None, tr, c), lambda i: (layer, i, 0))
    extra = [] if prev is None else list(prev)
    return pl.pallas_call(
        body, name=name, grid=(r // tr,),
        in_specs=[pl.BlockSpec((N_DEV, None, tr, c), lambda i: (0, mid, i, 0)), blk, blk, blk]
        + [pl.BlockSpec(memory_space=pl.ANY)] * len(extra),
        out_specs=[blk] * 4, out_shape=[_sds(w.shape, F32)] * 4,
        input_output_aliases={4 + k: k for k in range(len(extra))},
        compiler_params=_params(),
    )(parts, w, m, v, *extra)


def _sum_parts(parts, name):
    _, r, c = parts.shape

    def body(p_ref, o_ref):
        g = p_ref[0]
        for k in range(1, N_DEV):
            g = g + p_ref[k]
        o_ref[...] = g

    return pl.pallas_call(body, name=name, out_shape=_sds((r, c), F32),
                          compiler_params=pltpu.CompilerParams(vmem_limit_bytes=VMEM_LIMIT))(parts)


def _adamw_small(g, w, m, v, name):
    def body(g_ref, w_ref, m_ref, v_ref, d_ref, mo_ref, vo_ref):
        d_ref[...], mo_ref[...], vo_ref[...] = _adamw_math(g_ref[...], w_ref[...], m_ref[...], v_ref[...])

    return pl.pallas_call(body, name=name, out_shape=[_sds(w.shape, F32)] * 3)(g, w, m, v)


HBM_SPEC = pl.BlockSpec(memory_space=pltpu.HBM)


def _position():
    return lax.axis_index("x"), lax.axis_index("y"), lax.axis_index("c")


def _device_index(chip, core):
    return 4 * chip[0] + 2 * chip[1] + core


def _gather(shards, layer):
    n = len(shards)
    per = 8

    def first_copies(ins, outs, send, recv):
        x, y, c = _position()
        me = 4 * x + 2 * y + c
        targets = [(x, y, 1 - c), (1 - x, y, c), (x, 1 - y, c), (1 - x, 1 - y, c)]
        remote = [pltpu.make_async_remote_copy(
            src_ref=ins[t].at[layer], dst_ref=outs[t].at[me], send_sem=send.at[per * t + k],
            recv_sem=recv.at[per * t + k], device_id=to, device_id_type=MESH)
            for t in range(n) for k, to in enumerate(targets)]
        local = [pltpu.make_async_copy(ins[t].at[layer], outs[t].at[me], send.at[per * t + 4]) for t in range(n)]
        return remote, local

    def passed_on(outs, send, recv):
        x, y, c = _position()
        chips = [(1 - x, y), (x, 1 - y), (1 - x, 1 - y)]
        return [pltpu.make_async_remote_copy(
            src_ref=outs[t].at[_device_index(chip, c)], dst_ref=outs[t].at[_device_index(chip, c)],
            send_sem=send.at[per * t + 5 + j], recv_sem=recv.at[per * t + 5 + j], device_id=(x, y, 1 - c),
            device_id_type=MESH)
            for t in range(n) for j, chip in enumerate(chips)]

    def start(ins, outs, send, recv):
        remote, local = first_copies(ins, outs, send, recv)
        for cp in local + remote:
            cp.start()

    def mid(ins, outs, send, recv):
        remote, local = first_copies(ins, outs, send, recv)
        for cp in remote:
            cp.wait()
        for cp in local:
            cp.wait()
        for cp in passed_on(outs, send, recv):
            cp.start()

    def finish(ins, outs, send, recv):
        for cp in passed_on(outs, send, recv):
            cp.wait()

    return _Comm(shards, [_sds((N_DEV,) + a.shape[1:], a.dtype) for a in shards], per * n, start, finish, mid)


def _run_comms(comms, name):
    def body():
        pass

    _, extra = _call(body, comms, None, None, None, (), name=name, in_specs=[], out_specs=[], out_shape=[])
    return extra


def _exchange(parts):
    n = len(parts)
    per = 8
    flips = [(0, 0, 1), (1, 0, 0), (1, 0, 1), (0, 1, 0), (0, 1, 1), (1, 1, 0), (1, 1, 1)]

    def copies(ins, outs, send, recv):
        x, y, c = _position()
        me = 4 * x + 2 * y + c
        remote = []
        for t in range(n):
            for k, (fx, fy, fc) in enumerate(flips):
                peer = ((1 - x if fx else x), (1 - y if fy else y), (1 - c if fc else c))
                remote.append(pltpu.make_async_remote_copy(
                    src_ref=ins[t].at[_device_index(peer[:2], peer[2])], dst_ref=outs[t].at[me],
                    send_sem=send.at[per * t + k], recv_sem=recv.at[per * t + k], device_id=peer, device_id_type=MESH))
        local = [pltpu.make_async_copy(ins[t].at[me], outs[t].at[me], send.at[per * t + 7]) for t in range(n)]
        return remote, local

    def start(ins, outs, send, recv):
        remote, local = copies(ins, outs, send, recv)
        for cp in local + remote:
            cp.start()

    def finish(ins, outs, send, recv):
        remote, local = copies(ins, outs, send, recv)
        for cp in remote:
            cp.wait()
        for cp in local:
            cp.wait()

    return _Comm(parts, [_sds(a.shape, a.dtype) for a in parts], per * n, start, finish)


def _rows128(a):
    return a.reshape(-1, 128)


def kernel(x, g_mix, w_in, w_pool, pool_scale, g_sgu, w_spatial, b_spatial, conv_c, w_branch_a, w_branch_b, w_branch_c, w_o, g_ffn, w_up, conv_ffn, conv_ffn_b, w_down, g_final, loss_target, m_g_mix, m_w_in, m_w_pool, m_pool_scale, m_g_sgu, m_w_spatial, m_b_spatial, m_conv_c, m_w_branch_a, m_w_branch_b, m_w_branch_c, m_w_o, m_g_ffn, m_w_up, m_conv_ffn, m_conv_ffn_b, m_w_down, m_g_final, v_g_mix, v_w_in, v_w_pool, v_pool_scale, v_g_sgu, v_w_spatial, v_b_spatial, v_conv_c, v_w_branch_a, v_w_branch_b, v_w_branch_c, v_w_o, v_g_ffn, v_w_up, v_conv_ffn, v_conv_ffn_b, v_w_down, v_g_final):
    s = x.shape[1]
    n_layers = g_mix.shape[0]
    x0 = x.reshape(s, D)
    target = loss_target.reshape(s, D)
    me = 4 * lax.axis_index("x") + 2 * lax.axis_index("y") + lax.axis_index("c")

    first_shards = [w_in.astype(MXU), conv_c]
    mix_shards = [w_branch_a.astype(MXU), w_branch_b.astype(MXU), w_branch_c.astype(MXU), w_o.astype(MXU),
                  w_down.astype(MXU)]
    up_shards = [w_up.astype(MXU), conv_ffn]
    (first_now,) = _run_comms([_gather(first_shards, 0)], "gather_first_0")
    mix_now = up_now = None
    wpool_b = w_pool.astype(MXU)
    bsp_t = jnp.swapaxes(b_spatial, 1, 2)
    convb_blk = conv_ffn_b.reshape(n_layers, N_DEV, NB_UP)

    saved = []
    weights = []
    xl = x0
    for l in range(n_layers):
        win8, convc8 = first_now
        convc_full = jnp.transpose(convc8, (1, 0, 2)).reshape(3, WA)
        more = l + 1 < n_layers
        mixer_args = (wpool_b[l], pool_scale[l:l + 1], g_sgu[l:l + 1], w_spatial[l], bsp_t[l], convc_full)
        if l == 0:
            (p, h), (mix_now,) = _rms_proj(xl, g_mix[l:l + 1], win8, f"in_proj_{l}", [_gather(mix_shards, l)])
            wa8, wb8, wc8, wo8, wd8 = mix_now
            (xmid, yabc, pacz, babc, merged), (up_now,) = _mixer_fwd(
                xl, p, *mixer_args, wa8, wb8, wc8, wo8, f"mixer_fwd_{l}", [_gather(up_shards, l)])
        else:
            wa8, wb8, wc8, wo8, wd8 = mix_now
            (xmid, yabc, pacz, babc, merged, p, h), (up_now,) = _mixer_fwd(
                xl, None, *mixer_args, wa8, wb8, wc8, wo8, f"mixer_fwd_{l}", [_gather(up_shards, l)],
                in_proj=(g_mix[l:l + 1], win8))
        wup8, convf8 = up_now
        weights.append((win8, wa8, wb8, wc8, wo8, wup8, wd8, convc_full, convf8))
        if more:
            (xout, h2, upre, up, act), (first_now, mix_now) = _ffn_block_fwd(
                xmid, g_ffn[l:l + 1], wup8, convf8, convb_blk[l], wd8, f"ffn_fwd_{l}",
                [_gather(first_shards, l + 1), _gather(mix_shards, l + 1)])
        else:
            (xout, h2, upre, up, act, dg_final, loss_local), _ = _ffn_block_fwd(
                xmid, g_ffn[l:l + 1], wup8, convf8, convb_blk[l], wd8, f"ffn_fwd_{l}",
                loss_head=(g_final.reshape(1, D), target))
        saved.append((xl, p, h, xmid, yabc, pacz, babc, merged, upre, h2, act, up))
        xl = xout

    dx = xl

    received = [dict() for _ in range(n_layers)]
    small = {("final", "g_final"): dg_final}
    small_sums = {}
    waiting = None

    def exchange_of(named):
        return [_exchange([a for _, a in named])]

    def land(layer, named, got):
        received[layer].update({k: a for (k, _), a in zip(named, got[0])})

    def gather_small(keys):
        packed = jnp.concatenate([_rows128(small[k]) for k in keys], axis=0)[None]
        return _gather([packed], 0)

    def sum_small(keys, gathered, name):
        summed = _sum_parts(gathered, name)
        row = 0
        for k in keys:
            n_rows = small[k].size // 128
            small_sums[k] = summed[row:row + n_rows].reshape(small[k].shape)
            row += n_rows

    for l in reversed(range(n_layers)):
        xin, p, h, xmid, yabc, pacz, babc, merged, upre, h2, act, up = saved[l]
        win8, wa8, wb8, wc8, wo8, wup8, wd8, convc_full, convf8 = weights[l]
        last = l == 0
        (dupre, dconvf, dxmid, dg_ffn), got = _ffn_block_bwd(
            dx, upre, up, convf8, wd8, wup8, xmid, g_ffn[l:l + 1], f"ffn_bwd_{l}",
            None if waiting is None else exchange_of(waiting[1]))
        if waiting is not None:
            land(waiting[0], waiting[1], got)
        small[(l, "conv_ffn")] = dconvf
        small[(l, "g_ffn")] = dg_ffn
        keys_a = [k for k in small if k not in small_sums]
        g_wdown, got = _wgrad_down(act, dx, f"wgrad_down_{l}", [gather_small(keys_a)] if last else None)
        if last:
            sum_small(keys_a, got[0][0], "sum_small_grads_a")
        down = [("w_down", g_wdown.reshape(N_DEV, ROWS_DN, D))]
        g_wup, got = _wgrad_up(h2, dupre, f"wgrad_up_{l}", exchange_of(down) if last else None)
        if last:
            land(l, down, got)
        upw = [("w_up", g_wup)]
        mixer_args = (dxmid, p, yabc, pacz, babc, wpool_b[l], pool_scale[l:l + 1], g_sgu[l:l + 1], w_spatial[l],
                      bsp_t[l], convc_full, wa8, wb8, wc8, wo8, f"mixer_bwd_{l}")
        if last:
            (dp, dbabc, dwp, dws, mixer_small, dbs), got = _mixer_bwd(*mixer_args, exchange_of(upw))
            land(l, upw, got)
        else:
            (dp, dbabc, dwp, dws, mixer_small, dbs, dx, dg_mix), got = _mixer_bwd(
                *mixer_args, exchange_of(down + upw), in_proj=(win8, xin, g_mix[l:l + 1]))
            land(l, down + upw, got)
        small.update({(l, "w_pool"): dwp, (l, "mixer_small"): mixer_small, (l, "w_spatial"): dws,
                      (l, "b_spatial"): dbs})
        g_wo, _ = _wgrad_o(merged, dxmid, f"wgrad_o_{l}")
        g_br, _ = _wgrad_branches(yabc, dbabc, f"wgrad_branches_{l}")
        mixer_w = [("branches", g_br), ("w_o", g_wo.reshape(N_DEV, ROWS_O, D))]
        keys_b = [k for k in small if k not in small_sums]
        g_win, got = _wgrad_in(h, dp, f"wgrad_in_{l}",
                               exchange_of(mixer_w) + ([gather_small(keys_b)] if last else []))
        land(l, mixer_w, got)
        inw = [("w_in", g_win)]
        if last:
            sum_small(keys_b, got[1][0], "sum_small_grads_b")
            (dx, dg_mix), got = _proj_bwd(dp, win8, xin, g_mix[l:l + 1], dxmid, f"in_proj_bwd_{l}", exchange_of(inw))
            land(l, inw, got)
        else:
            waiting = (l, inw)
        small[(l, "g_mix")] = dg_mix
    grad_x = dx.reshape(1, s, D)
    late_keys = [k for k in small if k not in small_sums]
    (gathered_late,) = _run_comms([gather_small(late_keys)], "gather_last_small_grads")[0]
    sum_small(late_keys, gathered_late, "sum_last_small_grads")

    def update_big(key, mid, w, m, v, tr, tag, transposed=False):
        swap = (lambda a: jnp.swapaxes(a, 1, 2)) if transposed else (lambda a: a)
        w, m, v = swap(w), swap(m), swap(v)
        outs = None
        for l in range(n_layers):
            parts = received[l][key]
            if parts.ndim == 3:
                parts = parts.reshape(N_DEV, 1, *parts.shape[1:])
            outs = _adamw_sum(parts, mid, w, m, v, l, outs, tr, f"adamw_{tag}_{l}", transposed)
        return [swap(o) for o in outs]

    up_in = update_big("w_in", 0, w_in, m_w_in, v_w_in, 256, "w_in")
    up_a = update_big("branches", 0, w_branch_a, m_w_branch_a, v_w_branch_a, WA, "w_branch_a")
    up_b = update_big("branches", 1, w_branch_b, m_w_branch_b, v_w_branch_b, WA, "w_branch_b")
    up_c = update_big("branches", 2, w_branch_c, m_w_branch_c, v_w_branch_c, WA, "w_branch_c")
    up_o = update_big("w_o", 0, w_o, m_w_o, v_w_o, ROWS_O, "w_o")
    up_up = update_big("w_up", 0, w_up, m_w_up, v_w_up, 256, "w_up", transposed=True)
    up_down = update_big("w_down", 0, w_down, m_w_down, v_w_down, ROWS_DN, "w_down")

    stack = lambda kind: jnp.stack([small_sums[(l, kind)] for l in range(n_layers)], axis=0)
    grad_g_mix = stack("g_mix")[:, 0, :]
    grad_w_pool = stack("w_pool")
    mixer_sums = stack("mixer_small")
    grad_pool_scale = mixer_sums[:, 0, :]
    grad_g_sgu = mixer_sums[:, 1, :]
    grad_conv_c = lax.dynamic_slice_in_dim(mixer_sums[:, 2:5, :], me * (WA // N_DEV), WA // N_DEV, axis=2)
    grad_w_spatial = stack("w_spatial")
    grad_b_spatial = stack("b_spatial")[:, 0:HEADS, :]
    grad_g_ffn = stack("g_ffn")[:, 0, :]
    conv_grads = stack("conv_ffn")
    grad_conv_ffn = lax.dynamic_index_in_dim(conv_grads, me, axis=1, keepdims=False)[:, 0:3, :]
    grad_conv_ffn_b = conv_grads[:, :, 3, :].reshape(n_layers, 2 * DFF)
    grad_g_final = small_sums[("final", "g_final")][0]

    def update_small(g, w, m, v, tag):
        shape2 = (-1, w.shape[-1])
        outs = _adamw_small(g.reshape(shape2), w.reshape(shape2), m.reshape(shape2), v.reshape(shape2), f"adamw_{tag}")
        return [g] + [o.reshape(w.shape) for o in outs]

    up = {
        "g_mix": update_small(grad_g_mix, g_mix, m_g_mix, v_g_mix, "g_mix"),
        "w_in": up_in,
        "w_pool": update_small(grad_w_pool, w_pool, m_w_pool, v_w_pool, "w_pool"),
        "pool_scale": update_small(grad_pool_scale, pool_scale, m_pool_scale, v_pool_scale, "pool_scale"),
        "g_sgu": update_small(grad_g_sgu, g_sgu, m_g_sgu, v_g_sgu, "g_sgu"),
        "w_spatial": update_small(grad_w_spatial, w_spatial, m_w_spatial, v_w_spatial, "w_spatial"),
        "b_spatial": update_small(grad_b_spatial, b_spatial, m_b_spatial, v_b_spatial, "b_spatial"),
        "conv_c": update_small(grad_conv_c, conv_c, m_conv_c, v_conv_c, "conv_c"),
        "w_branch_a": up_a,
        "w_branch_b": up_b,
        "w_branch_c": up_c,
        "w_o": up_o,
        "g_ffn": update_small(grad_g_ffn, g_ffn, m_g_ffn, v_g_ffn, "g_ffn"),
        "w_up": up_up,
        "conv_ffn": update_small(grad_conv_ffn, conv_ffn, m_conv_ffn, v_conv_ffn, "conv_ffn"),
        "conv_ffn_b": update_small(grad_conv_ffn_b, conv_ffn_b, m_conv_ffn_b, v_conv_ffn_b, "conv_ffn_b"),
        "w_down": up_down,
        "g_final": update_small(grad_g_final, g_final, m_g_final, v_g_final, "g_final"),
    }
    loss = lax.psum(loss_local[0, 0], AXES)
    order = list(up)
    return (loss, grad_x, *[up[k][0] for k in order], *[up[k][1] for k in order], *[up[k][2] for k in order],
            *[up[k][3] for k in order])
```

```python
import functools

import jax
import jax.numpy as jnp
from jax import lax
from jax.experimental import pallas as pl
from jax.experimental.pallas import tpu as pltpu

F32 = jnp.float32
BF16 = jnp.bfloat16
MXU = BF16
ACT = BF16
WIRE = BF16

N_DEV = 8
D = 1024
WA = 512
NCOL = 6144
DFF = 2816
NB_IN = NCOL // N_DEV
NB_UP = 2 * DFF // N_DEV
NB_BR = D // N_DEV
ROWS_O = D // N_DEV
ROWS_DN = DFF // N_DEV
CHUNK = 128
HEADS = 4
POOL_WINDOWS = (2, 4, 8, 16)
EPS = 1e-6
A0, UV0, CB0, CC0, CX0, GA0, GB0, GC0 = 0, 512, 1536, 2048, 2560, 3072, 4096, 5120

ADAM_LR = 0.001
ADAM_B1 = 0.9
ADAM_B2 = 0.999
ADAM_EPS = 1e-08
ADAM_WD = 0.01
ADAM_STEP = 10

TM = 256
TS_WGRAD = 4096
HALO_POOL = 16
HALO_CONV = 8
VMEM_LIMIT = 56 * 1024 * 1024
MESH = pl.DeviceIdType.MESH
AXES = ("x", "y", "c")


def _sds(shape, dtype):
    return jax.ShapeDtypeStruct(tuple(shape), dtype)


def _params(n_grid=1):
    return pltpu.CompilerParams(dimension_semantics=("arbitrary",) * n_grid, vmem_limit_bytes=VMEM_LIMIT)


def _const(block, index):
    return pl.BlockSpec(block, lambda *_: index, pipeline_mode=pl.Buffered(1))


def _dot(a, b):
    return jnp.dot(a, b, preferred_element_type=F32)


def _dot_nt(a, b):
    return lax.dot_general(a, b, (((1,), (1,)), ((), ())), preferred_element_type=F32)


def _dot_tn(a, b):
    return lax.dot_general(a, b, (((0,), (0,)), ((), ())), preferred_element_type=F32)


def _sigmoid(v):
    return 0.5 * jnp.tanh(0.5 * v) + 0.5


def _shift_down(v, k):
    return pltpu.roll(v, k, axis=0)


def _shift_up(v, k):
    return pltpu.roll(v, v.shape[0] - k, axis=0)


def _colsum(v):
    return jnp.sum(v, axis=0, keepdims=True)


def _lane_cat(ref):
    return jnp.concatenate([ref[d] for d in range(N_DEV)], axis=1)


class _Comm:
    def __init__(self, operands, out_shapes, n_sems, start, finish, mid=None):
        self.operands = list(operands)
        self.out_shapes = list(out_shapes)
        self.n_sems = n_sems
        self.start = start
        self.mid = mid
        self.finish = finish


def _call(body, comms, is_first, is_mid, is_last, operands, *, in_specs, out_specs, out_shape, scratch_shapes=(), **kw):
    n_in, n_out, n_scr = len(in_specs), len(out_specs), len(scratch_shapes)
    comms = [c for c in (comms or []) if c is not None]
    if not comms:
        res = pl.pallas_call(body, in_specs=in_specs, out_specs=out_specs, out_shape=out_shape,
                             scratch_shapes=list(scratch_shapes), **kw)(*operands)
        return res, []
    nci = [len(c.operands) for c in comms]
    nco = [len(c.out_shapes) for c in comms]

    def split(refs, sizes):
        parts = []
        for n in sizes:
            parts.append(refs[:n])
            refs = refs[n:]
        return parts, refs

    def carrier(*refs):
        ins, refs = refs[:n_in], refs[n_in:]
        cins, refs = split(refs, nci)
        outs, refs = refs[:n_out], refs[n_out:]
        couts, refs = split(refs, nco)
        scr, sems = refs[:n_scr], refs[n_scr:]

        def run(step):
            for k, c in enumerate(comms):
                if getattr(c, step) is not None:
                    getattr(c, step)(cins[k], couts[k], sems[2 * k], sems[2 * k + 1])

        def at(mark, step):
            if mark is None:
                run(step)
            else:
                pl.when(mark())(lambda: run(step))

        at(is_first, "start")
        body(*ins, *outs, *scr)
        at(is_mid, "mid")
        at(is_last, "finish")

    res = pl.pallas_call(
        carrier, in_specs=list(in_specs) + [HBM_SPEC] * sum(nci), out_specs=list(out_specs) + [HBM_SPEC] * sum(nco),
        out_shape=list(out_shape) + [s for c in comms for s in c.out_shapes],
        scratch_shapes=list(scratch_shapes) + [pltpu.SemaphoreType.DMA((c.n_sems,)) for c in comms for _ in range(2)],
        **kw,
    )(*operands, *[a for c in comms for a in c.operands])
    extra, _ = split(res[n_out:], nco)
    return res[:n_out], extra


def _grid_marks(nt):
    return (lambda: pl.program_id(0) == 0), (lambda: pl.program_id(0) == (3 * nt) // 4), (lambda: pl.program_id(0) == nt - 1)


def _rms_proj(x, g, w_all, name, comms=None):
    s = x.shape[0]
    nb = w_all.shape[-1]
    nt = s // TM

    def body(x_ref, g_ref, w_ref, p_ref, h_ref):
        xf = x_ref[...]
        r = lax.rsqrt(jnp.mean(xf * xf, axis=-1, keepdims=True) + EPS)
        h = (xf * r * g_ref[...]).astype(MXU)
        h_ref[...] = h
        for j in range(N_DEV):
            p_ref[:, j * nb:(j + 1) * nb] = _dot(h, w_ref[j]).astype(p_ref.dtype)

    row = lambda n: pl.BlockSpec((TM, n), lambda i: (i, 0))
    return _call(
        body, comms, *_grid_marks(nt), (x, g, w_all),
        name=name, grid=(nt,),
        in_specs=[row(D), _const((1, D), (0, 0)), _const((N_DEV, D, nb), (0, 0, 0))],
        out_specs=[row(N_DEV * nb), row(D)],
        out_shape=[_sds((s, N_DEV * nb), ACT), _sds((s, D), MXU)],
        compiler_params=_params(),
    )


def _tril_mask():
    r = lax.broadcasted_iota(jnp.int32, (CHUNK, CHUNK), 0)
    c = lax.broadcasted_iota(jnp.int32, (CHUNK, CHUNK), 1)
    return r >= c


def _gelu_parts(v):
    c0 = 0.7978845608028654
    th = jnp.tanh(c0 * (v + 0.044715 * (v * v * v)))
    cdf = 0.5 * (1.0 + th)
    dgelu = cdf + v * (0.5 * c0) * (1.0 - th * th) * (1.0 + 3.0 * 0.044715 * (v * v))
    return v * cdf, dgelu


def _mixer_fwd(x, p, wpool, pscale, gsgu, wsp, bsp_t, convc, wa_all, wb_all, wc_all, wo_all, name, comms=None,
               in_proj=None):
    s = x.shape[0]
    nt = s // TM
    fused = in_proj is not None

    def body(*refs):
        if fused:
            (x_ref, g_ref, win_ref, wpool_ref, ps_ref, gs_ref, wsp_ref, bsp_ref, cc_ref, wa_ref, wb_ref, wc_ref, wo_ref,
             xmid_ref, y_ref, pz_ref, b_ref, m_ref, p_ref, h_ref, carry_a, carry_z) = refs
        else:
            (x_ref, p_ref, wpool_ref, ps_ref, gs_ref, wsp_ref, bsp_ref, cc_ref, wa_ref, wb_ref, wc_ref, wo_ref,
             xmid_ref, y_ref, pz_ref, b_ref, m_ref, carry_a, carry_z) = refs
        i = pl.program_id(0)

        @pl.when(i == 0)
        def _():
            carry_a[...] = jnp.zeros_like(carry_a)
            carry_z[...] = jnp.zeros_like(carry_z)

        def pf(lo, n):
            return p_ref[:, lo:lo + n].astype(F32)

        def project(blocks):
            if fused:
                for j in blocks:
                    p_ref[:, j * NB_IN:(j + 1) * NB_IN] = _dot(h_ref[...], win_ref[j]).astype(ACT)

        if fused:
            xf = x_ref[...]
            r = lax.rsqrt(jnp.mean(xf * xf, axis=-1, keepdims=True) + EPS)
            h_ref[...] = (xf * r * g_ref[...]).astype(MXU)
        project((0, 1, 2))

        a = pf(A0, WA)
        ext = jnp.concatenate([carry_a[...], a], axis=0)
        carry_a[...] = a[TM - HALO_POOL:, :]
        t_pos = (i * TM + lax.broadcasted_iota(jnp.int32, (TM, 1), 0)).astype(F32)
        for g, win in enumerate(POOL_WINDOWS):
            cols = slice(g * CHUNK, (g + 1) * CHUNK)
            acc = ext[:, cols]
            k = 1
            while k < win:
                acc = acc + _shift_down(acc, k)
                k *= 2
            cnt = jnp.minimum(t_pos + 1.0, float(win))
            pa_g = (acc[HALO_POOL:, :] / cnt - a[:, cols]).astype(MXU)
            pz_ref[:, cols] = pa_g
            y_ref[:, cols] = (_dot(pa_g, wpool_ref[g]) * ps_ref[:, cols]).astype(ACT)

        project((3,))
        uvg, _ = _gelu_parts(pf(UV0, 2 * WA))
        u = uvg[:, :WA]
        v = uvg[:, WA:]
        rv = lax.rsqrt(jnp.mean(v * v, axis=-1, keepdims=True) + EPS)
        vn = (v * rv * gs_ref[...]).astype(MXU)
        mask = _tril_mask()
        for g in range(HEADS):
            cols = slice(g * CHUNK, (g + 1) * CHUNK)
            wt = jnp.where(mask, wsp_ref[g], 0.0).astype(MXU)
            bcol = bsp_ref[:, g:g + 1]
            for c in range(TM // CHUNK):
                rows = slice(c * CHUNK, (c + 1) * CHUNK)
                sv = _dot(wt, vn[rows, cols]) + bcol
                y_ref[rows, WA + g * CHUNK:WA + (g + 1) * CHUNK] = (u[rows, cols] * sv).astype(ACT)

        project((4, 5))
        z = pf(CC0, WA) * pf(CX0, WA)
        extz = jnp.concatenate([carry_z[...], z], axis=0)
        carry_z[...] = z[TM - HALO_CONV:, :]
        cz = (cc_ref[0:1, :] * _shift_down(extz, 2)[HALO_CONV:, :]
              + cc_ref[1:2, :] * _shift_down(extz, 1)[HALO_CONV:, :] + cc_ref[2:3, :] * z)
        pz_ref[:, WA:2 * WA] = cz.astype(ACT)
        y_ref[:, 2 * WA:3 * WA] = (pf(CB0, WA) * cz).astype(ACT)

        project((6, 7))
        merged = jnp.zeros((TM, D), F32)
        for k, (w_ref, glo) in enumerate(((wa_ref, GA0), (wb_ref, GB0), (wc_ref, GC0))):
            br = _dot(y_ref[:, k * WA:(k + 1) * WA], _lane_cat(w_ref))
            b_ref[:, k * D:(k + 1) * D] = br.astype(ACT)
            merged = merged + _sigmoid(pf(glo, D)) * br
        mb = merged.astype(MXU)
        m_ref[...] = mb
        xmid_ref[...] = x_ref[...] + _dot(mb, wo_ref[...].reshape(D, D))

    row = lambda n: pl.BlockSpec((TM, n), lambda i: (i, 0))
    br_spec = _const((N_DEV, WA, NB_BR), (0, 0, 0))
    if fused:
        lead_specs = [row(D), _const((1, D), (0, 0)), _const((N_DEV, D, NB_IN), (0, 0, 0))]
        lead = (x,) + tuple(in_proj)
    else:
        lead_specs = [row(D), row(NCOL)]
        lead = (x, p)
    more_specs = [row(NCOL), row(D)] if fused else []
    more_shapes = [_sds((s, NCOL), ACT), _sds((s, D), MXU)] if fused else []
    return _call(
        body, comms, *_grid_marks(nt), lead + (wpool, pscale, gsgu, wsp, bsp_t, convc, wa_all, wb_all, wc_all, wo_all),
        name=name, grid=(nt,),
        in_specs=lead_specs + [_const((HEADS, CHUNK, CHUNK), (0, 0, 0)), _const((1, WA), (0, 0)),
                               _const((1, WA), (0, 0)), _const((HEADS, CHUNK, CHUNK), (0, 0, 0)),
                               _const((CHUNK, HEADS), (0, 0)), _const((3, WA), (0, 0)), br_spec, br_spec, br_spec,
                               _const((N_DEV, ROWS_O, D), (0, 0, 0))],
        out_specs=[row(D), row(3 * WA), row(2 * WA), row(3 * D), row(D)] + more_specs,
        out_shape=[_sds((s, D), F32), _sds((s, 3 * WA), ACT), _sds((s, 2 * WA), ACT), _sds((s, 3 * D), ACT),
                   _sds((s, D), MXU)] + more_shapes,
        scratch_shapes=[pltpu.VMEM((HALO_POOL, WA), F32), pltpu.VMEM((HALO_CONV, WA), F32)],
        compiler_params=_params(),
    )


def _conv_up(ext, cur, w_ref, j, b_row):
    return (w_ref[j, 0:1, :] * _shift_down(ext, 2)[HALO_CONV:, :] + w_ref[j, 1:2, :] * _shift_down(ext, 1)[HALO_CONV:, :]
            + w_ref[j, 2:3, :] * cur + b_row)


def _ffn_block_fwd(xmid, g, wup_all, convf_all, convb, wd_all, name, comms=None, loss_head=None):
    s = xmid.shape[0]
    nt = s // TM
    half = N_DEV // 2
    with_loss = loss_head is not None

    def body(*refs):
        (x_ref, g_ref, wup_ref, cw_ref, cb_ref, wd_ref), refs = refs[:6], refs[6:]
        if with_loss:
            (gf_ref, t_ref), refs = refs[:2], refs[2:]
        (xo_ref, h_ref, u_ref, up_ref, act_ref), refs = refs[:5], refs[5:]
        if with_loss:
            (dgf_ref, loss_ref, dxb_ref), refs = refs[:3], refs[3:]
        (carry,) = refs
        i = pl.program_id(0)

        @pl.when(i == 0)
        def _():
            carry[...] = jnp.zeros_like(carry)
            if with_loss:
                dgf_ref[...] = jnp.zeros_like(dgf_ref)
                loss_ref[...] = jnp.zeros_like(loss_ref)

        xf = x_ref[...]
        r = lax.rsqrt(jnp.mean(xf * xf, axis=-1, keepdims=True) + EPS)
        h = (xf * r * g_ref[...]).astype(MXU)
        h_ref[...] = h

        def project(j):
            return _dot(h, wup_ref[j]).astype(ACT)

        def conv(j, pre):
            u_ref[j] = pre
            cur = pre.astype(F32)
            ext = jnp.concatenate([carry[j], cur], axis=0)
            carry[j] = cur[TM - HALO_CONV:, :]
            up = _conv_up(ext, cur, cw_ref, j, cb_ref[j:j + 1, :])
            up_ref[j] = up.astype(ACT)
            return up

        order = [j + k * half for j in range(half) for k in range(2)]
        acc = xf
        ahead = 2
        pres = {n: project(order[n]) for n in range(ahead)}
        ups = {}
        for n, j in enumerate(order):
            if n + ahead < N_DEV:
                pres[n + ahead] = project(order[n + ahead])
            ups[j] = conv(j, pres.pop(n))
            if j >= half:
                gate, val = ups.pop(j - half), ups.pop(j)
                act = (gate * _sigmoid(gate) * val).astype(MXU)
                act_ref[j - half] = act
                wd = jnp.concatenate([wd_ref[2 * (j - half)], wd_ref[2 * (j - half) + 1]], axis=0)
                acc = acc + _dot(act, wd)
        if not with_loss:
            xo_ref[...] = acc
        else:
            ro = lax.rsqrt(jnp.mean(acc * acc, axis=-1, keepdims=True) + EPS)
            xn = acc * ro
            err = xn * gf_ref[...] - t_ref[...]
            loss_ref[...] += 0.5 * jnp.sum(jnp.mean(err * err, axis=-1, keepdims=True), axis=0, keepdims=True)
            dy = err * (1.0 / D)
            dgf_ref[0:1, :] += _colsum(dy * xn)
            dyg = dy * gf_ref[...]
            dxo = ro * (dyg - xn * jnp.mean(dyg * xn, axis=-1, keepdims=True))
            xo_ref[...] = dxo
            dxb_ref[...] = dxo.astype(MXU)

    row = pl.BlockSpec((TM, D), lambda i: (i, 0))
    blocks = pl.BlockSpec((N_DEV, TM, NB_UP), lambda i: (0, i, 0))
    return _call(
        body, comms, *_grid_marks(nt), (xmid, g, wup_all, convf_all, convb, wd_all) + (tuple(loss_head) if with_loss else ()),
        name=name, grid=(nt,),
        in_specs=[row, _const((1, D), (0, 0)), _const((N_DEV, D, NB_UP), (0, 0, 0)),
                  _const((N_DEV, 3, NB_UP), (0, 0, 0)), _const((N_DEV, NB_UP), (0, 0)),
                  _const((N_DEV, ROWS_DN, D), (0, 0, 0))] + ([_const((1, D), (0, 0)), row] if with_loss else []),
        out_specs=[row, row, blocks, blocks, pl.BlockSpec((half, TM, NB_UP), lambda i: (0, i, 0))]
        + ([pl.BlockSpec((8, D), lambda i: (0, 0)), pl.BlockSpec((1, 1), lambda i: (0, 0)), row] if with_loss else []),
        out_shape=[_sds((s, D), F32), _sds((s, D), MXU), _sds((N_DEV, s, NB_UP), ACT), _sds((N_DEV, s, NB_UP), ACT),
                   _sds((half, s, NB_UP), MXU)]
        + ([_sds((8, D), F32), _sds((1, 1), F32), _sds((s, D), MXU)] if with_loss else []),
        scratch_shapes=[pltpu.VMEM((N_DEV, HALO_CONV, NB_UP), F32)],
        compiler_params=_params(),
    )


def _ffn_block_bwd(dxo, upre, up, convf_all, wd_all, wup_all, xmid, g, name, comms=None):
    s = dxo.shape[0]
    nt = s // TM
    half = N_DEV // 2

    def body(dx_ref, u_ref, up_ref, cw_ref, wd_ref, wup_ref, x_ref, g_ref, du_ref, dc_ref, dxm_ref, dg_ref, dxmb_ref,
             carry):
        step = pl.program_id(0)

        @pl.when(step == 0)
        def _():
            carry[...] = jnp.zeros_like(carry)
            dc_ref[...] = jnp.zeros_like(dc_ref)
            dg_ref[...] = jnp.zeros_like(dg_ref)

        dxo_t = dx_ref[...]
        dxb = dxo_t.astype(MXU)

        def adjoint(j, d_up):
            cur = u_ref[j].astype(F32)
            ext = jnp.concatenate([d_up, carry[j]], axis=0)
            carry[j] = d_up[:HALO_CONV, :]
            up1 = _shift_up(ext, 1)[:TM, :]
            up2 = _shift_up(ext, 2)[:TM, :]
            du = (cw_ref[j, 2:3, :] * d_up + cw_ref[j, 1:2, :] * up1 + cw_ref[j, 0:1, :] * up2).astype(du_ref.dtype)
            du_ref[j] = du
            dc_ref[j, 0:1, :] += _colsum(cur * up2)
            dc_ref[j, 1:2, :] += _colsum(cur * up1)
            dc_ref[j, 2:3, :] += _colsum(cur * d_up)
            dc_ref[j, 3:4, :] += _colsum(d_up)
            return _dot_nt(du, wup_ref[j])

        def d_act(j):
            return _dot_nt(dxb, jnp.concatenate([wd_ref[2 * j], wd_ref[2 * j + 1]], axis=0))

        dh = jnp.zeros((TM, D), F32)
        dact = d_act(0)
        for j in range(half):
            nxt = d_act(j + 1) if j + 1 < half else None
            gate = up_ref[j].astype(F32)
            val = up_ref[j + half].astype(F32)
            sg = _sigmoid(gate)
            dh = dh + adjoint(j, dact * val * sg * (1.0 + gate * (1.0 - sg)))
            dh = dh + adjoint(j + half, dact * gate * sg)
            dact = nxt

        xf = x_ref[...]
        r = lax.rsqrt(jnp.mean(xf * xf, axis=-1, keepdims=True) + EPS)
        xn = xf * r
        dg_ref[0:1, :] += _colsum(dh * xn)
        dhg = dh * g_ref[...]
        dxm = dxo_t + r * (dhg - xn * jnp.mean(dhg * xn, axis=-1, keepdims=True))
        dxm_ref[...] = dxm
        dxmb_ref[...] = dxm.astype(MXU)

    blocks = pl.BlockSpec((N_DEV, TM, NB_UP), lambda i: (0, nt - 1 - i, 0))
    row = pl.BlockSpec((TM, D), lambda i: (nt - 1 - i, 0))
    return _call(
        body, comms, *_grid_marks(nt), (dxo, upre, up, convf_all, wd_all, wup_all, xmid, g),
        name=name, grid=(nt,),
        in_specs=[row, blocks, blocks, _const((N_DEV, 3, NB_UP), (0, 0, 0)), _const((N_DEV, ROWS_DN, D), (0, 0, 0)),
                  _const((N_DEV, D, NB_UP), (0, 0, 0)), row, _const((1, D), (0, 0))],
        out_specs=[blocks, pl.BlockSpec((N_DEV, 8, NB_UP), lambda i: (0, 0, 0)), row,
                   pl.BlockSpec((8, D), lambda i: (0, 0)), row],
        out_shape=[_sds((N_DEV, s, NB_UP), MXU), _sds((N_DEV, 8, NB_UP), F32), _sds((s, D), F32), _sds((8, D), F32),
                   _sds((s, D), MXU)],
        scratch_shapes=[pltpu.VMEM((N_DEV, HALO_CONV, NB_UP), F32)],
        compiler_params=_params(),
    )


def _proj_bwd(dy, w_all, x, g, dres, name, comms=None):
    s = x.shape[0]
    nb = w_all.shape[-1]
    nt = s // TM

    def body(dy_ref, w_ref, x_ref, g_ref, dres_ref, dx_ref, dg_ref):
        i = pl.program_id(0)

        @pl.when(i == 0)
        def _():
            dg_ref[...] = jnp.zeros_like(dg_ref)

        dh = jnp.zeros((TM, D), F32)
        for j in range(N_DEV):
            dh = dh + _dot_nt(dy_ref[:, j * nb:(j + 1) * nb], w_ref[j])
        xf = x_ref[...]
        r = lax.rsqrt(jnp.mean(xf * xf, axis=-1, keepdims=True) + EPS)
        xn = xf * r
        dg_ref[0:1, :] += _colsum(dh * xn)
        dhg = dh * g_ref[...]
        dx_ref[...] = dres_ref[...] + r * (dhg - xn * jnp.mean(dhg * xn, axis=-1, keepdims=True))

    dy_spec = pl.BlockSpec((TM, N_DEV * nb), lambda i: (i, 0))
    row = pl.BlockSpec((TM, D), lambda i: (i, 0))
    return _call(
        body, comms, *_grid_marks(nt), (dy, w_all, x, g, dres),
        name=name, grid=(nt,),
        in_specs=[dy_spec, _const((N_DEV, D, nb), (0, 0, 0)), row, _const((1, D), (0, 0)), row],
        out_specs=[row, pl.BlockSpec((8, D), lambda i: (0, 0))],
        out_shape=[_sds((s, D), F32), _sds((8, D), F32)],
        compiler_params=_params(),
    )


def _mixer_bwd(dxmid, p, yabc, pacz, babc, wpool, pscale, gsgu, wsp, bsp_t, convc, wa_all, wb_all, wc_all, wo_all,
               name, comms=None, in_proj=None):
    s = dxmid.shape[0]
    nt = s // TM
    fused = in_proj is not None

    def body(*refs):
        (dx_ref, p_ref, y_ref, pz_ref, b_ref, wpool_ref, ps_ref, gs_ref, wsp_ref, bsp_ref, cc_ref,
         wa_ref, wb_ref, wc_ref, wo_ref) = refs[:15]
        refs = refs[15:]
        if fused:
            (win_ref, x_ref, g_ref), refs = refs[:3], refs[3:]
        (dp_ref, db_ref, dwp_ref, dws_ref, small_ref, dbs_ref), refs = refs[:6], refs[6:]
        if fused:
            (dxin_ref, dg_ref, dxinb_ref), refs = refs[:3], refs[3:]
        carry_pa, carry_cz, dbs_acc, du_s, dvn_s = refs
        step = pl.program_id(0)
        tile = nt - 1 - step

        @pl.when(step == 0)
        def _():
            for ref in (carry_pa, carry_cz, dbs_acc, dwp_ref, dws_ref, small_ref, dbs_ref) + ((dg_ref,) if fused else ()):
                ref[...] = jnp.zeros_like(ref)

        def pf(lo, n):
            return p_ref[:, lo:lo + n].astype(F32)

        def back_project(blocks):
            return sum(_dot_nt(dp_ref[:, j * NB_IN:(j + 1) * NB_IN], win_ref[j]) for j in blocks)

        dxm = dx_ref[...]
        dm = _dot_nt(dxm.astype(MXU), wo_ref[...].reshape(D, D))

        def through_gate(k, glo, w_ref):
            sg = _sigmoid(pf(glo, D))
            br = b_ref[:, k * D:(k + 1) * D].astype(F32)
            dp_ref[:, glo:glo + D] = (dm * br * sg * (1.0 - sg)).astype(dp_ref.dtype)
            dbr = (dm * sg).astype(MXU)
            db_ref[:, k * D:(k + 1) * D] = dbr
            return _dot_nt(dbr, _lane_cat(w_ref))

        dya = through_gate(0, GA0, wa_ref)
        dyb = through_gate(1, GB0, wb_ref)
        dyc = through_gate(2, GC0, wc_ref)
        if fused:
            dh = back_project((4, 5, 6, 7))

        t_pos = (tile * TM + lax.broadcasted_iota(jnp.int32, (TM, 1), 0)).astype(F32)
        for g, win in enumerate(POOL_WINDOWS):
            cols = slice(g * CHUNK, (g + 1) * CHUNK)
            pa_g = pz_ref[:, cols]
            q = _dot(pa_g, wpool_ref[g])
            dya_g = dya[:, cols]
            small_ref[0:1, cols] += _colsum(dya_g * q)
            dq = (dya_g * ps_ref[:, cols]).astype(MXU)
            dpa_g = _dot_nt(dq, wpool_ref[g])
            dwp_ref[g] += _dot_tn(pa_g, dq)
            dpw = dpa_g / jnp.minimum(t_pos + 1.0, float(win))
            acc = jnp.concatenate([dpw, carry_pa[:, cols]], axis=0)
            carry_pa[:, cols] = dpw[:HALO_POOL, :]
            k = 1
            while k < win:
                acc = acc + _shift_up(acc, k)
                k *= 2
            dp_ref[:, cols] = (acc[:TM, :] - dpa_g).astype(dp_ref.dtype)

        uvp = pf(UV0, 2 * WA)
        uvg, dgelu = _gelu_parts(uvp)
        u = uvg[:, :WA]
        v = uvg[:, WA:]
        rv = lax.rsqrt(jnp.mean(v * v, axis=-1, keepdims=True) + EPS)
        vh = v * rv
        vn = (vh * gs_ref[...]).astype(MXU)
        mask = _tril_mask()
        for g in range(HEADS):
            cols = slice(g * CHUNK, (g + 1) * CHUNK)
            wt32 = jnp.where(mask, wsp_ref[g], 0.0)
            wt = wt32.astype(MXU)
            wt_t = wt32.T.astype(MXU)
            bcol = bsp_ref[:, g:g + 1]
            for c in range(TM // CHUNK):
                rows = slice(c * CHUNK, (c + 1) * CHUNK)
                vn_cg = vn[rows, cols]
                sv = _dot(wt, vn_cg) + bcol
                dyb_cg = dyb[rows, cols]
                du_s[rows, cols] = dyb_cg * sv
                dsv = dyb_cg * u[rows, cols]
                dbs_acc[g] += dsv
                dsv_b = dsv.astype(MXU)
                dws_ref[g] += _dot_nt(dsv_b, vn_cg)
                dvn_s[rows, cols] = _dot(wt_t, dsv_b)
        dvn = dvn_s[...]
        small_ref[1:2, :] += _colsum(dvn * vh)
        dvg = dvn * gs_ref[...]
        dv = rv * (dvg - vh * jnp.mean(dvg * vh, axis=-1, keepdims=True))
        dp_ref[:, UV0:UV0 + WA] = (du_s[...] * dgelu[:, :WA]).astype(dp_ref.dtype)
        dp_ref[:, UV0 + WA:UV0 + 2 * WA] = (dv * dgelu[:, WA:]).astype(dp_ref.dtype)
        if fused:
            dh = dh + back_project((0, 1))

        cb = pf(CB0, WA)
        cc = pf(CC0, WA)
        cx = pf(CX0, WA)
        z = cc * cx
        dp_ref[:, CB0:CB0 + WA] = (dyc * pz_ref[:, WA:2 * WA].astype(F32)).astype(dp_ref.dtype)
        dcz = dyc * cb
        extz = jnp.concatenate([dcz, carry_cz[...]], axis=0)
        carry_cz[...] = dcz[:HALO_CONV, :]
        up1 = _shift_up(extz, 1)[:TM, :]
        up2 = _shift_up(extz, 2)[:TM, :]
        dz = cc_ref[2:3, :] * dcz + cc_ref[1:2, :] * up1 + cc_ref[0:1, :] * up2
        small_ref[2:3, :] += _colsum(z * up2)
        small_ref[3:4, :] += _colsum(z * up1)
        small_ref[4:5, :] += _colsum(z * dcz)
        dp_ref[:, CC0:CC0 + WA] = (dz * cx).astype(dp_ref.dtype)
        dp_ref[:, CX0:CX0 + WA] = (dz * cc).astype(dp_ref.dtype)

        if fused:
            dh = dh + back_project((2, 3))
            xf = x_ref[...]
            r = lax.rsqrt(jnp.mean(xf * xf, axis=-1, keepdims=True) + EPS)
            xn = xf * r
            dg_ref[0:1, :] += _colsum(dh * xn)
            dhg = dh * g_ref[...]
            dxin = dxm + r * (dhg - xn * jnp.mean(dhg * xn, axis=-1, keepdims=True))
            dxin_ref[...] = dxin
            dxinb_ref[...] = dxin.astype(MXU)

        @pl.when(step == nt - 1)
        def _():
            ones = jnp.ones((8, CHUNK), F32)
            for g in range(HEADS):
                dws_ref[g] = jnp.where(mask, dws_ref[g], 0.0)
                row = lax.dot_general(ones, dbs_acc[g], (((1,), (1,)), ((), ())), preferred_element_type=F32,
                                      precision=lax.Precision.HIGHEST)
                dbs_ref[g:g + 1, :] = row[0:1, :]

    row = lambda n: pl.BlockSpec((TM, n), lambda i: (nt - 1 - i, 0))
    br_spec = _const((N_DEV, WA, NB_BR), (0, 0, 0))
    acc_spec = lambda shape: pl.BlockSpec(shape, lambda i: (0,) * len(shape))
    more_in = tuple(in_proj) if fused else ()
    more_in_specs = [_const((N_DEV, D, NB_IN), (0, 0, 0)), row(D), _const((1, D), (0, 0))] if fused else []
    more_out_specs = [row(D), acc_spec((8, D)), row(D)] if fused else []
    more_out_shapes = [_sds((s, D), F32), _sds((8, D), F32), _sds((s, D), MXU)] if fused else []
    return _call(
        body, comms, *_grid_marks(nt),
        (dxmid, p, yabc, pacz, babc, wpool, pscale, gsgu, wsp, bsp_t, convc, wa_all, wb_all, wc_all, wo_all) + more_in,
        name=name, grid=(nt,),
        in_specs=[row(D), row(NCOL), row(3 * WA), row(2 * WA), row(3 * D),
                  _const((HEADS, CHUNK, CHUNK), (0, 0, 0)), _const((1, WA), (0, 0)), _const((1, WA), (0, 0)),
                  _const((HEADS, CHUNK, CHUNK), (0, 0, 0)), _const((CHUNK, HEADS), (0, 0)), _const((3, WA), (0, 0)),
                  br_spec, br_spec, br_spec, _const((N_DEV, ROWS_O, D), (0, 0, 0))] + more_in_specs,
        out_specs=[row(NCOL), row(3 * D), acc_spec((HEADS, CHUNK, CHUNK)), acc_spec((HEADS, CHUNK, CHUNK)),
                   acc_spec((8, WA)), acc_spec((8, CHUNK))] + more_out_specs,
        out_shape=[_sds((s, NCOL), MXU), _sds((s, 3 * D), MXU), _sds((HEADS, CHUNK, CHUNK), F32),
                   _sds((HEADS, CHUNK, CHUNK), F32), _sds((8, WA), F32), _sds((8, CHUNK), F32)] + more_out_shapes,
        scratch_shapes=[pltpu.VMEM((HALO_POOL, WA), F32), pltpu.VMEM((HALO_CONV, WA), F32),
                        pltpu.VMEM((HEADS, CHUNK, CHUNK), F32), pltpu.VMEM((TM, WA), F32), pltpu.VMEM((TM, WA), F32)],
        compiler_params=_params(),
    )


def _wgrad(a, b, a_spec, b_spec, n_out, acc_shape, out_shape, out_spec, store, name, comms=None):
    s = a.shape[-2]
    ts = min(TS_WGRAD, s)
    n_steps = s // ts

    def body(a_ref, b_ref, o_ref, acc_ref):
        k = pl.program_id(1)

        @pl.when(k == 0)
        def _():
            acc_ref[...] = jnp.zeros_like(acc_ref)

        acc_ref[...] += _dot_tn(a_ref[...], b_ref[...])

        @pl.when(k == n_steps - 1)
        def _():
            store(o_ref, acc_ref)

    (res,), extra = _call(
        body, comms, lambda: (pl.program_id(0) == 0) & (pl.program_id(1) == 0),
        lambda: (pl.program_id(0) == (3 * n_out) // 4) & (pl.program_id(1) == 0),
        lambda: (pl.program_id(0) == n_out - 1) & (pl.program_id(1) == n_steps - 1), (a, b),
        name=name, grid=(n_out, n_steps),
        in_specs=[a_spec(ts), b_spec(ts)], out_specs=[out_spec], out_shape=[_sds(out_shape, WIRE)],
        scratch_shapes=[pltpu.VMEM(acc_shape, F32)],
        compiler_params=_params(2),
    )
    return res, extra


def _store_plain(o_ref, acc_ref):
    o_ref[...] = acc_ref[...].astype(o_ref.dtype)


def _store_lane_blocks(o_ref, acc_ref):
    for d in range(N_DEV):
        o_ref[d] = acc_ref[:, d * NB_BR:(d + 1) * NB_BR].astype(o_ref.dtype)


def _wgrad_in(h, dp, name, comm=None):
    return _wgrad(h, dp, lambda ts: pl.BlockSpec((ts, D), lambda j, k: (k, 0)),
                  lambda ts: pl.BlockSpec((ts, NB_IN), lambda j, k: (k, j)), N_DEV, (D, NB_IN),
                  (N_DEV, D, NB_IN), pl.BlockSpec((None, D, NB_IN), lambda j, k: (j, 0, 0)), _store_plain, name, comm)


def _wgrad_up(h, du, name, comm=None):
    return _wgrad(h, du, lambda ts: pl.BlockSpec((ts, D), lambda j, k: (k, 0)),
                  lambda ts: pl.BlockSpec((None, ts, NB_UP), lambda j, k: (j, k, 0)), N_DEV, (D, NB_UP),
                  (N_DEV, D, NB_UP), pl.BlockSpec((None, D, NB_UP), lambda j, k: (j, 0, 0)), _store_plain, name, comm)


def _wgrad_down(act, dxo, name, comm=None):
    return _wgrad(act, dxo, lambda ts: pl.BlockSpec((None, ts, NB_UP), lambda j, k: (j, k, 0)),
                  lambda ts: pl.BlockSpec((ts, D), lambda j, k: (k, 0)), N_DEV // 2, (NB_UP, D),
                  (DFF, D), pl.BlockSpec((NB_UP, D), lambda j, k: (j, 0)), _store_plain, name, comm)


def _wgrad_o(merged, dxmid, name, comm=None):
    return _wgrad(merged, dxmid, lambda ts: pl.BlockSpec((ts, D), lambda j, k: (k, 0)),
                  lambda ts: pl.BlockSpec((ts, D), lambda j, k: (k, 0)), 1, (D, D),
                  (D, D), pl.BlockSpec((D, D), lambda j, k: (0, 0)), _store_plain, name, comm)


def _wgrad_branches(yabc, dbabc, name, comm=None):
    return _wgrad(yabc, dbabc, lambda ts: pl.BlockSpec((ts, WA), lambda j, k: (k, j)),
                  lambda ts: pl.BlockSpec((ts, D), lambda j, k: (k, j)), 3, (WA, D),
                  (N_DEV, 3, WA, NB_BR), pl.BlockSpec((N_DEV, None, WA, NB_BR), lambda j, k: (0, j, 0, 0)),
                  _store_lane_blocks, name, comm)


def _adamw_math(g, w, m, v):
    m = ADAM_B1 * m + (1.0 - ADAM_B1) * g
    v = ADAM_B2 * v + (1.0 - ADAM_B2) * (g * g)
    m_hat = m / (1.0 - ADAM_B1 ** ADAM_STEP)
    v_hat = v / (1.0 - ADAM_B2 ** ADAM_STEP)
    delta = -ADAM_LR * (m_hat / (jnp.sqrt(v_hat) + ADAM_EPS) + ADAM_WD * w)
    return delta, m, v


def _adamw_sum(parts, mid, w, m, v, layer, prev, tr, name, transposed=False):
    n_layers, r, c = w.shape[0], parts.shape[2], parts.shape[3]

    def body(p_ref, w_ref, m_ref, v_ref, *rest):
        g_ref, d_ref, mo_ref, vo_ref = rest[-4:]
        g = p_ref[0].astype(F32)
        for k in range(1, N_DEV):
            g = g + p_ref[k].astype(F32)
        if transposed:
            g = g.T
        g_ref[...] = g
        d_ref[...], mo_ref[...], vo_ref[...] = _adamw_math(g, w_ref[...], m_ref[...], v_ref[...])

    if transposed:
        blk = pl.BlockSpec((None, c, tr), lambda i: (layer, 0, i))
    else:
        blk = pl.BlockSpec((None, tr, c), lambda i: (layer, i, 0))
    extra = [] if prev is None else list(prev)
    return pl.pallas_call(
        body, name=name, grid=(r // tr,),
        in_specs=[pl.BlockSpec((N_DEV, None, tr, c), lambda i: (0, mid, i, 0)), blk, blk, blk]
        + [pl.BlockSpec(memory_space=pl.ANY)] * len(extra),
        out_specs=[blk] * 4, out_shape=[_sds(w.shape, F32)] * 4,
        input_output_aliases={4 + k: k for k in range(len(extra))},
        compiler_params=_params(),
    )(parts, w, m, v, *extra)


def _sum_parts(parts, name):
    _, r, c = parts.shape

    def body(p_ref, o_ref):
        g = p_ref[0]
        for k in range(1, N_DEV):
            g = g + p_ref[k]
        o_ref[...] = g

    return pl.pallas_call(body, name=name, out_shape=_sds((r, c), F32),
                          compiler_params=pltpu.CompilerParams(vmem_limit_bytes=VMEM_LIMIT))(parts)


def _adamw_small(g, w, m, v, name):
    def body(g_ref, w_ref, m_ref, v_ref, d_ref, mo_ref, vo_ref):
        d_ref[...], mo_ref[...], vo_ref[...] = _adamw_math(g_ref[...], w_ref[...], m_ref[...], v_ref[...])

    return pl.pallas_call(body, name=name, out_shape=[_sds(w.shape, F32)] * 3)(g, w, m, v)


HBM_SPEC = pl.BlockSpec(memory_space=pltpu.HBM)


def _position():
    return lax.axis_index("x"), lax.axis_index("y"), lax.axis_index("c")


def _device_index(chip, core):
    return 4 * chip[0] + 2 * chip[1] + core


def _gather(shards, layer):
    n = len(shards)
    per = 8

    def first_copies(ins, outs, send, recv):
        x, y, c = _position()
        me = 4 * x + 2 * y + c
        targets = [(x, y, 1 - c), (1 - x, y, c), (x, 1 - y, c), (1 - x, 1 - y, c)]
        remote = [pltpu.make_async_remote_copy(
            src_ref=ins[t].at[layer], dst_ref=outs[t].at[me], send_sem=send.at[per * t + k],
            recv_sem=recv.at[per * t + k], device_id=to, device_id_type=MESH)
            for t in range(n) for k, to in enumerate(targets)]
        local = [pltpu.make_async_copy(ins[t].at[layer], outs[t].at[me], send.at[per * t + 4]) for t in range(n)]
        return remote, local

    def passed_on(outs, send, recv):
        x, y, c = _position()
        chips = [(1 - x, y), (x, 1 - y), (1 - x, 1 - y)]
        return [pltpu.make_async_remote_copy(
            src_ref=outs[t].at[_device_index(chip, c)], dst_ref=outs[t].at[_device_index(chip, c)],
            send_sem=send.at[per * t + 5 + j], recv_sem=recv.at[per * t + 5 + j], device_id=(x, y, 1 - c),
            device_id_type=MESH)
            for t in range(n) for j, chip in enumerate(chips)]

    def start(ins, outs, send, recv):
        remote, local = first_copies(ins, outs, send, recv)
        for cp in local + remote:
            cp.start()

    def mid(ins, outs, send, recv):
        remote, local = first_copies(ins, outs, send, recv)
        for cp in remote:
            cp.wait()
        for cp in local:
            cp.wait()
        for cp in passed_on(outs, send, recv):
            cp.start()

    def finish(ins, outs, send, recv):
        for cp in passed_on(outs, send, recv):
            cp.wait()

    return _Comm(shards, [_sds((N_DEV,) + a.shape[1:], a.dtype) for a in shards], per * n, start, finish, mid)


def _run_comms(comms, name):
    def body():
        pass

    _, extra = _call(body, comms, None, None, None, (), name=name, in_specs=[], out_specs=[], out_shape=[])
    return extra


def _exchange(parts):
    n = len(parts)
    per = 8
    flips = [(0, 0, 1), (1, 0, 0), (1, 0, 1), (0, 1, 0), (0, 1, 1), (1, 1, 0), (1, 1, 1)]

    def copies(ins, outs, send, recv):
        x, y, c = _position()
        me = 4 * x + 2 * y + c
        remote = []
        for t in range(n):
            for k, (fx, fy, fc) in enumerate(flips):
                peer = ((1 - x if fx else x), (1 - y if fy else y), (1 - c if fc else c))
                remote.append(pltpu.make_async_remote_copy(
                    src_ref=ins[t].at[_device_index(peer[:2], peer[2])], dst_ref=outs[t].at[me],
                    send_sem=send.at[per * t + k], recv_sem=recv.at[per * t + k], device_id=peer, device_id_type=MESH))
        local = [pltpu.make_async_copy(ins[t].at[me], outs[t].at[me], send.at[per * t + 7]) for t in range(n)]
        return remote, local

    def start(ins, outs, send, recv):
        remote, local = copies(ins, outs, send, recv)
        for cp in local + remote:
            cp.start()

    def finish(ins, outs, send, recv):
        remote, local = copies(ins, outs, send, recv)
        for cp in remote:
            cp.wait()
        for cp in local:
            cp.wait()

    return _Comm(parts, [_sds(a.shape, a.dtype) for a in parts], per * n, start, finish)


def _rows128(a):
    return a.reshape(-1, 128)


def kernel(x, g_mix, w_in, w_pool, pool_scale, g_sgu, w_spatial, b_spatial, conv_c, w_branch_a, w_branch_b, w_branch_c, w_o, g_ffn, w_up, conv_ffn, conv_ffn_b, w_down, g_final, loss_target, m_g_mix, m_w_in, m_w_pool, m_pool_scale, m_g_sgu, m_w_spatial, m_b_spatial, m_conv_c, m_w_branch_a, m_w_branch_b, m_w_branch_c, m_w_o, m_g_ffn, m_w_up, m_conv_ffn, m_conv_ffn_b, m_w_down, m_g_final, v_g_mix, v_w_in, v_w_pool, v_pool_scale, v_g_sgu, v_w_spatial, v_b_spatial, v_conv_c, v_w_branch_a, v_w_branch_b, v_w_branch_c, v_w_o, v_g_ffn, v_w_up, v_conv_ffn, v_conv_ffn_b, v_w_down, v_g_final):
    s = x.shape[1]
    n_layers = g_mix.shape[0]
    x0 = x.reshape(s, D)
    target = loss_target.reshape(s, D)
    me = 4 * lax.axis_index("x") + 2 * lax.axis_index("y") + lax.axis_index("c")

    first_shards = [w_in.astype(MXU), conv_c]
    mix_shards = [w_branch_a.astype(MXU), w_branch_b.astype(MXU), w_branch_c.astype(MXU), w_o.astype(MXU),
                  w_down.astype(MXU)]
    up_shards = [w_up.astype(MXU), conv_ffn]
    (first_now,) = _run_comms([_gather(first_shards, 0)], "gather_first_0")
    mix_now = up_now = None
    wpool_b = w_pool.astype(MXU)
    bsp_t = jnp.swapaxes(b_spatial, 1, 2)
    convb_blk = conv_ffn_b.reshape(n_layers, N_DEV, NB_UP)

    saved = []
    weights = []
    xl = x0
    for l in range(n_layers):
        win8, convc8 = first_now
        convc_full = jnp.transpose(convc8, (1, 0, 2)).reshape(3, WA)
        more = l + 1 < n_layers
        mixer_args = (wpool_b[l], pool_scale[l:l + 1], g_sgu[l:l + 1], w_spatial[l], bsp_t[l], convc_full)
        if l == 0:
            (p, h), (mix_now,) = _rms_proj(xl, g_mix[l:l + 1], win8, f"in_proj_{l}", [_gather(mix_shards, l)])
            wa8, wb8, wc8, wo8, wd8 = mix_now
            (xmid, yabc, pacz, babc, merged), (up_now,) = _mixer_fwd(
                xl, p, *mixer_args, wa8, wb8, wc8, wo8, f"mixer_fwd_{l}", [_gather(up_shards, l)])
        else:
            wa8, wb8, wc8, wo8, wd8 = mix_now
            (xmid, yabc, pacz, babc, merged, p, h), (up_now,) = _mixer_fwd(
                xl, None, *mixer_args, wa8, wb8, wc8, wo8, f"mixer_fwd_{l}", [_gather(up_shards, l)],
                in_proj=(g_mix[l:l + 1], win8))
        wup8, convf8 = up_now
        weights.append((win8, wa8, wb8, wc8, wo8, wup8, wd8, convc_full, convf8))
        if more:
            (xout, h2, upre, up, act), (first_now, mix_now) = _ffn_block_fwd(
                xmid, g_ffn[l:l + 1], wup8, convf8, convb_blk[l], wd8, f"ffn_fwd_{l}",
                [_gather(first_shards, l + 1), _gather(mix_shards, l + 1)])
        else:
            (xout, h2, upre, up, act, dg_final, loss_local, dx_b), _ = _ffn_block_fwd(
                xmid, g_ffn[l:l + 1], wup8, convf8, convb_blk[l], wd8, f"ffn_fwd_{l}",
                loss_head=(g_final.reshape(1, D), target))
        saved.append((xl, p, h, xmid, yabc, pacz, babc, merged, upre, h2, act, up))
        xl = xout

    dx = xl

    received = [dict() for _ in range(n_layers)]
    small = {("final", "g_final"): dg_final}
    small_sums = {}
    waiting = None

    def exchange_of(named):
        return [_exchange([a for _, a in named])]

    def land(layer, named, got):
        received[layer].update({k: a for (k, _), a in zip(named, got[0])})

    def gather_small(keys):
        packed = jnp.concatenate([_rows128(small[k]) for k in keys], axis=0)[None]
        return _gather([packed], 0)

    def sum_small(keys, gathered, name):
        summed = _sum_parts(gathered, name)
        row = 0
        for k in keys:
            n_rows = small[k].size // 128
            small_sums[k] = summed[row:row + n_rows].reshape(small[k].shape)
            row += n_rows

    for l in reversed(range(n_layers)):
        xin, p, h, xmid, yabc, pacz, babc, merged, upre, h2, act, up = saved[l]
        win8, wa8, wb8, wc8, wo8, wup8, wd8, convc_full, convf8 = weights[l]
        last = l == 0
        (dupre, dconvf, dxmid, dg_ffn, dxmid_b), got = _ffn_block_bwd(
            dx, upre, up, convf8, wd8, wup8, xmid, g_ffn[l:l + 1], f"ffn_bwd_{l}",
            None if waiting is None else exchange_of(waiting[1]))
        if waiting is not None:
            land(waiting[0], waiting[1], got)
        small[(l, "conv_ffn")] = dconvf
        small[(l, "g_ffn")] = dg_ffn
        keys_a = [k for k in small if k not in small_sums]
        g_wdown, got = _wgrad_down(act, dx_b, f"wgrad_down_{l}", [gather_small(keys_a)] if last else None)
        if last:
            sum_small(keys_a, got[0][0], "sum_small_grads_a")
        down = [("w_down", g_wdown.reshape(N_DEV, ROWS_DN, D))]
        g_wup, got = _wgrad_up(h2, dupre, f"wgrad_up_{l}", exchange_of(down) if last else None)
        if last:
            land(l, down, got)
        upw = [("w_up", g_wup)]
        mixer_args = (dxmid, p, yabc, pacz, babc, wpool_b[l], pool_scale[l:l + 1], g_sgu[l:l + 1], w_spatial[l],
                      bsp_t[l], convc_full, wa8, wb8, wc8, wo8, f"mixer_bwd_{l}")
        if last:
            (dp, dbabc, dwp, dws, mixer_small, dbs), got = _mixer_bwd(*mixer_args, exchange_of(upw))
            land(l, upw, got)
        else:
            (dp, dbabc, dwp, dws, mixer_small, dbs, dx, dg_mix, dx_b), got = _mixer_bwd(
                *mixer_args, exchange_of(down + upw), in_proj=(win8, xin, g_mix[l:l + 1]))
            land(l, down + upw, got)
        small.update({(l, "w_pool"): dwp, (l, "mixer_small"): mixer_small, (l, "w_spatial"): dws,
                      (l, "b_spatial"): dbs})
        g_wo, _ = _wgrad_o(merged, dxmid_b, f"wgrad_o_{l}")
        g_br, _ = _wgrad_branches(yabc, dbabc, f"wgrad_branches_{l}")
        mixer_w = [("branches", g_br), ("w_o", g_wo.reshape(N_DEV, ROWS_O, D))]
        keys_b = [k for k in small if k not in small_sums]
        g_win, got = _wgrad_in(h, dp, f"wgrad_in_{l}",
                               exchange_of(mixer_w) + ([gather_small(keys_b)] if last else []))
        land(l, mixer_w, got)
        inw = [("w_in", g_win)]
        if last:
            sum_small(keys_b, got[1][0], "sum_small_grads_b")
            (dx, dg_mix), got = _proj_bwd(dp, win8, xin, g_mix[l:l + 1], dxmid, f"in_proj_bwd_{l}", exchange_of(inw))
            land(l, inw, got)
        else:
            waiting = (l, inw)
        small[(l, "g_mix")] = dg_mix
    grad_x = dx.reshape(1, s, D)
    late_keys = [k for k in small if k not in small_sums]
    (gathered_late,) = _run_comms([gather_small(late_keys)], "gather_last_small_grads")[0]
    sum_small(late_keys, gathered_late, "sum_last_small_grads")

    def update_big(key, mid, w, m, v, tr, tag, transposed=False):
        swap = (lambda a: jnp.swapaxes(a, 1, 2)) if transposed else (lambda a: a)
        w, m, v = swap(w), swap(m), swap(v)
        outs = None
        for l in range(n_layers):
            parts = received[l][key]
            if parts.ndim == 3:
                parts = parts.reshape(N_DEV, 1, *parts.shape[1:])
            outs = _adamw_sum(parts, mid, w, m, v, l, outs, tr, f"adamw_{tag}_{l}", transposed)
        return [swap(o) for o in outs]

    up_in = update_big("w_in", 0, w_in, m_w_in, v_w_in, 256, "w_in")
    up_a = update_big("branches", 0, w_branch_a, m_w_branch_a, v_w_branch_a, WA, "w_branch_a")
    up_b = update_big("branches", 1, w_branch_b, m_w_branch_b, v_w_branch_b, WA, "w_branch_b")
    up_c = update_big("branches", 2, w_branch_c, m_w_branch_c, v_w_branch_c, WA, "w_branch_c")
    up_o = update_big("w_o", 0, w_o, m_w_o, v_w_o, ROWS_O, "w_o")
    up_up = update_big("w_up", 0, w_up, m_w_up, v_w_up, 256, "w_up", transposed=True)
    up_down = update_big("w_down", 0, w_down, m_w_down, v_w_down, ROWS_DN, "w_down")

    stack = lambda kind: jnp.stack([small_sums[(l, kind)] for l in range(n_layers)], axis=0)
    grad_g_mix = stack("g_mix")[:, 0, :]
    grad_w_pool = stack("w_pool")
    mixer_sums = stack("mixer_small")
    grad_pool_scale = mixer_sums[:, 0, :]
    grad_g_sgu = mixer_sums[:, 1, :]
    grad_conv_c = lax.dynamic_slice_in_dim(mixer_sums[:, 2:5, :], me * (WA // N_DEV), WA // N_DEV, axis=2)
    grad_w_spatial = stack("w_spatial")
    grad_b_spatial = stack("b_spatial")[:, 0:HEADS, :]
    grad_g_ffn = stack("g_ffn")[:, 0, :]
    conv_grads = stack("conv_ffn")
    grad_conv_ffn = lax.dynamic_index_in_dim(conv_grads, me, axis=1, keepdims=False)[:, 0:3, :]
    grad_conv_ffn_b = conv_grads[:, :, 3, :].reshape(n_layers, 2 * DFF)
    grad_g_final = small_sums[("final", "g_final")][0]

    def update_small(g, w, m, v, tag):
        shape2 = (-1, w.shape[-1])
        outs = _adamw_small(g.reshape(shape2), w.reshape(shape2), m.reshape(shape2), v.reshape(shape2), f"adamw_{tag}")
        return [g] + [o.reshape(w.shape) for o in outs]

    up = {
        "g_mix": update_small(grad_g_mix, g_mix, m_g_mix, v_g_mix, "g_mix"),
        "w_in": up_in,
        "w_pool": update_small(grad_w_pool, w_pool, m_w_pool, v_w_pool, "w_pool"),
        "pool_scale": update_small(grad_pool_scale, pool_scale, m_pool_scale, v_pool_scale, "pool_scale"),
        "g_sgu": update_small(grad_g_sgu, g_sgu, m_g_sgu, v_g_sgu, "g_sgu"),
        "w_spatial": update_small(grad_w_spatial, w_spatial, m_w_spatial, v_w_spatial, "w_spatial"),
        "b_spatial": update_small(grad_b_spatial, b_spatial, m_b_spatial, v_b_spatial, "b_spatial"),
        "conv_c": update_small(grad_conv_c, conv_c, m_conv_c, v_conv_c, "conv_c"),
        "w_branch_a": up_a,
        "w_branch_b": up_b,
        "w_branch_c": up_c,
        "w_o": up_o,
        "g_ffn": update_small(grad_g_ffn, g_ffn, m_g_ffn, v_g_ffn, "g_ffn"),
        "w_up": up_up,
        "conv_ffn": update_small(grad_conv_ffn, conv_ffn, m_conv_ffn, v_conv_ffn, "conv_ffn"),
        "conv_ffn_b": update_small(grad_conv_ffn_b, conv_ffn_b, m_conv_ffn_b, v_conv_ffn_b, "conv_ffn_b"),
        "w_down": up_down,
        "g_final": update_small(grad_g_final, g_final, m_g_final, v_g_final, "g_final"),
    }
    loss = lax.psum(loss_local[0, 0], AXES)
    order = list(up)
    return (loss, grad_x, *[up[k][0] for k in order], *[up[k][1] for k in order], *[up[k][2] for k in order],
            *[up[k][3] for k in order])
```

```python
import jax
import jax.numpy as jnp
from jax import lax
from jax.experimental import pallas as pl
from jax.experimental.pallas import tpu as pltpu

F32 = jnp.float32
BF16 = jnp.bfloat16
MXU = BF16
ACT = BF16
WIRE = BF16

N_DEV = 8
D = 1024
WA = 512
NCOL = 6144
DFF = 2816
NB_IN = NCOL // N_DEV
NB_UP = 2 * DFF // N_DEV
NB_BR = D // N_DEV
ROWS_O = D // N_DEV
ROWS_DN = DFF // N_DEV
CHUNK = 128
HEADS = 4
POOL_WINDOWS = (2, 4, 8, 16)
EPS = 1e-6
A0, UV0, CB0, CC0, CX0, GA0, GB0, GC0 = 0, 512, 1536, 2048, 2560, 3072, 4096, 5120

ADAM_LR = 0.001
ADAM_B1 = 0.9
ADAM_B2 = 0.999
ADAM_EPS = 1e-08
ADAM_WD = 0.01
ADAM_STEP = 10

TM = 256
TM_PROJ = 512
TS_WGRAD = 4096
HALO_POOL = 16
HALO_CONV = 8
VMEM_LIMIT = 56 * 1024 * 1024
MESH = pl.DeviceIdType.MESH
AXES = ("x", "y", "c")


def _sds(shape, dtype):
    return jax.ShapeDtypeStruct(tuple(shape), dtype)


def _params(n_grid=1):
    return pltpu.CompilerParams(dimension_semantics=("arbitrary",) * n_grid, vmem_limit_bytes=VMEM_LIMIT)


def _const(block, index):
    return pl.BlockSpec(block, lambda *_: index, pipeline_mode=pl.Buffered(1))


def _dot(a, b):
    return jnp.dot(a, b, preferred_element_type=F32)


def _dot_nt(a, b):
    return lax.dot_general(a, b, (((1,), (1,)), ((), ())), preferred_element_type=F32)


def _dot_tn(a, b):
    return lax.dot_general(a, b, (((0,), (0,)), ((), ())), preferred_element_type=F32)


def _sigmoid(v):
    return 0.5 * jnp.tanh(0.5 * v) + 0.5


def _shift_down(v, k):
    return pltpu.roll(v, k, axis=0)


def _shift_up(v, k):
    return pltpu.roll(v, v.shape[0] - k, axis=0)


def _colsum(v):
    return jnp.sum(v, axis=0, keepdims=True)


def _lane_cat(ref):
    return jnp.concatenate([ref[d] for d in range(N_DEV)], axis=1)


class _Comm:
    def __init__(self, operands, out_shapes, n_sems, start, finish, mid=None):
        self.operands = list(operands)
        self.out_shapes = list(out_shapes)
        self.n_sems = n_sems
        self.start = start
        self.mid = mid
        self.finish = finish


def _call(body, comms, is_first, is_mid, is_last, operands, *, in_specs, out_specs, out_shape, scratch_shapes=(), **kw):
    n_in, n_out, n_scr = len(in_specs), len(out_specs), len(scratch_shapes)
    comms = [c for c in (comms or []) if c is not None]
    if not comms:
        res = pl.pallas_call(body, in_specs=in_specs, out_specs=out_specs, out_shape=out_shape,
                             scratch_shapes=list(scratch_shapes), **kw)(*operands)
        return res, []
    nci = [len(c.operands) for c in comms]
    nco = [len(c.out_shapes) for c in comms]

    def split(refs, sizes):
        parts = []
        for n in sizes:
            parts.append(refs[:n])
            refs = refs[n:]
        return parts, refs

    def carrier(*refs):
        ins, refs = refs[:n_in], refs[n_in:]
        cins, refs = split(refs, nci)
        outs, refs = refs[:n_out], refs[n_out:]
        couts, refs = split(refs, nco)
        scr, sems = refs[:n_scr], refs[n_scr:]

        def run(step):
            for k, c in enumerate(comms):
                if getattr(c, step) is not None:
                    getattr(c, step)(cins[k], couts[k], sems[2 * k], sems[2 * k + 1])

        def at(mark, step):
            if mark is None:
                run(step)
            else:
                pl.when(mark())(lambda: run(step))

        at(is_first, "start")
        body(*ins, *outs, *scr)
        at(is_mid, "mid")
        at(is_last, "finish")

    res = pl.pallas_call(
        carrier, in_specs=list(in_specs) + [HBM_SPEC] * sum(nci), out_specs=list(out_specs) + [HBM_SPEC] * sum(nco),
        out_shape=list(out_shape) + [s for c in comms for s in c.out_shapes],
        scratch_shapes=list(scratch_shapes) + [pltpu.SemaphoreType.DMA((c.n_sems,)) for c in comms for _ in range(2)],
        **kw,
    )(*operands, *[a for c in comms for a in c.operands])
    extra, _ = split(res[n_out:], nco)
    return res[:n_out], extra


def _grid_marks(nt):
    return (lambda: pl.program_id(0) == 0), (lambda: pl.program_id(0) == (3 * nt) // 4), (lambda: pl.program_id(0) == nt - 1)


def _rms_proj(x, g, w_all, name, comms=None):
    s = x.shape[0]
    nb = w_all.shape[-1]
    tm = min(TM_PROJ, s)
    nt = s // tm

    def body(x_ref, g_ref, w_ref, p_ref, h_ref):
        xf = x_ref[...]
        r = lax.rsqrt(jnp.mean(xf * xf, axis=-1, keepdims=True) + EPS)
        h = (xf * r * g_ref[...]).astype(MXU)
        h_ref[...] = h
        for j in range(N_DEV):
            p_ref[:, j * nb:(j + 1) * nb] = _dot(h, w_ref[j]).astype(p_ref.dtype)

    row = lambda n: pl.BlockSpec((tm, n), lambda i: (i, 0))
    return _call(
        body, comms, *_grid_marks(nt), (x, g, w_all),
        name=name, grid=(nt,),
        in_specs=[row(D), _const((1, D), (0, 0)), _const((N_DEV, D, nb), (0, 0, 0))],
        out_specs=[row(N_DEV * nb), row(D)],
        out_shape=[_sds((s, N_DEV * nb), ACT), _sds((s, D), MXU)],
        compiler_params=_params(),
    )


def _tril_mask():
    r = lax.broadcasted_iota(jnp.int32, (CHUNK, CHUNK), 0)
    c = lax.broadcasted_iota(jnp.int32, (CHUNK, CHUNK), 1)
    return r >= c


def _gelu_parts(v):
    c0 = 0.7978845608028654
    th = jnp.tanh(c0 * (v + 0.044715 * (v * v * v)))
    cdf = 0.5 * (1.0 + th)
    dgelu = cdf + v * (0.5 * c0) * (1.0 - th * th) * (1.0 + 3.0 * 0.044715 * (v * v))
    return v * cdf, dgelu


def _mixer_fwd(x, p, wpool, pscale, gsgu, wsp, bsp_t, convc, wa_all, wb_all, wc_all, wo_all, name, comms=None,
               in_proj=None):
    s = x.shape[0]
    nt = s // TM
    fused = in_proj is not None

    def body(*refs):
        if fused:
            (x_ref, g_ref, win_ref, wpool_ref, ps_ref, gs_ref, wsp_ref, bsp_ref, cc_ref, wa_ref, wb_ref, wc_ref, wo_ref,
             xmid_ref, y_ref, pz_ref, b_ref, m_ref, p_ref, h_ref, carry_a, carry_z) = refs
        else:
            (x_ref, p_ref, wpool_ref, ps_ref, gs_ref, wsp_ref, bsp_ref, cc_ref, wa_ref, wb_ref, wc_ref, wo_ref,
             xmid_ref, y_ref, pz_ref, b_ref, m_ref, carry_a, carry_z) = refs
        i = pl.program_id(0)

        @pl.when(i == 0)
        def _():
            carry_a[...] = jnp.zeros_like(carry_a)
            carry_z[...] = jnp.zeros_like(carry_z)

        def pf(lo, n):
            return p_ref[:, lo:lo + n].astype(F32)

        def project(blocks):
            if fused:
                for j in blocks:
                    p_ref[:, j * NB_IN:(j + 1) * NB_IN] = _dot(h_ref[...], win_ref[j]).astype(ACT)

        if fused:
            xf = x_ref[...]
            r = lax.rsqrt(jnp.mean(xf * xf, axis=-1, keepdims=True) + EPS)
            h_ref[...] = (xf * r * g_ref[...]).astype(MXU)
        project((0, 1, 2))

        a = pf(A0, WA)
        ext = jnp.concatenate([carry_a[...], a], axis=0)
        carry_a[...] = a[TM - HALO_POOL:, :]
        t_pos = (i * TM + lax.broadcasted_iota(jnp.int32, (TM, 1), 0)).astype(F32)
        for g, win in enumerate(POOL_WINDOWS):
            cols = slice(g * CHUNK, (g + 1) * CHUNK)
            acc = ext[:, cols]
            k = 1
            while k < win:
                acc = acc + _shift_down(acc, k)
                k *= 2
            cnt = jnp.minimum(t_pos + 1.0, float(win))
            pa_g = (acc[HALO_POOL:, :] / cnt - a[:, cols]).astype(MXU)
            pz_ref[:, cols] = pa_g
            y_ref[:, cols] = (_dot(pa_g, wpool_ref[g]) * ps_ref[:, cols]).astype(ACT)

        project((3,))
        uvg, _ = _gelu_parts(pf(UV0, 2 * WA))
        u = uvg[:, :WA]
        v = uvg[:, WA:]
        rv = lax.rsqrt(jnp.mean(v * v, axis=-1, keepdims=True) + EPS)
        vn = (v * rv * gs_ref[...]).astype(MXU)
        mask = _tril_mask()
        for g in range(HEADS):
            cols = slice(g * CHUNK, (g + 1) * CHUNK)
            wt = jnp.where(mask, wsp_ref[g], 0.0).astype(MXU)
            bcol = bsp_ref[:, g:g + 1]
            for c in range(TM // CHUNK):
                rows = slice(c * CHUNK, (c + 1) * CHUNK)
                sv = _dot(wt, vn[rows, cols]) + bcol
                y_ref[rows, WA + g * CHUNK:WA + (g + 1) * CHUNK] = (u[rows, cols] * sv).astype(ACT)

        project((4, 5))
        z = pf(CC0, WA) * pf(CX0, WA)
        extz = jnp.concatenate([carry_z[...], z], axis=0)
        carry_z[...] = z[TM - HALO_CONV:, :]
        cz = (cc_ref[0:1, :] * _shift_down(extz, 2)[HALO_CONV:, :]
              + cc_ref[1:2, :] * _shift_down(extz, 1)[HALO_CONV:, :] + cc_ref[2:3, :] * z)
        pz_ref[:, WA:2 * WA] = cz.astype(ACT)
        y_ref[:, 2 * WA:3 * WA] = (pf(CB0, WA) * cz).astype(ACT)

        project((6, 7))
        merged = jnp.zeros((TM, D), F32)
        for k, (w_ref, glo) in enumerate(((wa_ref, GA0), (wb_ref, GB0), (wc_ref, GC0))):
            br = _dot(y_ref[:, k * WA:(k + 1) * WA], _lane_cat(w_ref))
            b_ref[:, k * D:(k + 1) * D] = br.astype(ACT)
            merged = merged + _sigmoid(pf(glo, D)) * br
        mb = merged.astype(MXU)
        m_ref[...] = mb
        xmid_ref[...] = x_ref[...] + _dot(mb, wo_ref[...].reshape(D, D))

    row = lambda n: pl.BlockSpec((TM, n), lambda i: (i, 0))
    br_spec = _const((N_DEV, WA, NB_BR), (0, 0, 0))
    if fused:
        lead_specs = [row(D), _const((1, D), (0, 0)), _const((N_DEV, D, NB_IN), (0, 0, 0))]
        lead = (x,) + tuple(in_proj)
    else:
        lead_specs = [row(D), row(NCOL)]
        lead = (x, p)
    more_specs = [row(NCOL), row(D)] if fused else []
    more_shapes = [_sds((s, NCOL), ACT), _sds((s, D), MXU)] if fused else []
    return _call(
        body, comms, *_grid_marks(nt), lead + (wpool, pscale, gsgu, wsp, bsp_t, convc, wa_all, wb_all, wc_all, wo_all),
        name=name, grid=(nt,),
        in_specs=lead_specs + [_const((HEADS, CHUNK, CHUNK), (0, 0, 0)), _const((1, WA), (0, 0)),
                               _const((1, WA), (0, 0)), _const((HEADS, CHUNK, CHUNK), (0, 0, 0)),
                               _const((CHUNK, HEADS), (0, 0)), _const((3, WA), (0, 0)), br_spec, br_spec, br_spec,
                               _const((N_DEV, ROWS_O, D), (0, 0, 0))],
        out_specs=[row(D), row(3 * WA), row(2 * WA), row(3 * D), row(D)] + more_specs,
        out_shape=[_sds((s, D), F32), _sds((s, 3 * WA), ACT), _sds((s, 2 * WA), ACT), _sds((s, 3 * D), ACT),
                   _sds((s, D), MXU)] + more_shapes,
        scratch_shapes=[pltpu.VMEM((HALO_POOL, WA), F32), pltpu.VMEM((HALO_CONV, WA), F32)],
        compiler_params=_params(),
    )


def _conv_up(ext, cur, w_ref, j, b_row):
    return (w_ref[j, 0:1, :] * _shift_down(ext, 2)[HALO_CONV:, :] + w_ref[j, 1:2, :] * _shift_down(ext, 1)[HALO_CONV:, :]
            + w_ref[j, 2:3, :] * cur + b_row)


def _ffn_block_fwd(xmid, g, wup_all, convf_all, convb, wd_all, name, comms=None, loss_head=None):
    s = xmid.shape[0]
    nt = s // TM
    half = N_DEV // 2
    with_loss = loss_head is not None

    def body(*refs):
        (x_ref, g_ref, wup_ref, cw_ref, cb_ref, wd_ref), refs = refs[:6], refs[6:]
        if with_loss:
            (gf_ref, t_ref), refs = refs[:2], refs[2:]
        (xo_ref, h_ref, u_ref, up_ref, act_ref), refs = refs[:5], refs[5:]
        if with_loss:
            (dgf_ref, loss_ref, dxb_ref), refs = refs[:3], refs[3:]
        (carry,) = refs
        i = pl.program_id(0)

        @pl.when(i == 0)
        def _():
            carry[...] = jnp.zeros_like(carry)
            if with_loss:
                dgf_ref[...] = jnp.zeros_like(dgf_ref)
                loss_ref[...] = jnp.zeros_like(loss_ref)

        xf = x_ref[...]
        r = lax.rsqrt(jnp.mean(xf * xf, axis=-1, keepdims=True) + EPS)
        h = (xf * r * g_ref[...]).astype(MXU)
        h_ref[...] = h

        def project(j):
            return _dot(h, wup_ref[j]).astype(ACT)

        def conv(j, pre):
            u_ref[j] = pre
            cur = pre.astype(F32)
            ext = jnp.concatenate([carry[j], cur], axis=0)
            carry[j] = cur[TM - HALO_CONV:, :]
            up = _conv_up(ext, cur, cw_ref, j, cb_ref[j:j + 1, :])
            up_ref[j] = up.astype(ACT)
            return up

        order = [j + k * half for j in range(half) for k in range(2)]
        acc = xf
        ahead = 2
        pres = {n: project(order[n]) for n in range(ahead)}
        ups = {}
        for n, j in enumerate(order):
            if n + ahead < N_DEV:
                pres[n + ahead] = project(order[n + ahead])
            ups[j] = conv(j, pres.pop(n))
            if j >= half:
                gate, val = ups.pop(j - half), ups.pop(j)
                act = (gate * _sigmoid(gate) * val).astype(MXU)
                act_ref[j - half] = act
                wd = jnp.concatenate([wd_ref[2 * (j - half)], wd_ref[2 * (j - half) + 1]], axis=0)
                acc = acc + _dot(act, wd)
        if not with_loss:
            xo_ref[...] = acc
        else:
            ro = lax.rsqrt(jnp.mean(acc * acc, axis=-1, keepdims=True) + EPS)
            xn = acc * ro
            err = xn * gf_ref[...] - t_ref[...]
            loss_ref[...] += 0.5 * jnp.sum(jnp.mean(err * err, axis=-1, keepdims=True), axis=0, keepdims=True)
            dy = err * (1.0 / D)
            dgf_ref[0:1, :] += _colsum(dy * xn)
            dyg = dy * gf_ref[...]
            dxo = ro * (dyg - xn * jnp.mean(dyg * xn, axis=-1, keepdims=True))
            xo_ref[...] = dxo
            dxb_ref[...] = dxo.astype(MXU)

    row = pl.BlockSpec((TM, D), lambda i: (i, 0))
    blocks = pl.BlockSpec((N_DEV, TM, NB_UP), lambda i: (0, i, 0))
    return _call(
        body, comms, *_grid_marks(nt), (xmid, g, wup_all, convf_all, convb, wd_all) + (tuple(loss_head) if with_loss else ()),
        name=name, grid=(nt,),
        in_specs=[row, _const((1, D), (0, 0)), _const((N_DEV, D, NB_UP), (0, 0, 0)),
                  _const((N_DEV, 3, NB_UP), (0, 0, 0)), _const((N_DEV, NB_UP), (0, 0)),
                  _const((N_DEV, ROWS_DN, D), (0, 0, 0))] + ([_const((1, D), (0, 0)), row] if with_loss else []),
        out_specs=[row, row, blocks, blocks, pl.BlockSpec((half, TM, NB_UP), lambda i: (0, i, 0))]
        + ([pl.BlockSpec((8, D), lambda i: (0, 0)), pl.BlockSpec((1, 1), lambda i: (0, 0)), row] if with_loss else []),
        out_shape=[_sds((s, D), F32), _sds((s, D), MXU), _sds((N_DEV, s, NB_UP), ACT), _sds((N_DEV, s, NB_UP), ACT),
                   _sds((half, s, NB_UP), MXU)]
        + ([_sds((8, D), F32), _sds((1, 1), F32), _sds((s, D), MXU)] if with_loss else []),
        scratch_shapes=[pltpu.VMEM((N_DEV, HALO_CONV, NB_UP), F32)],
        compiler_params=_params(),
    )


def _ffn_block_bwd(dxo, upre, up, convf_all, wd_all, wup_all, xmid, g, name, comms=None):
    s = dxo.shape[0]
    nt = s // TM
    half = N_DEV // 2

    def body(dx_ref, u_ref, up_ref, cw_ref, wd_ref, wup_ref, x_ref, g_ref, du_ref, dc_ref, dxm_ref, dg_ref, dxmb_ref,
             carry):
        step = pl.program_id(0)

        @pl.when(step == 0)
        def _():
            carry[...] = jnp.zeros_like(carry)
            dc_ref[...] = jnp.zeros_like(dc_ref)
            dg_ref[...] = jnp.zeros_like(dg_ref)

        dxo_t = dx_ref[...]
        dxb = dxo_t.astype(MXU)

        def adjoint(j, d_up):
            cur = u_ref[j].astype(F32)
            ext = jnp.concatenate([d_up, carry[j]], axis=0)
            carry[j] = d_up[:HALO_CONV, :]
            up1 = _shift_up(ext, 1)[:TM, :]
            up2 = _shift_up(ext, 2)[:TM, :]
            du = (cw_ref[j, 2:3, :] * d_up + cw_ref[j, 1:2, :] * up1 + cw_ref[j, 0:1, :] * up2).astype(du_ref.dtype)
            du_ref[j] = du
            dc_ref[j, 0:1, :] += _colsum(cur * up2)
            dc_ref[j, 1:2, :] += _colsum(cur * up1)
            dc_ref[j, 2:3, :] += _colsum(cur * d_up)
            dc_ref[j, 3:4, :] += _colsum(d_up)
            return _dot_nt(du, wup_ref[j])

        def d_act(j):
            return _dot_nt(dxb, jnp.concatenate([wd_ref[2 * j], wd_ref[2 * j + 1]], axis=0))

        dh = jnp.zeros((TM, D), F32)
        dact = d_act(0)
        for j in range(half):
            nxt = d_act(j + 1) if j + 1 < half else None
            gate = up_ref[j].astype(F32)
            val = up_ref[j + half].astype(F32)
            sg = _sigmoid(gate)
            dh = dh + adjoint(j, dact * val * sg * (1.0 + gate * (1.0 - sg)))
            dh = dh + adjoint(j + half, dact * gate * sg)
            dact = nxt

        xf = x_ref[...]
        r = lax.rsqrt(jnp.mean(xf * xf, axis=-1, keepdims=True) + EPS)
        xn = xf * r
        dg_ref[0:1, :] += _colsum(dh * xn)
        dhg = dh * g_ref[...]
        dxm = dxo_t + r * (dhg - xn * jnp.mean(dhg * xn, axis=-1, keepdims=True))
        dxm_ref[...] = dxm
        dxmb_ref[...] = dxm.astype(MXU)

    blocks = pl.BlockSpec((N_DEV, TM, NB_UP), lambda i: (0, nt - 1 - i, 0))
    row = pl.BlockSpec((TM, D), lambda i: (nt - 1 - i, 0))
    return _call(
        body, comms, *_grid_marks(nt), (dxo, upre, up, convf_all, wd_all, wup_all, xmid, g),
        name=name, grid=(nt,),
        in_specs=[row, blocks, blocks, _const((N_DEV, 3, NB_UP), (0, 0, 0)), _const((N_DEV, ROWS_DN, D), (0, 0, 0)),
                  _const((N_DEV, D, NB_UP), (0, 0, 0)), row, _const((1, D), (0, 0))],
        out_specs=[blocks, pl.BlockSpec((N_DEV, 8, NB_UP), lambda i: (0, 0, 0)), row,
                   pl.BlockSpec((8, D), lambda i: (0, 0)), row],
        out_shape=[_sds((N_DEV, s, NB_UP), MXU), _sds((N_DEV, 8, NB_UP), F32), _sds((s, D), F32), _sds((8, D), F32),
                   _sds((s, D), MXU)],
        scratch_shapes=[pltpu.VMEM((N_DEV, HALO_CONV, NB_UP), F32)],
        compiler_params=_params(),
    )


def _proj_bwd(dy, w_all, x, g, dres, name, comms=None):
    s = x.shape[0]
    nb = w_all.shape[-1]
    tm = min(TM_PROJ, s)
    nt = s // tm

    def body(dy_ref, w_ref, x_ref, g_ref, dres_ref, dx_ref, dg_ref):
        i = pl.program_id(0)

        @pl.when(i == 0)
        def _():
            dg_ref[...] = jnp.zeros_like(dg_ref)

        dh = jnp.zeros((tm, D), F32)
        for j in range(N_DEV):
            dh = dh + _dot_nt(dy_ref[:, j * nb:(j + 1) * nb], w_ref[j])
        xf = x_ref[...]
        r = lax.rsqrt(jnp.mean(xf * xf, axis=-1, keepdims=True) + EPS)
        xn = xf * r
        dg_ref[0:1, :] += _colsum(dh * xn)
        dhg = dh * g_ref[...]
        dx_ref[...] = dres_ref[...] + r * (dhg - xn * jnp.mean(dhg * xn, axis=-1, keepdims=True))

    dy_spec = pl.BlockSpec((tm, N_DEV * nb), lambda i: (i, 0))
    row = pl.BlockSpec((tm, D), lambda i: (i, 0))
    return _call(
        body, comms, *_grid_marks(nt), (dy, w_all, x, g, dres),
        name=name, grid=(nt,),
        in_specs=[dy_spec, _const((N_DEV, D, nb), (0, 0, 0)), row, _const((1, D), (0, 0)), row],
        out_specs=[row, pl.BlockSpec((8, D), lambda i: (0, 0))],
        out_shape=[_sds((s, D), F32), _sds((8, D), F32)],
        compiler_params=_params(),
    )


def _mixer_bwd(dxmid, p, yabc, pacz, babc, wpool, pscale, gsgu, wsp, bsp_t, convc, wa_all, wb_all, wc_all, wo_all,
               name, comms=None, in_proj=None):
    s = dxmid.shape[0]
    nt = s // TM
    fused = in_proj is not None

    def body(*refs):
        (dx_ref, p_ref, y_ref, pz_ref, b_ref, wpool_ref, ps_ref, gs_ref, wsp_ref, bsp_ref, cc_ref,
         wa_ref, wb_ref, wc_ref, wo_ref) = refs[:15]
        refs = refs[15:]
        if fused:
            (win_ref, x_ref, g_ref), refs = refs[:3], refs[3:]
        (dp_ref, db_ref, dwp_ref, dws_ref, small_ref, dbs_ref), refs = refs[:6], refs[6:]
        if fused:
            (dxin_ref, dg_ref, dxinb_ref), refs = refs[:3], refs[3:]
        carry_pa, carry_cz, dbs_acc, du_s, dvn_s = refs
        step = pl.program_id(0)
        tile = nt - 1 - step

        @pl.when(step == 0)
        def _():
            for ref in (carry_pa, carry_cz, dbs_acc, dwp_ref, dws_ref, small_ref, dbs_ref) + ((dg_ref,) if fused else ()):
                ref[...] = jnp.zeros_like(ref)

        def pf(lo, n):
            return p_ref[:, lo:lo + n].astype(F32)

        def back_project(blocks):
            return sum(_dot_nt(dp_ref[:, j * NB_IN:(j + 1) * NB_IN], win_ref[j]) for j in blocks)

        dxm = dx_ref[...]
        dm = _dot_nt(dxm.astype(MXU), wo_ref[...].reshape(D, D))

        def through_gate(k, glo, w_ref):
            sg = _sigmoid(pf(glo, D))
            br = b_ref[:, k * D:(k + 1) * D].astype(F32)
            dp_ref[:, glo:glo + D] = (dm * br * sg * (1.0 - sg)).astype(dp_ref.dtype)
            dbr = (dm * sg).astype(MXU)
            db_ref[:, k * D:(k + 1) * D] = dbr
            return _dot_nt(dbr, _lane_cat(w_ref))

        dya = through_gate(0, GA0, wa_ref)
        dyb = through_gate(1, GB0, wb_ref)
        dyc = through_gate(2, GC0, wc_ref)
        if fused:
            dh = back_project((4, 5, 6, 7))

        t_pos = (tile * TM + lax.broadcasted_iota(jnp.int32, (TM, 1), 0)).astype(F32)
        for g, win in enumerate(POOL_WINDOWS):
            cols = slice(g * CHUNK, (g + 1) * CHUNK)
            pa_g = pz_ref[:, cols]
            q = _dot(pa_g, wpool_ref[g])
            dya_g = dya[:, cols]
            small_ref[0:1, cols] += _colsum(dya_g * q)
            dq = (dya_g * ps_ref[:, cols]).astype(MXU)
            dpa_g = _dot_nt(dq, wpool_ref[g])
            dwp_ref[g] += _dot_tn(pa_g, dq)
            dpw = dpa_g / jnp.minimum(t_pos + 1.0, float(win))
            acc = jnp.concatenate([dpw, carry_pa[:, cols]], axis=0)
            carry_pa[:, cols] = dpw[:HALO_POOL, :]
            k = 1
            while k < win:
                acc = acc + _shift_up(acc, k)
                k *= 2
            dp_ref[:, cols] = (acc[:TM, :] - dpa_g).astype(dp_ref.dtype)

        uvp = pf(UV0, 2 * WA)
        uvg, dgelu = _gelu_parts(uvp)
        u = uvg[:, :WA]
        v = uvg[:, WA:]
        rv = lax.rsqrt(jnp.mean(v * v, axis=-1, keepdims=True) + EPS)
        vh = v * rv
        vn = (vh * gs_ref[...]).astype(MXU)
        mask = _tril_mask()
        for g in range(HEADS):
            cols = slice(g * CHUNK, (g + 1) * CHUNK)
            wt32 = jnp.where(mask, wsp_ref[g], 0.0)
            wt = wt32.astype(MXU)
            wt_t = wt32.T.astype(MXU)
            bcol = bsp_ref[:, g:g + 1]
            for c in range(TM // CHUNK):
                rows = slice(c * CHUNK, (c + 1) * CHUNK)
                vn_cg = vn[rows, cols]
                sv = _dot(wt, vn_cg) + bcol
                dyb_cg = dyb[rows, cols]
                du_s[rows, cols] = dyb_cg * sv
                dsv = dyb_cg * u[rows, cols]
                dbs_acc[g] += dsv
                dsv_b = dsv.astype(MXU)
                dws_ref[g] += _dot_nt(dsv_b, vn_cg)
                dvn_s[rows, cols] = _dot(wt_t, dsv_b)
        dvn = dvn_s[...]
        small_ref[1:2, :] += _colsum(dvn * vh)
        dvg = dvn * gs_ref[...]
        dv = rv * (dvg - vh * jnp.mean(dvg * vh, axis=-1, keepdims=True))
        dp_ref[:, UV0:UV0 + WA] = (du_s[...] * dgelu[:, :WA]).astype(dp_ref.dtype)
        dp_ref[:, UV0 + WA:UV0 + 2 * WA] = (dv * dgelu[:, WA:]).astype(dp_ref.dtype)
        if fused:
            dh = dh + back_project((0, 1))

        cb = pf(CB0, WA)
        cc = pf(CC0, WA)
        cx = pf(CX0, WA)
        z = cc * cx
        dp_ref[:, CB0:CB0 + WA] = (dyc * pz_ref[:, WA:2 * WA].astype(F32)).astype(dp_ref.dtype)
        dcz = dyc * cb
        extz = jnp.concatenate([dcz, carry_cz[...]], axis=0)
        carry_cz[...] = dcz[:HALO_CONV, :]
        up1 = _shift_up(extz, 1)[:TM, :]
        up2 = _shift_up(extz, 2)[:TM, :]
        dz = cc_ref[2:3, :] * dcz + cc_ref[1:2, :] * up1 + cc_ref[0:1, :] * up2
        small_ref[2:3, :] += _colsum(z * up2)
        small_ref[3:4, :] += _colsum(z * up1)
        small_ref[4:5, :] += _colsum(z * dcz)
        dp_ref[:, CC0:CC0 + WA] = (dz * cx).astype(dp_ref.dtype)
        dp_ref[:, CX0:CX0 + WA] = (dz * cc).astype(dp_ref.dtype)

        if fused:
            dh = dh + back_project((2, 3))
            xf = x_ref[...]
            r = lax.rsqrt(jnp.mean(xf * xf, axis=-1, keepdims=True) + EPS)
            xn = xf * r
            dg_ref[0:1, :] += _colsum(dh * xn)
            dhg = dh * g_ref[...]
            dxin = dxm + r * (dhg - xn * jnp.mean(dhg * xn, axis=-1, keepdims=True))
            dxin_ref[...] = dxin
            dxinb_ref[...] = dxin.astype(MXU)

        @pl.when(step == nt - 1)
        def _():
            ones = jnp.ones((8, CHUNK), F32)
            for g in range(HEADS):
                dws_ref[g] = jnp.where(mask, dws_ref[g], 0.0)
                row = lax.dot_general(ones, dbs_acc[g], (((1,), (1,)), ((), ())), preferred_element_type=F32,
                                      precision=lax.Precision.HIGHEST)
                dbs_ref[g:g + 1, :] = row[0:1, :]

    row = lambda n: pl.BlockSpec((TM, n), lambda i: (nt - 1 - i, 0))
    br_spec = _const((N_DEV, WA, NB_BR), (0, 0, 0))
    acc_spec = lambda shape: pl.BlockSpec(shape, lambda i: (0,) * len(shape))
    more_in = tuple(in_proj) if fused else ()
    more_in_specs = [_const((N_DEV, D, NB_IN), (0, 0, 0)), row(D), _const((1, D), (0, 0))] if fused else []
    more_out_specs = [row(D), acc_spec((8, D)), row(D)] if fused else []
    more_out_shapes = [_sds((s, D), F32), _sds((8, D), F32), _sds((s, D), MXU)] if fused else []
    return _call(
        body, comms, *_grid_marks(nt),
        (dxmid, p, yabc, pacz, babc, wpool, pscale, gsgu, wsp, bsp_t, convc, wa_all, wb_all, wc_all, wo_all) + more_in,
        name=name, grid=(nt,),
        in_specs=[row(D), row(NCOL), row(3 * WA), row(2 * WA), row(3 * D),
                  _const((HEADS, CHUNK, CHUNK), (0, 0, 0)), _const((1, WA), (0, 0)), _const((1, WA), (0, 0)),
                  _const((HEADS, CHUNK, CHUNK), (0, 0, 0)), _const((CHUNK, HEADS), (0, 0)), _const((3, WA), (0, 0)),
                  br_spec, br_spec, br_spec, _const((N_DEV, ROWS_O, D), (0, 0, 0))] + more_in_specs,
        out_specs=[row(NCOL), row(3 * D), acc_spec((HEADS, CHUNK, CHUNK)), acc_spec((HEADS, CHUNK, CHUNK)),
                   acc_spec((8, WA)), acc_spec((8, CHUNK))] + more_out_specs,
        out_shape=[_sds((s, NCOL), MXU), _sds((s, 3 * D), MXU), _sds((HEADS, CHUNK, CHUNK), F32),
                   _sds((HEADS, CHUNK, CHUNK), F32), _sds((8, WA), F32), _sds((8, CHUNK), F32)] + more_out_shapes,
        scratch_shapes=[pltpu.VMEM((HALO_POOL, WA), F32), pltpu.VMEM((HALO_CONV, WA), F32),
                        pltpu.VMEM((HEADS, CHUNK, CHUNK), F32), pltpu.VMEM((TM, WA), F32), pltpu.VMEM((TM, WA), F32)],
        compiler_params=_params(),
    )


def _wgrad(a, b, a_spec, b_spec, n_out, acc_shape, out_shape, out_spec, store, name, comms=None):
    s = a.shape[-2]
    ts = min(TS_WGRAD, s)
    n_steps = s // ts

    def body(a_ref, b_ref, o_ref, acc_ref):
        k = pl.program_id(1)

        @pl.when(k == 0)
        def _():
            acc_ref[...] = jnp.zeros_like(acc_ref)

        acc_ref[...] += _dot_tn(a_ref[...], b_ref[...])

        @pl.when(k == n_steps - 1)
        def _():
            store(o_ref, acc_ref)

    (res,), extra = _call(
        body, comms, lambda: (pl.program_id(0) == 0) & (pl.program_id(1) == 0),
        lambda: (pl.program_id(0) == (3 * n_out) // 4) & (pl.program_id(1) == 0),
        lambda: (pl.program_id(0) == n_out - 1) & (pl.program_id(1) == n_steps - 1), (a, b),
        name=name, grid=(n_out, n_steps),
        in_specs=[a_spec(ts), b_spec(ts)], out_specs=[out_spec], out_shape=[_sds(out_shape, WIRE)],
        scratch_shapes=[pltpu.VMEM(acc_shape, F32)],
        compiler_params=_params(2),
    )
    return res, extra


def _store_plain(o_ref, acc_ref):
    o_ref[...] = acc_ref[...].astype(o_ref.dtype)


def _store_lane_blocks(o_ref, acc_ref):
    for d in range(N_DEV):
        o_ref[d] = acc_ref[:, d * NB_BR:(d + 1) * NB_BR].astype(o_ref.dtype)


def _wgrad_in(h, dp, name, comms=None):
    return _wgrad(h, dp, lambda ts: pl.BlockSpec((ts, D), lambda j, k: (k, 0)),
                  lambda ts: pl.BlockSpec((ts, NB_IN), lambda j, k: (k, j)), N_DEV, (D, NB_IN),
                  (N_DEV, D, NB_IN), pl.BlockSpec((None, D, NB_IN), lambda j, k: (j, 0, 0)), _store_plain, name, comms)


def _wgrad_up(h, du, name, comms=None):
    return _wgrad(h, du, lambda ts: pl.BlockSpec((ts, D), lambda j, k: (k, 0)),
                  lambda ts: pl.BlockSpec((None, ts, NB_UP), lambda j, k: (j, k, 0)), N_DEV, (D, NB_UP),
                  (N_DEV, D, NB_UP), pl.BlockSpec((None, D, NB_UP), lambda j, k: (j, 0, 0)), _store_plain, name, comms)


def _wgrad_down(act, dxo, name, comms=None):
    return _wgrad(act, dxo, lambda ts: pl.BlockSpec((None, ts, NB_UP), lambda j, k: (j, k, 0)),
                  lambda ts: pl.BlockSpec((ts, D), lambda j, k: (k, 0)), N_DEV // 2, (NB_UP, D),
                  (DFF, D), pl.BlockSpec((NB_UP, D), lambda j, k: (j, 0)), _store_plain, name, comms)


def _wgrad_o(merged, dxmid, name, comms=None):
    return _wgrad(merged, dxmid, lambda ts: pl.BlockSpec((ts, D), lambda j, k: (k, 0)),
                  lambda ts: pl.BlockSpec((ts, D), lambda j, k: (k, 0)), 1, (D, D),
                  (D, D), pl.BlockSpec((D, D), lambda j, k: (0, 0)), _store_plain, name, comms)


def _wgrad_branches(yabc, dbabc, name, comms=None):
    return _wgrad(yabc, dbabc, lambda ts: pl.BlockSpec((ts, WA), lambda j, k: (k, j)),
                  lambda ts: pl.BlockSpec((ts, D), lambda j, k: (k, j)), 3, (WA, D),
                  (N_DEV, 3, WA, NB_BR), pl.BlockSpec((N_DEV, None, WA, NB_BR), lambda j, k: (0, j, 0, 0)),
                  _store_lane_blocks, name, comms)


def _adamw_math(g, w, m, v):
    m = ADAM_B1 * m + (1.0 - ADAM_B1) * g
    v = ADAM_B2 * v + (1.0 - ADAM_B2) * (g * g)
    m_hat = m / (1.0 - ADAM_B1 ** ADAM_STEP)
    v_hat = v / (1.0 - ADAM_B2 ** ADAM_STEP)
    delta = -ADAM_LR * (m_hat / (jnp.sqrt(v_hat) + ADAM_EPS) + ADAM_WD * w)
    return delta, m, v


def _adamw_sum(parts, mid, w, m, v, layer, prev, tr, name, transposed=False):
    n_layers, r, c = w.shape[0], parts.shape[2], parts.shape[3]

    def body(p_ref, w_ref, m_ref, v_ref, *rest):
        g_ref, d_ref, mo_ref, vo_ref = rest[-4:]
        g = p_ref[0].astype(F32)
        for k in range(1, N_DEV):
            g = g + p_ref[k].astype(F32)
        if transposed:
            g = g.T
        g_ref[...] = g
        d_ref[...], mo_ref[...], vo_ref[...] = _adamw_math(g, w_ref[...], m_ref[...], v_ref[...])

    if transposed:
        blk = pl.BlockSpec((None, c, tr), lambda i: (layer, 0, i))
    else:
        blk = pl.BlockSpec((None, tr, c), lambda i: (layer, i, 0))
    extra = [] if prev is None else list(prev)
    return pl.pallas_call(
        body, name=name, grid=(r // tr,),
        in_specs=[pl.BlockSpec((N_DEV, None, tr, c), lambda i: (0, mid, i, 0)), blk, blk, blk]
        + [pl.BlockSpec(memory_space=pl.ANY)] * len(extra),
        out_specs=[blk] * 4, out_shape=[_sds(w.shape, F32)] * 4,
        input_output_aliases={4 + k: k for k in range(len(extra))},
        compiler_params=_params(),
    )(parts, w, m, v, *extra)


def _sum_parts(parts, name):
    _, r, c = parts.shape

    def body(p_ref, o_ref):
        g = p_ref[0]
        for k in range(1, N_DEV):
            g = g + p_ref[k]
        o_ref[...] = g

    return pl.pallas_call(body, name=name, out_shape=_sds((r, c), F32),
                          compiler_params=pltpu.CompilerParams(vmem_limit_bytes=VMEM_LIMIT))(parts)


def _adamw_small(g, w, m, v, name):
    def body(g_ref, w_ref, m_ref, v_ref, d_ref, mo_ref, vo_ref):
        d_ref[...], mo_ref[...], vo_ref[...] = _adamw_math(g_ref[...], w_ref[...], m_ref[...], v_ref[...])

    return pl.pallas_call(body, name=name, out_shape=[_sds(w.shape, F32)] * 3)(g, w, m, v)


HBM_SPEC = pl.BlockSpec(memory_space=pltpu.HBM)


def _position():
    return lax.axis_index("x"), lax.axis_index("y"), lax.axis_index("c")


def _device_index(chip, core):
    return 4 * chip[0] + 2 * chip[1] + core


def _gather(shards, layer):
    n = len(shards)
    per = 8

    def first_copies(ins, outs, send, recv):
        x, y, c = _position()
        me = 4 * x + 2 * y + c
        targets = [(x, y, 1 - c), (1 - x, y, c), (x, 1 - y, c), (1 - x, 1 - y, c)]
        remote = [pltpu.make_async_remote_copy(
            src_ref=ins[t].at[layer], dst_ref=outs[t].at[me], send_sem=send.at[per * t + k],
            recv_sem=recv.at[per * t + k], device_id=to, device_id_type=MESH)
            for t in range(n) for k, to in enumerate(targets)]
        local = [pltpu.make_async_copy(ins[t].at[layer], outs[t].at[me], send.at[per * t + 4]) for t in range(n)]
        return remote, local

    def passed_on(outs, send, recv):
        x, y, c = _position()
        chips = [(1 - x, y), (x, 1 - y), (1 - x, 1 - y)]
        return [pltpu.make_async_remote_copy(
            src_ref=outs[t].at[_device_index(chip, c)], dst_ref=outs[t].at[_device_index(chip, c)],
            send_sem=send.at[per * t + 5 + j], recv_sem=recv.at[per * t + 5 + j], device_id=(x, y, 1 - c),
            device_id_type=MESH)
            for t in range(n) for j, chip in enumerate(chips)]

    def start(ins, outs, send, recv):
        remote, local = first_copies(ins, outs, send, recv)
        for cp in local + remote:
            cp.start()

    def mid(ins, outs, send, recv):
        remote, local = first_copies(ins, outs, send, recv)
        for cp in remote:
            cp.wait()
        for cp in local:
            cp.wait()
        for cp in passed_on(outs, send, recv):
            cp.start()

    def finish(ins, outs, send, recv):
        for cp in passed_on(outs, send, recv):
            cp.wait()

    return _Comm(shards, [_sds((N_DEV,) + a.shape[1:], a.dtype) for a in shards], per * n, start, finish, mid)


def _run_comms(comms, name):
    def body():
        pass

    _, extra = _call(body, comms, None, None, None, (), name=name, in_specs=[], out_specs=[], out_shape=[])
    return extra


def _exchange(parts):
    n = len(parts)
    per = 8
    flips = [(0, 0, 1), (1, 0, 0), (1, 0, 1), (0, 1, 0), (0, 1, 1), (1, 1, 0), (1, 1, 1)]

    def copies(ins, outs, send, recv):
        x, y, c = _position()
        me = 4 * x + 2 * y + c
        remote = []
        for t in range(n):
            for k, (fx, fy, fc) in enumerate(flips):
                peer = ((1 - x if fx else x), (1 - y if fy else y), (1 - c if fc else c))
                remote.append(pltpu.make_async_remote_copy(
                    src_ref=ins[t].at[_device_index(peer[:2], peer[2])], dst_ref=outs[t].at[me],
                    send_sem=send.at[per * t + k], recv_sem=recv.at[per * t + k], device_id=peer, device_id_type=MESH))
        local = [pltpu.make_async_copy(ins[t].at[me], outs[t].at[me], send.at[per * t + 7]) for t in range(n)]
        return remote, local

    def start(ins, outs, send, recv):
        remote, local = copies(ins, outs, send, recv)
        for cp in local + remote:
            cp.start()

    def finish(ins, outs, send, recv):
        remote, local = copies(ins, outs, send, recv)
        for cp in remote:
            cp.wait()
        for cp in local:
            cp.wait()

    return _Comm(parts, [_sds(a.shape, a.dtype) for a in parts], per * n, start, finish)


def _rows128(a):
    return a.reshape(-1, 128)


def kernel(x, g_mix, w_in, w_pool, pool_scale, g_sgu, w_spatial, b_spatial, conv_c, w_branch_a, w_branch_b, w_branch_c, w_o, g_ffn, w_up, conv_ffn, conv_ffn_b, w_down, g_final, loss_target, m_g_mix, m_w_in, m_w_pool, m_pool_scale, m_g_sgu, m_w_spatial, m_b_spatial, m_conv_c, m_w_branch_a, m_w_branch_b, m_w_branch_c, m_w_o, m_g_ffn, m_w_up, m_conv_ffn, m_conv_ffn_b, m_w_down, m_g_final, v_g_mix, v_w_in, v_w_pool, v_pool_scale, v_g_sgu, v_w_spatial, v_b_spatial, v_conv_c, v_w_branch_a, v_w_branch_b, v_w_branch_c, v_w_o, v_g_ffn, v_w_up, v_conv_ffn, v_conv_ffn_b, v_w_down, v_g_final):
    s = x.shape[1]
    n_layers = g_mix.shape[0]
    x0 = x.reshape(s, D)
    target = loss_target.reshape(s, D)
    me = 4 * lax.axis_index("x") + 2 * lax.axis_index("y") + lax.axis_index("c")

    first_shards = [w_in.astype(MXU), conv_c]
    mix_shards = [w_branch_a.astype(MXU), w_branch_b.astype(MXU), w_branch_c.astype(MXU), w_o.astype(MXU),
                  w_down.astype(MXU)]
    up_shards = [w_up.astype(MXU), conv_ffn]
    (first_now,) = _run_comms([_gather(first_shards, 0)], "gather_first_0")
    mix_now = up_now = None
    wpool_b = w_pool.astype(MXU)
    bsp_t = jnp.swapaxes(b_spatial, 1, 2)
    convb_blk = conv_ffn_b.reshape(n_layers, N_DEV, NB_UP)

    saved = []
    weights = []
    xl = x0
    for l in range(n_layers):
        win8, convc8 = first_now
        convc_full = jnp.transpose(convc8, (1, 0, 2)).reshape(3, WA)
        more = l + 1 < n_layers
        mixer_args = (wpool_b[l], pool_scale[l:l + 1], g_sgu[l:l + 1], w_spatial[l], bsp_t[l], convc_full)
        if l == 0:
            (p, h), (mix_now,) = _rms_proj(xl, g_mix[l:l + 1], win8, f"in_proj_{l}", [_gather(mix_shards, l)])
            wa8, wb8, wc8, wo8, wd8 = mix_now
            (xmid, yabc, pacz, babc, merged), (up_now,) = _mixer_fwd(
                xl, p, *mixer_args, wa8, wb8, wc8, wo8, f"mixer_fwd_{l}", [_gather(up_shards, l)])
        else:
            wa8, wb8, wc8, wo8, wd8 = mix_now
            (xmid, yabc, pacz, babc, merged, p, h), (up_now,) = _mixer_fwd(
                xl, None, *mixer_args, wa8, wb8, wc8, wo8, f"mixer_fwd_{l}", [_gather(up_shards, l)],
                in_proj=(g_mix[l:l + 1], win8))
        wup8, convf8 = up_now
        weights.append((win8, wa8, wb8, wc8, wo8, wup8, wd8, convc_full, convf8))
        if more:
            (xout, h2, upre, up, act), (first_now, mix_now) = _ffn_block_fwd(
                xmid, g_ffn[l:l + 1], wup8, convf8, convb_blk[l], wd8, f"ffn_fwd_{l}",
                [_gather(first_shards, l + 1), _gather(mix_shards, l + 1)])
        else:
            (xout, h2, upre, up, act, dg_final, loss_local, dx_b), _ = _ffn_block_fwd(
                xmid, g_ffn[l:l + 1], wup8, convf8, convb_blk[l], wd8, f"ffn_fwd_{l}",
                loss_head=(g_final.reshape(1, D), target))
        saved.append((xl, p, h, xmid, yabc, pacz, babc, merged, upre, h2, act, up))
        xl = xout

    dx = xl

    received = [dict() for _ in range(n_layers)]
    small = {("final", "g_final"): dg_final}
    small_sums = {}
    waiting = None

    def exchange_of(named):
        return [_exchange([a for _, a in named])]

    def land(layer, named, got):
        received[layer].update({k: a for (k, _), a in zip(named, got[0])})

    def gather_small(keys):
        packed = jnp.concatenate([_rows128(small[k]) for k in keys], axis=0)[None]
        return _gather([packed], 0)

    def sum_small(keys, gathered, name):
        summed = _sum_parts(gathered, name)
        row = 0
        for k in keys:
            n_rows = small[k].size // 128
            small_sums[k] = summed[row:row + n_rows].reshape(small[k].shape)
            row += n_rows

    for l in reversed(range(n_layers)):
        xin, p, h, xmid, yabc, pacz, babc, merged, upre, h2, act, up = saved[l]
        win8, wa8, wb8, wc8, wo8, wup8, wd8, convc_full, convf8 = weights[l]
        last = l == 0
        (dupre, dconvf, dxmid, dg_ffn, dxmid_b), got = _ffn_block_bwd(
            dx, upre, up, convf8, wd8, wup8, xmid, g_ffn[l:l + 1], f"ffn_bwd_{l}",
            None if waiting is None else exchange_of(waiting[1]))
        if waiting is not None:
            land(waiting[0], waiting[1], got)
        small[(l, "conv_ffn")] = dconvf
        small[(l, "g_ffn")] = dg_ffn
        keys_a = [k for k in small if k not in small_sums]
        g_wdown, got = _wgrad_down(act, dx_b, f"wgrad_down_{l}", [gather_small(keys_a)] if last else None)
        if last:
            sum_small(keys_a, got[0][0], "sum_small_grads_a")
        down = [("w_down", g_wdown.reshape(N_DEV, ROWS_DN, D))]
        g_wup, got = _wgrad_up(h2, dupre, f"wgrad_up_{l}", exchange_of(down) if last else None)
        if last:
            land(l, down, got)
        upw = [("w_up", g_wup)]
        mixer_args = (dxmid, p, yabc, pacz, babc, wpool_b[l], pool_scale[l:l + 1], g_sgu[l:l + 1], w_spatial[l],
                      bsp_t[l], convc_full, wa8, wb8, wc8, wo8, f"mixer_bwd_{l}")
        if last:
            (dp, dbabc, dwp, dws, mixer_small, dbs), got = _mixer_bwd(*mixer_args, exchange_of(upw))
            land(l, upw, got)
        else:
            (dp, dbabc, dwp, dws, mixer_small, dbs, dx, dg_mix, dx_b), got = _mixer_bwd(
                *mixer_args, exchange_of(down + upw), in_proj=(win8, xin, g_mix[l:l + 1]))
            land(l, down + upw, got)
        small.update({(l, "w_pool"): dwp, (l, "mixer_small"): mixer_small, (l, "w_spatial"): dws,
                      (l, "b_spatial"): dbs})
        g_wo, _ = _wgrad_o(merged, dxmid_b, f"wgrad_o_{l}")
        g_br, _ = _wgrad_branches(yabc, dbabc, f"wgrad_branches_{l}")
        mixer_w = [("branches", g_br), ("w_o", g_wo.reshape(N_DEV, ROWS_O, D))]
        keys_b = [k for k in small if k not in small_sums]
        g_win, got = _wgrad_in(h, dp, f"wgrad_in_{l}",
                               exchange_of(mixer_w) + ([gather_small(keys_b)] if last else []))
        land(l, mixer_w, got)
        inw = [("w_in", g_win)]
        if last:
            sum_small(keys_b, got[1][0], "sum_small_grads_b")
            (dx, dg_mix), got = _proj_bwd(dp, win8, xin, g_mix[l:l + 1], dxmid, f"in_proj_bwd_{l}", exchange_of(inw))
            land(l, inw, got)
        else:
            waiting = (l, inw)
        small[(l, "g_mix")] = dg_mix
    grad_x = dx.reshape(1, s, D)
    late_keys = [k for k in small if k not in small_sums]
    (gathered_late,) = _run_comms([gather_small(late_keys)], "gather_last_small_grads")[0]
    sum_small(late_keys, gathered_late, "sum_last_small_grads")

    def update_big(key, mid, w, m, v, tr, tag, transposed=False):
        swap = (lambda a: jnp.swapaxes(a, 1, 2)) if transposed else (lambda a: a)
        w, m, v = swap(w), swap(m), swap(v)
        outs = None
        for l in range(n_layers):
            parts = received[l][key]
            if parts.ndim == 3:
                parts = parts.reshape(N_DEV, 1, *parts.shape[1:])
            outs = _adamw_sum(parts, mid, w, m, v, l, outs, tr, f"adamw_{tag}_{l}", transposed)
        return [swap(o) for o in outs]

    up_in = update_big("w_in", 0, w_in, m_w_in, v_w_in, 256, "w_in")
    up_a = update_big("branches", 0, w_branch_a, m_w_branch_a, v_w_branch_a, WA, "w_branch_a")
    up_b = update_big("branches", 1, w_branch_b, m_w_branch_b, v_w_branch_b, WA, "w_branch_b")
    up_c = update_big("branches", 2, w_branch_c, m_w_branch_c, v_w_branch_c, WA, "w_branch_c")
    up_o = update_big("w_o", 0, w_o, m_w_o, v_w_o, ROWS_O, "w_o")
    up_up = update_big("w_up", 0, w_up, m_w_up, v_w_up, 256, "w_up", transposed=True)
    up_down = update_big("w_down", 0, w_down, m_w_down, v_w_down, ROWS_DN, "w_down")

    stack = lambda kind: jnp.stack([small_sums[(l, kind)] for l in range(n_layers)], axis=0)
    grad_g_mix = stack("g_mix")[:, 0, :]
    grad_w_pool = stack("w_pool")
    mixer_sums = stack("mixer_small")
    grad_pool_scale = mixer_sums[:, 0, :]
    grad_g_sgu = mixer_sums[:, 1, :]
    grad_conv_c = lax.dynamic_slice_in_dim(mixer_sums[:, 2:5, :], me * (WA // N_DEV), WA // N_DEV, axis=2)
    grad_w_spatial = stack("w_spatial")
    grad_b_spatial = stack("b_spatial")[:, 0:HEADS, :]
    grad_g_ffn = stack("g_ffn")[:, 0, :]
    conv_grads = stack("conv_ffn")
    grad_conv_ffn = lax.dynamic_index_in_dim(conv_grads, me, axis=1, keepdims=False)[:, 0:3, :]
    grad_conv_ffn_b = conv_grads[:, :, 3, :].reshape(n_layers, 2 * DFF)
    grad_g_final = small_sums[("final", "g_final")][0]

    def update_small(g, w, m, v, tag):
        shape2 = (-1, w.shape[-1])
        outs = _adamw_small(g.reshape(shape2), w.reshape(shape2), m.reshape(shape2), v.reshape(shape2), f"adamw_{tag}")
        return [g] + [o.reshape(w.shape) for o in outs]

    up = {
        "g_mix": update_small(grad_g_mix, g_mix, m_g_mix, v_g_mix, "g_mix"),
        "w_in": up_in,
        "w_pool": update_small(grad_w_pool, w_pool, m_w_pool, v_w_pool, "w_pool"),
        "pool_scale": update_small(grad_pool_scale, pool_scale, m_pool_scale, v_pool_scale, "pool_scale"),
        "g_sgu": update_small(grad_g_sgu, g_sgu, m_g_sgu, v_g_sgu, "g_sgu"),
        "w_spatial": update_small(grad_w_spatial, w_spatial, m_w_spatial, v_w_spatial, "w_spatial"),
        "b_spatial": update_small(grad_b_spatial, b_spatial, m_b_spatial, v_b_spatial, "b_spatial"),
        "conv_c": update_small(grad_conv_c, conv_c, m_conv_c, v_conv_c, "conv_c"),
        "w_branch_a": up_a,
        "w_branch_b": up_b,
        "w_branch_c": up_c,
        "w_o": up_o,
        "g_ffn": update_small(grad_g_ffn, g_ffn, m_g_ffn, v_g_ffn, "g_ffn"),
        "w_up": up_up,
        "conv_ffn": update_small(grad_conv_ffn, conv_ffn, m_conv_ffn, v_conv_ffn, "conv_ffn"),
        "conv_ffn_b": update_small(grad_conv_ffn_b, conv_ffn_b, m_conv_ffn_b, v_conv_ffn_b, "conv_ffn_b"),
        "w_down": up_down,
        "g_final": update_small(grad_g_final, g_final, m_g_final, v_g_final, "g_final"),
    }
    loss = lax.psum(loss_local[0, 0], AXES)
    order = list(up)
    return (loss, grad_x, *[up[k][0] for k in order], *[up[k][1] for k in order], *[up[k][2] for k in order],
            *[up[k][3] for k in order])
```

```python
import jax
import jax.numpy as jnp
from jax import lax
from jax.experimental import pallas as pl
from jax.experimental.pallas import tpu as pltpu

F32 = jnp.float32
BF16 = jnp.bfloat16
MXU = BF16
ACT = BF16
WIRE = BF16

N_DEV = 8
D = 1024
WA = 512
NCOL = 6144
DFF = 2816
NB_IN = NCOL // N_DEV
NB_UP = 2 * DFF // N_DEV
NB_BR = D // N_DEV
ROWS_O = D // N_DEV
ROWS_DN = DFF // N_DEV
CHUNK = 128
HEADS = 4
POOL_WINDOWS = (2, 4, 8, 16)
EPS = 1e-6
A0, UV0, CB0, CC0, CX0, GA0, GB0, GC0 = 0, 512, 1536, 2048, 2560, 3072, 4096, 5120

ADAM_LR = 0.001
ADAM_B1 = 0.9
ADAM_B2 = 0.999
ADAM_EPS = 1e-08
ADAM_WD = 0.01
ADAM_STEP = 10

TM = 256
TM_PROJ = 512
TS_WGRAD = 4096
HALO_POOL = 16
HALO_CONV = 8
VMEM_LIMIT = 56 * 1024 * 1024
MESH = pl.DeviceIdType.MESH
AXES = ("x", "y", "c")


def _sds(shape, dtype):
    return jax.ShapeDtypeStruct(tuple(shape), dtype)


def _params(n_grid=1):
    return pltpu.CompilerParams(dimension_semantics=("arbitrary",) * n_grid, vmem_limit_bytes=VMEM_LIMIT)


def _const(block, index):
    return pl.BlockSpec(block, lambda *_: index, pipeline_mode=pl.Buffered(1))


def _dot(a, b):
    return jnp.dot(a, b, preferred_element_type=F32)


def _dot_nt(a, b):
    return lax.dot_general(a, b, (((1,), (1,)), ((), ())), preferred_element_type=F32)


def _dot_tn(a, b):
    return lax.dot_general(a, b, (((0,), (0,)), ((), ())), preferred_element_type=F32)


def _sigmoid(v):
    return 0.5 * jnp.tanh(0.5 * v) + 0.5


def _shift_down(v, k):
    return pltpu.roll(v, k, axis=0)


def _shift_up(v, k):
    return pltpu.roll(v, v.shape[0] - k, axis=0)


def _colsum(v):
    return jnp.sum(v, axis=0, keepdims=True)


def _lane_cat(ref):
    return jnp.concatenate([ref[d] for d in range(N_DEV)], axis=1)


class _Comm:
    def __init__(self, operands, out_shapes, n_sems, start, finish, mid=None):
        self.operands = list(operands)
        self.out_shapes = list(out_shapes)
        self.n_sems = n_sems
        self.start = start
        self.mid = mid
        self.finish = finish


def _call(body, comms, is_first, is_mid, is_last, operands, *, in_specs, out_specs, out_shape, scratch_shapes=(), **kw):
    n_in, n_out, n_scr = len(in_specs), len(out_specs), len(scratch_shapes)
    comms = [c for c in (comms or []) if c is not None]
    if not comms:
        res = pl.pallas_call(body, in_specs=in_specs, out_specs=out_specs, out_shape=out_shape,
                             scratch_shapes=list(scratch_shapes), **kw)(*operands)
        return res, []
    nci = [len(c.operands) for c in comms]
    nco = [len(c.out_shapes) for c in comms]

    def split(refs, sizes):
        parts = []
        for n in sizes:
            parts.append(refs[:n])
            refs = refs[n:]
        return parts, refs

    def carrier(*refs):
        ins, refs = refs[:n_in], refs[n_in:]
        cins, refs = split(refs, nci)
        outs, refs = refs[:n_out], refs[n_out:]
        couts, refs = split(refs, nco)
        scr, sems = refs[:n_scr], refs[n_scr:]

        def run(step):
            for k, c in enumerate(comms):
                if getattr(c, step) is not None:
                    getattr(c, step)(cins[k], couts[k], sems[2 * k], sems[2 * k + 1])

        def at(mark, step):
            if mark is None:
                run(step)
            else:
                pl.when(mark())(lambda: run(step))

        at(is_first, "start")
        body(*ins, *outs, *scr)
        at(is_mid, "mid")
        at(is_last, "finish")

    res = pl.pallas_call(
        carrier, in_specs=list(in_specs) + [HBM_SPEC] * sum(nci), out_specs=list(out_specs) + [HBM_SPEC] * sum(nco),
        out_shape=list(out_shape) + [s for c in comms for s in c.out_shapes],
        scratch_shapes=list(scratch_shapes) + [pltpu.SemaphoreType.DMA((c.n_sems,)) for c in comms for _ in range(2)],
        **kw,
    )(*operands, *[a for c in comms for a in c.operands])
    extra, _ = split(res[n_out:], nco)
    return res[:n_out], extra


def _grid_marks(nt):
    return (lambda: pl.program_id(0) == 0), (lambda: pl.program_id(0) == (3 * nt) // 4), (lambda: pl.program_id(0) == nt - 1)


def _rms_proj(x, g, w_all, name, comms=None):
    s = x.shape[0]
    nb = w_all.shape[-1]
    tm = min(TM_PROJ, s)
    nt = s // tm

    def body(x_ref, g_ref, w_ref, p_ref, h_ref):
        xf = x_ref[...]
        r = lax.rsqrt(jnp.mean(xf * xf, axis=-1, keepdims=True) + EPS)
        h = (xf * r * g_ref[...]).astype(MXU)
        h_ref[...] = h
        for j in range(N_DEV):
            p_ref[:, j * nb:(j + 1) * nb] = _dot(h, w_ref[j]).astype(p_ref.dtype)

    row = lambda n: pl.BlockSpec((tm, n), lambda i: (i, 0))
    return _call(
        body, comms, *_grid_marks(nt), (x, g, w_all),
        name=name, grid=(nt,),
        in_specs=[row(D), _const((1, D), (0, 0)), _const((N_DEV, D, nb), (0, 0, 0))],
        out_specs=[row(N_DEV * nb), row(D)],
        out_shape=[_sds((s, N_DEV * nb), ACT), _sds((s, D), MXU)],
        compiler_params=_params(),
    )


def _tril_mask():
    r = lax.broadcasted_iota(jnp.int32, (CHUNK, CHUNK), 0)
    c = lax.broadcasted_iota(jnp.int32, (CHUNK, CHUNK), 1)
    return r >= c


def _gelu_parts(v):
    c0 = 0.7978845608028654
    th = jnp.tanh(c0 * (v + 0.044715 * (v * v * v)))
    cdf = 0.5 * (1.0 + th)
    dgelu = cdf + v * (0.5 * c0) * (1.0 - th * th) * (1.0 + 3.0 * 0.044715 * (v * v))
    return v * cdf, dgelu


def _mixer_fwd(x, p, wpool, pscale, gsgu, wsp, bsp_t, convc, wa_all, wb_all, wc_all, wo_all, name, comms=None,
               in_proj=None):
    s = x.shape[0]
    nt = s // TM
    fused = in_proj is not None

    def body(*refs):
        if fused:
            (x_ref, g_ref, win_ref, wpool_ref, ps_ref, gs_ref, wsp_ref, bsp_ref, cc_ref, wa_ref, wb_ref, wc_ref, wo_ref,
             xmid_ref, y_ref, pz_ref, b_ref, m_ref, p_ref, h_ref, carry_a, carry_z) = refs
        else:
            (x_ref, p_ref, wpool_ref, ps_ref, gs_ref, wsp_ref, bsp_ref, cc_ref, wa_ref, wb_ref, wc_ref, wo_ref,
             xmid_ref, y_ref, pz_ref, b_ref, m_ref, carry_a, carry_z) = refs
        i = pl.program_id(0)

        @pl.when(i == 0)
        def _():
            carry_a[...] = jnp.zeros_like(carry_a)
            carry_z[...] = jnp.zeros_like(carry_z)

        def pf(lo, n):
            return p_ref[:, lo:lo + n].astype(F32)

        def project(blocks):
            if fused:
                for j in blocks:
                    p_ref[:, j * NB_IN:(j + 1) * NB_IN] = _dot(h_ref[...], win_ref[j]).astype(ACT)

        if fused:
            xf = x_ref[...]
            r = lax.rsqrt(jnp.mean(xf * xf, axis=-1, keepdims=True) + EPS)
            h_ref[...] = (xf * r * g_ref[...]).astype(MXU)
        project((0, 1, 2))

        a = pf(A0, WA)
        ext = jnp.concatenate([carry_a[...], a], axis=0)
        carry_a[...] = a[TM - HALO_POOL:, :]
        t_pos = (i * TM + lax.broadcasted_iota(jnp.int32, (TM, 1), 0)).astype(F32)
        for g, win in enumerate(POOL_WINDOWS):
            cols = slice(g * CHUNK, (g + 1) * CHUNK)
            acc = ext[:, cols]
            k = 1
            while k < win:
                acc = acc + _shift_down(acc, k)
                k *= 2
            cnt = jnp.minimum(t_pos + 1.0, float(win))
            pa_g = (acc[HALO_POOL:, :] / cnt - a[:, cols]).astype(MXU)
            pz_ref[:, cols] = pa_g
            y_ref[:, cols] = (_dot(pa_g, wpool_ref[g]) * ps_ref[:, cols]).astype(ACT)

        project((3,))
        uvg, _ = _gelu_parts(pf(UV0, 2 * WA))
        u = uvg[:, :WA]
        v = uvg[:, WA:]
        rv = lax.rsqrt(jnp.mean(v * v, axis=-1, keepdims=True) + EPS)
        vn = (v * rv * gs_ref[...]).astype(MXU)
        mask = _tril_mask()
        for g in range(HEADS):
            cols = slice(g * CHUNK, (g + 1) * CHUNK)
            wt = jnp.where(mask, wsp_ref[g], 0.0).astype(MXU)
            bcol = bsp_ref[:, g:g + 1]
            for c in range(TM // CHUNK):
                rows = slice(c * CHUNK, (c + 1) * CHUNK)
                sv = _dot(wt, vn[rows, cols]) + bcol
                y_ref[rows, WA + g * CHUNK:WA + (g + 1) * CHUNK] = (u[rows, cols] * sv).astype(ACT)

        project((4, 5))
        z = pf(CC0, WA) * pf(CX0, WA)
        extz = jnp.concatenate([carry_z[...], z], axis=0)
        carry_z[...] = z[TM - HALO_CONV:, :]
        cz = (cc_ref[0:1, :] * _shift_down(extz, 2)[HALO_CONV:, :]
              + cc_ref[1:2, :] * _shift_down(extz, 1)[HALO_CONV:, :] + cc_ref[2:3, :] * z)
        pz_ref[:, WA:2 * WA] = cz.astype(ACT)
        y_ref[:, 2 * WA:3 * WA] = (pf(CB0, WA) * cz).astype(ACT)

        project((6, 7))
        merged = jnp.zeros((TM, D), F32)
        for k, (w_ref, glo) in enumerate(((wa_ref, GA0), (wb_ref, GB0), (wc_ref, GC0))):
            br = _dot(y_ref[:, k * WA:(k + 1) * WA], _lane_cat(w_ref))
            b_ref[:, k * D:(k + 1) * D] = br.astype(ACT)
            merged = merged + _sigmoid(pf(glo, D)) * br
        mb = merged.astype(MXU)
        m_ref[...] = mb
        xmid_ref[...] = x_ref[...] + _dot(mb, wo_ref[...].reshape(D, D))

    row = lambda n: pl.BlockSpec((TM, n), lambda i: (i, 0))
    br_spec = _const((N_DEV, WA, NB_BR), (0, 0, 0))
    if fused:
        lead_specs = [row(D), _const((1, D), (0, 0)), _const((N_DEV, D, NB_IN), (0, 0, 0))]
        lead = (x,) + tuple(in_proj)
    else:
        lead_specs = [row(D), row(NCOL)]
        lead = (x, p)
    more_specs = [row(NCOL), row(D)] if fused else []
    more_shapes = [_sds((s, NCOL), ACT), _sds((s, D), MXU)] if fused else []
    return _call(
        body, comms, *_grid_marks(nt), lead + (wpool, pscale, gsgu, wsp, bsp_t, convc, wa_all, wb_all, wc_all, wo_all),
        name=name, grid=(nt,),
        in_specs=lead_specs + [_const((HEADS, CHUNK, CHUNK), (0, 0, 0)), _const((1, WA), (0, 0)),
                               _const((1, WA), (0, 0)), _const((HEADS, CHUNK, CHUNK), (0, 0, 0)),
                               _const((CHUNK, HEADS), (0, 0)), _const((3, WA), (0, 0)), br_spec, br_spec, br_spec,
                               _const((N_DEV, ROWS_O, D), (0, 0, 0))],
        out_specs=[row(D), row(3 * WA), row(2 * WA), row(3 * D), row(D)] + more_specs,
        out_shape=[_sds((s, D), F32), _sds((s, 3 * WA), ACT), _sds((s, 2 * WA), ACT), _sds((s, 3 * D), ACT),
                   _sds((s, D), MXU)] + more_shapes,
        scratch_shapes=[pltpu.VMEM((HALO_POOL, WA), F32), pltpu.VMEM((HALO_CONV, WA), F32)],
        compiler_params=_params(),
    )


def _conv_up(ext, cur, w_ref, j, b_row):
    return (w_ref[j, 0:1, :] * _shift_down(ext, 2)[HALO_CONV:, :] + w_ref[j, 1:2, :] * _shift_down(ext, 1)[HALO_CONV:, :]
            + w_ref[j, 2:3, :] * cur + b_row)


def _ffn_block_fwd(xmid, g, wup_all, convf_all, convb, wd_all, name, comms=None, loss_head=None):
    s = xmid.shape[0]
    nt = s // TM
    half = N_DEV // 2
    with_loss = loss_head is not None

    def body(*refs):
        (x_ref, g_ref, wup_ref, cw_ref, cb_ref, wd_ref), refs = refs[:6], refs[6:]
        if with_loss:
            (gf_ref, t_ref), refs = refs[:2], refs[2:]
        (xo_ref, h_ref, u_ref, up_ref, act_ref), refs = refs[:5], refs[5:]
        if with_loss:
            (dgf_ref, loss_ref, dxb_ref), refs = refs[:3], refs[3:]
        (carry,) = refs
        i = pl.program_id(0)

        @pl.when(i == 0)
        def _():
            carry[...] = jnp.zeros_like(carry)
            if with_loss:
                dgf_ref[...] = jnp.zeros_like(dgf_ref)
                loss_ref[...] = jnp.zeros_like(loss_ref)

        xf = x_ref[...]
        r = lax.rsqrt(jnp.mean(xf * xf, axis=-1, keepdims=True) + EPS)
        h = (xf * r * g_ref[...]).astype(MXU)
        h_ref[...] = h

        def project(j):
            return _dot(h, wup_ref[j]).astype(ACT)

        def conv(j, pre):
            u_ref[j] = pre
            cur = pre.astype(F32)
            ext = jnp.concatenate([carry[j], cur], axis=0)
            carry[j] = cur[TM - HALO_CONV:, :]
            up = _conv_up(ext, cur, cw_ref, j, cb_ref[j:j + 1, :])
            up_ref[j] = up.astype(ACT)
            return up

        order = [j + k * half for j in range(half) for k in range(2)]
        acc = xf
        ahead = 2
        pres = {n: project(order[n]) for n in range(ahead)}
        ups = {}
        for n, j in enumerate(order):
            if n + ahead < N_DEV:
                pres[n + ahead] = project(order[n + ahead])
            ups[j] = conv(j, pres.pop(n))
            if j >= half:
                gate, val = ups.pop(j - half), ups.pop(j)
                act = (gate * _sigmoid(gate) * val).astype(MXU)
                act_ref[j - half] = act
                wd = jnp.concatenate([wd_ref[2 * (j - half)], wd_ref[2 * (j - half) + 1]], axis=0)
                acc = acc + _dot(act, wd)
        if not with_loss:
            xo_ref[...] = acc
        else:
            ro = lax.rsqrt(jnp.mean(acc * acc, axis=-1, keepdims=True) + EPS)
            xn = acc * ro
            err = xn * gf_ref[...] - t_ref[...]
            loss_ref[...] += 0.5 * jnp.sum(jnp.mean(err * err, axis=-1, keepdims=True), axis=0, keepdims=True)
            dy = err * (1.0 / D)
            dgf_ref[0:1, :] += _colsum(dy * xn)
            dyg = dy * gf_ref[...]
            dxo = ro * (dyg - xn * jnp.mean(dyg * xn, axis=-1, keepdims=True))
            xo_ref[...] = dxo
            dxb_ref[...] = dxo.astype(MXU)

    row = pl.BlockSpec((TM, D), lambda i: (i, 0))
    blocks = pl.BlockSpec((N_DEV, TM, NB_UP), lambda i: (0, i, 0))
    return _call(
        body, comms, *_grid_marks(nt), (xmid, g, wup_all, convf_all, convb, wd_all) + (tuple(loss_head) if with_loss else ()),
        name=name, grid=(nt,),
        in_specs=[row, _const((1, D), (0, 0)), _const((N_DEV, D, NB_UP), (0, 0, 0)),
                  _const((N_DEV, 3, NB_UP), (0, 0, 0)), _const((N_DEV, NB_UP), (0, 0)),
                  _const((N_DEV, ROWS_DN, D), (0, 0, 0))] + ([_const((1, D), (0, 0)), row] if with_loss else []),
        out_specs=[row, row, blocks, blocks, pl.BlockSpec((half, TM, NB_UP), lambda i: (0, i, 0))]
        + ([pl.BlockSpec((8, D), lambda i: (0, 0)), pl.BlockSpec((1, 1), lambda i: (0, 0)), row] if with_loss else []),
        out_shape=[_sds((s, D), F32), _sds((s, D), MXU), _sds((N_DEV, s, NB_UP), ACT), _sds((N_DEV, s, NB_UP), ACT),
                   _sds((half, s, NB_UP), MXU)]
        + ([_sds((8, D), F32), _sds((1, 1), F32), _sds((s, D), MXU)] if with_loss else []),
        scratch_shapes=[pltpu.VMEM((N_DEV, HALO_CONV, NB_UP), F32)],
        compiler_params=_params(),
    )


def _ffn_block_bwd(dxo, upre, up, convf_all, wd_all, wup_all, xmid, g, name, comms=None):
    s = dxo.shape[0]
    nt = s // TM
    half = N_DEV // 2

    def body(dx_ref, u_ref, up_ref, cw_ref, wd_ref, wup_ref, x_ref, g_ref, du_ref, dc_ref, dxm_ref, dg_ref, dxmb_ref,
             carry):
        step = pl.program_id(0)

        @pl.when(step == 0)
        def _():
            carry[...] = jnp.zeros_like(carry)
            dc_ref[...] = jnp.zeros_like(dc_ref)
            dg_ref[...] = jnp.zeros_like(dg_ref)

        dxo_t = dx_ref[...]
        dxb = dxo_t.astype(MXU)

        def adjoint(j, d_up):
            cur = u_ref[j].astype(F32)
            ext = jnp.concatenate([d_up, carry[j]], axis=0)
            carry[j] = d_up[:HALO_CONV, :]
            up1 = _shift_up(ext, 1)[:TM, :]
            up2 = _shift_up(ext, 2)[:TM, :]
            du = (cw_ref[j, 2:3, :] * d_up + cw_ref[j, 1:2, :] * up1 + cw_ref[j, 0:1, :] * up2).astype(du_ref.dtype)
            du_ref[j] = du
            dc_ref[j, 0:1, :] += _colsum(cur * up2)
            dc_ref[j, 1:2, :] += _colsum(cur * up1)
            dc_ref[j, 2:3, :] += _colsum(cur * d_up)
            dc_ref[j, 3:4, :] += _colsum(d_up)
            return _dot_nt(du, wup_ref[j])

        def d_act(j):
            return _dot_nt(dxb, jnp.concatenate([wd_ref[2 * j], wd_ref[2 * j + 1]], axis=0))

        dh = jnp.zeros((TM, D), F32)
        dact = d_act(0)
        for j in range(half):
            nxt = d_act(j + 1) if j + 1 < half else None
            gate = up_ref[j].astype(F32)
            val = up_ref[j + half].astype(F32)
            sg = _sigmoid(gate)
            dh = dh + adjoint(j, dact * val * sg * (1.0 + gate * (1.0 - sg)))
            dh = dh + adjoint(j + half, dact * gate * sg)
            dact = nxt

        xf = x_ref[...]
        r = lax.rsqrt(jnp.mean(xf * xf, axis=-1, keepdims=True) + EPS)
        xn = xf * r
        dg_ref[0:1, :] += _colsum(dh * xn)
        dhg = dh * g_ref[...]
        dxm = dxo_t + r * (dhg - xn * jnp.mean(dhg * xn, axis=-1, keepdims=True))
        dxm_ref[...] = dxm
        dxmb_ref[...] = dxm.astype(MXU)

    blocks = pl.BlockSpec((N_DEV, TM, NB_UP), lambda i: (0, nt - 1 - i, 0))
    row = pl.BlockSpec((TM, D), lambda i: (nt - 1 - i, 0))
    return _call(
        body, comms, *_grid_marks(nt), (dxo, upre, up, convf_all, wd_all, wup_all, xmid, g),
        name=name, grid=(nt,),
        in_specs=[row, blocks, blocks, _const((N_DEV, 3, NB_UP), (0, 0, 0)), _const((N_DEV, ROWS_DN, D), (0, 0, 0)),
                  _const((N_DEV, D, NB_UP), (0, 0, 0)), row, _const((1, D), (0, 0))],
        out_specs=[blocks, pl.BlockSpec((N_DEV, 8, NB_UP), lambda i: (0, 0, 0)), row,
                   pl.BlockSpec((8, D), lambda i: (0, 0)), row],
        out_shape=[_sds((N_DEV, s, NB_UP), MXU), _sds((N_DEV, 8, NB_UP), F32), _sds((s, D), F32), _sds((8, D), F32),
                   _sds((s, D), MXU)],
        scratch_shapes=[pltpu.VMEM((N_DEV, HALO_CONV, NB_UP), F32)],
        compiler_params=_params(),
    )


def _proj_bwd(dy, w_all, x, g, dres, name, comms=None):
    s = x.shape[0]
    nb = w_all.shape[-1]
    tm = TM
    nt = s // tm

    def body(dy_ref, w_ref, x_ref, g_ref, dres_ref, dx_ref, dg_ref):
        i = pl.program_id(0)

        @pl.when(i == 0)
        def _():
            dg_ref[...] = jnp.zeros_like(dg_ref)

        dh = jnp.zeros((tm, D), F32)
        for j in range(N_DEV):
            dh = dh + _dot_nt(dy_ref[:, j * nb:(j + 1) * nb], w_ref[j])
        xf = x_ref[...]
        r = lax.rsqrt(jnp.mean(xf * xf, axis=-1, keepdims=True) + EPS)
        xn = xf * r
        dg_ref[0:1, :] += _colsum(dh * xn)
        dhg = dh * g_ref[...]
        dx_ref[...] = dres_ref[...] + r * (dhg - xn * jnp.mean(dhg * xn, axis=-1, keepdims=True))

    dy_spec = pl.BlockSpec((tm, N_DEV * nb), lambda i: (i, 0))
    row = pl.BlockSpec((tm, D), lambda i: (i, 0))
    return _call(
        body, comms, *_grid_marks(nt), (dy, w_all, x, g, dres),
        name=name, grid=(nt,),
        in_specs=[dy_spec, _const((N_DEV, D, nb), (0, 0, 0)), row, _const((1, D), (0, 0)), row],
        out_specs=[row, pl.BlockSpec((8, D), lambda i: (0, 0))],
        out_shape=[_sds((s, D), F32), _sds((8, D), F32)],
        compiler_params=_params(),
    )


def _mixer_bwd(dxmid, p, yabc, pacz, babc, wpool, pscale, gsgu, wsp, bsp_t, convc, wa_all, wb_all, wc_all, wo_all,
               name, comms=None, in_proj=None):
    s = dxmid.shape[0]
    nt = s // TM
    fused = in_proj is not None

    def body(*refs):
        (dx_ref, p_ref, y_ref, pz_ref, b_ref, wpool_ref, ps_ref, gs_ref, wsp_ref, bsp_ref, cc_ref,
         wa_ref, wb_ref, wc_ref, wo_ref) = refs[:15]
        refs = refs[15:]
        if fused:
            (win_ref, x_ref, g_ref), refs = refs[:3], refs[3:]
        (dp_ref, db_ref, dwp_ref, dws_ref, small_ref, dbs_ref), refs = refs[:6], refs[6:]
        if fused:
            (dxin_ref, dg_ref, dxinb_ref), refs = refs[:3], refs[3:]
        carry_pa, carry_cz, dbs_acc, du_s, dvn_s = refs
        step = pl.program_id(0)
        tile = nt - 1 - step

        @pl.when(step == 0)
        def _():
            for ref in (carry_pa, carry_cz, dbs_acc, dwp_ref, dws_ref, small_ref, dbs_ref) + ((dg_ref,) if fused else ()):
                ref[...] = jnp.zeros_like(ref)

        def pf(lo, n):
            return p_ref[:, lo:lo + n].astype(F32)

        def back_project(blocks):
            return sum(_dot_nt(dp_ref[:, j * NB_IN:(j + 1) * NB_IN], win_ref[j]) for j in blocks)

        dxm = dx_ref[...]
        dm = _dot_nt(dxm.astype(MXU), wo_ref[...].reshape(D, D))

        def through_gate(k, glo, w_ref):
            sg = _sigmoid(pf(glo, D))
            br = b_ref[:, k * D:(k + 1) * D].astype(F32)
            dp_ref[:, glo:glo + D] = (dm * br * sg * (1.0 - sg)).astype(dp_ref.dtype)
            dbr = (dm * sg).astype(MXU)
            db_ref[:, k * D:(k + 1) * D] = dbr
            return _dot_nt(dbr, _lane_cat(w_ref))

        dya = through_gate(0, GA0, wa_ref)
        dyb = through_gate(1, GB0, wb_ref)
        dyc = through_gate(2, GC0, wc_ref)
        if fused:
            dh = back_project((4, 5, 6, 7))

        t_pos = (tile * TM + lax.broadcasted_iota(jnp.int32, (TM, 1), 0)).astype(F32)
        for g, win in enumerate(POOL_WINDOWS):
            cols = slice(g * CHUNK, (g + 1) * CHUNK)
            pa_g = pz_ref[:, cols]
            q = _dot(pa_g, wpool_ref[g])
            dya_g = dya[:, cols]
            small_ref[0:1, cols] += _colsum(dya_g * q)
            dq = (dya_g * ps_ref[:, cols]).astype(MXU)
            dpa_g = _dot_nt(dq, wpool_ref[g])
            dwp_ref[g] += _dot_tn(pa_g, dq)
            dpw = dpa_g / jnp.minimum(t_pos + 1.0, float(win))
            acc = jnp.concatenate([dpw, carry_pa[:, cols]], axis=0)
            carry_pa[:, cols] = dpw[:HALO_POOL, :]
            k = 1
            while k < win:
                acc = acc + _shift_up(acc, k)
                k *= 2
            dp_ref[:, cols] = (acc[:TM, :] - dpa_g).astype(dp_ref.dtype)

        uvp = pf(UV0, 2 * WA)
        uvg, dgelu = _gelu_parts(uvp)
        u = uvg[:, :WA]
        v = uvg[:, WA:]
        rv = lax.rsqrt(jnp.mean(v * v, axis=-1, keepdims=True) + EPS)
        vh = v * rv
        vn = (vh * gs_ref[...]).astype(MXU)
        mask = _tril_mask()
        for g in range(HEADS):
            cols = slice(g * CHUNK, (g + 1) * CHUNK)
            wt32 = jnp.where(mask, wsp_ref[g], 0.0)
            wt = wt32.astype(MXU)
            wt_t = wt32.T.astype(MXU)
            bcol = bsp_ref[:, g:g + 1]
            for c in range(TM // CHUNK):
                rows = slice(c * CHUNK, (c + 1) * CHUNK)
                vn_cg = vn[rows, cols]
                sv = _dot(wt, vn_cg) + bcol
                dyb_cg = dyb[rows, cols]
                du_s[rows, cols] = dyb_cg * sv
                dsv = dyb_cg * u[rows, cols]
                dbs_acc[g] += dsv
                dsv_b = dsv.astype(MXU)
                dws_ref[g] += _dot_nt(dsv_b, vn_cg)
                dvn_s[rows, cols] = _dot(wt_t, dsv_b)
        dvn = dvn_s[...]
        small_ref[1:2, :] += _colsum(dvn * vh)
        dvg = dvn * gs_ref[...]
        dv = rv * (dvg - vh * jnp.mean(dvg * vh, axis=-1, keepdims=True))
        dp_ref[:, UV0:UV0 + WA] = (du_s[...] * dgelu[:, :WA]).astype(dp_ref.dtype)
        dp_ref[:, UV0 + WA:UV0 + 2 * WA] = (dv * dgelu[:, WA:]).astype(dp_ref.dtype)
        if fused:
            dh = dh + back_project((0, 1))

        cb = pf(CB0, WA)
        cc = pf(CC0, WA)
        cx = pf(CX0, WA)
        z = cc * cx
        dp_ref[:, CB0:CB0 + WA] = (dyc * pz_ref[:, WA:2 * WA].astype(F32)).astype(dp_ref.dtype)
        dcz = dyc * cb
        extz = jnp.concatenate([dcz, carry_cz[...]], axis=0)
        carry_cz[...] = dcz[:HALO_CONV, :]
        up1 = _shift_up(extz, 1)[:TM, :]
        up2 = _shift_up(extz, 2)[:TM, :]
        dz = cc_ref[2:3, :] * dcz + cc_ref[1:2, :] * up1 + cc_ref[0:1, :] * up2
        small_ref[2:3, :] += _colsum(z * up2)
        small_ref[3:4, :] += _colsum(z * up1)
        small_ref[4:5, :] += _colsum(z * dcz)
        dp_ref[:, CC0:CC0 + WA] = (dz * cx).astype(dp_ref.dtype)
        dp_ref[:, CX0:CX0 + WA] = (dz * cc).astype(dp_ref.dtype)

        if fused:
            dh = dh + back_project((2, 3))
            xf = x_ref[...]
            r = lax.rsqrt(jnp.mean(xf * xf, axis=-1, keepdims=True) + EPS)
            xn = xf * r
            dg_ref[0:1, :] += _colsum(dh * xn)
            dhg = dh * g_ref[...]
            dxin = dxm + r * (dhg - xn * jnp.mean(dhg * xn, axis=-1, keepdims=True))
            dxin_ref[...] = dxin
            dxinb_ref[...] = dxin.astype(MXU)

        @pl.when(step == nt - 1)
        def _():
            ones = jnp.ones((8, CHUNK), F32)
            for g in range(HEADS):
                dws_ref[g] = jnp.where(mask, dws_ref[g], 0.0)
                row = lax.dot_general(ones, dbs_acc[g], (((1,), (1,)), ((), ())), preferred_element_type=F32,
                                      precision=lax.Precision.HIGHEST)
                dbs_ref[g:g + 1, :] = row[0:1, :]

    row = lambda n: pl.BlockSpec((TM, n), lambda i: (nt - 1 - i, 0))
    br_spec = _const((N_DEV, WA, NB_BR), (0, 0, 0))
    acc_spec = lambda shape: pl.BlockSpec(shape, lambda i: (0,) * len(shape))
    more_in = tuple(in_proj) if fused else ()
    more_in_specs = [_const((N_DEV, D, NB_IN), (0, 0, 0)), row(D), _const((1, D), (0, 0))] if fused else []
    more_out_specs = [row(D), acc_spec((8, D)), row(D)] if fused else []
    more_out_shapes = [_sds((s, D), F32), _sds((8, D), F32), _sds((s, D), MXU)] if fused else []
    return _call(
        body, comms, *_grid_marks(nt),
        (dxmid, p, yabc, pacz, babc, wpool, pscale, gsgu, wsp, bsp_t, convc, wa_all, wb_all, wc_all, wo_all) + more_in,
        name=name, grid=(nt,),
        in_specs=[row(D), row(NCOL), row(3 * WA), row(2 * WA), row(3 * D),
                  _const((HEADS, CHUNK, CHUNK), (0, 0, 0)), _const((1, WA), (0, 0)), _const((1, WA), (0, 0)),
                  _const((HEADS, CHUNK, CHUNK), (0, 0, 0)), _const((CHUNK, HEADS), (0, 0)), _const((3, WA), (0, 0)),
                  br_spec, br_spec, br_spec, _const((N_DEV, ROWS_O, D), (0, 0, 0))] + more_in_specs,
        out_specs=[row(NCOL), row(3 * D), acc_spec((HEADS, CHUNK, CHUNK)), acc_spec((HEADS, CHUNK, CHUNK)),
                   acc_spec((8, WA)), acc_spec((8, CHUNK))] + more_out_specs,
        out_shape=[_sds((s, NCOL), MXU), _sds((s, 3 * D), MXU), _sds((HEADS, CHUNK, CHUNK), F32),
                   _sds((HEADS, CHUNK, CHUNK), F32), _sds((8, WA), F32), _sds((8, CHUNK), F32)] + more_out_shapes,
        scratch_shapes=[pltpu.VMEM((HALO_POOL, WA), F32), pltpu.VMEM((HALO_CONV, WA), F32),
                        pltpu.VMEM((HEADS, CHUNK, CHUNK), F32), pltpu.VMEM((TM, WA), F32), pltpu.VMEM((TM, WA), F32)],
        compiler_params=_params(),
    )


def _wgrad(a, b, a_spec, b_spec, n_out, acc_shape, out_shape, out_spec, store, name, comms=None):
    s = a.shape[-2]
    ts = min(TS_WGRAD, s)
    n_steps = s // ts

    def body(a_ref, b_ref, o_ref, acc_ref):
        k = pl.program_id(1)

        @pl.when(k == 0)
        def _():
            acc_ref[...] = jnp.zeros_like(acc_ref)

        acc_ref[...] += _dot_tn(a_ref[...], b_ref[...])

        @pl.when(k == n_steps - 1)
        def _():
            store(o_ref, acc_ref)

    (res,), extra = _call(
        body, comms, lambda: (pl.program_id(0) == 0) & (pl.program_id(1) == 0),
        lambda: (pl.program_id(0) == (3 * n_out) // 4) & (pl.program_id(1) == 0),
        lambda: (pl.program_id(0) == n_out - 1) & (pl.program_id(1) == n_steps - 1), (a, b),
        name=name, grid=(n_out, n_steps),
        in_specs=[a_spec(ts), b_spec(ts)], out_specs=[out_spec], out_shape=[_sds(out_shape, WIRE)],
        scratch_shapes=[pltpu.VMEM(acc_shape, F32)],
        compiler_params=_params(2),
    )
    return res, extra


def _store_plain(o_ref, acc_ref):
    o_ref[...] = acc_ref[...].astype(o_ref.dtype)


def _store_lane_blocks(o_ref, acc_ref):
    for d in range(N_DEV):
        o_ref[d] = acc_ref[:, d * NB_BR:(d + 1) * NB_BR].astype(o_ref.dtype)


def _wgrad_in(h, dp, name, comms=None):
    return _wgrad(h, dp, lambda ts: pl.BlockSpec((ts, D), lambda j, k: (k, 0)),
                  lambda ts: pl.BlockSpec((ts, NB_IN), lambda j, k: (k, j)), N_DEV, (D, NB_IN),
                  (N_DEV, D, NB_IN), pl.BlockSpec((None, D, NB_IN), lambda j, k: (j, 0, 0)), _store_plain, name, comms)


def _wgrad_up(h, du, name, comms=None):
    return _wgrad(h, du, lambda ts: pl.BlockSpec((ts, D), lambda j, k: (k, 0)),
                  lambda ts: pl.BlockSpec((None, ts, NB_UP), lambda j, k: (j, k, 0)), N_DEV, (D, NB_UP),
                  (N_DEV, D, NB_UP), pl.BlockSpec((None, D, NB_UP), lambda j, k: (j, 0, 0)), _store_plain, name, comms)


def _wgrad_down(act, dxo, name, comms=None):
    return _wgrad(act, dxo, lambda ts: pl.BlockSpec((None, ts, NB_UP), lambda j, k: (j, k, 0)),
                  lambda ts: pl.BlockSpec((ts, D), lambda j, k: (k, 0)), N_DEV // 2, (NB_UP, D),
                  (DFF, D), pl.BlockSpec((NB_UP, D), lambda j, k: (j, 0)), _store_plain, name, comms)


def _wgrad_o(merged, dxmid, name, comms=None):
    return _wgrad(merged, dxmid, lambda ts: pl.BlockSpec((ts, D), lambda j, k: (k, 0)),
                  lambda ts: pl.BlockSpec((ts, D), lambda j, k: (k, 0)), 1, (D, D),
                  (D, D), pl.BlockSpec((D, D), lambda j, k: (0, 0)), _store_plain, name, comms)


def _wgrad_branches(yabc, dbabc, name, comms=None):
    return _wgrad(yabc, dbabc, lambda ts: pl.BlockSpec((ts, WA), lambda j, k: (k, j)),
                  lambda ts: pl.BlockSpec((ts, D), lambda j, k: (k, j)), 3, (WA, D),
                  (N_DEV, 3, WA, NB_BR), pl.BlockSpec((N_DEV, None, WA, NB_BR), lambda j, k: (0, j, 0, 0)),
                  _store_lane_blocks, name, comms)


def _adamw_math(g, w, m, v):
    m = ADAM_B1 * m + (1.0 - ADAM_B1) * g
    v = ADAM_B2 * v + (1.0 - ADAM_B2) * (g * g)
    m_hat = m / (1.0 - ADAM_B1 ** ADAM_STEP)
    v_hat = v / (1.0 - ADAM_B2 ** ADAM_STEP)
    delta = -ADAM_LR * (m_hat / (jnp.sqrt(v_hat) + ADAM_EPS) + ADAM_WD * w)
    return delta, m, v


def _adamw_sum(parts, mid, w, m, v, layer, prev, tr, name, transposed=False):
    n_layers, r, c = w.shape[0], parts.shape[2], parts.shape[3]

    def body(p_ref, w_ref, m_ref, v_ref, *rest):
        g_ref, d_ref, mo_ref, vo_ref = rest[-4:]
        g = p_ref[0].astype(F32)
        for k in range(1, N_DEV):
            g = g + p_ref[k].astype(F32)
        if transposed:
            g = g.T
        g_ref[...] = g
        d_ref[...], mo_ref[...], vo_ref[...] = _adamw_math(g, w_ref[...], m_ref[...], v_ref[...])

    if transposed:
        blk = pl.BlockSpec((None, c, tr), lambda i: (layer, 0, i))
    else:
        blk = pl.BlockSpec((None, tr, c), lambda i: (layer, i, 0))
    extra = [] if prev is None else list(prev)
    return pl.pallas_call(
        body, name=name, grid=(r // tr,),
        in_specs=[pl.BlockSpec((N_DEV, None, tr, c), lambda i: (0, mid, i, 0)), blk, blk, blk]
        + [pl.BlockSpec(memory_space=pl.ANY)] * len(extra),
        out_specs=[blk] * 4, out_shape=[_sds(w.shape, F32)] * 4,
        input_output_aliases={4 + k: k for k in range(len(extra))},
        compiler_params=_params(),
    )(parts, w, m, v, *extra)


def _sum_parts(parts, name):
    _, r, c = parts.shape

    def body(p_ref, o_ref):
        g = p_ref[0]
        for k in range(1, N_DEV):
            g = g + p_ref[k]
        o_ref[...] = g

    return pl.pallas_call(body, name=name, out_shape=_sds((r, c), F32),
                          compiler_params=pltpu.CompilerParams(vmem_limit_bytes=VMEM_LIMIT))(parts)


def _adamw_small(g, w, m, v, name):
    def body(g_ref, w_ref, m_ref, v_ref, d_ref, mo_ref, vo_ref):
        d_ref[...], mo_ref[...], vo_ref[...] = _adamw_math(g_ref[...], w_ref[...], m_ref[...], v_ref[...])

    return pl.pallas_call(body, name=name, out_shape=[_sds(w.shape, F32)] * 3)(g, w, m, v)


HBM_SPEC = pl.BlockSpec(memory_space=pltpu.HBM)


def _position():
    return lax.axis_index("x"), lax.axis_index("y"), lax.axis_index("c")


def _device_index(chip, core):
    return 4 * chip[0] + 2 * chip[1] + core


def _gather(shards, layer):
    n = len(shards)
    per = 8

    def first_copies(ins, outs, send, recv):
        x, y, c = _position()
        me = 4 * x + 2 * y + c
        targets = [(x, y, 1 - c), (1 - x, y, c), (x, 1 - y, c), (1 - x, 1 - y, c)]
        remote = [pltpu.make_async_remote_copy(
            src_ref=ins[t].at[layer], dst_ref=outs[t].at[me], send_sem=send.at[per * t + k],
            recv_sem=recv.at[per * t + k], device_id=to, device_id_type=MESH)
            for t in range(n) for k, to in enumerate(targets)]
        local = [pltpu.make_async_copy(ins[t].at[layer], outs[t].at[me], send.at[per * t + 4]) for t in range(n)]
        return remote, local

    def passed_on(outs, send, recv):
        x, y, c = _position()
        chips = [(1 - x, y), (x, 1 - y), (1 - x, 1 - y)]
        return [pltpu.make_async_remote_copy(
            src_ref=outs[t].at[_device_index(chip, c)], dst_ref=outs[t].at[_device_index(chip, c)],
            send_sem=send.at[per * t + 5 + j], recv_sem=recv.at[per * t + 5 + j], device_id=(x, y, 1 - c),
            device_id_type=MESH)
            for t in range(n) for j, chip in enumerate(chips)]

    def start(ins, outs, send, recv):
        remote, local = first_copies(ins, outs, send, recv)
        for cp in local + remote:
            cp.start()

    def mid(ins, outs, send, recv):
        remote, local = first_copies(ins, outs, send, recv)
        for cp in remote:
            cp.wait()
        for cp in local:
            cp.wait()
        for cp in passed_on(outs, send, recv):
            cp.start()

    def finish(ins, outs, send, recv):
        for cp in passed_on(outs, send, recv):
            cp.wait()

    return _Comm(shards, [_sds((N_DEV,) + a.shape[1:], a.dtype) for a in shards], per * n, start, finish, mid)


def _run_comms(comms, name):
    def body():
        pass

    _, extra = _call(body, comms, None, None, None, (), name=name, in_specs=[], out_specs=[], out_shape=[])
    return extra


def _exchange(parts):
    n = len(parts)
    per = 8
    flips = [(0, 0, 1), (1, 0, 0), (1, 0, 1), (0, 1, 0), (0, 1, 1), (1, 1, 0), (1, 1, 1)]

    def copies(ins, outs, send, recv):
        x, y, c = _position()
        me = 4 * x + 2 * y + c
        remote = []
        for t in range(n):
            for k, (fx, fy, fc) in enumerate(flips):
                peer = ((1 - x if fx else x), (1 - y if fy else y), (1 - c if fc else c))
                remote.append(pltpu.make_async_remote_copy(
                    src_ref=ins[t].at[_device_index(peer[:2], peer[2])], dst_ref=outs[t].at[me],
                    send_sem=send.at[per * t + k], recv_sem=recv.at[per * t + k], device_id=peer, device_id_type=MESH))
        local = [pltpu.make_async_copy(ins[t].at[me], outs[t].at[me], send.at[per * t + 7]) for t in range(n)]
        return remote, local

    def start(ins, outs, send, recv):
        remote, local = copies(ins, outs, send, recv)
        for cp in local + remote:
            cp.start()

    def finish(ins, outs, send, recv):
        remote, local = copies(ins, outs, send, recv)
        for cp in remote:
            cp.wait()
        for cp in local:
            cp.wait()

    return _Comm(parts, [_sds(a.shape, a.dtype) for a in parts], per * n, start, finish)


def _rows128(a):
    return a.reshape(-1, 128)


def kernel(x, g_mix, w_in, w_pool, pool_scale, g_sgu, w_spatial, b_spatial, conv_c, w_branch_a, w_branch_b, w_branch_c, w_o, g_ffn, w_up, conv_ffn, conv_ffn_b, w_down, g_final, loss_target, m_g_mix, m_w_in, m_w_pool, m_pool_scale, m_g_sgu, m_w_spatial, m_b_spatial, m_conv_c, m_w_branch_a, m_w_branch_b, m_w_branch_c, m_w_o, m_g_ffn, m_w_up, m_conv_ffn, m_conv_ffn_b, m_w_down, m_g_final, v_g_mix, v_w_in, v_w_pool, v_pool_scale, v_g_sgu, v_w_spatial, v_b_spatial, v_conv_c, v_w_branch_a, v_w_branch_b, v_w_branch_c, v_w_o, v_g_ffn, v_w_up, v_conv_ffn, v_conv_ffn_b, v_w_down, v_g_final):
    s = x.shape[1]
    n_layers = g_mix.shape[0]
    x0 = x.reshape(s, D)
    target = loss_target.reshape(s, D)
    me = 4 * lax.axis_index("x") + 2 * lax.axis_index("y") + lax.axis_index("c")

    first_shards = [w_in.astype(MXU), conv_c]
    mix_shards = [w_branch_a.astype(MXU), w_branch_b.astype(MXU), w_branch_c.astype(MXU), w_o.astype(MXU),
                  w_down.astype(MXU)]
    up_shards = [w_up.astype(MXU), conv_ffn]
    (first_now,) = _run_comms([_gather(first_shards, 0)], "gather_first_0")
    mix_now = up_now = None
    wpool_b = w_pool.astype(MXU)
    bsp_t = jnp.swapaxes(b_spatial, 1, 2)
    convb_blk = conv_ffn_b.reshape(n_layers, N_DEV, NB_UP)

    saved = []
    weights = []
    xl = x0
    for l in range(n_layers):
        win8, convc8 = first_now
        convc_full = jnp.transpose(convc8, (1, 0, 2)).reshape(3, WA)
        more = l + 1 < n_layers
        mixer_args = (wpool_b[l], pool_scale[l:l + 1], g_sgu[l:l + 1], w_spatial[l], bsp_t[l], convc_full)
        if l == 0:
            (p, h), (mix_now,) = _rms_proj(xl, g_mix[l:l + 1], win8, f"in_proj_{l}", [_gather(mix_shards, l)])
            wa8, wb8, wc8, wo8, wd8 = mix_now
            (xmid, yabc, pacz, babc, merged), (up_now,) = _mixer_fwd(
                xl, p, *mixer_args, wa8, wb8, wc8, wo8, f"mixer_fwd_{l}", [_gather(up_shards, l)])
        else:
            wa8, wb8, wc8, wo8, wd8 = mix_now
            (xmid, yabc, pacz, babc, merged, p, h), (up_now,) = _mixer_fwd(
                xl, None, *mixer_args, wa8, wb8, wc8, wo8, f"mixer_fwd_{l}", [_gather(up_shards, l)],
                in_proj=(g_mix[l:l + 1], win8))
        wup8, convf8 = up_now
        weights.append((win8, wa8, wb8, wc8, wo8, wup8, wd8, convc_full, convf8))
        if more:
            (xout, h2, upre, up, act), (first_now, mix_now) = _ffn_block_fwd(
                xmid, g_ffn[l:l + 1], wup8, convf8, convb_blk[l], wd8, f"ffn_fwd_{l}",
                [_gather(first_shards, l + 1), _gather(mix_shards, l + 1)])
        else:
            (xout, h2, upre, up, act, dg_final, loss_local, dx_b), _ = _ffn_block_fwd(
                xmid, g_ffn[l:l + 1], wup8, convf8, convb_blk[l], wd8, f"ffn_fwd_{l}",
                loss_head=(g_final.reshape(1, D), target))
        saved.append((xl, p, h, xmid, yabc, pacz, babc, merged, upre, h2, act, up))
        xl = xout

    dx = xl

    received = [dict() for _ in range(n_layers)]
    small = {("final", "g_final"): dg_final}
    small_sums = {}
    waiting = None

    def exchange_of(named):
        return [_exchange([a for _, a in named])]

    def land(layer, named, got):
        received[layer].update({k: a for (k, _), a in zip(named, got[0])})

    def gather_small(keys):
        packed = jnp.concatenate([_rows128(small[k]) for k in keys], axis=0)[None]
        return _gather([packed], 0)

    def sum_small(keys, gathered, name):
        summed = _sum_parts(gathered, name)
        row = 0
        for k in keys:
            n_rows = small[k].size // 128
            small_sums[k] = summed[row:row + n_rows].reshape(small[k].shape)
            row += n_rows

    for l in reversed(range(n_layers)):
        xin, p, h, xmid, yabc, pacz, babc, merged, upre, h2, act, up = saved[l]
        win8, wa8, wb8, wc8, wo8, wup8, wd8, convc_full, convf8 = weights[l]
        last = l == 0
        (dupre, dconvf, dxmid, dg_ffn, dxmid_b), got = _ffn_block_bwd(
            dx, upre, up, convf8, wd8, wup8, xmid, g_ffn[l:l + 1], f"ffn_bwd_{l}",
            None if waiting is None else exchange_of(waiting[1]))
        if waiting is not None:
            land(waiting[0], waiting[1], got)
        small[(l, "conv_ffn")] = dconvf
        small[(l, "g_ffn")] = dg_ffn
        keys_a = [k for k in small if k not in small_sums]
        g_wdown, got = _wgrad_down(act, dx_b, f"wgrad_down_{l}", [gather_small(keys_a)] if last else None)
        if last:
            sum_small(keys_a, got[0][0], "sum_small_grads_a")
        down = [("w_down", g_wdown.reshape(N_DEV, ROWS_DN, D))]
        g_wup, got = _wgrad_up(h2, dupre, f"wgrad_up_{l}", exchange_of(down) if last else None)
        if last:
            land(l, down, got)
        upw = [("w_up", g_wup)]
        mixer_args = (dxmid, p, yabc, pacz, babc, wpool_b[l], pool_scale[l:l + 1], g_sgu[l:l + 1], w_spatial[l],
                      bsp_t[l], convc_full, wa8, wb8, wc8, wo8, f"mixer_bwd_{l}")
        if last:
            (dp, dbabc, dwp, dws, mixer_small, dbs), got = _mixer_bwd(*mixer_args, exchange_of(upw))
            land(l, upw, got)
        else:
            (dp, dbabc, dwp, dws, mixer_small, dbs, dx, dg_mix, dx_b), got = _mixer_bwd(
                *mixer_args, exchange_of(down + upw), in_proj=(win8, xin, g_mix[l:l + 1]))
            land(l, down + upw, got)
        small.update({(l, "w_pool"): dwp, (l, "mixer_small"): mixer_small, (l, "w_spatial"): dws,
                      (l, "b_spatial"): dbs})
        g_wo, _ = _wgrad_o(merged, dxmid_b, f"wgrad_o_{l}")
        g_br, _ = _wgrad_branches(yabc, dbabc, f"wgrad_branches_{l}")
        mixer_w = [("branches", g_br), ("w_o", g_wo.reshape(N_DEV, ROWS_O, D))]
        keys_b = [k for k in small if k not in small_sums]
        g_win, got = _wgrad_in(h, dp, f"wgrad_in_{l}",
                               exchange_of(mixer_w) + ([gather_small(keys_b)] if last else []))
        land(l, mixer_w, got)
        inw = [("w_in", g_win)]
        if last:
            sum_small(keys_b, got[1][0], "sum_small_grads_b")
            (dx, dg_mix), got = _proj_bwd(dp, win8, xin, g_mix[l:l + 1], dxmid, f"in_proj_bwd_{l}", exchange_of(inw))
            land(l, inw, got)
        else:
            waiting = (l, inw)
        small[(l, "g_mix")] = dg_mix
    grad_x = dx.reshape(1, s, D)
    late_keys = [k for k in small if k not in small_sums]
    (gathered_late,) = _run_comms([gather_small(late_keys)], "gather_last_small_grads")[0]
    sum_small(late_keys, gathered_late, "sum_last_small_grads")

    def update_big(key, mid, w, m, v, tr, tag, transposed=False):
        swap = (lambda a: jnp.swapaxes(a, 1, 2)) if transposed else (lambda a: a)
        w, m, v = swap(w), swap(m), swap(v)
        outs = None
        for l in range(n_layers):
            parts = received[l][key]
            if parts.ndim == 3:
                parts = parts.reshape(N_DEV, 1, *parts.shape[1:])
            outs = _adamw_sum(parts, mid, w, m, v, l, outs, tr, f"adamw_{tag}_{l}", transposed)
        return [swap(o) for o in outs]

    up_in = update_big("w_in", 0, w_in, m_w_in, v_w_in, 512, "w_in")
    up_a = update_big("branches", 0, w_branch_a, m_w_branch_a, v_w_branch_a, WA, "w_branch_a")
    up_b = update_big("branches", 1, w_branch_b, m_w_branch_b, v_w_branch_b, WA, "w_branch_b")
    up_c = update_big("branches", 2, w_branch_c, m_w_branch_c, v_w_branch_c, WA, "w_branch_c")
    up_o = update_big("w_o", 0, w_o, m_w_o, v_w_o, ROWS_O, "w_o")
    up_up = update_big("w_up", 0, w_up, m_w_up, v_w_up, 512, "w_up", transposed=True)
    up_down = update_big("w_down", 0, w_down, m_w_down, v_w_down, ROWS_DN, "w_down")

    stack = lambda kind: jnp.stack([small_sums[(l, kind)] for l in range(n_layers)], axis=0)
    grad_g_mix = stack("g_mix")[:, 0, :]
    grad_w_pool = stack("w_pool")
    mixer_sums = stack("mixer_small")
    grad_pool_scale = mixer_sums[:, 0, :]
    grad_g_sgu = mixer_sums[:, 1, :]
    grad_conv_c = lax.dynamic_slice_in_dim(mixer_sums[:, 2:5, :], me * (WA // N_DEV), WA // N_DEV, axis=2)
    grad_w_spatial = stack("w_spatial")
    grad_b_spatial = stack("b_spatial")[:, 0:HEADS, :]
    grad_g_ffn = stack("g_ffn")[:, 0, :]
    conv_grads = stack("conv_ffn")
    grad_conv_ffn = lax.dynamic_index_in_dim(conv_grads, me, axis=1, keepdims=False)[:, 0:3, :]
    grad_conv_ffn_b = conv_grads[:, :, 3, :].reshape(n_layers, 2 * DFF)
    grad_g_final = small_sums[("final", "g_final")][0]

    def update_small(g, w, m, v, tag):
        shape2 = (-1, w.shape[-1])
        outs = _adamw_small(g.reshape(shape2), w.reshape(shape2), m.reshape(shape2), v.reshape(shape2), f"adamw_{tag}")
        return [g] + [o.reshape(w.shape) for o in outs]

    up = {
        "g_mix": update_small(grad_g_mix, g_mix, m_g_mix, v_g_mix, "g_mix"),
        "w_in": up_in,
        "w_pool": update_small(grad_w_pool, w_pool, m_w_pool, v_w_pool, "w_pool"),
        "pool_scale": update_small(grad_pool_scale, pool_scale, m_pool_scale, v_pool_scale, "pool_scale"),
        "g_sgu": update_small(grad_g_sgu, g_sgu, m_g_sgu, v_g_sgu, "g_sgu"),
        "w_spatial": update_small(grad_w_spatial, w_spatial, m_w_spatial, v_w_spatial, "w_spatial"),
        "b_spatial": update_small(grad_b_spatial, b_spatial, m_b_spatial, v_b_spatial, "b_spatial"),
        "conv_c": update_small(grad_conv_c, conv_c, m_conv_c, v_conv_c, "conv_c"),
        "w_branch_a": up_a,
        "w_branch_b": up_b,
        "w_branch_c": up_c,
        "w_o": up_o,
        "g_ffn": update_small(grad_g_ffn, g_ffn, m_g_ffn, v_g_ffn, "g_ffn"),
        "w_up": up_up,
        "conv_ffn": update_small(grad_conv_ffn, conv_ffn, m_conv_ffn, v_conv_ffn, "conv_ffn"),
        "conv_ffn_b": update_small(grad_conv_ffn_b, conv_ffn_b, m_conv_ffn_b, v_conv_ffn_b, "conv_ffn_b"),
        "w_down": up_down,
        "g_final": update_small(grad_g_final, g_final, m_g_final, v_g_final, "g_final"),
    }
    loss = lax.psum(loss_local[0, 0], AXES)
    order = list(up)
    return (loss, grad_x, *[up[k][0] for k in order], *[up[k][1] for k in order], *[up[k][2] for k in order],
            *[up[k][3] for k in order])
```
